```python
import math
import jax, jax.numpy as jnp
from jax import lax
import numpy as np

D_MODEL = 1024
BATCH = 8
SEQ = 2048
DEPTH = 1

FOX_HEADS = 8
FOX_HEAD_DIM = 64
FOX_WIDTH = FOX_HEADS * FOX_HEAD_DIM
MLA_HEADS = 8
MLA_NOPE_DIM = 64
MLA_ROPE_DIM = 32
MLA_V_DIM = 64
MLA_Q_LORA = 768
MLA_KV_LORA = 256
MLA_WIDTH = MLA_HEADS * MLA_V_DIM
ROPE_THETA = 10000.0
D_FF = ((8 * D_MODEL + 3 * 256 - 1) // (3 * 256)) * 256
Q_BLOCK = 128
NORM_EPS = 1e-6

IN_WIDTHS = (
    FOX_WIDTH,
    FOX_WIDTH,
    FOX_WIDTH,
    FOX_HEADS,
    MLA_Q_LORA,
    MLA_KV_LORA,
    MLA_ROPE_DIM,
    D_MODEL,
    D_MODEL,
)
D_IN = sum(IN_WIDTHS)

kernel_name = "hybrid_fox_mla_sandwich_adaln_block"


def rmsnorm(x, g):
    xf = x.astype(jnp.float32)
    y = xf * lax.rsqrt(jnp.mean(xf * xf, axis=-1, keepdims=True) + NORM_EPS)
    return (y * g.astype(jnp.float32)).astype(x.dtype)


def rope(x, cos, sin):
    x1, x2 = jnp.split(x, 2, axis=-1)
    return jnp.concatenate([x1 * cos - x2 * sin, x2 * cos + x1 * sin], axis=-1).astype(x.dtype)


def blocked_causal_attention(q, k, v, scale, log_decay=None):
    B, H, S, dk = q.shape
    nb = S // Q_BLOCK
    q_blocks = q.reshape(B, H, nb, Q_BLOCK, dk).transpose(2, 0, 1, 3, 4)
    d_blocks = None if log_decay is None else log_decay.reshape(B, H, nb, Q_BLOCK).transpose(2, 0, 1, 3)
    key_pos = jnp.arange(S)

    def one_block(args):
        blk, q_blk, d_blk = args
        s = jnp.einsum("bhqd,bhkd->bhqk", q_blk, k, preferred_element_type=jnp.float32) * scale
        if d_blk is not None:
            s = s + (d_blk[..., :, None] - log_decay[:, :, None, :])
        query_pos = blk * Q_BLOCK + jnp.arange(Q_BLOCK)
        s = jnp.where(key_pos[None, :] <= query_pos[:, None], s, -jnp.inf)
        p = jax.nn.softmax(s, axis=-1)
        return jnp.einsum("bhqk,bhkd->bhqd", p.astype(v.dtype), v)

    out = lax.map(one_block, (jnp.arange(nb), q_blocks, d_blocks))
    return out.transpose(1, 2, 0, 3, 4).reshape(B, H, S, v.shape[-1])


def _fwd_setup_inputs(seed: int = 0) -> dict:
    key = jax.random.key(seed)
    ks = jax.random.split(key, 24)
    f32 = jnp.float32

    def normal(k, shape, fan_in):
        return jax.random.normal(k, shape, f32) * (fan_in ** -0.5)

    def gain(k, shape):
        return 1.0 + 0.05 * jax.random.normal(k, shape, f32)

    x = jax.random.normal(ks[0], (BATCH, SEQ, D_MODEL), f32)
    c = jax.random.normal(ks[1], (BATCH, D_MODEL), f32)
    offsets = jax.random.randint(ks[2], (BATCH, 1), 0, 1024, dtype=jnp.int32)
    positions = (offsets + jnp.arange(SEQ, dtype=jnp.int32)[None, :]).astype(jnp.int32)

    return {
        "x": x,
        "c": c,
        "positions": positions,
        "w_ada": normal(ks[3], (DEPTH, D_MODEL, 6 * D_MODEL), D_MODEL),
        "b_ada": 0.02 * jax.random.normal(ks[4], (DEPTH, 6 * D_MODEL), f32),
        "g_pre_mix": gain(ks[5], (DEPTH, D_MODEL)),
        "g_post_mix": gain(ks[6], (DEPTH, D_MODEL)),
        "g_pre_ffn": gain(ks[7], (DEPTH, D_MODEL)),
        "g_post_ffn": gain(ks[8], (DEPTH, D_MODEL)),
        "w_in": normal(ks[9], (DEPTH, D_MODEL, D_IN), D_MODEL),
        "b_forget": 3.0 + 0.5 * jax.random.normal(ks[10], (DEPTH, FOX_HEADS), f32),
        "g_q_lora": gain(ks[11], (DEPTH, MLA_Q_LORA)),
        "w_uq": normal(ks[12], (DEPTH, MLA_Q_LORA, MLA_HEADS * (MLA_NOPE_DIM + MLA_ROPE_DIM)), MLA_Q_LORA),
        "g_kv_lora": gain(ks[13], (DEPTH, MLA_KV_LORA)),
        "w_ukv": normal(ks[14], (DEPTH, MLA_KV_LORA, MLA_HEADS * (MLA_NOPE_DIM + MLA_V_DIM)), MLA_KV_LORA),
        "w_proj_fox": normal(ks[15], (DEPTH, FOX_WIDTH, D_MODEL), FOX_WIDTH),
        "w_proj_mla": normal(ks[16], (DEPTH, MLA_WIDTH, D_MODEL), MLA_WIDTH),
        "w_out": normal(ks[17], (DEPTH, D_MODEL, D_MODEL), D_MODEL),
        "w_ffn_in": normal(ks[18], (DEPTH, D_MODEL, 2 * D_FF), D_MODEL),
        "w_ffn_out": normal(ks[19], (DEPTH, D_FF, D_MODEL), D_FF),
    }


def _fwd_reference(x, c, positions, w_ada, b_ada, g_pre_mix, g_post_mix, g_pre_ffn, g_post_ffn,
              w_in, b_forget, g_q_lora, w_uq, g_kv_lora, w_ukv, w_proj_fox, w_proj_mla,
              w_out, w_ffn_in, w_ffn_out):
    B, S, D = x.shape
    inv_freq = 1.0 / (ROPE_THETA ** (jnp.arange(0, MLA_ROPE_DIM, 2, dtype=jnp.float32) / MLA_ROPE_DIM))
    angles = positions.astype(jnp.float32)[..., None] * inv_freq
    cos, sin = jnp.cos(angles), jnp.sin(angles)
    split_points = [int(v) for v in np.cumsum(IN_WIDTHS)[:-1]]
    silu_c = jax.nn.silu(c)

    for l in range(DEPTH):
        mod = (silu_c @ w_ada[l] + b_ada[l])[:, None, :]
        shift_mix, scale_mix, gate_mix, shift_ffn, scale_ffn, gate_ffn = jnp.split(mod, 6, axis=-1)

        h = rmsnorm(x, g_pre_mix[l]) * (1.0 + scale_mix) + shift_mix
        proj = h @ w_in[l]
        (fq, fk, fv, f_logit, cq, ckv, k_rope_in, gate_fox, gate_mla) = jnp.split(proj, split_points, axis=-1)

        q_a = fq.reshape(B, S, FOX_HEADS, FOX_HEAD_DIM).transpose(0, 2, 1, 3)
        k_a = fk.reshape(B, S, FOX_HEADS, FOX_HEAD_DIM).transpose(0, 2, 1, 3)
        v_a = fv.reshape(B, S, FOX_HEADS, FOX_HEAD_DIM).transpose(0, 2, 1, 3)
        log_f = jax.nn.log_sigmoid((f_logit + b_forget[l]).astype(jnp.float32))
        cum_log_f = jnp.cumsum(log_f, axis=1).transpose(0, 2, 1)
        o_a = blocked_causal_attention(q_a, k_a, v_a, 1.0 / math.sqrt(FOX_HEAD_DIM), cum_log_f)
        o_a = o_a.transpose(0, 2, 1, 3).reshape(B, S, FOX_WIDTH)

        q_b = (rmsnorm(cq, g_q_lora[l]) @ w_uq[l]).reshape(B, S, MLA_HEADS, MLA_NOPE_DIM + MLA_ROPE_DIM)
        q_nope, q_pe = jnp.split(q_b, [MLA_NOPE_DIM], axis=-1)
        q_pe = rope(q_pe, cos[:, :, None, :], sin[:, :, None, :])
        q_b = jnp.concatenate([q_nope, q_pe], axis=-1).transpose(0, 2, 1, 3)
        kv_b = (rmsnorm(ckv, g_kv_lora[l]) @ w_ukv[l]).reshape(B, S, MLA_HEADS, MLA_NOPE_DIM + MLA_V_DIM)
        k_nope, v_b = jnp.split(kv_b, [MLA_NOPE_DIM], axis=-1)
        k_pe = rope(k_rope_in, cos, sin)
        k_pe = jnp.broadcast_to(k_pe[:, :, None, :], (B, S, MLA_HEADS, MLA_ROPE_DIM))
        k_b = jnp.concatenate([k_nope, k_pe], axis=-1).transpose(0, 2, 1, 3)
        v_b = v_b.transpose(0, 2, 1, 3)
        o_b = blocked_causal_attention(q_b, k_b, v_b, 1.0 / math.sqrt(MLA_NOPE_DIM + MLA_ROPE_DIM))
        o_b = o_b.transpose(0, 2, 1, 3).reshape(B, S, MLA_WIDTH)

        merged = (jax.nn.sigmoid(gate_fox) * (o_a @ w_proj_fox[l])
                  + jax.nn.sigmoid(gate_mla) * (o_b @ w_proj_mla[l]))
        y = merged @ w_out[l]
        x = x + gate_mix * rmsnorm(y, g_post_mix[l])

        h = rmsnorm(x, g_pre_ffn[l]) * (1.0 + scale_ffn) + shift_ffn
        g, u = jnp.split(h @ w_ffn_in[l], 2, axis=-1)
        y = (jax.nn.silu(g) * u) @ w_ffn_out[l]
        x = x + gate_ffn * rmsnorm(y, g_post_ffn[l])

    return x


import jax as _jax
import jax.numpy as _jnp

TWIN_FORMAT = 'train_step'
FWD_PARAMS = ['x', 'c', 'positions', 'w_ada', 'b_ada', 'g_pre_mix', 'g_post_mix', 'g_pre_ffn', 'g_post_ffn', 'w_in', 'b_forget', 'g_q_lora', 'w_uq', 'g_kv_lora', 'w_ukv', 'w_proj_fox', 'w_proj_mla', 'w_out', 'w_ffn_in', 'w_ffn_out']
TWIN_WEIGHTS = ['w_ada', 'b_ada', 'g_pre_mix', 'g_post_mix', 'g_pre_ffn', 'g_post_ffn', 'w_in', 'b_forget', 'g_q_lora', 'w_uq', 'g_kv_lora', 'w_ukv', 'w_proj_fox', 'w_proj_mla', 'w_out', 'w_ffn_in', 'w_ffn_out']
TWIN_DIFF_INPUT = 'x'
TWIN_INPUTS = ['x', 'c', 'positions', 'w_ada', 'b_ada', 'g_pre_mix', 'g_post_mix', 'g_pre_ffn', 'g_post_ffn', 'w_in', 'b_forget', 'g_q_lora', 'w_uq', 'g_kv_lora', 'w_ukv', 'w_proj_fox', 'w_proj_mla', 'w_out', 'w_ffn_in', 'w_ffn_out', 'loss_target', 'm_w_ada', 'm_b_ada', 'm_g_pre_mix', 'm_g_post_mix', 'm_g_pre_ffn', 'm_g_post_ffn', 'm_w_in', 'm_b_forget', 'm_g_q_lora', 'm_w_uq', 'm_g_kv_lora', 'm_w_ukv', 'm_w_proj_fox', 'm_w_proj_mla', 'm_w_out', 'm_w_ffn_in', 'm_w_ffn_out', 'v_w_ada', 'v_b_ada', 'v_g_pre_mix', 'v_g_post_mix', 'v_g_pre_ffn', 'v_g_post_ffn', 'v_w_in', 'v_b_forget', 'v_g_q_lora', 'v_w_uq', 'v_g_kv_lora', 'v_w_ukv', 'v_w_proj_fox', 'v_w_proj_mla', 'v_w_out', 'v_w_ffn_in', 'v_w_ffn_out']
TWIN_OUTPUTS = ['loss', 'grad_x', 'grad_w_ada', 'grad_b_ada', 'grad_g_pre_mix', 'grad_g_post_mix', 'grad_g_pre_ffn', 'grad_g_post_ffn', 'grad_w_in', 'grad_b_forget', 'grad_g_q_lora', 'grad_w_uq', 'grad_g_kv_lora', 'grad_w_ukv', 'grad_w_proj_fox', 'grad_w_proj_mla', 'grad_w_out', 'grad_w_ffn_in', 'grad_w_ffn_out', 'delta_w_ada', 'delta_b_ada', 'delta_g_pre_mix', 'delta_g_post_mix', 'delta_g_pre_ffn', 'delta_g_post_ffn', 'delta_w_in', 'delta_b_forget', 'delta_g_q_lora', 'delta_w_uq', 'delta_g_kv_lora', 'delta_w_ukv', 'delta_w_proj_fox', 'delta_w_proj_mla', 'delta_w_out', 'delta_w_ffn_in', 'delta_w_ffn_out', 'new_m_w_ada', 'new_m_b_ada', 'new_m_g_pre_mix', 'new_m_g_post_mix', 'new_m_g_pre_ffn', 'new_m_g_post_ffn', 'new_m_w_in', 'new_m_b_forget', 'new_m_g_q_lora', 'new_m_w_uq', 'new_m_g_kv_lora', 'new_m_w_ukv', 'new_m_w_proj_fox', 'new_m_w_proj_mla', 'new_m_w_out', 'new_m_w_ffn_in', 'new_m_w_ffn_out', 'new_v_w_ada', 'new_v_b_ada', 'new_v_g_pre_mix', 'new_v_g_post_mix', 'new_v_g_pre_ffn', 'new_v_g_post_ffn', 'new_v_w_in', 'new_v_b_forget', 'new_v_g_q_lora', 'new_v_w_uq', 'new_v_g_kv_lora', 'new_v_w_ukv', 'new_v_w_proj_fox', 'new_v_w_proj_mla', 'new_v_w_out', 'new_v_w_ffn_in', 'new_v_w_ffn_out']
TWIN_LEAF_KINDS = {'loss': 'loss', 'grad_x': 'grad_x', 'grad_w_ada': 'grad_w', 'grad_b_ada': 'grad_w', 'grad_g_pre_mix': 'grad_w', 'grad_g_post_mix': 'grad_w', 'grad_g_pre_ffn': 'grad_w', 'grad_g_post_ffn': 'grad_w', 'grad_w_in': 'grad_w', 'grad_b_forget': 'grad_w', 'grad_g_q_lora': 'grad_w', 'grad_w_uq': 'grad_w', 'grad_g_kv_lora': 'grad_w', 'grad_w_ukv': 'grad_w', 'grad_w_proj_fox': 'grad_w', 'grad_w_proj_mla': 'grad_w', 'grad_w_out': 'grad_w', 'grad_w_ffn_in': 'grad_w', 'grad_w_ffn_out': 'grad_w', 'delta_w_ada': 'delta_w', 'delta_b_ada': 'delta_w', 'delta_g_pre_mix': 'delta_w', 'delta_g_post_mix': 'delta_w', 'delta_g_pre_ffn': 'delta_w', 'delta_g_post_ffn': 'delta_w', 'delta_w_in': 'delta_w', 'delta_b_forget': 'delta_w', 'delta_g_q_lora': 'delta_w', 'delta_w_uq': 'delta_w', 'delta_g_kv_lora': 'delta_w', 'delta_w_ukv': 'delta_w', 'delta_w_proj_fox': 'delta_w', 'delta_w_proj_mla': 'delta_w', 'delta_w_out': 'delta_w', 'delta_w_ffn_in': 'delta_w', 'delta_w_ffn_out': 'delta_w', 'new_m_w_ada': 'new_m', 'new_m_b_ada': 'new_m', 'new_m_g_pre_mix': 'new_m', 'new_m_g_post_mix': 'new_m', 'new_m_g_pre_ffn': 'new_m', 'new_m_g_post_ffn': 'new_m', 'new_m_w_in': 'new_m', 'new_m_b_forget': 'new_m', 'new_m_g_q_lora': 'new_m', 'new_m_w_uq': 'new_m', 'new_m_g_kv_lora': 'new_m', 'new_m_w_ukv': 'new_m', 'new_m_w_proj_fox': 'new_m', 'new_m_w_proj_mla': 'new_m', 'new_m_w_out': 'new_m', 'new_m_w_ffn_in': 'new_m', 'new_m_w_ffn_out': 'new_m', 'new_v_w_ada': 'new_v', 'new_v_b_ada': 'new_v', 'new_v_g_pre_mix': 'new_v', 'new_v_g_post_mix': 'new_v', 'new_v_g_pre_ffn': 'new_v', 'new_v_g_post_ffn': 'new_v', 'new_v_w_in': 'new_v', 'new_v_b_forget': 'new_v', 'new_v_g_q_lora': 'new_v', 'new_v_w_uq': 'new_v', 'new_v_g_kv_lora': 'new_v', 'new_v_w_ukv': 'new_v', 'new_v_w_proj_fox': 'new_v', 'new_v_w_proj_mla': 'new_v', 'new_v_w_out': 'new_v', 'new_v_w_ffn_in': 'new_v', 'new_v_w_ffn_out': 'new_v'}


def _forward(args):
    return _fwd_reference(*[args[k] for k in FWD_PARAMS])


def _output_shape():
    out = _jax.eval_shape(lambda: _forward(_fwd_setup_inputs(0)))
    return out.shape, out.dtype

N_MICROBATCH = 1
ADAM_LR = 0.001
ADAM_B1 = 0.9
ADAM_B2 = 0.999
ADAM_EPS = 1e-08
ADAM_WD = 0.01
ADAM_STEP = 10
PER_EXAMPLE_BATCH_AXIS = {'x': 0, 'c': 0, 'positions': 0, 'loss_target': 0}
SHARED_INPUTS = []
_WEIGHT_DTYPES = {'w_ada': _jnp.float32, 'b_ada': _jnp.float32, 'g_pre_mix': _jnp.float32, 'g_post_mix': _jnp.float32, 'g_pre_ffn': _jnp.float32, 'g_post_ffn': _jnp.float32, 'w_in': _jnp.float32, 'b_forget': _jnp.float32, 'g_q_lora': _jnp.float32, 'w_uq': _jnp.float32, 'g_kv_lora': _jnp.float32, 'w_ukv': _jnp.float32, 'w_proj_fox': _jnp.float32, 'w_proj_mla': _jnp.float32, 'w_out': _jnp.float32, 'w_ffn_in': _jnp.float32, 'w_ffn_out': _jnp.float32}
MOMENT_SCALE = {'w_ada': 2.517589e+00, 'b_ada': 4.429803e+00, 'g_pre_mix': 3.159601e-01, 'g_post_mix': 8.528846e+00, 'g_pre_ffn': 7.625470e-01, 'g_post_ffn': 7.880382e+00, 'w_in': 1.265683e+00, 'b_forget': 1.666495e+00, 'g_q_lora': 8.143494e-02, 'w_uq': 7.640498e-02, 'g_kv_lora': 2.904901e+00, 'w_ukv': 1.530031e+00, 'w_proj_fox': 2.052565e+00, 'w_proj_mla': 1.518971e+00, 'w_out': 2.585037e+00, 'w_ffn_in': 5.646535e-01, 'w_ffn_out': 1.041201e+00}


def _to_microbatches(a, axis):
    t = _jnp.moveaxis(a, axis, 0)
    t = t.reshape((N_MICROBATCH, t.shape[0] // N_MICROBATCH) + t.shape[1:])
    return _jnp.moveaxis(t, 1, axis + 1)


def setup_inputs(seed: int = 0) -> dict:
    inp = _fwd_setup_inputs(seed)
    key = _jax.random.fold_in(_jax.random.key(seed), 7919)
    shape, _ = _output_shape()
    out = dict(inp)
    out["loss_target"] = _jax.random.normal(_jax.random.fold_in(key, 0), shape, _jnp.float32)
    for i, name in enumerate(TWIN_WEIGHTS):
        w = inp[name].astype(_jnp.float32)
        if MOMENT_SCALE is None:
            s = _jnp.sqrt(_jnp.mean(_jnp.square(w)) + 1e-30)
        else:
            s = MOMENT_SCALE[name]
        km, kv = _jax.random.split(_jax.random.fold_in(key, i + 1))
        out[name] = w
        out["m_" + name] = s * _jax.random.normal(km, w.shape, _jnp.float32)
        out["v_" + name] = (s * s) * _jax.random.uniform(kv, w.shape, _jnp.float32, 0.5, 1.5)
    if N_MICROBATCH > 1:
        for name, axis in PER_EXAMPLE_BATCH_AXIS.items():
            out[name] = _to_microbatches(out[name], axis)
    return {'x': out['x'], 'c': out['c'], 'positions': out['positions'], 'w_ada': out['w_ada'], 'b_ada': out['b_ada'], 'g_pre_mix': out['g_pre_mix'], 'g_post_mix': out['g_post_mix'], 'g_pre_ffn': out['g_pre_ffn'], 'g_post_ffn': out['g_post_ffn'], 'w_in': out['w_in'], 'b_forget': out['b_forget'], 'g_q_lora': out['g_q_lora'], 'w_uq': out['w_uq'], 'g_kv_lora': out['g_kv_lora'], 'w_ukv': out['w_ukv'], 'w_proj_fox': out['w_proj_fox'], 'w_proj_mla': out['w_proj_mla'], 'w_out': out['w_out'], 'w_ffn_in': out['w_ffn_in'], 'w_ffn_out': out['w_ffn_out'], 'loss_target': out['loss_target'], 'm_w_ada': out['m_w_ada'], 'm_b_ada': out['m_b_ada'], 'm_g_pre_mix': out['m_g_pre_mix'], 'm_g_post_mix': out['m_g_post_mix'], 'm_g_pre_ffn': out['m_g_pre_ffn'], 'm_g_post_ffn': out['m_g_post_ffn'], 'm_w_in': out['m_w_in'], 'm_b_forget': out['m_b_forget'], 'm_g_q_lora': out['m_g_q_lora'], 'm_w_uq': out['m_w_uq'], 'm_g_kv_lora': out['m_g_kv_lora'], 'm_w_ukv': out['m_w_ukv'], 'm_w_proj_fox': out['m_w_proj_fox'], 'm_w_proj_mla': out['m_w_proj_mla'], 'm_w_out': out['m_w_out'], 'm_w_ffn_in': out['m_w_ffn_in'], 'm_w_ffn_out': out['m_w_ffn_out'], 'v_w_ada': out['v_w_ada'], 'v_b_ada': out['v_b_ada'], 'v_g_pre_mix': out['v_g_pre_mix'], 'v_g_post_mix': out['v_g_post_mix'], 'v_g_pre_ffn': out['v_g_pre_ffn'], 'v_g_post_ffn': out['v_g_post_ffn'], 'v_w_in': out['v_w_in'], 'v_b_forget': out['v_b_forget'], 'v_g_q_lora': out['v_g_q_lora'], 'v_w_uq': out['v_w_uq'], 'v_g_kv_lora': out['v_g_kv_lora'], 'v_w_ukv': out['v_w_ukv'], 'v_w_proj_fox': out['v_w_proj_fox'], 'v_w_proj_mla': out['v_w_proj_mla'], 'v_w_out': out['v_w_out'], 'v_w_ffn_in': out['v_w_ffn_in'], 'v_w_ffn_out': out['v_w_ffn_out']}


def _loss(weights, diff, rest, loss_target):
    with _jax.named_scope("forward"):
        args = {**rest, TWIN_DIFF_INPUT: diff, **{k: w.astype(_WEIGHT_DTYPES[k]) for k, w in weights.items()}}
        y = _forward(args)
    with _jax.named_scope("loss_head"):
        err = _jnp.square(y.astype(_jnp.float32) - loss_target)
        return 0.5 * _jnp.sum(_jnp.mean(err, axis=-1)) if err.ndim else 0.5 * err


def _adamw(w, g, m, v):
    m = ADAM_B1 * m + (1.0 - ADAM_B1) * g
    v = ADAM_B2 * v + (1.0 - ADAM_B2) * _jnp.square(g)
    m_hat = m / (1.0 - ADAM_B1 ** ADAM_STEP)
    v_hat = v / (1.0 - ADAM_B2 ** ADAM_STEP)
    delta = -ADAM_LR * (m_hat / (_jnp.sqrt(v_hat) + ADAM_EPS) + ADAM_WD * w)
    return delta, m, v


def reference(x, c, positions, w_ada, b_ada, g_pre_mix, g_post_mix, g_pre_ffn, g_post_ffn, w_in, b_forget, g_q_lora, w_uq, g_kv_lora, w_ukv, w_proj_fox, w_proj_mla, w_out, w_ffn_in, w_ffn_out, loss_target, m_w_ada, m_b_ada, m_g_pre_mix, m_g_post_mix, m_g_pre_ffn, m_g_post_ffn, m_w_in, m_b_forget, m_g_q_lora, m_w_uq, m_g_kv_lora, m_w_ukv, m_w_proj_fox, m_w_proj_mla, m_w_out, m_w_ffn_in, m_w_ffn_out, v_w_ada, v_b_ada, v_g_pre_mix, v_g_post_mix, v_g_pre_ffn, v_g_post_ffn, v_w_in, v_b_forget, v_g_q_lora, v_w_uq, v_g_kv_lora, v_w_ukv, v_w_proj_fox, v_w_proj_mla, v_w_out, v_w_ffn_in, v_w_ffn_out):
    given = dict(x=x, c=c, positions=positions, w_ada=w_ada, b_ada=b_ada, g_pre_mix=g_pre_mix, g_post_mix=g_post_mix, g_pre_ffn=g_pre_ffn, g_post_ffn=g_post_ffn, w_in=w_in, b_forget=b_forget, g_q_lora=g_q_lora, w_uq=w_uq, g_kv_lora=g_kv_lora, w_ukv=w_ukv, w_proj_fox=w_proj_fox, w_proj_mla=w_proj_mla, w_out=w_out, w_ffn_in=w_ffn_in, w_ffn_out=w_ffn_out, loss_target=loss_target, m_w_ada=m_w_ada, m_b_ada=m_b_ada, m_g_pre_mix=m_g_pre_mix, m_g_post_mix=m_g_post_mix, m_g_pre_ffn=m_g_pre_ffn, m_g_post_ffn=m_g_post_ffn, m_w_in=m_w_in, m_b_forget=m_b_forget, m_g_q_lora=m_g_q_lora, m_w_uq=m_w_uq, m_g_kv_lora=m_g_kv_lora, m_w_ukv=m_w_ukv, m_w_proj_fox=m_w_proj_fox, m_w_proj_mla=m_w_proj_mla, m_w_out=m_w_out, m_w_ffn_in=m_w_ffn_in, m_w_ffn_out=m_w_ffn_out, v_w_ada=v_w_ada, v_b_ada=v_b_ada, v_g_pre_mix=v_g_pre_mix, v_g_post_mix=v_g_post_mix, v_g_pre_ffn=v_g_pre_ffn, v_g_post_ffn=v_g_post_ffn, v_w_in=v_w_in, v_b_forget=v_b_forget, v_g_q_lora=v_g_q_lora, v_w_uq=v_w_uq, v_g_kv_lora=v_g_kv_lora, v_w_ukv=v_w_ukv, v_w_proj_fox=v_w_proj_fox, v_w_proj_mla=v_w_proj_mla, v_w_out=v_w_out, v_w_ffn_in=v_w_ffn_in, v_w_ffn_out=v_w_ffn_out)
    weights = {n: given[n] for n in TWIN_WEIGHTS}
    shared = {n: given[n] for n in SHARED_INPUTS}
    per_example = {n: given[n] for n in ['x', 'c', 'positions']}
    grad_fn = _jax.value_and_grad(_loss, argnums=(0, 1))

    def one_microbatch(ex, loss_target):
        ex = dict(ex)
        diff = ex.pop(TWIN_DIFF_INPUT)
        return grad_fn(weights, diff, {**shared, **ex}, loss_target)

    if N_MICROBATCH == 1:
        loss, (grad_w, grad_x) = one_microbatch(per_example, given["loss_target"])
    else:
        def body(carry, xs):
            loss_sum, grad_sum = carry
            l_k, (gw_k, gx_k) = one_microbatch(xs[0], xs[1])
            with _jax.named_scope("update"):
                return (loss_sum + l_k, _jax.tree.map(_jnp.add, grad_sum, gw_k)), gx_k

        init = (_jnp.zeros((), _jnp.float32), _jax.tree.map(_jnp.zeros_like, weights))
        (loss, grad_w), grad_x = _jax.lax.scan(body, init, (per_example, given["loss_target"]))
    with _jax.named_scope("update"):
        delta_w, new_m, new_v = {}, {}, {}
        for n in TWIN_WEIGHTS:
            delta_w[n], new_m[n], new_v[n] = _adamw(weights[n], grad_w[n], given["m_" + n], given["v_" + n])
    return (loss, grad_x, *[grad_w[n] for n in TWIN_WEIGHTS], *[delta_w[n] for n in TWIN_WEIGHTS],
            *[new_m[n] for n in TWIN_WEIGHTS], *[new_v[n] for n in TWIN_WEIGHTS])
```

```python
import functools
import math

import jax
import jax.numpy as jnp
from jax import lax
from jax.experimental import pallas as pl
from jax.experimental.pallas import tpu as pltpu

F32 = jnp.float32
BF16 = jnp.bfloat16
_MXU = jnp.bfloat16

S = 2048
D = 1024
H = 8
HP = 128
FOX_HD = 64
MLA_NOPE = 64
MLA_ROPE = 32
MLA_V = 64
Q_LORA = 768
KV_LORA = 256
D_FF = 2816
NORM_EPS = 1e-6
ROPE_THETA = 10000.0
NEG = -1e30

ADAM_LR = 0.001
ADAM_B1 = 0.9
ADAM_B2 = 0.999
ADAM_EPS = 1e-08
ADAM_WD = 0.01
ADAM_STEP = 10

LANES = 128
VMEM_CAP = 60 * 1024 * 1024
MESH = pl.DeviceIdType.MESH

NQKV = 3 * H * HP
OFF_CQ = 0
OFF_CKV = Q_LORA
OFF_GFOX = 1024
OFF_GMLA = 2048
OFF_FLOG = 3072
OFF_KRIN = 3200
NREST = 3328
KRIN_LANE = 64
ROW_SUM_LANE = 64
COL_SUM_LANE = 65


def _limit(nbytes):
    return int(min(VMEM_CAP, nbytes * 1.25 + (4 << 20)))


def _nbytes(shape, dtype):
    n = 1
    for s in shape:
        n *= s
    return n * jnp.dtype(dtype).itemsize


def _pick(n, cap):
    best = None
    for t in range(LANES, min(n, cap) + 1, LANES):
        if n % t == 0:
            best = t
    return best if best is not None else n


def _matmul(a, b, *, ta=False, tb=False, out_dtype=F32, name, tm_cap=1024, tn_cap=512, tk_cap=1024):
    if ta:
        K, M = a.shape
    else:
        M, K = a.shape
    if tb:
        N, K2 = b.shape
    else:
        K2, N = b.shape
    assert K == K2, (a.shape, b.shape, ta, tb)
    tm = _pick(M, tm_cap)
    tn = _pick(N, tn_cap)
    tk = K if K <= tk_cap else _pick(K, tk_cap)
    nk = K // tk
    dims = (((0 if ta else 1,), (1 if tb else 0,)), ((), ()))

    def body(a_ref, b_ref, o_ref, acc_ref):
        k = pl.program_id(2)

        @pl.when(k == 0)
        def _():
            acc_ref[...] = jnp.zeros_like(acc_ref)

        acc_ref[...] += lax.dot_general(a_ref[...].astype(_MXU), b_ref[...].astype(_MXU), dims,
                                        preferred_element_type=F32)

        @pl.when(k == nk - 1)
        def _():
            o_ref[...] = acc_ref[...].astype(out_dtype)

    a_spec = pl.BlockSpec((tk, tm), lambda i, j, k: (k, i)) if ta else pl.BlockSpec((tm, tk), lambda i, j, k: (i, k))
    b_spec = pl.BlockSpec((tn, tk), lambda i, j, k: (j, k)) if tb else pl.BlockSpec((tk, tn), lambda i, j, k: (k, j))
    need = (2 * _nbytes((tm, tk), a.dtype) + 2 * _nbytes((tk, tn), b.dtype) + 2 * _nbytes((tm, tn), out_dtype)
            + _nbytes((tm, tn), F32) * 2 + _nbytes((tm, tk), _MXU) + _nbytes((tk, tn), _MXU))
    return pl.pallas_call(
        body, name=name,
        out_shape=jax.ShapeDtypeStruct((M, N), out_dtype),
        grid=(M // tm, N // tn, nk),
        in_specs=[a_spec, b_spec],
        out_specs=pl.BlockSpec((tm, tn), lambda i, j, k: (i, j)),
        scratch_shapes=[pltpu.VMEM((tm, tn), F32)],
        compiler_params=pltpu.CompilerParams(dimension_semantics=("parallel", "parallel", "arbitrary"),
                                             vmem_limit_bytes=_limit(need)),
    )(a, b)


TM = 256


def _vec(w):
    return pl.BlockSpec((1, w), lambda i: (0, 0))


def _rows(w, col=0):
    return pl.BlockSpec((TM, w), lambda i: (i, col))


def _row_params(need, carried=False):
    return pltpu.CompilerParams(dimension_semantics=("arbitrary" if carried else "parallel",),
                                vmem_limit_bytes=_limit(need))


def _norm_mod(x, col, w, g, ops, sh, *, name):
    def body(x_ref, g_ref, ops_ref, sh_ref, o_ref):
        xv = x_ref[...]
        r = lax.rsqrt(jnp.mean(xv * xv, axis=-1, keepdims=True) + NORM_EPS)
        o_ref[...] = (((xv * r) * g_ref[...]) * ops_ref[...] + sh_ref[...]).astype(o_ref.dtype)

    return pl.pallas_call(
        body, name=name, out_shape=jax.ShapeDtypeStruct((S, w), _MXU), grid=(S // TM,),
        in_specs=[_rows(w, col), _vec(w), _vec(w), _vec(w)], out_specs=_rows(w),
        compiler_params=_row_params(8 * _nbytes((TM, w), F32)),
    )(x, g, ops, sh)


def _norm_mod_bwd(x, col, w, dh, g, ops, dres, *, name):
    has_res = dres is not None

    def body(*refs):
        if has_res:
            x_ref, dh_ref, g_ref, ops_ref, dres_ref, dx_ref, s1_ref, s2_ref = refs
        else:
            x_ref, dh_ref, g_ref, ops_ref, dx_ref, s1_ref, s2_ref = refs
        i = pl.program_id(0)

        @pl.when(i == 0)
        def _():
            s1_ref[...] = jnp.zeros_like(s1_ref)
            s2_ref[...] = jnp.zeros_like(s2_ref)

        xv = x_ref[...]
        dhv = dh_ref[...]
        r = lax.rsqrt(jnp.mean(xv * xv, axis=-1, keepdims=True) + NORM_EPS)
        xn = xv * r
        dxn = dhv * (g_ref[...] * ops_ref[...])
        dx = r * (dxn - xn * jnp.mean(dxn * xn, axis=-1, keepdims=True))
        if has_res:
            dx = dx + dres_ref[...]
        dx_ref[...] = dx
        s1_ref[...] += jnp.sum(dhv, axis=0, keepdims=True)
        s2_ref[...] += jnp.sum(dhv * xn, axis=0, keepdims=True)

    in_specs = [_rows(w, col), _rows(w), _vec(w), _vec(w)] + ([_rows(w)] if has_res else [])
    args = (x, dh, g, ops) + ((dres,) if has_res else ())
    return pl.pallas_call(
        body, name=name,
        out_shape=(jax.ShapeDtypeStruct((S, w), F32), jax.ShapeDtypeStruct((1, w), F32), jax.ShapeDtypeStruct((1, w), F32)),
        grid=(S // TM,), in_specs=in_specs, out_specs=(_rows(w), _vec(w), _vec(w)),
        compiler_params=_row_params(12 * _nbytes((TM, w), F32), carried=True),
    )(*args)


def _post_res(xres, y, g, gt, *, name):
    def body(x_ref, y_ref, g_ref, gt_ref, o_ref):
        yv = y_ref[...]
        r = lax.rsqrt(jnp.mean(yv * yv, axis=-1, keepdims=True) + NORM_EPS)
        o_ref[...] = x_ref[...] + gt_ref[...] * ((yv * r) * g_ref[...])

    return pl.pallas_call(
        body, name=name, out_shape=jax.ShapeDtypeStruct((S, D), F32), grid=(S // TM,),
        in_specs=[_rows(D), _rows(D), _vec(D), _vec(D)], out_specs=_rows(D),
        compiler_params=_row_params(8 * _nbytes((TM, D), F32)),
    )(xres, y, g, gt)


def _post_res_loss(xres, y, g, gt, target, *, name):
    def body(x_ref, y_ref, g_ref, gt_ref, t_ref, dout_ref, loss_ref):
        i = pl.program_id(0)

        @pl.when(i == 0)
        def _():
            loss_ref[...] = jnp.zeros_like(loss_ref)

        yv = y_ref[...]
        r = lax.rsqrt(jnp.mean(yv * yv, axis=-1, keepdims=True) + NORM_EPS)
        out = x_ref[...] + gt_ref[...] * ((yv * r) * g_ref[...])
        err = out - t_ref[...]
        dout_ref[...] = err * (1.0 / D)
        per_row = jnp.mean(err * err, axis=-1, keepdims=True)
        loss_ref[...] += 0.5 * jnp.sum(per_row, axis=0, keepdims=True)

    return pl.pallas_call(
        body, name=name,
        out_shape=(jax.ShapeDtypeStruct((S, D), F32), jax.ShapeDtypeStruct((1, 1), F32)), grid=(S // TM,),
        in_specs=[_rows(D), _rows(D), _vec(D), _vec(D), _rows(D)],
        out_specs=(_rows(D), pl.BlockSpec((1, 1), lambda i: (0, 0))),
        compiler_params=_row_params(10 * _nbytes((TM, D), F32), carried=True),
    )(xres, y, g, gt, target)


def _post_res_bwd(dxn, y, g, gt, *, name):
    def body(d_ref, y_ref, g_ref, gt_ref, dy_ref, sgt_ref, sg_ref):
        i = pl.program_id(0)

        @pl.when(i == 0)
        def _():
            sgt_ref[...] = jnp.zeros_like(sgt_ref)
            sg_ref[...] = jnp.zeros_like(sg_ref)

        yv = y_ref[...]
        dv = d_ref[...]
        r = lax.rsqrt(jnp.mean(yv * yv, axis=-1, keepdims=True) + NORM_EPS)
        yn = yv * r
        dn = dv * gt_ref[...]
        dyn = dn * g_ref[...]
        dy_ref[...] = (r * (dyn - yn * jnp.mean(dyn * yn, axis=-1, keepdims=True))).astype(dy_ref.dtype)
        sgt_ref[...] += jnp.sum(dv * (yn * g_ref[...]), axis=0, keepdims=True)
        sg_ref[...] += jnp.sum(dn * yn, axis=0, keepdims=True)

    return pl.pallas_call(
        body, name=name,
        out_shape=(jax.ShapeDtypeStruct((S, D), _MXU), jax.ShapeDtypeStruct((1, D), F32), jax.ShapeDtypeStruct((1, D), F32)),
        grid=(S // TM,), in_specs=[_rows(D), _rows(D), _vec(D), _vec(D)], out_specs=(_rows(D), _vec(D), _vec(D)),
        compiler_params=_row_params(10 * _nbytes((TM, D), F32), carried=True),
    )(dxn, y, g, gt)


def _swiglu(gu, *, name):
    def body(g_ref, u_ref, o_ref):
        gv = g_ref[...]
        o_ref[...] = ((gv * jax.nn.sigmoid(gv)) * u_ref[...]).astype(o_ref.dtype)

    return pl.pallas_call(
        body, name=name, out_shape=jax.ShapeDtypeStruct((S, D_FF), _MXU), grid=(S // TM,),
        in_specs=[_rows(D_FF, 0), _rows(D_FF, 1)], out_specs=_rows(D_FF),
        compiler_params=_row_params(8 * _nbytes((TM, D_FF), F32)),
    )(gu, gu)


def _swiglu_bwd(gu, dact, *, name):
    def body(g_ref, u_ref, d_ref, o_ref):
        gv = g_ref[...]
        dv = d_ref[...]
        sg = jax.nn.sigmoid(gv)
        o_ref[:, :D_FF] = (dv * u_ref[...] * (sg * (1.0 + gv * (1.0 - sg)))).astype(o_ref.dtype)
        o_ref[:, D_FF:] = (dv * (gv * sg)).astype(o_ref.dtype)

    return pl.pallas_call(
        body, name=name, out_shape=jax.ShapeDtypeStruct((S, 2 * D_FF), _MXU), grid=(S // TM,),
        in_specs=[_rows(D_FF, 0), _rows(D_FF, 1), _rows(D_FF)], out_specs=_rows(2 * D_FF),
        compiler_params=_row_params(12 * _nbytes((TM, D_FF), F32)),
    )(gu, gu, dact)


def _merge(rest, pa, pb, *, name):
    def body(ga_ref, gb_ref, pa_ref, pb_ref, o_ref):
        o_ref[...] = (jax.nn.sigmoid(ga_ref[...]) * pa_ref[...] + jax.nn.sigmoid(gb_ref[...]) * pb_ref[...]).astype(o_ref.dtype)

    return pl.pallas_call(
        body, name=name, out_shape=jax.ShapeDtypeStruct((S, D), _MXU), grid=(S // TM,),
        in_specs=[_rows(D, OFF_GFOX // D), _rows(D, OFF_GMLA // D), _rows(D), _rows(D)], out_specs=_rows(D),
        compiler_params=_row_params(10 * _nbytes((TM, D), F32)),
    )(rest, rest, pa, pb)


def _merge_bwd(rest, pa, pb, dm, *, name):
    def body(ga_ref, gb_ref, pa_ref, pb_ref, d_ref, dpa_ref, dpb_ref, dga_ref, dgb_ref):
        dv = d_ref[...]
        sa = jax.nn.sigmoid(ga_ref[...])
        sb = jax.nn.sigmoid(gb_ref[...])
        dpa_ref[...] = (dv * sa).astype(dpa_ref.dtype)
        dpb_ref[...] = (dv * sb).astype(dpb_ref.dtype)
        dga_ref[...] = (dv * pa_ref[...] * (sa * (1.0 - sa))).astype(dga_ref.dtype)
        dgb_ref[...] = (dv * pb_ref[...] * (sb * (1.0 - sb))).astype(dgb_ref.dtype)

    o = jax.ShapeDtypeStruct((S, D), _MXU)
    return pl.pallas_call(
        body, name=name, out_shape=(o, o, o, o), grid=(S // TM,),
        in_specs=[_rows(D, OFF_GFOX // D), _rows(D, OFF_GMLA // D), _rows(D), _rows(D), _rows(D)],
        out_specs=(_rows(D), _rows(D), _rows(D), _rows(D)),
        compiler_params=_row_params(16 * _nbytes((TM, D), F32)),
    )(rest, rest, pa, pb, dm)


SCAN = 256


def _split_dot(tri, x):
    hi = x.astype(_MXU)
    r1 = x - hi.astype(F32)
    mid = r1.astype(_MXU)
    lo = (r1 - mid.astype(F32)).astype(_MXU)
    dot = functools.partial(jnp.dot, preferred_element_type=F32)
    return dot(tri, hi) + dot(tri, mid) + dot(tri, lo)


def _fox_prep(rest, bf, *, name):
    def body(z_ref, b_ref, f_ref):
        lane = lax.broadcasted_iota(jnp.int32, (SCAN, LANES), 1)
        tri = (lax.broadcasted_iota(jnp.int32, (SCAN, SCAN), 1) <= lax.broadcasted_iota(jnp.int32, (SCAN, SCAN), 0)).astype(_MXU)
        carry = jnp.zeros((1, LANES), F32)
        for c in range(S // SCAN):
            z = z_ref[c * SCAN:(c + 1) * SCAN, :] + b_ref[...]
            lf = jnp.minimum(z, 0.0) - jnp.log(1.0 + jnp.exp(-jnp.abs(z)))
            lf = jnp.where(lane < H, lf, 0.0)
            cum = _split_dot(tri, lf) + carry
            f_ref[c * SCAN:(c + 1) * SCAN, :] = cum
            carry = cum[SCAN - 1:SCAN, :]

    return pl.pallas_call(
        body, name=name, out_shape=jax.ShapeDtypeStruct((S, LANES), F32), grid=(1,),
        in_specs=[pl.BlockSpec((S, LANES), lambda i: (0, OFF_FLOG // LANES)), pl.BlockSpec((1, LANES), lambda i: (0, 0))],
        out_specs=pl.BlockSpec((S, LANES), lambda i: (0, 0)),
        compiler_params=pltpu.CompilerParams(vmem_limit_bytes=_limit(8 * _nbytes((S, LANES), F32))),
    )(rest, bf)


def _fox_bwd_prep(rest, bf, dF, *, name):
    def body(z_ref, b_ref, d_ref, o_ref, db_ref):
        lane = lax.broadcasted_iota(jnp.int32, (SCAN, LANES), 1)
        tri = (lax.broadcasted_iota(jnp.int32, (SCAN, SCAN), 1) >= lax.broadcasted_iota(jnp.int32, (SCAN, SCAN), 0)).astype(_MXU)
        carry = jnp.zeros((1, LANES), F32)
        db = jnp.zeros((1, LANES), F32)
        for c in range(S // SCAN - 1, -1, -1):
            rc = _split_dot(tri, d_ref[c * SCAN:(c + 1) * SCAN, :]) + carry
            z = z_ref[c * SCAN:(c + 1) * SCAN, :] + b_ref[...]
            dz = jnp.where(lane < H, rc * jax.nn.sigmoid(-z), 0.0)
            o_ref[c * SCAN:(c + 1) * SCAN, :] = dz
            db = db + jnp.sum(dz, axis=0, keepdims=True)
            carry = rc[0:1, :]
        db_ref[...] = db

    return pl.pallas_call(
        body, name=name,
        out_shape=(jax.ShapeDtypeStruct((S, LANES), F32), jax.ShapeDtypeStruct((1, LANES), F32)), grid=(1,),
        in_specs=[pl.BlockSpec((S, LANES), lambda i: (0, OFF_FLOG // LANES)), pl.BlockSpec((1, LANES), lambda i: (0, 0)),
                  pl.BlockSpec((S, LANES), lambda i: (0, 0))],
        out_specs=(pl.BlockSpec((S, LANES), lambda i: (0, 0)), pl.BlockSpec((1, LANES), lambda i: (0, 0))),
        compiler_params=pltpu.CompilerParams(vmem_limit_bytes=_limit(10 * _nbytes((S, LANES), F32))),
    )(rest, bf, dF)


def _swap16(x):
    lane = lax.broadcasted_iota(jnp.int32, x.shape, 1)
    half = MLA_ROPE // 2
    sw = jnp.where(lane < KRIN_LANE + half, pltpu.roll(x, LANES - half, 1), pltpu.roll(x, half, 1))
    return jnp.where((lane >= KRIN_LANE) & (lane < KRIN_LANE + MLA_ROPE), sw, 0.0)


def _mla_assemble(qb, kvb, rest, ctab, stab, *, name):
    def body(q_ref, kk_ref, kv_ref, kr_ref, c_ref, s_ref, qo_ref, ko_ref, vo_ref):
        cv = c_ref[...]
        sv = s_ref[...]
        kr = kr_ref[...]
        kpe = kr * cv + _swap16(kr) * sv
        for h in range(H):
            sl = slice(h * HP, (h + 1) * HP)
            qh = q_ref[:, sl]
            qo_ref[:, sl] = (qh * cv + _swap16(qh) * sv).astype(qo_ref.dtype)
            ko_ref[:, sl] = (kk_ref[:, sl] + kpe).astype(ko_ref.dtype)
        vo_ref[...] = kv_ref[...].astype(vo_ref.dtype)

    o = jax.ShapeDtypeStruct((S, H * HP), _MXU)
    return pl.pallas_call(
        body, name=name, out_shape=(o, o, o), grid=(S // TM,),
        in_specs=[_rows(H * HP), _rows(H * HP, 0), _rows(H * HP, 1), _rows(LANES, OFF_KRIN // LANES), _rows(LANES), _rows(LANES)],
        out_specs=(_rows(H * HP), _rows(H * HP), _rows(H * HP)),
        compiler_params=_row_params(14 * _nbytes((TM, H * HP), F32)),
    )(qb, kvb, kvb, rest, ctab, stab)


def _mla_assemble_bwd(dq, dk, dv, ctab, stab, *, name):
    def body(dq_ref, dk_ref, dv_ref, c_ref, s_ref, dqo_ref, dkv_ref, dkr_ref):
        cv = c_ref[...]
        sv = s_ref[...]
        lane = lax.broadcasted_iota(jnp.int32, (TM, LANES), 1)
        dsum = jnp.zeros((TM, LANES), F32)
        for h in range(H):
            sl = slice(h * HP, (h + 1) * HP)
            dqh = dq_ref[:, sl]
            dqo_ref[:, sl] = (dqh * cv + _swap16(dqh * sv)).astype(dqo_ref.dtype)
            dsum = dsum + dk_ref[:, sl]
        dkv_ref[:, :H * HP] = dk_ref[...].astype(dkv_ref.dtype)
        dkv_ref[:, H * HP:] = dv_ref[...].astype(dkv_ref.dtype)
        dkr = dsum * cv + _swap16(dsum * sv)
        dkr_ref[...] = jnp.where((lane >= KRIN_LANE) & (lane < KRIN_LANE + MLA_ROPE), dkr, 0.0)

    return pl.pallas_call(
        body, name=name,
        out_shape=(jax.ShapeDtypeStruct((S, H * HP), _MXU), jax.ShapeDtypeStruct((S, 2 * H * HP), _MXU),
                   jax.ShapeDtypeStruct((S, LANES), F32)),
        grid=(S // TM,),
        in_specs=[_rows(H * HP), _rows(H * HP), _rows(H * HP), _rows(LANES), _rows(LANES)],
        out_specs=(_rows(H * HP), _rows(2 * H * HP), _rows(LANES)),
        compiler_params=_row_params(14 * _nbytes((TM, H * HP), F32)),
    )(dq, dk, dv, ctab, stab)


TQ = 256
TKV = 256
NQ = S // TQ
NKV = S // TKV
_NT = (((1,), (1,)), ((), ()))
_TN = (((0,), (0,)), ((), ()))


def _attn_fwd(q, k, v, fcol, frow, *, scale, name):
    has_decay = fcol is not None

    def body(*refs):
        if has_decay:
            q_ref, k_ref, v_ref, fc_ref, fr_ref, o_ref, lse_ref = refs
        else:
            q_ref, k_ref, v_ref, o_ref, lse_ref = refs
        i = pl.program_id(1)
        qv = q_ref[...]
        rows = i * TQ + lax.broadcasted_iota(jnp.int32, (TQ, TKV), 0)
        col0 = lax.broadcasted_iota(jnp.int32, (TQ, TKV), 1)

        def step(j, carry):
            m, l, acc = carry
            k0 = pl.multiple_of(j * TKV, TKV)
            kj = k_ref[pl.ds(k0, TKV), :]
            vj = v_ref[pl.ds(k0, TKV), :]
            s = lax.dot_general(qv, kj, _NT, preferred_element_type=F32) * scale
            if has_decay:
                s = s + (fc_ref[0] - fr_ref[0, j])
            s = jnp.where(col0 + j * TKV <= rows, s, NEG)
            m_new = jnp.maximum(m, jnp.max(s, axis=-1, keepdims=True))
            alpha = jnp.exp(m - m_new)
            p = jnp.exp(s - m_new)
            l = alpha * l + jnp.sum(p, axis=-1, keepdims=True)
            acc = alpha * acc + jnp.dot(p.astype(_MXU), vj, preferred_element_type=F32)
            return m_new, l, acc

        init = (jnp.full((TQ, 1), NEG, F32), jnp.zeros((TQ, 1), F32), jnp.zeros((TQ, HP), F32))
        m, l, acc = lax.fori_loop(0, i + 1, step, init)
        o_ref[...] = acc / l
        lse_ref[0] = m + jnp.log(l)

    in_specs = [pl.BlockSpec((TQ, HP), lambda h, i: (i, h)), pl.BlockSpec((S, HP), lambda h, i: (0, h)),
                pl.BlockSpec((S, HP), lambda h, i: (0, h))]
    args = (q, k, v)
    if has_decay:
        in_specs += [pl.BlockSpec((1, TQ, 1), lambda h, i: (h, i, 0)), pl.BlockSpec((1, NKV, 1, TKV), lambda h, i: (h, 0, 0, 0))]
        args += (fcol, frow)
    return pl.pallas_call(
        body, name=name,
        out_shape=(jax.ShapeDtypeStruct((S, H * HP), F32), jax.ShapeDtypeStruct((H, S, 1), F32)),
        grid=(H, NQ), in_specs=in_specs,
        out_specs=(pl.BlockSpec((TQ, HP), lambda h, i: (i, h)), pl.BlockSpec((1, TQ, 1), lambda h, i: (h, i, 0))),
        compiler_params=pltpu.CompilerParams(dimension_semantics=("parallel", "parallel"),
                                             vmem_limit_bytes=_limit(6 * _nbytes((S, HP), F32))),
    )(*args)


def _attn_delta(do, o, *, name):
    def body(do_ref, o_ref, dl_ref, dob_ref):
        for h in range(H):
            sl = slice(h * HP, (h + 1) * HP)
            dl_ref[h] = jnp.sum(do_ref[:, sl] * o_ref[:, sl], axis=-1, keepdims=True)
        dob_ref[...] = do_ref[...].astype(dob_ref.dtype)

    return pl.pallas_call(
        body, name=name,
        out_shape=(jax.ShapeDtypeStruct((H, S, 1), F32), jax.ShapeDtypeStruct((S, H * HP), _MXU)), grid=(S // TM,),
        in_specs=[_rows(H * HP), _rows(H * HP)],
        out_specs=(pl.BlockSpec((H, TM, 1), lambda i: (0, i, 0)), _rows(H * HP)),
        compiler_params=_row_params(8 * _nbytes((TM, H * HP), F32)),
    )(do, o)


def _attn_bwd(q, k, v, dob, lse, delta, fcol, frow, *, scale, name):
    has_decay = fcol is not None

    def body(*refs):
        if has_decay:
            q_ref, k_ref, v_ref, do_ref, lse_ref, dl_ref, fc_ref, fr_ref, dq_ref, dk_ref, dv_ref, dq_acc = refs
        else:
            q_ref, k_ref, v_ref, do_ref, lse_ref, dl_ref, dq_ref, dk_ref, dv_ref, dq_acc = refs
        j = pl.program_id(1)

        @pl.when(j == 0)
        def _():
            dq_acc[...] = jnp.zeros_like(dq_acc)

        kj = k_ref[...]
        vj = v_ref[...]
        lane = lax.broadcasted_iota(jnp.int32, (TKV, HP), 1)
        kj_aug = jnp.where(lane == ROW_SUM_LANE, 1.0, kj).astype(kj.dtype) if has_decay else kj
        cols = j * TKV + lax.broadcasted_iota(jnp.int32, (TQ, TKV), 1)
        row0 = lax.broadcasted_iota(jnp.int32, (TQ, TKV), 0)

        def step(i, carry):
            dk, dv = carry
            r0 = pl.multiple_of(i * TQ, TQ)
            qi = q_ref[pl.ds(r0, TQ), :]
            doi = do_ref[pl.ds(r0, TQ), :]
            s = lax.dot_general(qi, kj, _NT, preferred_element_type=F32) * scale
            if has_decay:
                s = s + (fc_ref[0, pl.ds(r0, TQ), :] - fr_ref[0, 0])
            s = jnp.where(cols <= row0 + i * TQ, s, NEG)
            p = jnp.exp(s - lse_ref[0, pl.ds(r0, TQ), :])
            dv = dv + lax.dot_general(p.astype(_MXU), doi, _TN, preferred_element_type=F32)
            dp = lax.dot_general(doi, vj, _NT, preferred_element_type=F32)
            ds = (p * (dp - dl_ref[0, pl.ds(r0, TQ), :])).astype(_MXU)
            qi_aug = jnp.where(lane == COL_SUM_LANE, 1.0, qi).astype(qi.dtype) if has_decay else qi
            dk = dk + lax.dot_general(ds, qi_aug, _TN, preferred_element_type=F32)
            dq_acc[pl.ds(r0, TQ), :] += jnp.dot(ds, kj_aug, preferred_element_type=F32)
            return dk, dv

        zero = jnp.zeros((TKV, HP), F32)
        dk, dv = lax.fori_loop(j, NQ, step, (zero, zero))
        dk_ref[...] = dk * scale
        dv_ref[...] = dv

        @pl.when(j == NKV - 1)
        def _():
            dq_ref[...] = dq_acc[...] * scale

    head = pl.BlockSpec((S, HP), lambda h, j: (0, h))
    kv = pl.BlockSpec((TKV, HP), lambda h, j: (j, h))
    stat = pl.BlockSpec((1, S, 1), lambda h, j: (h, 0, 0))
    in_specs = [head, kv, kv, head, stat, stat]
    args = (q, k, v, dob, lse, delta)
    if has_decay:
        in_specs += [stat, pl.BlockSpec((1, 1, 1, TKV), lambda h, j: (h, j, 0, 0))]
        args += (fcol, frow)
    o = jax.ShapeDtypeStruct((S, H * HP), F32)
    return pl.pallas_call(
        body, name=name, out_shape=(o, o, o), grid=(H, NKV), in_specs=in_specs,
        out_specs=(head, kv, kv), scratch_shapes=[pltpu.VMEM((S, HP), F32)],
        compiler_params=pltpu.CompilerParams(dimension_semantics=("parallel", "arbitrary"),
                                             vmem_limit_bytes=_limit(12 * _nbytes((S, HP), F32))),
    )(*args)


def _ada_mod(c_all, w_shard, b_shard, *, name):
    R = c_all.shape[0]
    N = w_shard.shape[1]
    tn = 512

    def body(c_ref, w_ref, b_ref, o_ref, sc_ref):
        cv = c_ref[...]
        sc = (cv * jax.nn.sigmoid(cv)).astype(_MXU)
        sc_ref[...] = sc
        o_ref[...] = jnp.dot(sc, w_ref[...].astype(_MXU), preferred_element_type=F32) + b_ref[...]

    return pl.pallas_call(
        body, name=name,
        out_shape=(jax.ShapeDtypeStruct((R, N), F32), jax.ShapeDtypeStruct((R, D), _MXU)), grid=(N // tn,),
        in_specs=[pl.BlockSpec((R, D), lambda j: (0, 0)), pl.BlockSpec((D, tn), lambda j: (0, j)), pl.BlockSpec((1, tn), lambda j: (0, j))],
        out_specs=(pl.BlockSpec((R, tn), lambda j: (0, j)), pl.BlockSpec((R, D), lambda j: (0, 0))),
        compiler_params=pltpu.CompilerParams(dimension_semantics=("arbitrary",), vmem_limit_bytes=_limit(6 * _nbytes((D, tn), F32))),
    )(c_all, w_shard, b_shard)


def _rowsum(x, *, name):
    R, L = x.shape

    def body(x_ref, o_ref):
        acc = x_ref[0:1, :]
        for r in range(1, R):
            acc = acc + x_ref[r:r + 1, :]
        o_ref[...] = acc

    return pl.pallas_call(body, name=name, out_shape=jax.ShapeDtypeStruct((1, L), F32),
                          in_specs=[pl.BlockSpec(memory_space=pltpu.VMEM)], out_specs=pl.BlockSpec(memory_space=pltpu.VMEM))(x)


def _adamw(w, g, m, v, *, name):
    R, C = w.shape
    tr = R
    for t in range(8, R + 1, 8):
        if R % t == 0 and t * C * 4 <= (1 << 20):
            tr = t

    def body(w_ref, g_ref, m_ref, v_ref, d_ref, mo_ref, vo_ref):
        gv = g_ref[...]
        m2 = ADAM_B1 * m_ref[...] + (1.0 - ADAM_B1) * gv
        v2 = ADAM_B2 * v_ref[...] + (1.0 - ADAM_B2) * (gv * gv)
        m_hat = m2 / (1.0 - ADAM_B1 ** ADAM_STEP)
        v_hat = v2 / (1.0 - ADAM_B2 ** ADAM_STEP)
        d_ref[...] = -ADAM_LR * (m_hat / (jnp.sqrt(v_hat) + ADAM_EPS) + ADAM_WD * w_ref[...])
        mo_ref[...] = m2
        vo_ref[...] = v2

    blk = pl.BlockSpec((tr, C), lambda i: (i, 0))
    o = jax.ShapeDtypeStruct((R, C), F32)
    return pl.pallas_call(
        body, name=name, out_shape=(o, o, o), grid=(R // tr,), in_specs=[blk, blk, blk, blk], out_specs=(blk, blk, blk),
        compiler_params=pltpu.CompilerParams(dimension_semantics=("parallel",), vmem_limit_bytes=_limit(20 * _nbytes((tr, C), F32))),
    )(w, g, m, v)


def _place():
    x, y, c = lax.axis_index("x"), lax.axis_index("y"), lax.axis_index("c")
    return x, y, c, [(1 - x, y), (x, 1 - y), (1 - x, 1 - y)]


def _two_level_gather(x_ref, out_ref, send_sems, recv_sems, local_sem):
    x, y, c, chips = _place()
    me, sibling = (x, y, c), (x, y, 1 - c)

    def blk(px, py, pc):
        return out_ref.at[4 * px + 2 * py + pc]

    def copy(k, block, to, src=None):
        return pltpu.make_async_remote_copy(
            src_ref=blk(*block) if src is None else src, dst_ref=blk(*block),
            send_sem=send_sems.at[k], recv_sem=recv_sems.at[k], device_id=to, device_id_type=MESH)

    mine = pltpu.make_async_copy(x_ref, blk(*me), local_sem)
    mine.start()
    first = [copy(0, me, sibling, src=x_ref)]
    first += [copy(1 + j, me, (*chip, c), src=x_ref) for j, chip in enumerate(chips)]
    for cp in first:
        cp.start()
    passed = [copy(4 + j, (*chip, c), sibling) for j, chip in enumerate(chips)]
    for j, chip in enumerate(chips):
        copy(1 + j, (*chip, c), me).wait_recv()
        passed[j].start()
    copy(0, sibling, me).wait_recv()
    for j, chip in enumerate(chips):
        copy(4 + j, (*chip, 1 - c), me).wait_recv()
    for cp in first + passed:
        cp.wait_send()
    mine.wait()


_GATHER_SEMS = [pltpu.SemaphoreType.DMA((7,)), pltpu.SemaphoreType.DMA((7,)), pltpu.SemaphoreType.DMA]


def _all_gather_rows(x, *, name):
    R, C = x.shape

    def body(x_ref, out_ref, send_sems, recv_sems, local_sem):
        _two_level_gather(x_ref, out_ref, send_sems, recv_sems, local_sem)

    return pl.pallas_call(
        body, name=name, out_shape=jax.ShapeDtypeStruct((8, R, C), x.dtype),
        in_specs=[pl.BlockSpec(memory_space=pltpu.VMEM)], out_specs=pl.BlockSpec(memory_space=pltpu.VMEM),
        scratch_shapes=list(_GATHER_SEMS),
        compiler_params=pltpu.CompilerParams(vmem_limit_bytes=_limit(10 * _nbytes((R, C), x.dtype))),
    )(x)


CAST_ROWS = 16


def _gather_weight(w, *, name):
    r, c = w.shape
    rh = r // 2
    assert rh % CAST_ROWS == 0

    def body(w_hbm, out_ref, tmp, xb, send_sems, recv_sems, local_sem):
        core = lax.axis_index("c")
        ld = pltpu.make_async_copy(w_hbm.at[pl.ds(pl.multiple_of(core * rh, CAST_ROWS), rh), :], tmp, local_sem)
        ld.start()
        ld.wait()

        def cast(i, carry):
            r0 = pl.multiple_of(i * CAST_ROWS, CAST_ROWS)
            xb[pl.ds(r0, CAST_ROWS), :] = tmp[pl.ds(r0, CAST_ROWS), :].astype(BF16)
            return carry

        lax.fori_loop(0, rh // CAST_ROWS, cast, 0)
        _two_level_gather(xb, out_ref, send_sems, recv_sems, local_sem)

    need = _nbytes((8, rh, c), BF16) + _nbytes((rh, c), F32) + _nbytes((rh, c), BF16)
    out = pl.pallas_call(
        body, name=name, out_shape=jax.ShapeDtypeStruct((8, rh, c), BF16),
        in_specs=[pl.BlockSpec(memory_space=pl.ANY)], out_specs=pl.BlockSpec(memory_space=pltpu.VMEM),
        scratch_shapes=[pltpu.VMEM((rh, c), F32), pltpu.VMEM((rh, c), BF16)] + list(_GATHER_SEMS),
        compiler_params=pltpu.CompilerParams(vmem_limit_bytes=_limit(need * 1.3)),
    )(w)
    return out.reshape(4, r, c)


def _reduce_scatter_weight(g4, *, name):
    _, r, c = g4.shape
    rh = r // 2
    assert rh % CAST_ROWS == 0
    nsteps = rh // CAST_ROWS

    def body(g_hbm, out_ref, mine, tmp, sbuf, rbuf_a, rbuf_b, a_send, a_recv, b_send, b_recv, c_send, c_recv, lsem):
        x, y, core, chips = _place()
        sibling = (x, y, 1 - core)
        k = 2 * x + y
        my0 = pl.multiple_of(core * rh, CAST_ROWS)
        ot0 = pl.multiple_of((1 - core) * rh, CAST_ROWS)

        ld = pltpu.make_async_copy(g_hbm.at[:, pl.ds(my0, rh), :], mine, lsem)
        ld.start()
        ld.wait()
        for j in range(4):
            ldj = pltpu.make_async_copy(g_hbm.at[j, pl.ds(ot0, rh), :], tmp, lsem)
            ldj.start()
            ldj.wait()

            def cast(i, carry, j=j):
                r0 = pl.multiple_of(i * CAST_ROWS, CAST_ROWS)
                sbuf[j, pl.ds(r0, CAST_ROWS), :] = tmp[pl.ds(r0, CAST_ROWS), :].astype(BF16)
                return carry

            lax.fori_loop(0, nsteps, cast, 0)

        to_sib = pltpu.make_async_remote_copy(src_ref=sbuf, dst_ref=rbuf_a, send_sem=a_send, recv_sem=a_recv,
                                              device_id=sibling, device_id_type=MESH)
        to_sib.start()
        to_sib.wait()

        for j in range(4):
            def add(i, carry, j=j):
                r0 = pl.multiple_of(i * CAST_ROWS, CAST_ROWS)
                s = mine[j, pl.ds(r0, CAST_ROWS), :] + rbuf_a[j, pl.ds(r0, CAST_ROWS), :].astype(F32)
                mine[j, pl.ds(r0, CAST_ROWS), :] = s
                sbuf[j, pl.ds(r0, CAST_ROWS), :] = s.astype(BF16)
                return carry

            lax.fori_loop(0, nsteps, add, 0)

        sends = []
        for d, (px, py) in enumerate(chips):
            cp = pltpu.make_async_remote_copy(src_ref=sbuf.at[2 * px + py], dst_ref=rbuf_b.at[d], send_sem=b_send.at[d],
                                              recv_sem=b_recv.at[d], device_id=(px, py, core), device_id_type=MESH)
            cp.start()
            sends.append(cp)
        for cp in sends:
            cp.wait()

        def fin(i, carry):
            r0 = pl.multiple_of(i * CAST_ROWS, CAST_ROWS)
            s = mine[k, pl.ds(r0, CAST_ROWS), :]
            for d in range(3):
                s = s + rbuf_b[d, pl.ds(r0, CAST_ROWS), :].astype(F32)
            out_ref[pl.ds(my0 + r0, CAST_ROWS), :] = s
            return carry

        lax.fori_loop(0, nsteps, fin, 0)
        half = out_ref.at[pl.ds(my0, rh), :]
        swap = pltpu.make_async_remote_copy(src_ref=half, dst_ref=half, send_sem=c_send, recv_sem=c_recv,
                                            device_id=sibling, device_id_type=MESH)
        swap.start()
        swap.wait()

    need = (_nbytes((4, rh, c), F32) + _nbytes((rh, c), F32) + 2 * _nbytes((4, rh, c), BF16) + _nbytes((3, rh, c), BF16)
            + _nbytes((r, c), F32))
    return pl.pallas_call(
        body, name=name, out_shape=jax.ShapeDtypeStruct((r, c), F32),
        in_specs=[pl.BlockSpec(memory_space=pl.ANY)], out_specs=pl.BlockSpec(memory_space=pltpu.VMEM),
        scratch_shapes=[pltpu.VMEM((4, rh, c), F32), pltpu.VMEM((rh, c), F32), pltpu.VMEM((4, rh, c), BF16),
                        pltpu.VMEM((4, rh, c), BF16), pltpu.VMEM((3, rh, c), BF16),
                        pltpu.SemaphoreType.DMA, pltpu.SemaphoreType.DMA, pltpu.SemaphoreType.DMA((3,)),
                        pltpu.SemaphoreType.DMA((3,)), pltpu.SemaphoreType.DMA, pltpu.SemaphoreType.DMA,
                        pltpu.SemaphoreType.DMA],
        compiler_params=pltpu.CompilerParams(vmem_limit_bytes=_limit(need * 1.2)),
    )(g4)


def _cols_from_shards(g):
    n, K, c = g.shape
    return g.transpose(1, 0, 2).reshape(K, n * c)


def _cols_to_shards(w):
    K, N = w.shape
    return w.reshape(K, 4, N // 4).transpose(1, 0, 2)


def _pad_heads_cols(w, width, lane0=0):
    K = w.shape[0]
    w3 = w.reshape(K, H, width)
    return jnp.pad(w3, ((0, 0), (0, 0), (lane0, HP - lane0 - width))).reshape(K, H * HP)


def _unpad_heads_cols(w, width, lane0=0):
    K = w.shape[0]
    return w.reshape(K, H, HP)[:, :, lane0:lane0 + width].reshape(K, H * width)


def _pad_block(w, lane0=0):
    return jnp.pad(w, ((0, 0), (lane0, LANES - lane0 - w.shape[1])))


_IN_SPLITS = [512, 1024, 1536, 1544, 2312, 2568, 2600, 3624]


def _pad_w_in(w):
    fq, fk, fv, flog, cq, ckv, krin, gfox, gmla = jnp.split(w, _IN_SPLITS, axis=1)
    return jnp.concatenate([_pad_heads_cols(fq, FOX_HD), _pad_heads_cols(fk, FOX_HD), _pad_heads_cols(fv, FOX_HD),
                            cq, ckv, gfox, gmla, _pad_block(flog), _pad_block(krin, KRIN_LANE)], axis=1)


def _unpad_w_in(wp):
    qkv, rest = wp[:, :NQKV], wp[:, NQKV:]
    fq, fk, fv = (_unpad_heads_cols(qkv[:, i * H * HP:(i + 1) * H * HP], FOX_HD) for i in range(3))
    return jnp.concatenate([fq, fk, fv, rest[:, OFF_FLOG:OFF_FLOG + H], rest[:, OFF_CQ:OFF_CQ + Q_LORA],
                            rest[:, OFF_CKV:OFF_CKV + KV_LORA], rest[:, OFF_KRIN + KRIN_LANE:OFF_KRIN + KRIN_LANE + MLA_ROPE],
                            rest[:, OFF_GFOX:OFF_GFOX + D], rest[:, OFF_GMLA:OFF_GMLA + D]], axis=1)


def _pad_w_ukv(w):
    w3 = w.reshape(KV_LORA, H, MLA_NOPE + MLA_V)
    kp = jnp.pad(w3[:, :, :MLA_NOPE], ((0, 0), (0, 0), (0, HP - MLA_NOPE))).reshape(KV_LORA, H * HP)
    vp = jnp.pad(w3[:, :, MLA_NOPE:], ((0, 0), (0, 0), (0, HP - MLA_V))).reshape(KV_LORA, H * HP)
    return jnp.concatenate([kp, vp], axis=1)


def _unpad_w_ukv(wp):
    kp = wp[:, :H * HP].reshape(KV_LORA, H, HP)[:, :, :MLA_NOPE]
    vp = wp[:, H * HP:].reshape(KV_LORA, H, HP)[:, :, :MLA_V]
    return jnp.concatenate([kp, vp], axis=2).reshape(KV_LORA, H * (MLA_NOPE + MLA_V))


def _pad_heads_rows(w, width):
    N = w.shape[1]
    return jnp.pad(w.reshape(H, width, N), ((0, 0), (0, HP - width), (0, 0))).reshape(H * HP, N)


def _unpad_heads_rows(w, width):
    N = w.shape[1]
    return w.reshape(H, HP, N)[:, :width, :].reshape(H * width, N)


def _rope_tables(positions):
    inv_freq = 1.0 / (ROPE_THETA ** (jnp.arange(0, MLA_ROPE, 2, dtype=F32) / MLA_ROPE))
    ang = positions.reshape(S, 1).astype(F32) * inv_freq
    cos, sin = jnp.cos(ang), jnp.sin(ang)
    ones = jnp.ones((S, KRIN_LANE), F32)
    tail = jnp.zeros((S, LANES - KRIN_LANE - MLA_ROPE), F32)
    ctab = jnp.concatenate([ones, cos, cos, tail], axis=1)
    stab = jnp.concatenate([0.0 * ones, -sin, sin, tail], axis=1)
    return ctab, stab


def _local_step(x, target, mod, positions, gains, bf, W):
    sh1, sc1, gt1, sh2, sc2, gt2 = (mod[:, i * D:(i + 1) * D] for i in range(6))
    ops1, ops2 = 1.0 + sc1, 1.0 + sc2
    ones = lambda w: jnp.ones((1, w), F32)
    zeros = lambda w: jnp.zeros((1, w), F32)
    bf_blk = _pad_block(bf)
    ctab, stab = _rope_tables(positions)
    fox_scale = 1.0 / math.sqrt(FOX_HD)
    mla_scale = 1.0 / math.sqrt(MLA_NOPE + MLA_ROPE)

    h1 = _norm_mod(x, 0, D, gains["g_pre_mix"], ops1, sh1, name="f_pre_mix")
    qkv = _matmul(h1, W["w_in_qkv"], out_dtype=_MXU, name="f_proj_qkv")
    rest = _matmul(h1, W["w_in_rest"], name="f_proj_rest", tn_cap=256)
    F = _fox_prep(rest, bf_blk, name="f_fox_prep")
    Ft = F[:, :H].T
    fcol, frow = Ft.reshape(H, S, 1), Ft.reshape(H, NKV, 1, TKV)
    qa, ka, va = qkv[:, :H * HP], qkv[:, H * HP:2 * H * HP], qkv[:, 2 * H * HP:]
    oa, lse_a = _attn_fwd(qa, ka, va, fcol, frow, scale=fox_scale, name="f_attn_fox")

    cqn = _norm_mod(rest, OFF_CQ // Q_LORA, Q_LORA, gains["g_q_lora"], ones(Q_LORA), zeros(Q_LORA), name="f_norm_cq")
    ckvn = _norm_mod(rest, OFF_CKV // KV_LORA, KV_LORA, gains["g_kv_lora"], ones(KV_LORA), zeros(KV_LORA), name="f_norm_ckv")
    qb = _matmul(cqn, W["w_uq"], name="f_uq")
    kvb = _matmul(ckvn, W["w_ukv"], name="f_ukv")
    qm, km, vm = _mla_assemble(qb, kvb, rest, ctab, stab, name="f_mla_assemble")
    ob, lse_b = _attn_fwd(qm, km, vm, None, None, scale=mla_scale, name="f_attn_mla")

    pa = _matmul(oa, W["w_proj_fox"], name="f_proj_fox")
    pb = _matmul(ob, W["w_proj_mla"], name="f_proj_mla")
    merged = _merge(rest, pa, pb, name="f_merge")
    y1 = _matmul(merged, W["w_out"], name="f_out")
    x2 = _post_res(x, y1, gains["g_post_mix"], gt1, name="f_post_mix")
    h2 = _norm_mod(x2, 0, D, gains["g_pre_ffn"], ops2, sh2, name="f_pre_ffn")
    gu = _matmul(h2, W["w_ffn_in"], name="f_ffn_in")
    act = _swiglu(gu, name="f_swiglu")
    y2 = _matmul(act, W["w_ffn_out"], name="f_ffn_out", tk_cap=1408)
    dout, loss = _post_res_loss(x2, y2, gains["g_post_ffn"], gt2, target, name="f_post_ffn_loss")

    dy2, s_gt2, s_gpost2 = _post_res_bwd(dout, y2, gains["g_post_ffn"], gt2, name="b_post_ffn")
    dact = _matmul(dy2, W["w_ffn_out"], tb=True, name="b_ffn_out_dx", tn_cap=1408)
    dW_ffn_out = _matmul(act, dy2, ta=True, name="b_ffn_out_dw", tm_cap=1408)
    dgu = _swiglu_bwd(gu, dact, name="b_swiglu")
    dh2 = _matmul(dgu, W["w_ffn_in"], tb=True, name="b_ffn_in_dx", tk_cap=1408)
    dW_ffn_in = _matmul(h2, dgu, ta=True, name="b_ffn_in_dw")
    dx2, s_sh2, s_a2 = _norm_mod_bwd(x2, 0, D, dh2, gains["g_pre_ffn"], ops2, dout, name="b_pre_ffn")
    dy1, s_gt1, s_gpost1 = _post_res_bwd(dx2, y1, gains["g_post_mix"], gt1, name="b_post_mix")
    dmerged = _matmul(dy1, W["w_out"], tb=True, name="b_out_dx")
    dW_out = _matmul(merged, dy1, ta=True, name="b_out_dw")
    dpa, dpb, dgfox, dgmla = _merge_bwd(rest, pa, pb, dmerged, name="b_merge")
    doa = _matmul(dpa, W["w_proj_fox"], tb=True, name="b_proj_fox_dx")
    dW_proj_fox = _matmul(oa, dpa, ta=True, name="b_proj_fox_dw")
    dob = _matmul(dpb, W["w_proj_mla"], tb=True, name="b_proj_mla_dx")
    dW_proj_mla = _matmul(ob, dpb, ta=True, name="b_proj_mla_dw")

    delta_a, doa16 = _attn_delta(doa, oa, name="b_delta_fox")
    dqa, dka, dva = _attn_bwd(qa, ka, va, doa16, lse_a, delta_a, fcol, frow, scale=fox_scale, name="b_attn_fox")
    delta_b, dob16 = _attn_delta(dob, ob, name="b_delta_mla")
    dqm, dkm, dvm = _attn_bwd(qm, km, vm, dob16, lse_b, delta_b, None, None, scale=mla_scale, name="b_attn_mla")

    dF = (dqa[:, ROW_SUM_LANE::HP] - dka[:, COL_SUM_LANE::HP]) * (1.0 / fox_scale)
    dflog, s_bf = _fox_bwd_prep(rest, bf_blk, _pad_block(dF), name="b_fox_prep")

    dqb, dkvb, dkrin = _mla_assemble_bwd(dqm, dkm, dvm, ctab, stab, name="b_mla_assemble")
    dcqn = _matmul(dqb, W["w_uq"], tb=True, name="b_uq_dx")
    dW_uq = _matmul(cqn, dqb, ta=True, name="b_uq_dw")
    dckvn = _matmul(dkvb, W["w_ukv"], tb=True, name="b_ukv_dx")
    dW_ukv = _matmul(ckvn, dkvb, ta=True, name="b_ukv_dw")
    dcq, _, s_gq = _norm_mod_bwd(rest, OFF_CQ // Q_LORA, Q_LORA, dcqn, gains["g_q_lora"], ones(Q_LORA), None, name="b_norm_cq")
    dckv, _, s_gkv = _norm_mod_bwd(rest, OFF_CKV // KV_LORA, KV_LORA, dckvn, gains["g_kv_lora"], ones(KV_LORA), None, name="b_norm_ckv")

    c16 = lambda a: a.astype(_MXU)
    dproj = jnp.concatenate([c16(dqa), c16(dka), c16(dva), c16(dcq), c16(dckv), dgfox, dgmla, c16(dflog), c16(dkrin)], axis=1)
    w_in_full = jnp.concatenate([W["w_in_qkv"], W["w_in_rest"]], axis=1)
    dh1 = _matmul(dproj, w_in_full, tb=True, name="b_in_dx", tk_cap=1280)
    dW_in = _matmul(h1, dproj, ta=True, name="b_in_dw", tn_cap=640)
    grad_x, s_sh1, s_a1 = _norm_mod_bwd(x, 0, D, dh1, gains["g_pre_mix"], ops1, dx2, name="b_pre_mix")

    dmod = jnp.concatenate([s_sh1, s_a1 * gains["g_pre_mix"], s_gt1, s_sh2, s_a2 * gains["g_pre_ffn"], s_gt2], axis=1)
    small = dict(dmod=dmod, g_pre_mix=s_a1 * ops1, g_post_mix=s_gpost1, g_pre_ffn=s_a2 * ops2, g_post_ffn=s_gpost2,
                 g_q_lora=s_gq, g_kv_lora=s_gkv, b_forget=s_bf)
    dW = dict(w_in=dW_in, w_uq=dW_uq, w_ukv=dW_ukv, w_proj_fox=dW_proj_fox, w_proj_mla=dW_proj_mla, w_out=dW_out,
              w_ffn_in=dW_ffn_in, w_ffn_out=dW_ffn_out)
    return loss, grad_x, dW, small


_BIG = ["w_in", "w_uq", "w_ukv", "w_proj_fox", "w_proj_mla", "w_out", "w_ffn_in", "w_ffn_out"]
_COL_SHARDED = {"w_in", "w_ukv", "w_proj_fox", "w_proj_mla", "w_ffn_in"}
_SMALL = ["b_ada", "g_pre_mix", "g_post_mix", "g_pre_ffn", "g_post_ffn", "b_forget", "g_q_lora", "g_kv_lora"]
_ORDER = ["w_ada", "b_ada", "g_pre_mix", "g_post_mix", "g_pre_ffn", "g_post_ffn", "w_in", "b_forget", "g_q_lora", "w_uq",
          "g_kv_lora", "w_ukv", "w_proj_fox", "w_proj_mla", "w_out", "w_ffn_in", "w_ffn_out"]
_ROW = {}
_off = 0
for _n, _w in [("dmod", 6 * D), ("g_pre_mix", D), ("g_post_mix", D), ("g_pre_ffn", D), ("g_post_ffn", D), ("g_q_lora", Q_LORA),
               ("g_kv_lora", KV_LORA), ("b_forget", LANES), ("loss", LANES)]:
    _ROW[_n] = (_off, _w)
    _off += _w
_ROW_LEN = _off


def _full_weights(G):
    full = {n: (_cols_from_shards(G[n]) if n in _COL_SHARDED else G[n].reshape(-1, G[n].shape[2])) for n in _BIG}
    w_in = _pad_w_in(full["w_in"])
    return dict(
        w_in_qkv=w_in[:, :NQKV], w_in_rest=w_in[:, NQKV:],
        w_uq=_pad_heads_cols(full["w_uq"], MLA_NOPE + MLA_ROPE),
        w_ukv=_pad_w_ukv(full["w_ukv"]),
        w_proj_fox=_pad_heads_rows(full["w_proj_fox"], FOX_HD),
        w_proj_mla=_pad_heads_rows(full["w_proj_mla"], MLA_V),
        w_out=full["w_out"], w_ffn_in=full["w_ffn_in"], w_ffn_out=full["w_ffn_out"])


def _grad_shards(dW):
    nat = dict(
        w_in=_unpad_w_in(dW["w_in"]), w_uq=_unpad_heads_cols(dW["w_uq"], MLA_NOPE + MLA_ROPE), w_ukv=_unpad_w_ukv(dW["w_ukv"]),
        w_proj_fox=_unpad_heads_rows(dW["w_proj_fox"], FOX_HD), w_proj_mla=_unpad_heads_rows(dW["w_proj_mla"], MLA_V),
        w_out=dW["w_out"], w_ffn_in=dW["w_ffn_in"], w_ffn_out=dW["w_ffn_out"])
    return {n: (_cols_to_shards(g) if n in _COL_SHARDED else g.reshape(4, g.shape[0] // 4, g.shape[1])) for n, g in nat.items()}


def kernel(x, c, positions, w_ada, b_ada, g_pre_mix, g_post_mix, g_pre_ffn, g_post_ffn, w_in, b_forget, g_q_lora, w_uq, g_kv_lora, w_ukv, w_proj_fox, w_proj_mla, w_out, w_ffn_in, w_ffn_out, loss_target, m_w_ada, m_b_ada, m_g_pre_mix, m_g_post_mix, m_g_pre_ffn, m_g_post_ffn, m_w_in, m_b_forget, m_g_q_lora, m_w_uq, m_g_kv_lora, m_w_ukv, m_w_proj_fox, m_w_proj_mla, m_w_out, m_w_ffn_in, m_w_ffn_out, v_w_ada, v_b_ada, v_g_pre_mix, v_g_post_mix, v_g_pre_ffn, v_g_post_ffn, v_w_in, v_b_forget, v_g_q_lora, v_w_uq, v_g_kv_lora, v_w_ukv, v_w_proj_fox, v_w_proj_mla, v_w_out, v_w_ffn_in, v_w_ffn_out):
    P = dict(w_ada=w_ada, b_ada=b_ada, g_pre_mix=g_pre_mix, g_post_mix=g_post_mix, g_pre_ffn=g_pre_ffn, g_post_ffn=g_post_ffn,
             w_in=w_in, b_forget=b_forget, g_q_lora=g_q_lora, w_uq=w_uq, g_kv_lora=g_kv_lora, w_ukv=w_ukv,
             w_proj_fox=w_proj_fox, w_proj_mla=w_proj_mla, w_out=w_out, w_ffn_in=w_ffn_in, w_ffn_out=w_ffn_out)
    M = dict(w_ada=m_w_ada, b_ada=m_b_ada, g_pre_mix=m_g_pre_mix, g_post_mix=m_g_post_mix, g_pre_ffn=m_g_pre_ffn,
             g_post_ffn=m_g_post_ffn, w_in=m_w_in, b_forget=m_b_forget, g_q_lora=m_g_q_lora, w_uq=m_w_uq, g_kv_lora=m_g_kv_lora,
             w_ukv=m_w_ukv, w_proj_fox=m_w_proj_fox, w_proj_mla=m_w_proj_mla, w_out=m_w_out, w_ffn_in=m_w_ffn_in,
             w_ffn_out=m_w_ffn_out)
    V = dict(w_ada=v_w_ada, b_ada=v_b_ada, g_pre_mix=v_g_pre_mix, g_post_mix=v_g_post_mix, g_pre_ffn=v_g_pre_ffn,
             g_post_ffn=v_g_post_ffn, w_in=v_w_in, b_forget=v_b_forget, g_q_lora=v_g_q_lora, w_uq=v_w_uq, g_kv_lora=v_g_kv_lora,
             w_ukv=v_w_ukv, w_proj_fox=v_w_proj_fox, w_proj_mla=v_w_proj_mla, w_out=v_w_out, w_ffn_in=v_w_ffn_in,
             w_ffn_out=v_w_ffn_out)
    ax, ay, ac = lax.axis_index("x"), lax.axis_index("y"), lax.axis_index("c")
    chip = 2 * ax + ay
    me = 4 * ax + 2 * ay + ac
    n_ada = w_ada.shape[2]

    c_all = _all_gather_rows(jnp.pad(c, ((0, 7), (0, 0))), name="gather_c")[:, 0, :]
    c_all = jnp.pad(c_all, ((0, 8), (0, 0)))
    b_shard = lax.dynamic_slice(b_ada, (0, chip * n_ada), (1, n_ada))
    mod_blk, silu_c = _ada_mod(c_all, w_ada[0], b_shard, name="ada_mod")
    mod_all = _all_gather_rows(mod_blk, name="gather_mod")
    mod_mine = lax.dynamic_index_in_dim(mod_all, me, axis=1, keepdims=False)
    mod = lax.dynamic_index_in_dim(mod_mine.reshape(4, 2, n_ada), ac, axis=1, keepdims=False).reshape(1, 6 * D)

    G = {n: _gather_weight(P[n][0], name="gather_" + n) for n in _BIG}
    W = _full_weights(G)

    gains = {n: P[n] for n in ["g_pre_mix", "g_post_mix", "g_pre_ffn", "g_post_ffn", "g_q_lora", "g_kv_lora"]}
    loss, grad_x, dW, small = _local_step(x[0], loss_target[0], mod, positions, gains, b_forget, W)

    shards = _grad_shards(dW)
    grads = {n: _reduce_scatter_weight(shards[n], name="scatter_" + n) for n in _BIG}

    small = dict(small, loss=_pad_block(loss))
    row = jnp.concatenate([small[n] for n in _ROW], axis=1)
    rows = _all_gather_rows(jnp.pad(row, ((0, 7), (0, 0))), name="gather_small")[:, 0, :]
    tot = _rowsum(rows, name="sum_small")
    piece = lambda n: tot[:, _ROW[n][0]:_ROW[n][0] + _ROW[n][1]]
    grads["b_ada"] = piece("dmod")
    for n in ["g_pre_mix", "g_post_mix", "g_pre_ffn", "g_post_ffn", "g_q_lora", "g_kv_lora"]:
        grads[n] = piece(n)
    grads["b_forget"] = piece("b_forget")[:, :H]
    loss_out = piece("loss")[0, 0]
    dmod_all = rows[:, _ROW["dmod"][0]:_ROW["dmod"][0] + 6 * D]
    dmod_shard = jnp.pad(lax.dynamic_slice(dmod_all, (0, chip * n_ada), (8, n_ada)), ((0, 8), (0, 0)))
    grads["w_ada"] = _matmul(silu_c, dmod_shard, ta=True, name="ada_dw")

    delta, new_m, new_v = {}, {}, {}
    for n in ["w_ada"] + _BIG:
        delta[n], new_m[n], new_v[n] = _adamw(P[n][0], grads[n], M[n][0], V[n][0], name="adamw_" + n)
    cat = lambda T: jnp.concatenate([T[n] for n in _SMALL], axis=1)
    d_s, m_s, v_s = _adamw(cat(P), cat(grads), cat(M), cat(V), name="adamw_small")
    o = 0
    for n in _SMALL:
        wdt = P[n].shape[1]
        delta[n], new_m[n], new_v[n] = d_s[:, o:o + wdt], m_s[:, o:o + wdt], v_s[:, o:o + wdt]
        o += wdt

    def shaped(T, n):
        return T[n].reshape(P[n].shape)

    return (loss_out, grad_x[None], *[shaped(grads, n) for n in _ORDER], *[shaped(delta, n) for n in _ORDER],
            *[shaped(new_m, n) for n in _ORDER], *[shaped(new_v, n) for n in _ORDER])
```

```python
import functools
import math

import jax
import jax.numpy as jnp
from jax import lax
from jax.experimental import pallas as pl
from jax.experimental.pallas import tpu as pltpu

F32 = jnp.float32
BF16 = jnp.bfloat16
_MXU = jnp.bfloat16

S = 2048
D = 1024
H = 8
HP = 128
FOX_HD = 64
MLA_NOPE = 64
MLA_ROPE = 32
MLA_V = 64
Q_LORA = 768
KV_LORA = 256
D_FF = 2816
NORM_EPS = 1e-6
ROPE_THETA = 10000.0
NEG = -1e30

ADAM_LR = 0.001
ADAM_B1 = 0.9
ADAM_B2 = 0.999
ADAM_EPS = 1e-08
ADAM_WD = 0.01
ADAM_STEP = 10

LANES = 128
VMEM_CAP = 60 * 1024 * 1024
MESH = pl.DeviceIdType.MESH

NQKV = 3 * H * HP
OFF_CQ = 0
OFF_CKV = Q_LORA
OFF_GFOX = 1024
OFF_GMLA = 2048
OFF_FLOG = 3072
OFF_KRIN = 3200
NREST = 3328
KRIN_LANE = 64
ROW_SUM_LANE = 64
COL_SUM_LANE = 65


def _limit(nbytes):
    return int(min(VMEM_CAP, nbytes * 1.25 + (4 << 20)))


def _nbytes(shape, dtype):
    n = 1
    for s in shape:
        n *= s
    return n * jnp.dtype(dtype).itemsize


def _pick(n, cap):
    best = None
    for t in range(LANES, min(n, cap) + 1, LANES):
        if n % t == 0:
            best = t
    return best if best is not None else n


def _pcall(body, *, out_shape, **kw):
    outs = jax.tree.map(lambda s: pltpu.HBM(s.shape, s.dtype), out_shape)
    call = pl.pallas_call(body, out_shape=outs, **kw)
    return lambda *args: call(*[pltpu.with_memory_space_constraint(a, pltpu.HBM) for a in args])


def _matmul(a, b, *, ta=False, tb=False, out_dtype=F32, name, tm_cap=1024, tn_cap=512, tk_cap=1024,
            b_shards=False, out_shards=False):
    if ta:
        K, M = a.shape
    else:
        M, K = a.shape
    if b_shards:
        _, R, cb = b.shape
        N, K2 = (R, 4 * cb) if tb else (4 * cb, R)
    elif tb:
        N, K2 = b.shape
    else:
        K2, N = b.shape
    assert K == K2, (a.shape, b.shape, ta, tb)
    tm = _pick(M, tm_cap)
    tn = _pick(N, tn_cap)
    tk = K if K <= tk_cap else _pick(K, tk_cap)
    nk = K // tk
    dims = (((0 if ta else 1,), (1 if tb else 0,)), ((), ()))

    def body(a_ref, b_ref, o_ref, acc_ref):
        k = pl.program_id(2)

        @pl.when(k == 0)
        def _():
            acc_ref[...] = jnp.zeros_like(acc_ref)

        acc_ref[...] += lax.dot_general(a_ref[...].astype(_MXU), b_ref[...].astype(_MXU), dims,
                                        preferred_element_type=F32)

        @pl.when(k == nk - 1)
        def _():
            o_ref[...] = acc_ref[...].astype(out_dtype)

    a_spec = pl.BlockSpec((tk, tm), lambda i, j, k: (k, i)) if ta else pl.BlockSpec((tm, tk), lambda i, j, k: (i, k))
    if b_shards and tb:
        assert cb % tk == 0
        per = cb // tk
        b_spec = pl.BlockSpec((None, tn, tk), lambda i, j, k: (k // per, j, k % per))
    elif b_shards:
        assert cb % tn == 0
        per = cb // tn
        b_spec = pl.BlockSpec((None, tk, tn), lambda i, j, k: (j // per, k, j % per))
    elif tb:
        b_spec = pl.BlockSpec((tn, tk), lambda i, j, k: (j, k))
    else:
        b_spec = pl.BlockSpec((tk, tn), lambda i, j, k: (k, j))
    if out_shards:
        assert (N // 4) % tn == 0
        pern = N // 4 // tn
        out_shape = jax.ShapeDtypeStruct((4, M, N // 4), out_dtype)
        out_spec = pl.BlockSpec((None, tm, tn), lambda i, j, k: (j // pern, i, j % pern))
    else:
        out_shape = jax.ShapeDtypeStruct((M, N), out_dtype)
        out_spec = pl.BlockSpec((tm, tn), lambda i, j, k: (i, j))
    need = (2 * _nbytes((tm, tk), a.dtype) + 2 * _nbytes((tk, tn), b.dtype) + 2 * _nbytes((tm, tn), out_dtype)
            + _nbytes((tm, tn), F32) * 2 + _nbytes((tm, tk), _MXU) + _nbytes((tk, tn), _MXU))
    return _pcall(
        body, name=name,
        out_shape=out_shape,
        grid=(M // tm, N // tn, nk),
        in_specs=[a_spec, b_spec],
        out_specs=out_spec,
        scratch_shapes=[pltpu.VMEM((tm, tn), F32)],
        compiler_params=pltpu.CompilerParams(dimension_semantics=("parallel", "parallel", "arbitrary"),
                                             vmem_limit_bytes=_limit(need)),
    )(a, b)


TM = 256


def _vec(w):
    return pl.BlockSpec((1, w), lambda i: (0, 0))


def _rows(w, col=0):
    return pl.BlockSpec((TM, w), lambda i: (i, col))


def _row_params(need, carried=False):
    return pltpu.CompilerParams(dimension_semantics=("arbitrary" if carried else "parallel",),
                                vmem_limit_bytes=_limit(need))


def _norm_mod(x, col, w, g, ops, sh, *, name):
    def body(x_ref, g_ref, ops_ref, sh_ref, o_ref):
        xv = x_ref[...]
        r = lax.rsqrt(jnp.mean(xv * xv, axis=-1, keepdims=True) + NORM_EPS)
        o_ref[...] = (((xv * r) * g_ref[...]) * ops_ref[...] + sh_ref[...]).astype(o_ref.dtype)

    return _pcall(
        body, name=name, out_shape=jax.ShapeDtypeStruct((S, w), _MXU), grid=(S // TM,),
        in_specs=[_rows(w, col), _vec(w), _vec(w), _vec(w)], out_specs=_rows(w),
        compiler_params=_row_params(8 * _nbytes((TM, w), F32)),
    )(x, g, ops, sh)


def _norm_mod_bwd(x, col, w, dh, g, ops, dres, *, name):
    has_res = dres is not None

    def body(*refs):
        if has_res:
            x_ref, dh_ref, g_ref, ops_ref, dres_ref, dx_ref, s1_ref, s2_ref = refs
        else:
            x_ref, dh_ref, g_ref, ops_ref, dx_ref, s1_ref, s2_ref = refs
        i = pl.program_id(0)

        @pl.when(i == 0)
        def _():
            s1_ref[...] = jnp.zeros_like(s1_ref)
            s2_ref[...] = jnp.zeros_like(s2_ref)

        xv = x_ref[...]
        dhv = dh_ref[...]
        r = lax.rsqrt(jnp.mean(xv * xv, axis=-1, keepdims=True) + NORM_EPS)
        xn = xv * r
        dxn = dhv * (g_ref[...] * ops_ref[...])
        dx = r * (dxn - xn * jnp.mean(dxn * xn, axis=-1, keepdims=True))
        if has_res:
            dx = dx + dres_ref[...]
        dx_ref[...] = dx
        s1_ref[...] += jnp.sum(dhv, axis=0, keepdims=True)
        s2_ref[...] += jnp.sum(dhv * xn, axis=0, keepdims=True)

    in_specs = [_rows(w, col), _rows(w), _vec(w), _vec(w)] + ([_rows(w)] if has_res else [])
    args = (x, dh, g, ops) + ((dres,) if has_res else ())
    return _pcall(
        body, name=name,
        out_shape=(jax.ShapeDtypeStruct((S, w), F32), jax.ShapeDtypeStruct((1, w), F32), jax.ShapeDtypeStruct((1, w), F32)),
        grid=(S // TM,), in_specs=in_specs, out_specs=(_rows(w), _vec(w), _vec(w)),
        compiler_params=_row_params(12 * _nbytes((TM, w), F32), carried=True),
    )(*args)


def _post_res(xres, y, g, gt, *, name):
    def body(x_ref, y_ref, g_ref, gt_ref, o_ref):
        yv = y_ref[...]
        r = lax.rsqrt(jnp.mean(yv * yv, axis=-1, keepdims=True) + NORM_EPS)
        o_ref[...] = x_ref[...] + gt_ref[...] * ((yv * r) * g_ref[...])

    return _pcall(
        body, name=name, out_shape=jax.ShapeDtypeStruct((S, D), F32), grid=(S // TM,),
        in_specs=[_rows(D), _rows(D), _vec(D), _vec(D)], out_specs=_rows(D),
        compiler_params=_row_params(8 * _nbytes((TM, D), F32)),
    )(xres, y, g, gt)


def _post_res_loss(xres, y, g, gt, target, *, name):
    def body(x_ref, y_ref, g_ref, gt_ref, t_ref, dout_ref, loss_ref):
        i = pl.program_id(0)

        @pl.when(i == 0)
        def _():
            loss_ref[...] = jnp.zeros_like(loss_ref)

        yv = y_ref[...]
        r = lax.rsqrt(jnp.mean(yv * yv, axis=-1, keepdims=True) + NORM_EPS)
        out = x_ref[...] + gt_ref[...] * ((yv * r) * g_ref[...])
        err = out - t_ref[...]
        dout_ref[...] = err * (1.0 / D)
        per_row = jnp.mean(err * err, axis=-1, keepdims=True)
        loss_ref[...] += 0.5 * jnp.sum(per_row, axis=0, keepdims=True)

    return _pcall(
        body, name=name,
        out_shape=(jax.ShapeDtypeStruct((S, D), F32), jax.ShapeDtypeStruct((1, 1), F32)), grid=(S // TM,),
        in_specs=[_rows(D), _rows(D), _vec(D), _vec(D), _rows(D)],
        out_specs=(_rows(D), pl.BlockSpec((1, 1), lambda i: (0, 0))),
        compiler_params=_row_params(10 * _nbytes((TM, D), F32), carried=True),
    )(xres, y, g, gt, target)


def _post_res_bwd(dxn, y, g, gt, *, name):
    def body(d_ref, y_ref, g_ref, gt_ref, dy_ref, sgt_ref, sg_ref):
        i = pl.program_id(0)

        @pl.when(i == 0)
        def _():
            sgt_ref[...] = jnp.zeros_like(sgt_ref)
            sg_ref[...] = jnp.zeros_like(sg_ref)

        yv = y_ref[...]
        dv = d_ref[...]
        r = lax.rsqrt(jnp.mean(yv * yv, axis=-1, keepdims=True) + NORM_EPS)
        yn = yv * r
        dn = dv * gt_ref[...]
        dyn = dn * g_ref[...]
        dy_ref[...] = (r * (dyn - yn * jnp.mean(dyn * yn, axis=-1, keepdims=True))).astype(dy_ref.dtype)
        sgt_ref[...] += jnp.sum(dv * (yn * g_ref[...]), axis=0, keepdims=True)
        sg_ref[...] += jnp.sum(dn * yn, axis=0, keepdims=True)

    return _pcall(
        body, name=name,
        out_shape=(jax.ShapeDtypeStruct((S, D), _MXU), jax.ShapeDtypeStruct((1, D), F32), jax.ShapeDtypeStruct((1, D), F32)),
        grid=(S // TM,), in_specs=[_rows(D), _rows(D), _vec(D), _vec(D)], out_specs=(_rows(D), _vec(D), _vec(D)),
        compiler_params=_row_params(10 * _nbytes((TM, D), F32), carried=True),
    )(dxn, y, g, gt)


def _swiglu(gu, *, name):
    def body(g_ref, u_ref, o_ref):
        gv = g_ref[...]
        o_ref[...] = ((gv * jax.nn.sigmoid(gv)) * u_ref[...]).astype(o_ref.dtype)

    return _pcall(
        body, name=name, out_shape=jax.ShapeDtypeStruct((S, D_FF), _MXU), grid=(S // TM,),
        in_specs=[_rows(D_FF, 0), _rows(D_FF, 1)], out_specs=_rows(D_FF),
        compiler_params=_row_params(8 * _nbytes((TM, D_FF), F32)),
    )(gu, gu)


def _swiglu_bwd(gu, dact, *, name):
    def body(g_ref, u_ref, d_ref, o_ref):
        gv = g_ref[...]
        dv = d_ref[...]
        sg = jax.nn.sigmoid(gv)
        o_ref[:, :D_FF] = (dv * u_ref[...] * (sg * (1.0 + gv * (1.0 - sg)))).astype(o_ref.dtype)
        o_ref[:, D_FF:] = (dv * (gv * sg)).astype(o_ref.dtype)

    return _pcall(
        body, name=name, out_shape=jax.ShapeDtypeStruct((S, 2 * D_FF), _MXU), grid=(S // TM,),
        in_specs=[_rows(D_FF, 0), _rows(D_FF, 1), _rows(D_FF)], out_specs=_rows(2 * D_FF),
        compiler_params=_row_params(12 * _nbytes((TM, D_FF), F32)),
    )(gu, gu, dact)


def _merge(rest, pa, pb, *, name):
    def body(ga_ref, gb_ref, pa_ref, pb_ref, o_ref):
        o_ref[...] = (jax.nn.sigmoid(ga_ref[...]) * pa_ref[...] + jax.nn.sigmoid(gb_ref[...]) * pb_ref[...]).astype(o_ref.dtype)

    return _pcall(
        body, name=name, out_shape=jax.ShapeDtypeStruct((S, D), _MXU), grid=(S // TM,),
        in_specs=[_rows(D, OFF_GFOX // D), _rows(D, OFF_GMLA // D), _rows(D), _rows(D)], out_specs=_rows(D),
        compiler_params=_row_params(10 * _nbytes((TM, D), F32)),
    )(rest, rest, pa, pb)


def _merge_bwd(rest, pa, pb, dm, *, name):
    def body(ga_ref, gb_ref, pa_ref, pb_ref, d_ref, dpa_ref, dpb_ref, dga_ref, dgb_ref):
        dv = d_ref[...]
        sa = jax.nn.sigmoid(ga_ref[...])
        sb = jax.nn.sigmoid(gb_ref[...])
        dpa_ref[...] = (dv * sa).astype(dpa_ref.dtype)
        dpb_ref[...] = (dv * sb).astype(dpb_ref.dtype)
        dga_ref[...] = (dv * pa_ref[...] * (sa * (1.0 - sa))).astype(dga_ref.dtype)
        dgb_ref[...] = (dv * pb_ref[...] * (sb * (1.0 - sb))).astype(dgb_ref.dtype)

    o = jax.ShapeDtypeStruct((S, D), _MXU)
    return _pcall(
        body, name=name, out_shape=(o, o, o, o), grid=(S // TM,),
        in_specs=[_rows(D, OFF_GFOX // D), _rows(D, OFF_GMLA // D), _rows(D), _rows(D), _rows(D)],
        out_specs=(_rows(D), _rows(D), _rows(D), _rows(D)),
        compiler_params=_row_params(16 * _nbytes((TM, D), F32)),
    )(rest, rest, pa, pb, dm)


SCAN = 256


def _split_dot(tri, x):
    hi = x.astype(_MXU)
    r1 = x - hi.astype(F32)
    mid = r1.astype(_MXU)
    lo = (r1 - mid.astype(F32)).astype(_MXU)
    dot = functools.partial(jnp.dot, preferred_element_type=F32)
    return dot(tri, hi) + dot(tri, mid) + dot(tri, lo)


def _fox_prep(rest, bf, *, name):
    def body(z_ref, b_ref, f_ref):
        lane = lax.broadcasted_iota(jnp.int32, (SCAN, LANES), 1)
        tri = (lax.broadcasted_iota(jnp.int32, (SCAN, SCAN), 1) <= lax.broadcasted_iota(jnp.int32, (SCAN, SCAN), 0)).astype(_MXU)
        carry = jnp.zeros((1, LANES), F32)
        for c in range(S // SCAN):
            z = z_ref[c * SCAN:(c + 1) * SCAN, :] + b_ref[...]
            lf = jnp.minimum(z, 0.0) - jnp.log(1.0 + jnp.exp(-jnp.abs(z)))
            lf = jnp.where(lane < H, lf, 0.0)
            cum = _split_dot(tri, lf) + carry
            f_ref[c * SCAN:(c + 1) * SCAN, :] = cum
            carry = cum[SCAN - 1:SCAN, :]

    return _pcall(
        body, name=name, out_shape=jax.ShapeDtypeStruct((S, LANES), F32), grid=(1,),
        in_specs=[pl.BlockSpec((S, LANES), lambda i: (0, OFF_FLOG // LANES)), pl.BlockSpec((1, LANES), lambda i: (0, 0))],
        out_specs=pl.BlockSpec((S, LANES), lambda i: (0, 0)),
        compiler_params=pltpu.CompilerParams(vmem_limit_bytes=_limit(8 * _nbytes((S, LANES), F32))),
    )(rest, bf)


def _fox_bwd_prep(rest, bf, dF, *, name):
    def body(z_ref, b_ref, d_ref, o_ref, db_ref):
        lane = lax.broadcasted_iota(jnp.int32, (SCAN, LANES), 1)
        tri = (lax.broadcasted_iota(jnp.int32, (SCAN, SCAN), 1) >= lax.broadcasted_iota(jnp.int32, (SCAN, SCAN), 0)).astype(_MXU)
        carry = jnp.zeros((1, LANES), F32)
        db = jnp.zeros((1, LANES), F32)
        for c in range(S // SCAN - 1, -1, -1):
            rc = _split_dot(tri, d_ref[c * SCAN:(c + 1) * SCAN, :]) + carry
            z = z_ref[c * SCAN:(c + 1) * SCAN, :] + b_ref[...]
            dz = jnp.where(lane < H, rc * jax.nn.sigmoid(-z), 0.0)
            o_ref[c * SCAN:(c + 1) * SCAN, :] = dz
            db = db + jnp.sum(dz, axis=0, keepdims=True)
            carry = rc[0:1, :]
        db_ref[...] = db

    return _pcall(
        body, name=name,
        out_shape=(jax.ShapeDtypeStruct((S, LANES), F32), jax.ShapeDtypeStruct((1, LANES), F32)), grid=(1,),
        in_specs=[pl.BlockSpec((S, LANES), lambda i: (0, OFF_FLOG // LANES)), pl.BlockSpec((1, LANES), lambda i: (0, 0)),
                  pl.BlockSpec((S, LANES), lambda i: (0, 0))],
        out_specs=(pl.BlockSpec((S, LANES), lambda i: (0, 0)), pl.BlockSpec((1, LANES), lambda i: (0, 0))),
        compiler_params=pltpu.CompilerParams(vmem_limit_bytes=_limit(10 * _nbytes((S, LANES), F32))),
    )(rest, bf, dF)


def _swap16(x):
    lane = lax.broadcasted_iota(jnp.int32, x.shape, 1)
    half = MLA_ROPE // 2
    sw = jnp.where(lane < KRIN_LANE + half, pltpu.roll(x, LANES - half, 1), pltpu.roll(x, half, 1))
    return jnp.where((lane >= KRIN_LANE) & (lane < KRIN_LANE + MLA_ROPE), sw, 0.0)


def _mla_assemble(qb, kvb, rest, ctab, stab, *, name):
    def body(q_ref, kk_ref, kv_ref, kr_ref, c_ref, s_ref, qo_ref, ko_ref, vo_ref):
        cv = c_ref[...]
        sv = s_ref[...]
        kr = kr_ref[...]
        kpe = kr * cv + _swap16(kr) * sv
        for h in range(H):
            sl = slice(h * HP, (h + 1) * HP)
            qh = q_ref[:, sl]
            qo_ref[:, sl] = (qh * cv + _swap16(qh) * sv).astype(qo_ref.dtype)
            ko_ref[:, sl] = (kk_ref[:, sl] + kpe).astype(ko_ref.dtype)
        vo_ref[...] = kv_ref[...].astype(vo_ref.dtype)

    o = jax.ShapeDtypeStruct((S, H * HP), _MXU)
    return _pcall(
        body, name=name, out_shape=(o, o, o), grid=(S // TM,),
        in_specs=[_rows(H * HP), _rows(H * HP, 0), _rows(H * HP, 1), _rows(LANES, OFF_KRIN // LANES), _rows(LANES), _rows(LANES)],
        out_specs=(_rows(H * HP), _rows(H * HP), _rows(H * HP)),
        compiler_params=_row_params(14 * _nbytes((TM, H * HP), F32)),
    )(qb, kvb, kvb, rest, ctab, stab)


def _mla_assemble_bwd(dq, dk, dv, ctab, stab, *, name):
    def body(dq_ref, dk_ref, dv_ref, c_ref, s_ref, dqo_ref, dkv_ref, dkr_ref):
        cv = c_ref[...]
        sv = s_ref[...]
        lane = lax.broadcasted_iota(jnp.int32, (TM, LANES), 1)
        dsum = jnp.zeros((TM, LANES), F32)
        for h in range(H):
            sl = slice(h * HP, (h + 1) * HP)
            dqh = dq_ref[:, sl]
            dqo_ref[:, sl] = (dqh * cv + _swap16(dqh * sv)).astype(dqo_ref.dtype)
            dsum = dsum + dk_ref[:, sl]
        dkv_ref[:, :H * HP] = dk_ref[...].astype(dkv_ref.dtype)
        dkv_ref[:, H * HP:] = dv_ref[...].astype(dkv_ref.dtype)
        dkr = dsum * cv + _swap16(dsum * sv)
        dkr_ref[...] = jnp.where((lane >= KRIN_LANE) & (lane < KRIN_LANE + MLA_ROPE), dkr, 0.0)

    return _pcall(
        body, name=name,
        out_shape=(jax.ShapeDtypeStruct((S, H * HP), _MXU), jax.ShapeDtypeStruct((S, 2 * H * HP), _MXU),
                   jax.ShapeDtypeStruct((S, LANES), F32)),
        grid=(S // TM,),
        in_specs=[_rows(H * HP), _rows(H * HP), _rows(H * HP), _rows(LANES), _rows(LANES)],
        out_specs=(_rows(H * HP), _rows(2 * H * HP), _rows(LANES)),
        compiler_params=_row_params(14 * _nbytes((TM, H * HP), F32)),
    )(dq, dk, dv, ctab, stab)


TQ = 256
TKF = 512
TKB = 256
TQB = 512
_NT = (((1,), (1,)), ((), ()))
_TN = (((0,), (0,)), ((), ()))


def _is_pow2(x):
    return math.frexp(x)[0] == 0.5


def _attn_fwd(q, k, v, frow, *, scale, name):
    has_decay = frow is not None
    fold = _is_pow2(scale)

    def body(*refs):
        if has_decay:
            q_ref, k_ref, v_ref, fr_ref, o_ref, lse_ref = refs
        else:
            q_ref, k_ref, v_ref, o_ref, lse_ref = refs
        i = pl.program_id(1)
        qv = q_ref[...]
        if fold:
            qv = (qv * scale).astype(qv.dtype)
        last = (i * TQ) // TKF

        def tile(j, carry, masked):
            m, l, acc = carry
            k0 = pl.multiple_of(j * TKF, TKF)
            kj = k_ref[pl.ds(k0, TKF), :]
            vj = v_ref[pl.ds(k0, TKF), :]
            s = lax.dot_general(qv, kj, _NT, preferred_element_type=F32)
            if not fold:
                s = s * scale
            if has_decay:
                s = s - fr_ref[0, j]
            if masked:
                rows = i * TQ + lax.broadcasted_iota(jnp.int32, (TQ, TKF), 0)
                cols = j * TKF + lax.broadcasted_iota(jnp.int32, (TQ, TKF), 1)
                s = jnp.where(cols <= rows, s, NEG)
            m_new = jnp.maximum(m, jnp.max(s, axis=-1, keepdims=True))
            alpha = jnp.exp(m - m_new)
            p = jnp.exp(s - m_new)
            l = alpha * l + jnp.sum(p, axis=-1, keepdims=True)
            acc = alpha * acc + jnp.dot(p.astype(_MXU), vj, preferred_element_type=F32)
            return m_new, l, acc

        init = (jnp.full((TQ, 1), NEG, F32), jnp.zeros((TQ, 1), F32), jnp.zeros((TQ, HP), F32))
        carry = lax.fori_loop(0, last, lambda j, c: tile(j, c, False), init)
        m, l, acc = tile(last, carry, True)
        o_ref[...] = acc / l
        lse_ref[0] = m + jnp.log(l)

    in_specs = [pl.BlockSpec((TQ, HP), lambda h, i: (i, h)), pl.BlockSpec((S, HP), lambda h, i: (0, h)),
                pl.BlockSpec((S, HP), lambda h, i: (0, h))]
    args = (q, k, v)
    if has_decay:
        in_specs += [pl.BlockSpec((1, S // TKF, 1, TKF), lambda h, i: (h, 0, 0, 0))]
        args += (frow,)
    return _pcall(
        body, name=name,
        out_shape=(jax.ShapeDtypeStruct((S, H * HP), F32), jax.ShapeDtypeStruct((H, S, 1), F32)),
        grid=(H, S // TQ), in_specs=in_specs,
        out_specs=(pl.BlockSpec((TQ, HP), lambda h, i: (i, h)), pl.BlockSpec((1, TQ, 1), lambda h, i: (h, i, 0))),
        compiler_params=pltpu.CompilerParams(dimension_semantics=("parallel", "parallel"),
                                             vmem_limit_bytes=_limit(8 * _nbytes((S, HP), F32))),
    )(*args)


def _attn_delta(do, o, *, name):
    def body(do_ref, o_ref, dl_ref, dob_ref):
        for h in range(H):
            sl = slice(h * HP, (h + 1) * HP)
            dl_ref[h] = jnp.sum(do_ref[:, sl] * o_ref[:, sl], axis=-1, keepdims=True)
        dob_ref[...] = do_ref[...].astype(dob_ref.dtype)

    return _pcall(
        body, name=name,
        out_shape=(jax.ShapeDtypeStruct((H, S, 1), F32), jax.ShapeDtypeStruct((S, H * HP), _MXU)), grid=(S // TM,),
        in_specs=[_rows(H * HP), _rows(H * HP)],
        out_specs=(pl.BlockSpec((H, TM, 1), lambda i: (0, i, 0)), _rows(H * HP)),
        compiler_params=_row_params(8 * _nbytes((TM, H * HP), F32)),
    )(do, o)


def _attn_bwd(q, k, v, dob, lse_row, delta_row, fcol, *, scale, name):
    has_decay = fcol is not None
    fold = _is_pow2(scale)

    def body(*refs):
        if has_decay:
            q_ref, k_ref, v_ref, do_ref, lse_ref, dl_ref, fc_ref, dq_ref, dk_ref, dv_ref, dq_acc = refs
        else:
            q_ref, k_ref, v_ref, do_ref, lse_ref, dl_ref, dq_ref, dk_ref, dv_ref, dq_acc = refs
        j = pl.program_id(1)

        @pl.when(j == 0)
        def _():
            dq_acc[...] = jnp.zeros_like(dq_acc)

        kj = k_ref[...]
        vj = v_ref[...]
        kjs = (kj * scale).astype(kj.dtype) if fold else kj
        if has_decay:
            klane = lax.broadcasted_iota(jnp.int32, (TKB, HP), 1)
            kj = jnp.where(klane == ROW_SUM_LANE, 1.0, kj).astype(kj.dtype)
        first = (j * TKB) // TQB

        def tile(t, carry, masked):
            dk, dv = carry
            r0 = pl.multiple_of(t * TQB, TQB)
            qi = q_ref[pl.ds(r0, TQB), :]
            doi = do_ref[pl.ds(r0, TQB), :]
            st = lax.dot_general(kjs, qi, _NT, preferred_element_type=F32)
            if not fold:
                st = st * scale
            if has_decay:
                st = st - fc_ref[0]
            if masked:
                keys = j * TKB + lax.broadcasted_iota(jnp.int32, (TKB, TQB), 0)
                qpos = t * TQB + lax.broadcasted_iota(jnp.int32, (TKB, TQB), 1)
                st = jnp.where(keys <= qpos, st, NEG)
            pt = jnp.exp(st - lse_ref[0, t])
            dv = dv + jnp.dot(pt.astype(_MXU), doi, preferred_element_type=F32)
            dpt = lax.dot_general(vj, doi, _NT, preferred_element_type=F32)
            dst = (pt * (dpt - dl_ref[0, t])).astype(_MXU)
            if has_decay:
                lane = lax.broadcasted_iota(jnp.int32, (TQB, HP), 1)
                qi = jnp.where(lane == COL_SUM_LANE, 1.0, qi).astype(qi.dtype)
            dk = dk + jnp.dot(dst, qi, preferred_element_type=F32)
            dq_acc[pl.ds(r0, TQB), :] += lax.dot_general(dst, kj, _TN, preferred_element_type=F32)
            return dk, dv

        zero = jnp.zeros((TKB, HP), F32)
        carry = tile(first, (zero, zero), True)
        dk, dv = lax.fori_loop(first + 1, S // TQB, lambda t, c: tile(t, c, False), carry)
        dk_ref[...] = dk * scale
        dv_ref[...] = dv

        @pl.when(j == S // TKB - 1)
        def _():
            dq_ref[...] = dq_acc[...] * scale

    head = pl.BlockSpec((S, HP), lambda h, j: (0, h))
    kv = pl.BlockSpec((TKB, HP), lambda h, j: (j, h))
    stat = pl.BlockSpec((1, S // TQB, 1, TQB), lambda h, j: (h, 0, 0, 0))
    in_specs = [head, kv, kv, head, stat, stat]
    args = (q, k, v, dob, lse_row, delta_row)
    if has_decay:
        in_specs += [pl.BlockSpec((1, TKB, 1), lambda h, j: (h, j, 0))]
        args += (fcol,)
    o = jax.ShapeDtypeStruct((S, H * HP), F32)
    return _pcall(
        body, name=name, out_shape=(o, o, o), grid=(H, S // TKB), in_specs=in_specs,
        out_specs=(head, kv, kv), scratch_shapes=[pltpu.VMEM((S, HP), F32)],
        compiler_params=pltpu.CompilerParams(dimension_semantics=("parallel", "arbitrary"),
                                             vmem_limit_bytes=_limit(12 * _nbytes((S, HP), F32))),
    )(*args)


def _ada_mod(c_all, w_shard, b_shard, *, name):
    R = c_all.shape[0]
    N = w_shard.shape[1]
    tn = 512

    def body(c_ref, w_ref, b_ref, o_ref, sc_ref):
        cv = c_ref[...]
        sc = (cv * jax.nn.sigmoid(cv)).astype(_MXU)
        sc_ref[...] = sc
        o_ref[...] = jnp.dot(sc, w_ref[...].astype(_MXU), preferred_element_type=F32) + b_ref[...]

    return _pcall(
        body, name=name,
        out_shape=(jax.ShapeDtypeStruct((R, N), F32), jax.ShapeDtypeStruct((R, D), _MXU)), grid=(N // tn,),
        in_specs=[pl.BlockSpec((R, D), lambda j: (0, 0)), pl.BlockSpec((D, tn), lambda j: (0, j)), pl.BlockSpec((1, tn), lambda j: (0, j))],
        out_specs=(pl.BlockSpec((R, tn), lambda j: (0, j)), pl.BlockSpec((R, D), lambda j: (0, 0))),
        compiler_params=pltpu.CompilerParams(dimension_semantics=("arbitrary",), vmem_limit_bytes=_limit(6 * _nbytes((D, tn), F32))),
    )(c_all, w_shard, b_shard)


def _rowsum(x, *, name):
    R, L = x.shape

    def body(x_ref, o_ref):
        acc = x_ref[0:1, :]
        for r in range(1, R):
            acc = acc + x_ref[r:r + 1, :]
        o_ref[...] = acc

    return pl.pallas_call(body, name=name, out_shape=jax.ShapeDtypeStruct((1, L), F32),
                          in_specs=[pl.BlockSpec(memory_space=pltpu.VMEM)], out_specs=pl.BlockSpec(memory_space=pltpu.VMEM))(x)


def _adamw(w, g, m, v, *, name):
    _, R, C = w.shape
    tr = R
    for t in range(8, R + 1, 8):
        if R % t == 0 and t * C * 4 <= (1 << 20):
            tr = t

    def body(w_ref, g_ref, m_ref, v_ref, d_ref, mo_ref, vo_ref):
        gv = g_ref[...]
        m2 = ADAM_B1 * m_ref[...] + (1.0 - ADAM_B1) * gv
        v2 = ADAM_B2 * v_ref[...] + (1.0 - ADAM_B2) * (gv * gv)
        m_hat = m2 / (1.0 - ADAM_B1 ** ADAM_STEP)
        v_hat = v2 / (1.0 - ADAM_B2 ** ADAM_STEP)
        d_ref[...] = -ADAM_LR * (m_hat / (jnp.sqrt(v_hat) + ADAM_EPS) + ADAM_WD * w_ref[...])
        mo_ref[...] = m2
        vo_ref[...] = v2

    blk = pl.BlockSpec((None, tr, C), lambda i: (0, i, 0))
    o = jax.ShapeDtypeStruct((1, R, C), F32)
    return _pcall(
        body, name=name, out_shape=(o, o, o), grid=(R // tr,),
        in_specs=[blk, pl.BlockSpec((tr, C), lambda i: (i, 0)), blk, blk], out_specs=(blk, blk, blk),
        compiler_params=pltpu.CompilerParams(dimension_semantics=("parallel",), vmem_limit_bytes=_limit(20 * _nbytes((tr, C), F32))),
    )(w, g, m, v)


def _place():
    x, y, c = lax.axis_index("x"), lax.axis_index("y"), lax.axis_index("c")
    return x, y, c, [(1 - x, y), (x, 1 - y), (1 - x, 1 - y)]


def _two_level_gather(x_ref, out_ref, send_sems, recv_sems, local_sem):
    x, y, c, chips = _place()
    me, sibling = (x, y, c), (x, y, 1 - c)

    def blk(px, py, pc):
        return out_ref.at[4 * px + 2 * py + pc]

    def copy(k, block, to, src=None):
        return pltpu.make_async_remote_copy(
            src_ref=blk(*block) if src is None else src, dst_ref=blk(*block),
            send_sem=send_sems.at[k], recv_sem=recv_sems.at[k], device_id=to, device_id_type=MESH)

    mine = pltpu.make_async_copy(x_ref, blk(*me), local_sem)
    mine.start()
    first = [copy(0, me, sibling, src=x_ref)]
    first += [copy(1 + j, me, (*chip, c), src=x_ref) for j, chip in enumerate(chips)]
    for cp in first:
        cp.start()
    passed = [copy(4 + j, (*chip, c), sibling) for j, chip in enumerate(chips)]
    for j, chip in enumerate(chips):
        copy(1 + j, (*chip, c), me).wait_recv()
        passed[j].start()
    copy(0, sibling, me).wait_recv()
    for j, chip in enumerate(chips):
        copy(4 + j, (*chip, 1 - c), me).wait_recv()
    for cp in first + passed:
        cp.wait_send()
    mine.wait()


_GATHER_SEMS = [pltpu.SemaphoreType.DMA((7,)), pltpu.SemaphoreType.DMA((7,)), pltpu.SemaphoreType.DMA]


def _all_gather_rows(x, *, name):
    R, C = x.shape

    def body(x_ref, out_ref, send_sems, recv_sems, local_sem):
        _two_level_gather(x_ref, out_ref, send_sems, recv_sems, local_sem)

    return pl.pallas_call(
        body, name=name, out_shape=jax.ShapeDtypeStruct((8, R, C), x.dtype),
        in_specs=[pl.BlockSpec(memory_space=pltpu.VMEM)], out_specs=pl.BlockSpec(memory_space=pltpu.VMEM),
        scratch_shapes=list(_GATHER_SEMS),
        compiler_params=pltpu.CompilerParams(vmem_limit_bytes=_limit(10 * _nbytes((R, C), x.dtype))),
    )(x)


CAST_ROWS = 16


def _gather_weight(w, *, name):
    r, c = w.shape
    rh = r // 2
    assert rh % CAST_ROWS == 0

    def body(w_hbm, out_ref, tmp, xb, send_sems, recv_sems, local_sem):
        core = lax.axis_index("c")
        ld = pltpu.make_async_copy(w_hbm.at[pl.ds(pl.multiple_of(core * rh, CAST_ROWS), rh), :], tmp, local_sem)
        ld.start()
        ld.wait()

        def cast(i, carry):
            r0 = pl.multiple_of(i * CAST_ROWS, CAST_ROWS)
            xb[pl.ds(r0, CAST_ROWS), :] = tmp[pl.ds(r0, CAST_ROWS), :].astype(BF16)
            return carry

        lax.fori_loop(0, rh // CAST_ROWS, cast, 0)
        _two_level_gather(xb, out_ref, send_sems, recv_sems, local_sem)

    need = _nbytes((8, rh, c), BF16) + _nbytes((rh, c), F32) + _nbytes((rh, c), BF16)
    out = pl.pallas_call(
        body, name=name, out_shape=jax.ShapeDtypeStruct((8, rh, c), BF16),
        in_specs=[pl.BlockSpec(memory_space=pl.ANY)], out_specs=pl.BlockSpec(memory_space=pltpu.VMEM),
        scratch_shapes=[pltpu.VMEM((rh, c), F32), pltpu.VMEM((rh, c), BF16)] + list(_GATHER_SEMS),
        compiler_params=pltpu.CompilerParams(vmem_limit_bytes=_limit(need * 1.3)),
    )(w)
    return out.reshape(4, r, c)


def _reduce_scatter_weight(g4, *, name):
    _, r, c = g4.shape
    rh = r // 2
    assert rh % CAST_ROWS == 0
    nsteps = rh // CAST_ROWS

    def body(g_hbm, out_ref, mine, tmp, sbuf, rbuf_a, rbuf_b, a_send, a_recv, b_send, b_recv, c_send, c_recv, lsem):
        x, y, core, chips = _place()
        sibling = (x, y, 1 - core)
        k = 2 * x + y
        my0 = pl.multiple_of(core * rh, CAST_ROWS)
        ot0 = pl.multiple_of((1 - core) * rh, CAST_ROWS)

        ld = pltpu.make_async_copy(g_hbm.at[:, pl.ds(my0, rh), :], mine, lsem)
        ld.start()
        ld.wait()
        for j in range(4):
            ldj = pltpu.make_async_copy(g_hbm.at[j, pl.ds(ot0, rh), :], tmp, lsem)
            ldj.start()
            ldj.wait()

            def cast(i, carry, j=j):
                r0 = pl.multiple_of(i * CAST_ROWS, CAST_ROWS)
                sbuf[j, pl.ds(r0, CAST_ROWS), :] = tmp[pl.ds(r0, CAST_ROWS), :].astype(BF16)
                return carry

            lax.fori_loop(0, nsteps, cast, 0)

        to_sib = pltpu.make_async_remote_copy(src_ref=sbuf, dst_ref=rbuf_a, send_sem=a_send, recv_sem=a_recv,
                                              device_id=sibling, device_id_type=MESH)
        to_sib.start()
        to_sib.wait()

        for j in range(4):
            def add(i, carry, j=j):
                r0 = pl.multiple_of(i * CAST_ROWS, CAST_ROWS)
                s = mine[j, pl.ds(r0, CAST_ROWS), :] + rbuf_a[j, pl.ds(r0, CAST_ROWS), :].astype(F32)
                mine[j, pl.ds(r0, CAST_ROWS), :] = s
                sbuf[j, pl.ds(r0, CAST_ROWS), :] = s.astype(BF16)
                return carry

            lax.fori_loop(0, nsteps, add, 0)

        sends = []
        for d, (px, py) in enumerate(chips):
            cp = pltpu.make_async_remote_copy(src_ref=sbuf.at[2 * px + py], dst_ref=rbuf_b.at[d], send_sem=b_send.at[d],
                                              recv_sem=b_recv.at[d], device_id=(px, py, core), device_id_type=MESH)
            cp.start()
            sends.append(cp)
        for cp in sends:
            cp.wait()

        def fin(i, carry):
            r0 = pl.multiple_of(i * CAST_ROWS, CAST_ROWS)
            s = mine[k, pl.ds(r0, CAST_ROWS), :]
            for d in range(3):
                s = s + rbuf_b[d, pl.ds(r0, CAST_ROWS), :].astype(F32)
            out_ref[pl.ds(my0 + r0, CAST_ROWS), :] = s
            return carry

        lax.fori_loop(0, nsteps, fin, 0)
        half = out_ref.at[pl.ds(my0, rh), :]
        swap = pltpu.make_async_remote_copy(src_ref=half, dst_ref=half, send_sem=c_send, recv_sem=c_recv,
                                            device_id=sibling, device_id_type=MESH)
        swap.start()
        swap.wait()

    need = (_nbytes((4, rh, c), F32) + _nbytes((rh, c), F32) + 2 * _nbytes((4, rh, c), BF16) + _nbytes((3, rh, c), BF16)
            + _nbytes((r, c), F32))
    return pl.pallas_call(
        body, name=name, out_shape=jax.ShapeDtypeStruct((r, c), F32),
        in_specs=[pl.BlockSpec(memory_space=pl.ANY)], out_specs=pl.BlockSpec(memory_space=pltpu.VMEM),
        scratch_shapes=[pltpu.VMEM((4, rh, c), F32), pltpu.VMEM((rh, c), F32), pltpu.VMEM((4, rh, c), BF16),
                        pltpu.VMEM((4, rh, c), BF16), pltpu.VMEM((3, rh, c), BF16),
                        pltpu.SemaphoreType.DMA, pltpu.SemaphoreType.DMA, pltpu.SemaphoreType.DMA((3,)),
                        pltpu.SemaphoreType.DMA((3,)), pltpu.SemaphoreType.DMA, pltpu.SemaphoreType.DMA,
                        pltpu.SemaphoreType.DMA],
        compiler_params=pltpu.CompilerParams(vmem_limit_bytes=_limit(need * 1.2)),
    )(g4)


def _cols_from_shards(g):
    n, K, c = g.shape
    return g.transpose(1, 0, 2).reshape(K, n * c)


def _cols_to_shards(w):
    K, N = w.shape
    return w.reshape(K, 4, N // 4).transpose(1, 0, 2)


def _pad_heads_cols(w, width, lane0=0):
    K = w.shape[0]
    w3 = w.reshape(K, H, width)
    return jnp.pad(w3, ((0, 0), (0, 0), (lane0, HP - lane0 - width))).reshape(K, H * HP)


def _unpad_heads_cols(w, width, lane0=0):
    K = w.shape[0]
    return w.reshape(K, H, HP)[:, :, lane0:lane0 + width].reshape(K, H * width)


def _pad_block(w, lane0=0):
    return jnp.pad(w, ((0, 0), (lane0, LANES - lane0 - w.shape[1])))


_IN_SPLITS = [512, 1024, 1536, 1544, 2312, 2568, 2600, 3624]


def _pad_w_in(w):
    fq, fk, fv, flog, cq, ckv, krin, gfox, gmla = jnp.split(w, _IN_SPLITS, axis=1)
    return jnp.concatenate([_pad_heads_cols(fq, FOX_HD), _pad_heads_cols(fk, FOX_HD), _pad_heads_cols(fv, FOX_HD),
                            cq, ckv, gfox, gmla, _pad_block(flog), _pad_block(krin, KRIN_LANE)], axis=1)


def _unpad_w_in(wp):
    qkv, rest = wp[:, :NQKV], wp[:, NQKV:]
    fq, fk, fv = (_unpad_heads_cols(qkv[:, i * H * HP:(i + 1) * H * HP], FOX_HD) for i in range(3))
    return jnp.concatenate([fq, fk, fv, rest[:, OFF_FLOG:OFF_FLOG + H], rest[:, OFF_CQ:OFF_CQ + Q_LORA],
                            rest[:, OFF_CKV:OFF_CKV + KV_LORA], rest[:, OFF_KRIN + KRIN_LANE:OFF_KRIN + KRIN_LANE + MLA_ROPE],
                            rest[:, OFF_GFOX:OFF_GFOX + D], rest[:, OFF_GMLA:OFF_GMLA + D]], axis=1)


def _pad_w_ukv(w):
    w3 = w.reshape(KV_LORA, H, MLA_NOPE + MLA_V)
    kp = jnp.pad(w3[:, :, :MLA_NOPE], ((0, 0), (0, 0), (0, HP - MLA_NOPE))).reshape(KV_LORA, H * HP)
    vp = jnp.pad(w3[:, :, MLA_NOPE:], ((0, 0), (0, 0), (0, HP - MLA_V))).reshape(KV_LORA, H * HP)
    return jnp.concatenate([kp, vp], axis=1)


def _unpad_w_ukv(wp):
    kp = wp[:, :H * HP].reshape(KV_LORA, H, HP)[:, :, :MLA_NOPE]
    vp = wp[:, H * HP:].reshape(KV_LORA, H, HP)[:, :, :MLA_V]
    return jnp.concatenate([kp, vp], axis=2).reshape(KV_LORA, H * (MLA_NOPE + MLA_V))


def _pad_heads_rows(w, width):
    N = w.shape[1]
    return jnp.pad(w.reshape(H, width, N), ((0, 0), (0, HP - width), (0, 0))).reshape(H * HP, N)


def _unpad_heads_rows(w, width):
    N = w.shape[1]
    return w.reshape(H, HP, N)[:, :width, :].reshape(H * width, N)


def _rope_tables(positions):
    inv_freq = 1.0 / (ROPE_THETA ** (jnp.arange(0, MLA_ROPE, 2, dtype=F32) / MLA_ROPE))
    ang = positions.reshape(S, 1).astype(F32) * inv_freq
    cos, sin = jnp.cos(ang), jnp.sin(ang)
    ones = jnp.ones((S, KRIN_LANE), F32)
    tail = jnp.zeros((S, LANES - KRIN_LANE - MLA_ROPE), F32)
    ctab = jnp.concatenate([ones, cos, cos, tail], axis=1)
    stab = jnp.concatenate([0.0 * ones, -sin, sin, tail], axis=1)
    return ctab, stab


def _local_step(x, target, mod, positions, gains, bf, W):
    sh1, sc1, gt1, sh2, sc2, gt2 = (mod[:, i * D:(i + 1) * D] for i in range(6))
    ops1, ops2 = 1.0 + sc1, 1.0 + sc2
    ones = lambda w: jnp.ones((1, w), F32)
    zeros = lambda w: jnp.zeros((1, w), F32)
    bf_blk = _pad_block(bf)
    ctab, stab = _rope_tables(positions)
    fox_scale = 1.0 / math.sqrt(FOX_HD)
    mla_scale = 1.0 / math.sqrt(MLA_NOPE + MLA_ROPE)

    h1 = _norm_mod(x, 0, D, gains["g_pre_mix"], ops1, sh1, name="f_pre_mix")
    qkv = _matmul(h1, W["w_in_qkv"], out_dtype=_MXU, name="f_proj_qkv")
    rest = _matmul(h1, W["w_in_rest"], name="f_proj_rest", tn_cap=256)
    F = _fox_prep(rest, bf_blk, name="f_fox_prep")
    Ft = F[:, :H].T
    fcol, frow = Ft.reshape(H, S, 1), Ft.reshape(H, S // TKF, 1, TKF)
    qa, ka, va = qkv[:, :H * HP], qkv[:, H * HP:2 * H * HP], qkv[:, 2 * H * HP:]
    oa, lse_a = _attn_fwd(qa, ka, va, frow, scale=fox_scale, name="f_attn_fox")

    cqn = _norm_mod(rest, OFF_CQ // Q_LORA, Q_LORA, gains["g_q_lora"], ones(Q_LORA), zeros(Q_LORA), name="f_norm_cq")
    ckvn = _norm_mod(rest, OFF_CKV // KV_LORA, KV_LORA, gains["g_kv_lora"], ones(KV_LORA), zeros(KV_LORA), name="f_norm_ckv")
    qb = _matmul(cqn, W["w_uq"], name="f_uq")
    kvb = _matmul(ckvn, W["w_ukv"], name="f_ukv")
    qm, km, vm = _mla_assemble(qb, kvb, rest, ctab, stab, name="f_mla_assemble")
    ob, lse_b = _attn_fwd(qm, km, vm, None, scale=mla_scale, name="f_attn_mla")

    pa = _matmul(oa, W["w_proj_fox"], name="f_proj_fox")
    pb = _matmul(ob, W["w_proj_mla"], name="f_proj_mla")
    merged = _merge(rest, pa, pb, name="f_merge")
    y1 = _matmul(merged, W["w_out"], name="f_out")
    x2 = _post_res(x, y1, gains["g_post_mix"], gt1, name="f_post_mix")
    h2 = _norm_mod(x2, 0, D, gains["g_pre_ffn"], ops2, sh2, name="f_pre_ffn")
    gu = _matmul(h2, W["w_ffn_in"], name="f_ffn_in", b_shards=True, tn_cap=1408)
    act = _swiglu(gu, name="f_swiglu")
    y2 = _matmul(act, W["w_ffn_out"], name="f_ffn_out", tk_cap=1408)
    dout, loss = _post_res_loss(x2, y2, gains["g_post_ffn"], gt2, target, name="f_post_ffn_loss")

    dy2, s_gt2, s_gpost2 = _post_res_bwd(dout, y2, gains["g_post_ffn"], gt2, name="b_post_ffn")
    dact = _matmul(dy2, W["w_ffn_out"], tb=True, name="b_ffn_out_dx", tn_cap=1408)
    dW_ffn_out = _matmul(act, dy2, ta=True, name="b_ffn_out_dw", tm_cap=1408)
    dgu = _swiglu_bwd(gu, dact, name="b_swiglu")
    dh2 = _matmul(dgu, W["w_ffn_in"], tb=True, b_shards=True, name="b_ffn_in_dx", tk_cap=1408)
    dW_ffn_in = _matmul(h2, dgu, ta=True, name="b_ffn_in_dw", out_shards=True, tn_cap=1408)
    dx2, s_sh2, s_a2 = _norm_mod_bwd(x2, 0, D, dh2, gains["g_pre_ffn"], ops2, dout, name="b_pre_ffn")
    dy1, s_gt1, s_gpost1 = _post_res_bwd(dx2, y1, gains["g_post_mix"], gt1, name="b_post_mix")
    dmerged = _matmul(dy1, W["w_out"], tb=True, name="b_out_dx")
    dW_out = _matmul(merged, dy1, ta=True, name="b_out_dw")
    dpa, dpb, dgfox, dgmla = _merge_bwd(rest, pa, pb, dmerged, name="b_merge")
    doa = _matmul(dpa, W["w_proj_fox"], tb=True, name="b_proj_fox_dx")
    dW_proj_fox = _matmul(oa, dpa, ta=True, name="b_proj_fox_dw")
    dob = _matmul(dpb, W["w_proj_mla"], tb=True, name="b_proj_mla_dx")
    dW_proj_mla = _matmul(ob, dpb, ta=True, name="b_proj_mla_dw")

    delta_a, doa16 = _attn_delta(doa, oa, name="b_delta_fox")
    as_rows = lambda a: a.reshape(H, S // TQB, 1, TQB)
    dqa, dka, dva = _attn_bwd(qa, ka, va, doa16, as_rows(lse_a), as_rows(delta_a), fcol, scale=fox_scale, name="b_attn_fox")
    delta_b, dob16 = _attn_delta(dob, ob, name="b_delta_mla")
    dqm, dkm, dvm = _attn_bwd(qm, km, vm, dob16, as_rows(lse_b), as_rows(delta_b), None, scale=mla_scale, name="b_attn_mla")

    dF = (dqa[:, ROW_SUM_LANE::HP] - dka[:, COL_SUM_LANE::HP]) * (1.0 / fox_scale)
    dflog, s_bf = _fox_bwd_prep(rest, bf_blk, _pad_block(dF), name="b_fox_prep")

    dqb, dkvb, dkrin = _mla_assemble_bwd(dqm, dkm, dvm, ctab, stab, name="b_mla_assemble")
    dcqn = _matmul(dqb, W["w_uq"], tb=True, name="b_uq_dx")
    dW_uq = _matmul(cqn, dqb, ta=True, name="b_uq_dw")
    dckvn = _matmul(dkvb, W["w_ukv"], tb=True, name="b_ukv_dx")
    dW_ukv = _matmul(ckvn, dkvb, ta=True, name="b_ukv_dw")
    dcq, _, s_gq = _norm_mod_bwd(rest, OFF_CQ // Q_LORA, Q_LORA, dcqn, gains["g_q_lora"], ones(Q_LORA), None, name="b_norm_cq")
    dckv, _, s_gkv = _norm_mod_bwd(rest, OFF_CKV // KV_LORA, KV_LORA, dckvn, gains["g_kv_lora"], ones(KV_LORA), None, name="b_norm_ckv")

    c16 = lambda a: a.astype(_MXU)
    dproj = jnp.concatenate([c16(dqa), c16(dka), c16(dva), c16(dcq), c16(dckv), dgfox, dgmla, c16(dflog), c16(dkrin)], axis=1)
    w_in_full = jnp.concatenate([W["w_in_qkv"], W["w_in_rest"]], axis=1)
    dh1 = _matmul(dproj, w_in_full, tb=True, name="b_in_dx", tk_cap=1280)
    dW_in = _matmul(h1, dproj, ta=True, name="b_in_dw", tn_cap=640)
    grad_x, s_sh1, s_a1 = _norm_mod_bwd(x, 0, D, dh1, gains["g_pre_mix"], ops1, dx2, name="b_pre_mix")

    dmod = jnp.concatenate([s_sh1, s_a1 * gains["g_pre_mix"], s_gt1, s_sh2, s_a2 * gains["g_pre_ffn"], s_gt2], axis=1)
    small = dict(dmod=dmod, g_pre_mix=s_a1 * ops1, g_post_mix=s_gpost1, g_pre_ffn=s_a2 * ops2, g_post_ffn=s_gpost2,
                 g_q_lora=s_gq, g_kv_lora=s_gkv, b_forget=s_bf)
    dW = dict(w_in=dW_in, w_uq=dW_uq, w_ukv=dW_ukv, w_proj_fox=dW_proj_fox, w_proj_mla=dW_proj_mla, w_out=dW_out,
              w_ffn_in=dW_ffn_in, w_ffn_out=dW_ffn_out)
    return loss, grad_x, dW, small


_BIG = ["w_in", "w_uq", "w_ukv", "w_proj_fox", "w_proj_mla", "w_out", "w_ffn_in", "w_ffn_out"]
_COL_SHARDED = {"w_in", "w_ukv", "w_proj_fox", "w_proj_mla", "w_ffn_in"}
_SMALL = ["b_ada", "g_pre_mix", "g_post_mix", "g_pre_ffn", "g_post_ffn", "b_forget", "g_q_lora", "g_kv_lora"]
_ORDER = ["w_ada", "b_ada", "g_pre_mix", "g_post_mix", "g_pre_ffn", "g_post_ffn", "w_in", "b_forget", "g_q_lora", "w_uq",
          "g_kv_lora", "w_ukv", "w_proj_fox", "w_proj_mla", "w_out", "w_ffn_in", "w_ffn_out"]
_ROW = {}
_off = 0
for _n, _w in [("dmod", 6 * D), ("g_pre_mix", D), ("g_post_mix", D), ("g_pre_ffn", D), ("g_post_ffn", D), ("g_q_lora", Q_LORA),
               ("g_kv_lora", KV_LORA), ("b_forget", LANES), ("loss", LANES)]:
    _ROW[_n] = (_off, _w)
    _off += _w
_ROW_LEN = _off


def _full_weights(G):
    full = {n: (_cols_from_shards(G[n]) if n in _COL_SHARDED else G[n].reshape(-1, G[n].shape[2])) for n in _BIG
            if n != "w_ffn_in"}
    w_in = _pad_w_in(full["w_in"])
    return dict(
        w_in_qkv=w_in[:, :NQKV], w_in_rest=w_in[:, NQKV:],
        w_uq=_pad_heads_cols(full["w_uq"], MLA_NOPE + MLA_ROPE),
        w_ukv=_pad_w_ukv(full["w_ukv"]),
        w_proj_fox=_pad_heads_rows(full["w_proj_fox"], FOX_HD),
        w_proj_mla=_pad_heads_rows(full["w_proj_mla"], MLA_V),
        w_out=full["w_out"], w_ffn_in=G["w_ffn_in"], w_ffn_out=full["w_ffn_out"])


def _grad_shards(dW):
    nat = dict(
        w_in=_unpad_w_in(dW["w_in"]), w_uq=_unpad_heads_cols(dW["w_uq"], MLA_NOPE + MLA_ROPE), w_ukv=_unpad_w_ukv(dW["w_ukv"]),
        w_proj_fox=_unpad_heads_rows(dW["w_proj_fox"], FOX_HD), w_proj_mla=_unpad_heads_rows(dW["w_proj_mla"], MLA_V),
        w_out=dW["w_out"], w_ffn_out=dW["w_ffn_out"])
    out = {n: (_cols_to_shards(g) if n in _COL_SHARDED else g.reshape(4, g.shape[0] // 4, g.shape[1])) for n, g in nat.items()}
    out["w_ffn_in"] = dW["w_ffn_in"]
    return out


def kernel(x, c, positions, w_ada, b_ada, g_pre_mix, g_post_mix, g_pre_ffn, g_post_ffn, w_in, b_forget, g_q_lora, w_uq, g_kv_lora, w_ukv, w_proj_fox, w_proj_mla, w_out, w_ffn_in, w_ffn_out, loss_target, m_w_ada, m_b_ada, m_g_pre_mix, m_g_post_mix, m_g_pre_ffn, m_g_post_ffn, m_w_in, m_b_forget, m_g_q_lora, m_w_uq, m_g_kv_lora, m_w_ukv, m_w_proj_fox, m_w_proj_mla, m_w_out, m_w_ffn_in, m_w_ffn_out, v_w_ada, v_b_ada, v_g_pre_mix, v_g_post_mix, v_g_pre_ffn, v_g_post_ffn, v_w_in, v_b_forget, v_g_q_lora, v_w_uq, v_g_kv_lora, v_w_ukv, v_w_proj_fox, v_w_proj_mla, v_w_out, v_w_ffn_in, v_w_ffn_out):
    P = dict(w_ada=w_ada, b_ada=b_ada, g_pre_mix=g_pre_mix, g_post_mix=g_post_mix, g_pre_ffn=g_pre_ffn, g_post_ffn=g_post_ffn,
             w_in=w_in, b_forget=b_forget, g_q_lora=g_q_lora, w_uq=w_uq, g_kv_lora=g_kv_lora, w_ukv=w_ukv,
             w_proj_fox=w_proj_fox, w_proj_mla=w_proj_mla, w_out=w_out, w_ffn_in=w_ffn_in, w_ffn_out=w_ffn_out)
    M = dict(w_ada=m_w_ada, b_ada=m_b_ada, g_pre_mix=m_g_pre_mix, g_post_mix=m_g_post_mix, g_pre_ffn=m_g_pre_ffn,
             g_post_ffn=m_g_post_ffn, w_in=m_w_in, b_forget=m_b_forget, g_q_lora=m_g_q_lora, w_uq=m_w_uq, g_kv_lora=m_g_kv_lora,
             w_ukv=m_w_ukv, w_proj_fox=m_w_proj_fox, w_proj_mla=m_w_proj_mla, w_out=m_w_out, w_ffn_in=m_w_ffn_in,
             w_ffn_out=m_w_ffn_out)
    V = dict(w_ada=v_w_ada, b_ada=v_b_ada, g_pre_mix=v_g_pre_mix, g_post_mix=v_g_post_mix, g_pre_ffn=v_g_pre_ffn,
             g_post_ffn=v_g_post_ffn, w_in=v_w_in, b_forget=v_b_forget, g_q_lora=v_g_q_lora, w_uq=v_w_uq, g_kv_lora=v_g_kv_lora,
             w_ukv=v_w_ukv, w_proj_fox=v_w_proj_fox, w_proj_mla=v_w_proj_mla, w_out=v_w_out, w_ffn_in=v_w_ffn_in,
             w_ffn_out=v_w_ffn_out)
    ax, ay, ac = lax.axis_index("x"), lax.axis_index("y"), lax.axis_index("c")
    chip = 2 * ax + ay
    me = 4 * ax + 2 * ay + ac
    n_ada = w_ada.shape[2]

    c_all = _all_gather_rows(jnp.pad(c, ((0, 7), (0, 0))), name="gather_c")[:, 0, :]
    c_all = jnp.pad(c_all, ((0, 8), (0, 0)))
    b_shard = lax.dynamic_slice(b_ada, (0, chip * n_ada), (1, n_ada))
    mod_blk, silu_c = _ada_mod(c_all, w_ada[0], b_shard, name="ada_mod")
    mod_all = _all_gather_rows(mod_blk, name="gather_mod")
    mod_mine = lax.dynamic_index_in_dim(mod_all, me, axis=1, keepdims=False)
    mod = lax.dynamic_index_in_dim(mod_mine.reshape(4, 2, n_ada), ac, axis=1, keepdims=False).reshape(1, 6 * D)

    G = {n: _gather_weight(P[n][0], name="gather_" + n) for n in _BIG}
    W = _full_weights(G)

    gains = {n: P[n] for n in ["g_pre_mix", "g_post_mix", "g_pre_ffn", "g_post_ffn", "g_q_lora", "g_kv_lora"]}
    loss, grad_x, dW, small = _local_step(x[0], loss_target[0], mod, positions, gains, b_forget, W)

    shards = _grad_shards(dW)
    grads = {n: _reduce_scatter_weight(shards[n], name="scatter_" + n) for n in _BIG}

    small = dict(small, loss=_pad_block(loss))
    row = jnp.concatenate([small[n] for n in _ROW], axis=1)
    rows = _all_gather_rows(jnp.pad(row, ((0, 7), (0, 0))), name="gather_small")[:, 0, :]
    tot = _rowsum(rows, name="sum_small")
    piece = lambda n: tot[:, _ROW[n][0]:_ROW[n][0] + _ROW[n][1]]
    grads["b_ada"] = piece("dmod")
    for n in ["g_pre_mix", "g_post_mix", "g_pre_ffn", "g_post_ffn", "g_q_lora", "g_kv_lora"]:
        grads[n] = piece(n)
    grads["b_forget"] = piece("b_forget")[:, :H]
    loss_out = piece("loss")[0, 0]
    dmod_all = rows[:, _ROW["dmod"][0]:_ROW["dmod"][0] + 6 * D]
    dmod_shard = jnp.pad(lax.dynamic_slice(dmod_all, (0, chip * n_ada), (8, n_ada)), ((0, 8), (0, 0)))
    grads["w_ada"] = _matmul(silu_c, dmod_shard, ta=True, name="ada_dw")

    delta, new_m, new_v = {}, {}, {}
    for n in ["w_ada"] + _BIG:
        delta[n], new_m[n], new_v[n] = _adamw(P[n], grads[n], M[n], V[n], name="adamw_" + n)
    cat = lambda T: jnp.concatenate([T[n] for n in _SMALL], axis=1)
    d_s, m_s, v_s = (t[0] for t in _adamw(cat(P)[None], cat(grads), cat(M)[None], cat(V)[None], name="adamw_small"))
    o = 0
    for n in _SMALL:
        wdt = P[n].shape[1]
        delta[n], new_m[n], new_v[n] = d_s[:, o:o + wdt], m_s[:, o:o + wdt], v_s[:, o:o + wdt]
        o += wdt

    def shaped(T, n):
        return T[n].reshape(P[n].shape)

    return (loss_out, grad_x[None], *[shaped(grads, n) for n in _ORDER], *[shaped(delta, n) for n in _ORDER],
            *[shaped(new_m, n) for n in _ORDER], *[shaped(new_v, n) for n in _ORDER])
```

```python
import functools
import math
from typing import Callable, NamedTuple

import jax
import jax.numpy as jnp
from jax import lax
from jax.experimental import pallas as pl
from jax.experimental.pallas import tpu as pltpu

F32 = jnp.float32
BF16 = jnp.bfloat16
_MXU = jnp.bfloat16

S = 2048
D = 1024
H = 8
HP = 128
FOX_HD = 64
MLA_NOPE = 64
MLA_ROPE = 32
MLA_V = 64
Q_LORA = 768
KV_LORA = 256
D_FF = 2816
NORM_EPS = 1e-6
ROPE_THETA = 10000.0
NEG = -1e30

ADAM_LR = 0.001
ADAM_B1 = 0.9
ADAM_B2 = 0.999
ADAM_EPS = 1e-08
ADAM_WD = 0.01
ADAM_STEP = 10

LANES = 128
VMEM_CAP = 60 * 1024 * 1024
MESH = pl.DeviceIdType.MESH

NQKV = 3 * H * HP
OFF_CQ = 0
OFF_CKV = Q_LORA
OFF_GFOX = 1024
OFF_GMLA = 2048
OFF_FLOG = 3072
OFF_KRIN = 3200
NREST = 3328
KRIN_LANE = 64
ROW_SUM_LANE = 64
COL_SUM_LANE = 65


def _limit(nbytes):
    return int(min(VMEM_CAP, nbytes * 1.25 + (4 << 20)))


def _nbytes(shape, dtype):
    n = 1
    for s in shape:
        n *= s
    return n * jnp.dtype(dtype).itemsize


def _pick(n, cap):
    best = None
    for t in range(LANES, min(n, cap) + 1, LANES):
        if n % t == 0:
            best = t
    return best if best is not None else n


def _pcall(body, *, out_shape, **kw):
    outs = jax.tree.map(lambda s: pltpu.HBM(s.shape, s.dtype), out_shape)
    call = pl.pallas_call(body, out_shape=outs, **kw)
    return lambda *args: call(*[pltpu.with_memory_space_constraint(a, pltpu.HBM) for a in args])


def _matmul(a, b, *, ta=False, tb=False, out_dtype=F32, name, tm_cap=1024, tn_cap=512, tk_cap=1024,
            b_shards=False, out_shards=False):
    if ta:
        K, M = a.shape
    else:
        M, K = a.shape
    if b_shards:
        _, R, cb = b.shape
        N, K2 = (R, 4 * cb) if tb else (4 * cb, R)
    elif tb:
        N, K2 = b.shape
    else:
        K2, N = b.shape
    assert K == K2, (a.shape, b.shape, ta, tb)
    tm = _pick(M, tm_cap)
    tn = _pick(N, tn_cap)
    tk = K if K <= tk_cap else _pick(K, tk_cap)
    nk = K // tk
    dims = (((0 if ta else 1,), (1 if tb else 0,)), ((), ()))

    def body(a_ref, b_ref, o_ref, acc_ref):
        k = pl.program_id(2)

        @pl.when(k == 0)
        def _():
            acc_ref[...] = jnp.zeros_like(acc_ref)

        acc_ref[...] += lax.dot_general(a_ref[...].astype(_MXU), b_ref[...].astype(_MXU), dims,
                                        preferred_element_type=F32)

        @pl.when(k == nk - 1)
        def _():
            o_ref[...] = acc_ref[...].astype(out_dtype)

    a_spec = pl.BlockSpec((tk, tm), lambda i, j, k: (k, i)) if ta else pl.BlockSpec((tm, tk), lambda i, j, k: (i, k))
    if b_shards and tb:
        assert cb % tk == 0
        per = cb // tk
        b_spec = pl.BlockSpec((None, tn, tk), lambda i, j, k: (k // per, j, k % per))
    elif b_shards:
        assert cb % tn == 0
        per = cb // tn
        b_spec = pl.BlockSpec((None, tk, tn), lambda i, j, k: (j // per, k, j % per))
    elif tb:
        b_spec = pl.BlockSpec((tn, tk), lambda i, j, k: (j, k))
    else:
        b_spec = pl.BlockSpec((tk, tn), lambda i, j, k: (k, j))
    if out_shards:
        assert (N // 4) % tn == 0
        pern = N // 4 // tn
        out_shape = jax.ShapeDtypeStruct((4, M, N // 4), out_dtype)
        out_spec = pl.BlockSpec((None, tm, tn), lambda i, j, k: (j // pern, i, j % pern))
    else:
        out_shape = jax.ShapeDtypeStruct((M, N), out_dtype)
        out_spec = pl.BlockSpec((tm, tn), lambda i, j, k: (i, j))
    need = (2 * _nbytes((tm, tk), a.dtype) + 2 * _nbytes((tk, tn), b.dtype) + 2 * _nbytes((tm, tn), out_dtype)
            + _nbytes((tm, tn), F32) * 2 + _nbytes((tm, tk), _MXU) + _nbytes((tk, tn), _MXU))
    return _pcall(
        body, name=name,
        out_shape=out_shape,
        grid=(M // tm, N // tn, nk),
        in_specs=[a_spec, b_spec],
        out_specs=out_spec,
        scratch_shapes=[pltpu.VMEM((tm, tn), F32)],
        compiler_params=pltpu.CompilerParams(dimension_semantics=("parallel", "parallel", "arbitrary"),
                                             vmem_limit_bytes=_limit(need)),
    )(a, b)


TM = 256


def _vec(w):
    return pl.BlockSpec((1, w), lambda i: (0, 0))


def _rows(w, col=0):
    return pl.BlockSpec((TM, w), lambda i: (i, col))


def _row_params(need, carried=False):
    return pltpu.CompilerParams(dimension_semantics=("arbitrary" if carried else "parallel",),
                                vmem_limit_bytes=_limit(need))


def _norm_mod(x, col, w, g, ops, sh, *, name):
    def body(x_ref, g_ref, ops_ref, sh_ref, o_ref):
        xv = x_ref[...]
        r = lax.rsqrt(jnp.mean(xv * xv, axis=-1, keepdims=True) + NORM_EPS)
        o_ref[...] = (((xv * r) * g_ref[...]) * ops_ref[...] + sh_ref[...]).astype(o_ref.dtype)

    return _pcall(
        body, name=name, out_shape=jax.ShapeDtypeStruct((S, w), _MXU), grid=(S // TM,),
        in_specs=[_rows(w, col), _vec(w), _vec(w), _vec(w)], out_specs=_rows(w),
        compiler_params=_row_params(8 * _nbytes((TM, w), F32)),
    )(x, g, ops, sh)


def _norm_mod_bwd(x, col, w, dh, g, ops, dres, *, name):
    has_res = dres is not None

    def body(*refs):
        if has_res:
            x_ref, dh_ref, g_ref, ops_ref, dres_ref, dx_ref, s1_ref, s2_ref = refs
        else:
            x_ref, dh_ref, g_ref, ops_ref, dx_ref, s1_ref, s2_ref = refs
        i = pl.program_id(0)

        @pl.when(i == 0)
        def _():
            s1_ref[...] = jnp.zeros_like(s1_ref)
            s2_ref[...] = jnp.zeros_like(s2_ref)

        xv = x_ref[...]
        dhv = dh_ref[...]
        r = lax.rsqrt(jnp.mean(xv * xv, axis=-1, keepdims=True) + NORM_EPS)
        xn = xv * r
        dxn = dhv * (g_ref[...] * ops_ref[...])
        dx = r * (dxn - xn * jnp.mean(dxn * xn, axis=-1, keepdims=True))
        if has_res:
            dx = dx + dres_ref[...]
        dx_ref[...] = dx
        s1_ref[...] += jnp.sum(dhv, axis=0, keepdims=True)
        s2_ref[...] += jnp.sum(dhv * xn, axis=0, keepdims=True)

    in_specs = [_rows(w, col), _rows(w), _vec(w), _vec(w)] + ([_rows(w)] if has_res else [])
    args = (x, dh, g, ops) + ((dres,) if has_res else ())
    return _pcall(
        body, name=name,
        out_shape=(jax.ShapeDtypeStruct((S, w), F32), jax.ShapeDtypeStruct((1, w), F32), jax.ShapeDtypeStruct((1, w), F32)),
        grid=(S // TM,), in_specs=in_specs, out_specs=(_rows(w), _vec(w), _vec(w)),
        compiler_params=_row_params(12 * _nbytes((TM, w), F32), carried=True),
    )(*args)


def _post_res(xres, y, g, gt, *, name):
    def body(x_ref, y_ref, g_ref, gt_ref, o_ref):
        yv = y_ref[...]
        r = lax.rsqrt(jnp.mean(yv * yv, axis=-1, keepdims=True) + NORM_EPS)
        o_ref[...] = x_ref[...] + gt_ref[...] * ((yv * r) * g_ref[...])

    return _pcall(
        body, name=name, out_shape=jax.ShapeDtypeStruct((S, D), F32), grid=(S // TM,),
        in_specs=[_rows(D), _rows(D), _vec(D), _vec(D)], out_specs=_rows(D),
        compiler_params=_row_params(8 * _nbytes((TM, D), F32)),
    )(xres, y, g, gt)


def _post_res_loss(xres, y, g, gt, target, *, name):
    def body(x_ref, y_ref, g_ref, gt_ref, t_ref, dout_ref, loss_ref):
        i = pl.program_id(0)

        @pl.when(i == 0)
        def _():
            loss_ref[...] = jnp.zeros_like(loss_ref)

        yv = y_ref[...]
        r = lax.rsqrt(jnp.mean(yv * yv, axis=-1, keepdims=True) + NORM_EPS)
        out = x_ref[...] + gt_ref[...] * ((yv * r) * g_ref[...])
        err = out - t_ref[...]
        dout_ref[...] = err * (1.0 / D)
        per_row = jnp.mean(err * err, axis=-1, keepdims=True)
        loss_ref[...] += 0.5 * jnp.sum(per_row, axis=0, keepdims=True)

    return _pcall(
        body, name=name,
        out_shape=(jax.ShapeDtypeStruct((S, D), F32), jax.ShapeDtypeStruct((1, 1), F32)), grid=(S // TM,),
        in_specs=[_rows(D), _rows(D), _vec(D), _vec(D), _rows(D)],
        out_specs=(_rows(D), pl.BlockSpec((1, 1), lambda i: (0, 0))),
        compiler_params=_row_params(10 * _nbytes((TM, D), F32), carried=True),
    )(xres, y, g, gt, target)


def _post_res_bwd(dxn, y, g, gt, *, name):
    def body(d_ref, y_ref, g_ref, gt_ref, dy_ref, sgt_ref, sg_ref):
        i = pl.program_id(0)

        @pl.when(i == 0)
        def _():
            sgt_ref[...] = jnp.zeros_like(sgt_ref)
            sg_ref[...] = jnp.zeros_like(sg_ref)

        yv = y_ref[...]
        dv = d_ref[...]
        r = lax.rsqrt(jnp.mean(yv * yv, axis=-1, keepdims=True) + NORM_EPS)
        yn = yv * r
        dn = dv * gt_ref[...]
        dyn = dn * g_ref[...]
        dy_ref[...] = (r * (dyn - yn * jnp.mean(dyn * yn, axis=-1, keepdims=True))).astype(dy_ref.dtype)
        sgt_ref[...] += jnp.sum(dv * (yn * g_ref[...]), axis=0, keepdims=True)
        sg_ref[...] += jnp.sum(dn * yn, axis=0, keepdims=True)

    return _pcall(
        body, name=name,
        out_shape=(jax.ShapeDtypeStruct((S, D), _MXU), jax.ShapeDtypeStruct((1, D), F32), jax.ShapeDtypeStruct((1, D), F32)),
        grid=(S // TM,), in_specs=[_rows(D), _rows(D), _vec(D), _vec(D)], out_specs=(_rows(D), _vec(D), _vec(D)),
        compiler_params=_row_params(10 * _nbytes((TM, D), F32), carried=True),
    )(dxn, y, g, gt)


def _swiglu(gu, *, name):
    def body(g_ref, u_ref, o_ref):
        gv = g_ref[...]
        o_ref[...] = ((gv * jax.nn.sigmoid(gv)) * u_ref[...]).astype(o_ref.dtype)

    return _pcall(
        body, name=name, out_shape=jax.ShapeDtypeStruct((S, D_FF), _MXU), grid=(S // TM,),
        in_specs=[_rows(D_FF, 0), _rows(D_FF, 1)], out_specs=_rows(D_FF),
        compiler_params=_row_params(8 * _nbytes((TM, D_FF), F32)),
    )(gu, gu)


def _swiglu_bwd(gu, dact, *, name):
    def body(g_ref, u_ref, d_ref, o_ref):
        gv = g_ref[...]
        dv = d_ref[...]
        sg = jax.nn.sigmoid(gv)
        o_ref[:, :D_FF] = (dv * u_ref[...] * (sg * (1.0 + gv * (1.0 - sg)))).astype(o_ref.dtype)
        o_ref[:, D_FF:] = (dv * (gv * sg)).astype(o_ref.dtype)

    return _pcall(
        body, name=name, out_shape=jax.ShapeDtypeStruct((S, 2 * D_FF), _MXU), grid=(S // TM,),
        in_specs=[_rows(D_FF, 0), _rows(D_FF, 1), _rows(D_FF)], out_specs=_rows(2 * D_FF),
        compiler_params=_row_params(12 * _nbytes((TM, D_FF), F32)),
    )(gu, gu, dact)


def _merge(rest, pa, pb, *, name):
    def body(ga_ref, gb_ref, pa_ref, pb_ref, o_ref):
        o_ref[...] = (jax.nn.sigmoid(ga_ref[...]) * pa_ref[...] + jax.nn.sigmoid(gb_ref[...]) * pb_ref[...]).astype(o_ref.dtype)

    return _pcall(
        body, name=name, out_shape=jax.ShapeDtypeStruct((S, D), _MXU), grid=(S // TM,),
        in_specs=[_rows(D, OFF_GFOX // D), _rows(D, OFF_GMLA // D), _rows(D), _rows(D)], out_specs=_rows(D),
        compiler_params=_row_params(10 * _nbytes((TM, D), F32)),
    )(rest, rest, pa, pb)


def _merge_bwd(rest, pa, pb, dm, *, name):
    def body(ga_ref, gb_ref, pa_ref, pb_ref, d_ref, dpa_ref, dpb_ref, dga_ref, dgb_ref):
        dv = d_ref[...]
        sa = jax.nn.sigmoid(ga_ref[...])
        sb = jax.nn.sigmoid(gb_ref[...])
        dpa_ref[...] = (dv * sa).astype(dpa_ref.dtype)
        dpb_ref[...] = (dv * sb).astype(dpb_ref.dtype)
        dga_ref[...] = (dv * pa_ref[...] * (sa * (1.0 - sa))).astype(dga_ref.dtype)
        dgb_ref[...] = (dv * pb_ref[...] * (sb * (1.0 - sb))).astype(dgb_ref.dtype)

    o = jax.ShapeDtypeStruct((S, D), _MXU)
    return _pcall(
        body, name=name, out_shape=(o, o, o, o), grid=(S // TM,),
        in_specs=[_rows(D, OFF_GFOX // D), _rows(D, OFF_GMLA // D), _rows(D), _rows(D), _rows(D)],
        out_specs=(_rows(D), _rows(D), _rows(D), _rows(D)),
        compiler_params=_row_params(16 * _nbytes((TM, D), F32)),
    )(rest, rest, pa, pb, dm)


SCAN = 256


def _split_dot(tri, x):
    hi = x.astype(_MXU)
    r1 = x - hi.astype(F32)
    mid = r1.astype(_MXU)
    lo = (r1 - mid.astype(F32)).astype(_MXU)
    dot = functools.partial(jnp.dot, preferred_element_type=F32)
    return dot(tri, hi) + dot(tri, mid) + dot(tri, lo)


def _fox_prep(rest, bf, *, name):
    def body(z_ref, b_ref, f_ref):
        lane = lax.broadcasted_iota(jnp.int32, (SCAN, LANES), 1)
        tri = (lax.broadcasted_iota(jnp.int32, (SCAN, SCAN), 1) <= lax.broadcasted_iota(jnp.int32, (SCAN, SCAN), 0)).astype(_MXU)
        carry = jnp.zeros((1, LANES), F32)
        for c in range(S // SCAN):
            z = z_ref[c * SCAN:(c + 1) * SCAN, :] + b_ref[...]
            lf = jnp.minimum(z, 0.0) - jnp.log(1.0 + jnp.exp(-jnp.abs(z)))
            lf = jnp.where(lane < H, lf, 0.0)
            cum = _split_dot(tri, lf) + carry
            f_ref[c * SCAN:(c + 1) * SCAN, :] = cum
            carry = cum[SCAN - 1:SCAN, :]

    return _pcall(
        body, name=name, out_shape=jax.ShapeDtypeStruct((S, LANES), F32), grid=(1,),
        in_specs=[pl.BlockSpec((S, LANES), lambda i: (0, OFF_FLOG // LANES)), pl.BlockSpec((1, LANES), lambda i: (0, 0))],
        out_specs=pl.BlockSpec((S, LANES), lambda i: (0, 0)),
        compiler_params=pltpu.CompilerParams(vmem_limit_bytes=_limit(8 * _nbytes((S, LANES), F32))),
    )(rest, bf)


def _fox_bwd_prep(rest, bf, dF, *, name):
    def body(z_ref, b_ref, d_ref, o_ref, db_ref):
        lane = lax.broadcasted_iota(jnp.int32, (SCAN, LANES), 1)
        tri = (lax.broadcasted_iota(jnp.int32, (SCAN, SCAN), 1) >= lax.broadcasted_iota(jnp.int32, (SCAN, SCAN), 0)).astype(_MXU)
        carry = jnp.zeros((1, LANES), F32)
        db = jnp.zeros((1, LANES), F32)
        for c in range(S // SCAN - 1, -1, -1):
            rc = _split_dot(tri, d_ref[c * SCAN:(c + 1) * SCAN, :]) + carry
            z = z_ref[c * SCAN:(c + 1) * SCAN, :] + b_ref[...]
            dz = jnp.where(lane < H, rc * jax.nn.sigmoid(-z), 0.0)
            o_ref[c * SCAN:(c + 1) * SCAN, :] = dz
            db = db + jnp.sum(dz, axis=0, keepdims=True)
            carry = rc[0:1, :]
        db_ref[...] = db

    return _pcall(
        body, name=name,
        out_shape=(jax.ShapeDtypeStruct((S, LANES), F32), jax.ShapeDtypeStruct((1, LANES), F32)), grid=(1,),
        in_specs=[pl.BlockSpec((S, LANES), lambda i: (0, OFF_FLOG // LANES)), pl.BlockSpec((1, LANES), lambda i: (0, 0)),
                  pl.BlockSpec((S, LANES), lambda i: (0, 0))],
        out_specs=(pl.BlockSpec((S, LANES), lambda i: (0, 0)), pl.BlockSpec((1, LANES), lambda i: (0, 0))),
        compiler_params=pltpu.CompilerParams(vmem_limit_bytes=_limit(10 * _nbytes((S, LANES), F32))),
    )(rest, bf, dF)


def _swap16(x):
    lane = lax.broadcasted_iota(jnp.int32, x.shape, 1)
    half = MLA_ROPE // 2
    sw = jnp.where(lane < KRIN_LANE + half, pltpu.roll(x, LANES - half, 1), pltpu.roll(x, half, 1))
    return jnp.where((lane >= KRIN_LANE) & (lane < KRIN_LANE + MLA_ROPE), sw, 0.0)


def _mla_assemble(qb, kvb, rest, ctab, stab, *, name):
    def body(q_ref, kk_ref, kv_ref, kr_ref, c_ref, s_ref, qo_ref, ko_ref, vo_ref):
        cv = c_ref[...]
        sv = s_ref[...]
        kr = kr_ref[...]
        kpe = kr * cv + _swap16(kr) * sv
        for h in range(H):
            sl = slice(h * HP, (h + 1) * HP)
            qh = q_ref[:, sl]
            qo_ref[:, sl] = (qh * cv + _swap16(qh) * sv).astype(qo_ref.dtype)
            ko_ref[:, sl] = (kk_ref[:, sl] + kpe).astype(ko_ref.dtype)
        vo_ref[...] = kv_ref[...].astype(vo_ref.dtype)

    o = jax.ShapeDtypeStruct((S, H * HP), _MXU)
    return _pcall(
        body, name=name, out_shape=(o, o, o), grid=(S // TM,),
        in_specs=[_rows(H * HP), _rows(H * HP, 0), _rows(H * HP, 1), _rows(LANES, OFF_KRIN // LANES), _rows(LANES), _rows(LANES)],
        out_specs=(_rows(H * HP), _rows(H * HP), _rows(H * HP)),
        compiler_params=_row_params(14 * _nbytes((TM, H * HP), F32)),
    )(qb, kvb, kvb, rest, ctab, stab)


def _mla_assemble_bwd(dq, dk, dv, ctab, stab, *, name):
    def body(dq_ref, dk_ref, dv_ref, c_ref, s_ref, dqo_ref, dkv_ref, dkr_ref):
        cv = c_ref[...]
        sv = s_ref[...]
        lane = lax.broadcasted_iota(jnp.int32, (TM, LANES), 1)
        dsum = jnp.zeros((TM, LANES), F32)
        for h in range(H):
            sl = slice(h * HP, (h + 1) * HP)
            dqh = dq_ref[:, sl]
            dqo_ref[:, sl] = (dqh * cv + _swap16(dqh * sv)).astype(dqo_ref.dtype)
            dsum = dsum + dk_ref[:, sl]
        dkv_ref[:, :H * HP] = dk_ref[...].astype(dkv_ref.dtype)
        dkv_ref[:, H * HP:] = dv_ref[...].astype(dkv_ref.dtype)
        dkr = dsum * cv + _swap16(dsum * sv)
        dkr_ref[...] = jnp.where((lane >= KRIN_LANE) & (lane < KRIN_LANE + MLA_ROPE), dkr, 0.0)

    return _pcall(
        body, name=name,
        out_shape=(jax.ShapeDtypeStruct((S, H * HP), _MXU), jax.ShapeDtypeStruct((S, 2 * H * HP), _MXU),
                   jax.ShapeDtypeStruct((S, LANES), F32)),
        grid=(S // TM,),
        in_specs=[_rows(H * HP), _rows(H * HP), _rows(H * HP), _rows(LANES), _rows(LANES)],
        out_specs=(_rows(H * HP), _rows(2 * H * HP), _rows(LANES)),
        compiler_params=_row_params(14 * _nbytes((TM, H * HP), F32)),
    )(dq, dk, dv, ctab, stab)


TQ = 256
TKF = 512
TKB = 256
TQB = 512
_NT = (((1,), (1,)), ((), ()))
_TN = (((0,), (0,)), ((), ()))


def _is_pow2(x):
    return math.frexp(x)[0] == 0.5


def _attn_fwd(q, k, v, frow, *, scale, name, side=None):
    has_decay = frow is not None
    fold = _is_pow2(scale)
    n_in = 4 if has_decay else 3
    n_side_in = len(side.inputs) if side else 0
    n_side_out = len(side.out_shapes) if side else 0

    def body(*refs):
        q_ref, k_ref, v_ref = refs[:3]
        fr_ref = refs[3] if has_decay else None
        side_in = refs[n_in:n_in + n_side_in]
        o_ref, lse_ref = refs[n_in + n_side_in:n_in + n_side_in + 2]
        side_out = refs[n_in + n_side_in + 2:n_in + n_side_in + 2 + n_side_out]
        side_scratch = refs[n_in + n_side_in + 2 + n_side_out:]
        i = pl.program_id(1)
        if side:
            @pl.when((pl.program_id(0) == 0) & (i == 0))
            def _():
                side.start(side_in, side_out, side_scratch)
        qv = q_ref[...]
        if fold:
            qv = (qv * scale).astype(qv.dtype)
        last = (i * TQ) // TKF

        def tile(j, carry, masked):
            m, l, acc = carry
            k0 = pl.multiple_of(j * TKF, TKF)
            kj = k_ref[pl.ds(k0, TKF), :]
            vj = v_ref[pl.ds(k0, TKF), :]
            s = lax.dot_general(qv, kj, _NT, preferred_element_type=F32)
            if not fold:
                s = s * scale
            if has_decay:
                s = s - fr_ref[0, j]
            if masked:
                rows = i * TQ + lax.broadcasted_iota(jnp.int32, (TQ, TKF), 0)
                cols = j * TKF + lax.broadcasted_iota(jnp.int32, (TQ, TKF), 1)
                s = jnp.where(cols <= rows, s, NEG)
            m_new = jnp.maximum(m, jnp.max(s, axis=-1, keepdims=True))
            alpha = jnp.exp(m - m_new)
            p = jnp.exp(s - m_new)
            l = alpha * l + jnp.sum(p, axis=-1, keepdims=True)
            acc = alpha * acc + jnp.dot(p.astype(_MXU), vj, preferred_element_type=F32)
            return m_new, l, acc

        init = (jnp.full((TQ, 1), NEG, F32), jnp.zeros((TQ, 1), F32), jnp.zeros((TQ, HP), F32))
        carry = lax.fori_loop(0, last, lambda j, c: tile(j, c, False), init)
        m, l, acc = tile(last, carry, True)
        o_ref[...] = acc / l
        lse_ref[0] = m + jnp.log(l)
        if side:
            @pl.when((pl.program_id(0) == H - 1) & (i == S // TQ - 1))
            def _():
                side.finish(side_in, side_out, side_scratch)

    in_specs = [pl.BlockSpec((TQ, HP), lambda h, i: (i, h)), pl.BlockSpec((S, HP), lambda h, i: (0, h)),
                pl.BlockSpec((S, HP), lambda h, i: (0, h))]
    args = (q, k, v)
    if has_decay:
        in_specs += [pl.BlockSpec((1, S // TKF, 1, TKF), lambda h, i: (h, 0, 0, 0))]
        args += (frow,)
    out_shape = (jax.ShapeDtypeStruct((S, H * HP), F32), jax.ShapeDtypeStruct((H, S, 1), F32))
    out_specs = (pl.BlockSpec((TQ, HP), lambda h, i: (i, h)), pl.BlockSpec((1, TQ, 1), lambda h, i: (h, i, 0)))
    extra = {}
    if side:
        anywhere = pl.BlockSpec(memory_space=pl.ANY)
        in_specs += [anywhere] * n_side_in
        args += tuple(side.inputs)
        out_shape += tuple(side.out_shapes)
        out_specs += (anywhere,) * n_side_out
        extra = dict(scratch_shapes=list(side.scratch),
                     input_output_aliases={n_in + a: 2 + b for a, b in side.aliases.items()})
    return _pcall(
        body, name=name, out_shape=out_shape, grid=(H, S // TQ), in_specs=in_specs, out_specs=out_specs,
        compiler_params=pltpu.CompilerParams(dimension_semantics=("arbitrary", "arbitrary") if side else ("parallel", "parallel"),
                                             vmem_limit_bytes=_limit(8 * _nbytes((S, HP), F32))),
        **extra,
    )(*args)


def _attn_delta(do, o, *, name):
    def body(do_ref, o_ref, dl_ref, dob_ref):
        for h in range(H):
            sl = slice(h * HP, (h + 1) * HP)
            dl_ref[h] = jnp.sum(do_ref[:, sl] * o_ref[:, sl], axis=-1, keepdims=True)
        dob_ref[...] = do_ref[...].astype(dob_ref.dtype)

    return _pcall(
        body, name=name,
        out_shape=(jax.ShapeDtypeStruct((H, S, 1), F32), jax.ShapeDtypeStruct((S, H * HP), _MXU)), grid=(S // TM,),
        in_specs=[_rows(H * HP), _rows(H * HP)],
        out_specs=(pl.BlockSpec((H, TM, 1), lambda i: (0, i, 0)), _rows(H * HP)),
        compiler_params=_row_params(8 * _nbytes((TM, H * HP), F32)),
    )(do, o)


def _attn_bwd(q, k, v, dob, lse_row, delta_row, fcol, *, scale, name):
    has_decay = fcol is not None
    fold = _is_pow2(scale)

    def body(*refs):
        if has_decay:
            q_ref, k_ref, v_ref, do_ref, lse_ref, dl_ref, fc_ref, dq_ref, dk_ref, dv_ref, dq_acc = refs
        else:
            q_ref, k_ref, v_ref, do_ref, lse_ref, dl_ref, dq_ref, dk_ref, dv_ref, dq_acc = refs
        j = pl.program_id(1)

        @pl.when(j == 0)
        def _():
            dq_acc[...] = jnp.zeros_like(dq_acc)

        kj = k_ref[...]
        vj = v_ref[...]
        kjs = (kj * scale).astype(kj.dtype) if fold else kj
        if has_decay:
            klane = lax.broadcasted_iota(jnp.int32, (TKB, HP), 1)
            kj = jnp.where(klane == ROW_SUM_LANE, 1.0, kj).astype(kj.dtype)
        first = (j * TKB) // TQB

        def tile(t, carry, masked):
            dk, dv = carry
            r0 = pl.multiple_of(t * TQB, TQB)
            qi = q_ref[pl.ds(r0, TQB), :]
            doi = do_ref[pl.ds(r0, TQB), :]
            st = lax.dot_general(kjs, qi, _NT, preferred_element_type=F32)
            if not fold:
                st = st * scale
            if has_decay:
                st = st - fc_ref[0]
            if masked:
                keys = j * TKB + lax.broadcasted_iota(jnp.int32, (TKB, TQB), 0)
                qpos = t * TQB + lax.broadcasted_iota(jnp.int32, (TKB, TQB), 1)
                st = jnp.where(keys <= qpos, st, NEG)
            pt = jnp.exp(st - lse_ref[0, t])
            dv = dv + jnp.dot(pt.astype(_MXU), doi, preferred_element_type=F32)
            dpt = lax.dot_general(vj, doi, _NT, preferred_element_type=F32)
            dst = (pt * (dpt - dl_ref[0, t])).astype(_MXU)
            if has_decay:
                lane = lax.broadcasted_iota(jnp.int32, (TQB, HP), 1)
                qi = jnp.where(lane == COL_SUM_LANE, 1.0, qi).astype(qi.dtype)
            dk = dk + jnp.dot(dst, qi, preferred_element_type=F32)
            dq_acc[pl.ds(r0, TQB), :] += lax.dot_general(dst, kj, _TN, preferred_element_type=F32)
            return dk, dv

        zero = jnp.zeros((TKB, HP), F32)
        carry = tile(first, (zero, zero), True)
        dk, dv = lax.fori_loop(first + 1, S // TQB, lambda t, c: tile(t, c, False), carry)
        dk_ref[...] = dk * scale
        dv_ref[...] = dv

        @pl.when(j == S // TKB - 1)
        def _():
            dq_ref[...] = dq_acc[...] * scale

    head = pl.BlockSpec((S, HP), lambda h, j: (0, h))
    kv = pl.BlockSpec((TKB, HP), lambda h, j: (j, h))
    stat = pl.BlockSpec((1, S // TQB, 1, TQB), lambda h, j: (h, 0, 0, 0))
    in_specs = [head, kv, kv, head, stat, stat]
    args = (q, k, v, dob, lse_row, delta_row)
    if has_decay:
        in_specs += [pl.BlockSpec((1, TKB, 1), lambda h, j: (h, j, 0))]
        args += (fcol,)
    o = jax.ShapeDtypeStruct((S, H * HP), F32)
    return _pcall(
        body, name=name, out_shape=(o, o, o), grid=(H, S // TKB), in_specs=in_specs,
        out_specs=(head, kv, kv), scratch_shapes=[pltpu.VMEM((S, HP), F32)],
        compiler_params=pltpu.CompilerParams(dimension_semantics=("parallel", "arbitrary"),
                                             vmem_limit_bytes=_limit(12 * _nbytes((S, HP), F32))),
    )(*args)


def _ada_mod(c_all, w_shard, b_shard, *, name):
    R = c_all.shape[0]
    N = w_shard.shape[1]
    tn = 512

    def body(c_ref, w_ref, b_ref, o_ref, sc_ref):
        cv = c_ref[...]
        sc = (cv * jax.nn.sigmoid(cv)).astype(_MXU)
        sc_ref[...] = sc
        o_ref[...] = jnp.dot(sc, w_ref[...].astype(_MXU), preferred_element_type=F32) + b_ref[...]

    return _pcall(
        body, name=name,
        out_shape=(jax.ShapeDtypeStruct((R, N), F32), jax.ShapeDtypeStruct((R, D), _MXU)), grid=(N // tn,),
        in_specs=[pl.BlockSpec((R, D), lambda j: (0, 0)), pl.BlockSpec((D, tn), lambda j: (0, j)), pl.BlockSpec((1, tn), lambda j: (0, j))],
        out_specs=(pl.BlockSpec((R, tn), lambda j: (0, j)), pl.BlockSpec((R, D), lambda j: (0, 0))),
        compiler_params=pltpu.CompilerParams(dimension_semantics=("arbitrary",), vmem_limit_bytes=_limit(6 * _nbytes((D, tn), F32))),
    )(c_all, w_shard, b_shard)


def _rowsum(x, *, name):
    R, L = x.shape

    def body(x_ref, o_ref):
        acc = x_ref[0:1, :]
        for r in range(1, R):
            acc = acc + x_ref[r:r + 1, :]
        o_ref[...] = acc

    return pl.pallas_call(body, name=name, out_shape=jax.ShapeDtypeStruct((1, L), F32),
                          in_specs=[pl.BlockSpec(memory_space=pltpu.VMEM)], out_specs=pl.BlockSpec(memory_space=pltpu.VMEM))(x)


def _adamw(w, g, m, v, *, name):
    _, R, C = w.shape
    tr = R
    for t in range(8, R + 1, 8):
        if R % t == 0 and t * C * 4 <= (1 << 20):
            tr = t

    def body(w_ref, g_ref, m_ref, v_ref, d_ref, mo_ref, vo_ref):
        gv = g_ref[...]
        m2 = ADAM_B1 * m_ref[...] + (1.0 - ADAM_B1) * gv
        v2 = ADAM_B2 * v_ref[...] + (1.0 - ADAM_B2) * (gv * gv)
        m_hat = m2 / (1.0 - ADAM_B1 ** ADAM_STEP)
        v_hat = v2 / (1.0 - ADAM_B2 ** ADAM_STEP)
        d_ref[...] = -ADAM_LR * (m_hat / (jnp.sqrt(v_hat) + ADAM_EPS) + ADAM_WD * w_ref[...])
        mo_ref[...] = m2
        vo_ref[...] = v2

    blk = pl.BlockSpec((None, tr, C), lambda i: (0, i, 0))
    o = jax.ShapeDtypeStruct((1, R, C), F32)
    return _pcall(
        body, name=name, out_shape=(o, o, o), grid=(R // tr,),
        in_specs=[blk, pl.BlockSpec((tr, C), lambda i: (i, 0)), blk, blk], out_specs=(blk, blk, blk),
        compiler_params=pltpu.CompilerParams(dimension_semantics=("parallel",), vmem_limit_bytes=_limit(20 * _nbytes((tr, C), F32))),
    )(w, g, m, v)


def _place():
    x, y, c = lax.axis_index("x"), lax.axis_index("y"), lax.axis_index("c")
    return x, y, c, [(1 - x, y), (x, 1 - y), (1 - x, 1 - y)]


def _two_level_gather(x_ref, out_ref, send_sems, recv_sems, local_sem):
    x, y, c, chips = _place()
    me, sibling = (x, y, c), (x, y, 1 - c)

    def blk(px, py, pc):
        return out_ref.at[4 * px + 2 * py + pc]

    def copy(k, block, to, src=None):
        return pltpu.make_async_remote_copy(
            src_ref=blk(*block) if src is None else src, dst_ref=blk(*block),
            send_sem=send_sems.at[k], recv_sem=recv_sems.at[k], device_id=to, device_id_type=MESH)

    mine = pltpu.make_async_copy(x_ref, blk(*me), local_sem)
    mine.start()
    first = [copy(0, me, sibling, src=x_ref)]
    first += [copy(1 + j, me, (*chip, c), src=x_ref) for j, chip in enumerate(chips)]
    for cp in first:
        cp.start()
    passed = [copy(4 + j, (*chip, c), sibling) for j, chip in enumerate(chips)]
    for j, chip in enumerate(chips):
        copy(1 + j, (*chip, c), me).wait_recv()
        passed[j].start()
    copy(0, sibling, me).wait_recv()
    for j, chip in enumerate(chips):
        copy(4 + j, (*chip, 1 - c), me).wait_recv()
    for cp in first + passed:
        cp.wait_send()
    mine.wait()


_GATHER_SEMS = [pltpu.SemaphoreType.DMA((7,)), pltpu.SemaphoreType.DMA((7,)), pltpu.SemaphoreType.DMA]


class _SideJob(NamedTuple):
    inputs: tuple
    out_shapes: tuple
    aliases: dict
    scratch: tuple
    start: Callable
    finish: Callable


def _block_index(px, py, pc):
    return 4 * px + 2 * py + pc


def _gather_level1_job(halves):
    nw = len(halves)

    def copies(ins, outs, scratch, n):
        sends, recvs, _ = scratch
        x, y, c, chips = _place()
        mine = outs[n].at[_block_index(x, y, c)]
        peers = [(x, y, 1 - c)] + [(*chip, c) for chip in chips]
        out = []
        for t, peer in enumerate(peers):
            sem = dict(send_sem=sends.at[4 * n + t], recv_sem=recvs.at[4 * n + t], device_id_type=MESH)
            landing = outs[n].at[_block_index(*peer)]
            out.append((pltpu.make_async_remote_copy(src_ref=ins[n], dst_ref=mine, device_id=peer, **sem),
                        pltpu.make_async_remote_copy(src_ref=landing, dst_ref=landing, device_id=peer, **sem)))
        local = pltpu.make_async_copy(ins[n], mine, scratch[2].at[n])
        return out, local

    def start(ins, outs, scratch):
        for n in range(nw):
            pairs, local = copies(ins, outs, scratch, n)
            local.start()
            for to, _ in pairs:
                to.start()

    def finish(ins, outs, scratch):
        for n in range(nw):
            pairs, local = copies(ins, outs, scratch, n)
            for to, frm in pairs:
                frm.wait_recv()
                to.wait_send()
            local.wait()

    return _SideJob(
        inputs=tuple(halves), out_shapes=tuple(jax.ShapeDtypeStruct((8,) + h.shape, h.dtype) for h in halves), aliases={},
        scratch=(pltpu.SemaphoreType.DMA((4 * nw,)), pltpu.SemaphoreType.DMA((4 * nw,)), pltpu.SemaphoreType.DMA((nw,))),
        start=start, finish=finish)


def _gather_level2_job(gathered):
    nw = len(gathered)

    def copies(outs, scratch, n):
        sends, recvs = scratch
        x, y, c, chips = _place()
        out = []
        for j, chip in enumerate(chips):
            sem = dict(send_sem=sends.at[3 * n + j], recv_sem=recvs.at[3 * n + j], device_id=(x, y, 1 - c), device_id_type=MESH)
            going = outs[n].at[_block_index(*chip, c)]
            landing = outs[n].at[_block_index(*chip, 1 - c)]
            out.append((pltpu.make_async_remote_copy(src_ref=going, dst_ref=going, **sem),
                        pltpu.make_async_remote_copy(src_ref=landing, dst_ref=landing, **sem)))
        return out

    def start(ins, outs, scratch):
        for n in range(nw):
            for to, _ in copies(outs, scratch, n):
                to.start()

    def finish(ins, outs, scratch):
        for n in range(nw):
            for to, frm in copies(outs, scratch, n):
                frm.wait_recv()
                to.wait_send()

    return _SideJob(
        inputs=tuple(gathered), out_shapes=tuple(jax.ShapeDtypeStruct(g.shape, g.dtype) for g in gathered),
        aliases={n: n for n in range(nw)},
        scratch=(pltpu.SemaphoreType.DMA((3 * nw,)), pltpu.SemaphoreType.DMA((3 * nw,))),
        start=start, finish=finish)


def _all_gather_rows(x, *, name):
    R, C = x.shape

    def body(x_ref, out_ref, send_sems, recv_sems, local_sem):
        _two_level_gather(x_ref, out_ref, send_sems, recv_sems, local_sem)

    return pl.pallas_call(
        body, name=name, out_shape=jax.ShapeDtypeStruct((8, R, C), x.dtype),
        in_specs=[pl.BlockSpec(memory_space=pltpu.VMEM)], out_specs=pl.BlockSpec(memory_space=pltpu.VMEM),
        scratch_shapes=list(_GATHER_SEMS),
        compiler_params=pltpu.CompilerParams(vmem_limit_bytes=_limit(10 * _nbytes((R, C), x.dtype))),
    )(x)


CAST_ROWS = 16


def _gather_weight(w, *, name):
    r, c = w.shape
    rh = r // 2
    assert rh % CAST_ROWS == 0

    def body(w_hbm, out_ref, tmp, xb, send_sems, recv_sems, local_sem):
        core = lax.axis_index("c")
        ld = pltpu.make_async_copy(w_hbm.at[pl.ds(pl.multiple_of(core * rh, CAST_ROWS), rh), :], tmp, local_sem)
        ld.start()
        ld.wait()

        def cast(i, carry):
            r0 = pl.multiple_of(i * CAST_ROWS, CAST_ROWS)
            xb[pl.ds(r0, CAST_ROWS), :] = tmp[pl.ds(r0, CAST_ROWS), :].astype(BF16)
            return carry

        lax.fori_loop(0, rh // CAST_ROWS, cast, 0)
        _two_level_gather(xb, out_ref, send_sems, recv_sems, local_sem)

    need = _nbytes((8, rh, c), BF16) + _nbytes((rh, c), F32) + _nbytes((rh, c), BF16)
    out = pl.pallas_call(
        body, name=name, out_shape=jax.ShapeDtypeStruct((8, rh, c), BF16),
        in_specs=[pl.BlockSpec(memory_space=pl.ANY)], out_specs=pl.BlockSpec(memory_space=pltpu.VMEM),
        scratch_shapes=[pltpu.VMEM((rh, c), F32), pltpu.VMEM((rh, c), BF16)] + list(_GATHER_SEMS),
        compiler_params=pltpu.CompilerParams(vmem_limit_bytes=_limit(need * 1.3)),
    )(w)
    return out.reshape(4, r, c)


def _reduce_scatter_weight(g4, *, name):
    _, r, c = g4.shape
    rh = r // 2
    assert rh % CAST_ROWS == 0
    nsteps = rh // CAST_ROWS

    def body(g_hbm, out_ref, mine, tmp, sbuf, rbuf_a, rbuf_b, a_send, a_recv, b_send, b_recv, c_send, c_recv, lsem):
        x, y, core, chips = _place()
        sibling = (x, y, 1 - core)
        k = 2 * x + y
        my0 = pl.multiple_of(core * rh, CAST_ROWS)
        ot0 = pl.multiple_of((1 - core) * rh, CAST_ROWS)

        ld = pltpu.make_async_copy(g_hbm.at[:, pl.ds(my0, rh), :], mine, lsem)
        ld.start()
        ld.wait()
        for j in range(4):
            ldj = pltpu.make_async_copy(g_hbm.at[j, pl.ds(ot0, rh), :], tmp, lsem)
            ldj.start()
            ldj.wait()

            def cast(i, carry, j=j):
                r0 = pl.multiple_of(i * CAST_ROWS, CAST_ROWS)
                sbuf[j, pl.ds(r0, CAST_ROWS), :] = tmp[pl.ds(r0, CAST_ROWS), :].astype(BF16)
                return carry

            lax.fori_loop(0, nsteps, cast, 0)

        to_sib = pltpu.make_async_remote_copy(src_ref=sbuf, dst_ref=rbuf_a, send_sem=a_send, recv_sem=a_recv,
                                              device_id=sibling, device_id_type=MESH)
        to_sib.start()
        to_sib.wait()

        for j in range(4):
            def add(i, carry, j=j):
                r0 = pl.multiple_of(i * CAST_ROWS, CAST_ROWS)
                s = mine[j, pl.ds(r0, CAST_ROWS), :] + rbuf_a[j, pl.ds(r0, CAST_ROWS), :].astype(F32)
                mine[j, pl.ds(r0, CAST_ROWS), :] = s
                sbuf[j, pl.ds(r0, CAST_ROWS), :] = s.astype(BF16)
                return carry

            lax.fori_loop(0, nsteps, add, 0)

        sends = []
        for d, (px, py) in enumerate(chips):
            cp = pltpu.make_async_remote_copy(src_ref=sbuf.at[2 * px + py], dst_ref=rbuf_b.at[d], send_sem=b_send.at[d],
                                              recv_sem=b_recv.at[d], device_id=(px, py, core), device_id_type=MESH)
            cp.start()
            sends.append(cp)
        for cp in sends:
            cp.wait()

        def fin(i, carry):
            r0 = pl.multiple_of(i * CAST_ROWS, CAST_ROWS)
            s = mine[k, pl.ds(r0, CAST_ROWS), :]
            for d in range(3):
                s = s + rbuf_b[d, pl.ds(r0, CAST_ROWS), :].astype(F32)
            out_ref[pl.ds(my0 + r0, CAST_ROWS), :] = s
            return carry

        lax.fori_loop(0, nsteps, fin, 0)
        half = out_ref.at[pl.ds(my0, rh), :]
        swap = pltpu.make_async_remote_copy(src_ref=half, dst_ref=half, send_sem=c_send, recv_sem=c_recv,
                                            device_id=sibling, device_id_type=MESH)
        swap.start()
        swap.wait()

    need = (_nbytes((4, rh, c), F32) + _nbytes((rh, c), F32) + 2 * _nbytes((4, rh, c), BF16) + _nbytes((3, rh, c), BF16)
            + _nbytes((r, c), F32))
    return pl.pallas_call(
        body, name=name, out_shape=jax.ShapeDtypeStruct((r, c), F32),
        in_specs=[pl.BlockSpec(memory_space=pl.ANY)], out_specs=pl.BlockSpec(memory_space=pltpu.VMEM),
        scratch_shapes=[pltpu.VMEM((4, rh, c), F32), pltpu.VMEM((rh, c), F32), pltpu.VMEM((4, rh, c), BF16),
                        pltpu.VMEM((4, rh, c), BF16), pltpu.VMEM((3, rh, c), BF16),
                        pltpu.SemaphoreType.DMA, pltpu.SemaphoreType.DMA, pltpu.SemaphoreType.DMA((3,)),
                        pltpu.SemaphoreType.DMA((3,)), pltpu.SemaphoreType.DMA, pltpu.SemaphoreType.DMA,
                        pltpu.SemaphoreType.DMA],
        compiler_params=pltpu.CompilerParams(vmem_limit_bytes=_limit(need * 1.2)),
    )(g4)


def _cols_from_shards(g):
    n, K, c = g.shape
    return g.transpose(1, 0, 2).reshape(K, n * c)


def _cols_to_shards(w):
    K, N = w.shape
    return w.reshape(K, 4, N // 4).transpose(1, 0, 2)


def _pad_heads_cols(w, width, lane0=0):
    K = w.shape[0]
    w3 = w.reshape(K, H, width)
    return jnp.pad(w3, ((0, 0), (0, 0), (lane0, HP - lane0 - width))).reshape(K, H * HP)


def _unpad_heads_cols(w, width, lane0=0):
    K = w.shape[0]
    return w.reshape(K, H, HP)[:, :, lane0:lane0 + width].reshape(K, H * width)


def _pad_block(w, lane0=0):
    return jnp.pad(w, ((0, 0), (lane0, LANES - lane0 - w.shape[1])))


_IN_SPLITS = [512, 1024, 1536, 1544, 2312, 2568, 2600, 3624]


def _pad_w_in(w):
    fq, fk, fv, flog, cq, ckv, krin, gfox, gmla = jnp.split(w, _IN_SPLITS, axis=1)
    return jnp.concatenate([_pad_heads_cols(fq, FOX_HD), _pad_heads_cols(fk, FOX_HD), _pad_heads_cols(fv, FOX_HD),
                            cq, ckv, gfox, gmla, _pad_block(flog), _pad_block(krin, KRIN_LANE)], axis=1)


def _unpad_w_in(wp):
    qkv, rest = wp[:, :NQKV], wp[:, NQKV:]
    fq, fk, fv = (_unpad_heads_cols(qkv[:, i * H * HP:(i + 1) * H * HP], FOX_HD) for i in range(3))
    return jnp.concatenate([fq, fk, fv, rest[:, OFF_FLOG:OFF_FLOG + H], rest[:, OFF_CQ:OFF_CQ + Q_LORA],
                            rest[:, OFF_CKV:OFF_CKV + KV_LORA], rest[:, OFF_KRIN + KRIN_LANE:OFF_KRIN + KRIN_LANE + MLA_ROPE],
                            rest[:, OFF_GFOX:OFF_GFOX + D], rest[:, OFF_GMLA:OFF_GMLA + D]], axis=1)


def _pad_w_ukv(w):
    w3 = w.reshape(KV_LORA, H, MLA_NOPE + MLA_V)
    kp = jnp.pad(w3[:, :, :MLA_NOPE], ((0, 0), (0, 0), (0, HP - MLA_NOPE))).reshape(KV_LORA, H * HP)
    vp = jnp.pad(w3[:, :, MLA_NOPE:], ((0, 0), (0, 0), (0, HP - MLA_V))).reshape(KV_LORA, H * HP)
    return jnp.concatenate([kp, vp], axis=1)


def _unpad_w_ukv(wp):
    kp = wp[:, :H * HP].reshape(KV_LORA, H, HP)[:, :, :MLA_NOPE]
    vp = wp[:, H * HP:].reshape(KV_LORA, H, HP)[:, :, :MLA_V]
    return jnp.concatenate([kp, vp], axis=2).reshape(KV_LORA, H * (MLA_NOPE + MLA_V))


def _pad_heads_rows(w, width):
    N = w.shape[1]
    return jnp.pad(w.reshape(H, width, N), ((0, 0), (0, HP - width), (0, 0))).reshape(H * HP, N)


def _unpad_heads_rows(w, width):
    N = w.shape[1]
    return w.reshape(H, HP, N)[:, :width, :].reshape(H * width, N)


def _rope_tables(positions):
    inv_freq = 1.0 / (ROPE_THETA ** (jnp.arange(0, MLA_ROPE, 2, dtype=F32) / MLA_ROPE))
    ang = positions.reshape(S, 1).astype(F32) * inv_freq
    cos, sin = jnp.cos(ang), jnp.sin(ang)
    ones = jnp.ones((S, KRIN_LANE), F32)
    tail = jnp.zeros((S, LANES - KRIN_LANE - MLA_ROPE), F32)
    ctab = jnp.concatenate([ones, cos, cos, tail], axis=1)
    stab = jnp.concatenate([0.0 * ones, -sin, sin, tail], axis=1)
    return ctab, stab


def _local_step(x, target, mod, positions, gains, bf, W, late=None):
    W = dict(W)
    sh1, sc1, gt1, sh2, sc2, gt2 = (mod[:, i * D:(i + 1) * D] for i in range(6))
    ops1, ops2 = 1.0 + sc1, 1.0 + sc2
    ones = lambda w: jnp.ones((1, w), F32)
    zeros = lambda w: jnp.zeros((1, w), F32)
    bf_blk = _pad_block(bf)
    ctab, stab = _rope_tables(positions)
    fox_scale = 1.0 / math.sqrt(FOX_HD)
    mla_scale = 1.0 / math.sqrt(MLA_NOPE + MLA_ROPE)

    h1 = _norm_mod(x, 0, D, gains["g_pre_mix"], ops1, sh1, name="f_pre_mix")
    qkv = _matmul(h1, W["w_in_qkv"], out_dtype=_MXU, name="f_proj_qkv")
    rest = _matmul(h1, W["w_in_rest"], name="f_proj_rest", tn_cap=256)
    F = _fox_prep(rest, bf_blk, name="f_fox_prep")
    Ft = F[:, :H].T
    fcol, frow = Ft.reshape(H, S, 1), Ft.reshape(H, S // TKF, 1, TKF)
    qa, ka, va = qkv[:, :H * HP], qkv[:, H * HP:2 * H * HP], qkv[:, 2 * H * HP:]
    job = _gather_level1_job([late[n] for n in _LATE]) if late else None
    oa, lse_a, *landed = _attn_fwd(qa, ka, va, frow, scale=fox_scale, name="f_attn_fox", side=job)

    cqn = _norm_mod(rest, OFF_CQ // Q_LORA, Q_LORA, gains["g_q_lora"], ones(Q_LORA), zeros(Q_LORA), name="f_norm_cq")
    ckvn = _norm_mod(rest, OFF_CKV // KV_LORA, KV_LORA, gains["g_kv_lora"], ones(KV_LORA), zeros(KV_LORA), name="f_norm_ckv")
    qb = _matmul(cqn, W["w_uq"], name="f_uq")
    kvb = _matmul(ckvn, W["w_ukv"], name="f_ukv")
    qm, km, vm = _mla_assemble(qb, kvb, rest, ctab, stab, name="f_mla_assemble")
    job = _gather_level2_job(landed) if late else None
    ob, lse_b, *landed = _attn_fwd(qm, km, vm, None, scale=mla_scale, name="f_attn_mla", side=job)
    if late:
        W.update(_late_weights({n: g.reshape(4, 2 * g.shape[1], g.shape[2]) for n, g in zip(_LATE, landed)}))

    pa = _matmul(oa, W["w_proj_fox"], name="f_proj_fox")
    pb = _matmul(ob, W["w_proj_mla"], name="f_proj_mla")
    merged = _merge(rest, pa, pb, name="f_merge")
    y1 = _matmul(merged, W["w_out"], name="f_out")
    x2 = _post_res(x, y1, gains["g_post_mix"], gt1, name="f_post_mix")
    h2 = _norm_mod(x2, 0, D, gains["g_pre_ffn"], ops2, sh2, name="f_pre_ffn")
    gu = _matmul(h2, W["w_ffn_in"], name="f_ffn_in", b_shards=True, tn_cap=1408)
    act = _swiglu(gu, name="f_swiglu")
    y2 = _matmul(act, W["w_ffn_out"], name="f_ffn_out", tk_cap=1408)
    dout, loss = _post_res_loss(x2, y2, gains["g_post_ffn"], gt2, target, name="f_post_ffn_loss")

    dy2, s_gt2, s_gpost2 = _post_res_bwd(dout, y2, gains["g_post_ffn"], gt2, name="b_post_ffn")
    dact = _matmul(dy2, W["w_ffn_out"], tb=True, name="b_ffn_out_dx", tn_cap=1408)
    dW_ffn_out = _matmul(act, dy2, ta=True, name="b_ffn_out_dw", tm_cap=1408)
    dgu = _swiglu_bwd(gu, dact, name="b_swiglu")
    dh2 = _matmul(dgu, W["w_ffn_in"], tb=True, b_shards=True, name="b_ffn_in_dx", tk_cap=1408)
    dW_ffn_in = _matmul(h2, dgu, ta=True, name="b_ffn_in_dw", out_shards=True, tn_cap=1408)
    dx2, s_sh2, s_a2 = _norm_mod_bwd(x2, 0, D, dh2, gains["g_pre_ffn"], ops2, dout, name="b_pre_ffn")
    dy1, s_gt1, s_gpost1 = _post_res_bwd(dx2, y1, gains["g_post_mix"], gt1, name="b_post_mix")
    dmerged = _matmul(dy1, W["w_out"], tb=True, name="b_out_dx")
    dW_out = _matmul(merged, dy1, ta=True, name="b_out_dw")
    dpa, dpb, dgfox, dgmla = _merge_bwd(rest, pa, pb, dmerged, name="b_merge")
    doa = _matmul(dpa, W["w_proj_fox"], tb=True, name="b_proj_fox_dx")
    dW_proj_fox = _matmul(oa, dpa, ta=True, name="b_proj_fox_dw")
    dob = _matmul(dpb, W["w_proj_mla"], tb=True, name="b_proj_mla_dx")
    dW_proj_mla = _matmul(ob, dpb, ta=True, name="b_proj_mla_dw")

    delta_a, doa16 = _attn_delta(doa, oa, name="b_delta_fox")
    as_rows = lambda a: a.reshape(H, S // TQB, 1, TQB)
    dqa, dka, dva = _attn_bwd(qa, ka, va, doa16, as_rows(lse_a), as_rows(delta_a), fcol, scale=fox_scale, name="b_attn_fox")
    delta_b, dob16 = _attn_delta(dob, ob, name="b_delta_mla")
    dqm, dkm, dvm = _attn_bwd(qm, km, vm, dob16, as_rows(lse_b), as_rows(delta_b), None, scale=mla_scale, name="b_attn_mla")

    dF = (dqa[:, ROW_SUM_LANE::HP] - dka[:, COL_SUM_LANE::HP]) * (1.0 / fox_scale)
    dflog, s_bf = _fox_bwd_prep(rest, bf_blk, _pad_block(dF), name="b_fox_prep")

    dqb, dkvb, dkrin = _mla_assemble_bwd(dqm, dkm, dvm, ctab, stab, name="b_mla_assemble")
    dcqn = _matmul(dqb, W["w_uq"], tb=True, name="b_uq_dx")
    dW_uq = _matmul(cqn, dqb, ta=True, name="b_uq_dw")
    dckvn = _matmul(dkvb, W["w_ukv"], tb=True, name="b_ukv_dx")
    dW_ukv = _matmul(ckvn, dkvb, ta=True, name="b_ukv_dw")
    dcq, _, s_gq = _norm_mod_bwd(rest, OFF_CQ // Q_LORA, Q_LORA, dcqn, gains["g_q_lora"], ones(Q_LORA), None, name="b_norm_cq")
    dckv, _, s_gkv = _norm_mod_bwd(rest, OFF_CKV // KV_LORA, KV_LORA, dckvn, gains["g_kv_lora"], ones(KV_LORA), None, name="b_norm_ckv")

    c16 = lambda a: a.astype(_MXU)
    dproj = jnp.concatenate([c16(dqa), c16(dka), c16(dva), c16(dcq), c16(dckv), dgfox, dgmla, c16(dflog), c16(dkrin)], axis=1)
    w_in_full = jnp.concatenate([W["w_in_qkv"], W["w_in_rest"]], axis=1)
    dh1 = _matmul(dproj, w_in_full, tb=True, name="b_in_dx", tk_cap=1280)
    dW_in = _matmul(h1, dproj, ta=True, name="b_in_dw", tn_cap=640)
    grad_x, s_sh1, s_a1 = _norm_mod_bwd(x, 0, D, dh1, gains["g_pre_mix"], ops1, dx2, name="b_pre_mix")

    dmod = jnp.concatenate([s_sh1, s_a1 * gains["g_pre_mix"], s_gt1, s_sh2, s_a2 * gains["g_pre_ffn"], s_gt2], axis=1)
    small = dict(dmod=dmod, g_pre_mix=s_a1 * ops1, g_post_mix=s_gpost1, g_pre_ffn=s_a2 * ops2, g_post_ffn=s_gpost2,
                 g_q_lora=s_gq, g_kv_lora=s_gkv, b_forget=s_bf)
    dW = dict(w_in=dW_in, w_uq=dW_uq, w_ukv=dW_ukv, w_proj_fox=dW_proj_fox, w_proj_mla=dW_proj_mla, w_out=dW_out,
              w_ffn_in=dW_ffn_in, w_ffn_out=dW_ffn_out)
    return loss, grad_x, dW, small


_BIG = ["w_in", "w_uq", "w_ukv", "w_proj_fox", "w_proj_mla", "w_out", "w_ffn_in", "w_ffn_out"]
_COL_SHARDED = {"w_in", "w_ukv", "w_proj_fox", "w_proj_mla", "w_ffn_in"}
_SMALL = ["b_ada", "g_pre_mix", "g_post_mix", "g_pre_ffn", "g_post_ffn", "b_forget", "g_q_lora", "g_kv_lora"]
_ORDER = ["w_ada", "b_ada", "g_pre_mix", "g_post_mix", "g_pre_ffn", "g_post_ffn", "w_in", "b_forget", "g_q_lora", "w_uq",
          "g_kv_lora", "w_ukv", "w_proj_fox", "w_proj_mla", "w_out", "w_ffn_in", "w_ffn_out"]
_ROW = {}
_off = 0
for _n, _w in [("dmod", 6 * D), ("g_pre_mix", D), ("g_post_mix", D), ("g_pre_ffn", D), ("g_post_ffn", D), ("g_q_lora", Q_LORA),
               ("g_kv_lora", KV_LORA), ("b_forget", LANES), ("loss", LANES)]:
    _ROW[_n] = (_off, _w)
    _off += _w
_ROW_LEN = _off


_EARLY = ["w_in", "w_uq", "w_ukv"]
_LATE = ["w_proj_fox", "w_proj_mla", "w_out", "w_ffn_in", "w_ffn_out"]


def _rows_from_shards(g):
    return g.reshape(-1, g.shape[2])


def _early_weights(G):
    w_in = _pad_w_in(_cols_from_shards(G["w_in"]))
    return dict(
        w_in_qkv=w_in[:, :NQKV], w_in_rest=w_in[:, NQKV:],
        w_uq=_pad_heads_cols(_rows_from_shards(G["w_uq"]), MLA_NOPE + MLA_ROPE),
        w_ukv=_pad_w_ukv(_cols_from_shards(G["w_ukv"])))


def _late_weights(G):
    return dict(
        w_proj_fox=_pad_heads_rows(_cols_from_shards(G["w_proj_fox"]), FOX_HD),
        w_proj_mla=_pad_heads_rows(_cols_from_shards(G["w_proj_mla"]), MLA_V),
        w_out=_rows_from_shards(G["w_out"]), w_ffn_in=G["w_ffn_in"], w_ffn_out=_rows_from_shards(G["w_ffn_out"]))


def _full_weights(G):
    return {**_early_weights(G), **_late_weights(G)}


def _grad_shards(dW):
    nat = dict(
        w_in=_unpad_w_in(dW["w_in"]), w_uq=_unpad_heads_cols(dW["w_uq"], MLA_NOPE + MLA_ROPE), w_ukv=_unpad_w_ukv(dW["w_ukv"]),
        w_proj_fox=_unpad_heads_rows(dW["w_proj_fox"], FOX_HD), w_proj_mla=_unpad_heads_rows(dW["w_proj_mla"], MLA_V),
        w_out=dW["w_out"], w_ffn_out=dW["w_ffn_out"])
    out = {n: (_cols_to_shards(g) if n in _COL_SHARDED else g.reshape(4, g.shape[0] // 4, g.shape[1])) for n, g in nat.items()}
    out["w_ffn_in"] = dW["w_ffn_in"]
    return out


def kernel(x, c, positions, w_ada, b_ada, g_pre_mix, g_post_mix, g_pre_ffn, g_post_ffn, w_in, b_forget, g_q_lora, w_uq, g_kv_lora, w_ukv, w_proj_fox, w_proj_mla, w_out, w_ffn_in, w_ffn_out, loss_target, m_w_ada, m_b_ada, m_g_pre_mix, m_g_post_mix, m_g_pre_ffn, m_g_post_ffn, m_w_in, m_b_forget, m_g_q_lora, m_w_uq, m_g_kv_lora, m_w_ukv, m_w_proj_fox, m_w_proj_mla, m_w_out, m_w_ffn_in, m_w_ffn_out, v_w_ada, v_b_ada, v_g_pre_mix, v_g_post_mix, v_g_pre_ffn, v_g_post_ffn, v_w_in, v_b_forget, v_g_q_lora, v_w_uq, v_g_kv_lora, v_w_ukv, v_w_proj_fox, v_w_proj_mla, v_w_out, v_w_ffn_in, v_w_ffn_out):
    P = dict(w_ada=w_ada, b_ada=b_ada, g_pre_mix=g_pre_mix, g_post_mix=g_post_mix, g_pre_ffn=g_pre_ffn, g_post_ffn=g_post_ffn,
             w_in=w_in, b_forget=b_forget, g_q_lora=g_q_lora, w_uq=w_uq, g_kv_lora=g_kv_lora, w_ukv=w_ukv,
             w_proj_fox=w_proj_fox, w_proj_mla=w_proj_mla, w_out=w_out, w_ffn_in=w_ffn_in, w_ffn_out=w_ffn_out)
    M = dict(w_ada=m_w_ada, b_ada=m_b_ada, g_pre_mix=m_g_pre_mix, g_post_mix=m_g_post_mix, g_pre_ffn=m_g_pre_ffn,
             g_post_ffn=m_g_post_ffn, w_in=m_w_in, b_forget=m_b_forget, g_q_lora=m_g_q_lora, w_uq=m_w_uq, g_kv_lora=m_g_kv_lora,
             w_ukv=m_w_ukv, w_proj_fox=m_w_proj_fox, w_proj_mla=m_w_proj_mla, w_out=m_w_out, w_ffn_in=m_w_ffn_in,
             w_ffn_out=m_w_ffn_out)
    V = dict(w_ada=v_w_ada, b_ada=v_b_ada, g_pre_mix=v_g_pre_mix, g_post_mix=v_g_post_mix, g_pre_ffn=v_g_pre_ffn,
             g_post_ffn=v_g_post_ffn, w_in=v_w_in, b_forget=v_b_forget, g_q_lora=v_g_q_lora, w_uq=v_w_uq, g_kv_lora=v_g_kv_lora,
             w_ukv=v_w_ukv, w_proj_fox=v_w_proj_fox, w_proj_mla=v_w_proj_mla, w_out=v_w_out, w_ffn_in=v_w_ffn_in,
             w_ffn_out=v_w_ffn_out)
    ax, ay, ac = lax.axis_index("x"), lax.axis_index("y"), lax.axis_index("c")
    chip = 2 * ax + ay
    me = 4 * ax + 2 * ay + ac
    n_ada = w_ada.shape[2]

    c_all = _all_gather_rows(jnp.pad(c, ((0, 7), (0, 0))), name="gather_c")[:, 0, :]
    c_all = jnp.pad(c_all, ((0, 8), (0, 0)))
    b_shard = lax.dynamic_slice(b_ada, (0, chip * n_ada), (1, n_ada))
    mod_blk, silu_c = _ada_mod(c_all, w_ada[0], b_shard, name="ada_mod")
    mod_all = _all_gather_rows(mod_blk, name="gather_mod")
    mod_mine = lax.dynamic_index_in_dim(mod_all, me, axis=1, keepdims=False)
    mod = lax.dynamic_index_in_dim(mod_mine.reshape(4, 2, n_ada), ac, axis=1, keepdims=False).reshape(1, 6 * D)

    W = _early_weights({n: _gather_weight(P[n][0], name="gather_" + n) for n in _EARLY})
    late = {}
    for n in _LATE:
        rh = P[n].shape[1] // 2
        late[n] = lax.dynamic_slice_in_dim(P[n][0], ac * rh, rh, axis=0).astype(BF16)

    gains = {n: P[n] for n in ["g_pre_mix", "g_post_mix", "g_pre_ffn", "g_post_ffn", "g_q_lora", "g_kv_lora"]}
    loss, grad_x, dW, small = _local_step(x[0], loss_target[0], mod, positions, gains, b_forget, W, late)

    shards = _grad_shards(dW)
    grads = {n: _reduce_scatter_weight(shards[n], name="scatter_" + n) for n in _BIG}

    small = dict(small, loss=_pad_block(loss))
    row = jnp.concatenate([small[n] for n in _ROW], axis=1)
    rows = _all_gather_rows(jnp.pad(row, ((0, 7), (0, 0))), name="gather_small")[:, 0, :]
    tot = _rowsum(rows, name="sum_small")
    piece = lambda n: tot[:, _ROW[n][0]:_ROW[n][0] + _ROW[n][1]]
    grads["b_ada"] = piece("dmod")
    for n in ["g_pre_mix", "g_post_mix", "g_pre_ffn", "g_post_ffn", "g_q_lora", "g_kv_lora"]:
        grads[n] = piece(n)
    grads["b_forget"] = piece("b_forget")[:, :H]
    loss_out = piece("loss")[0, 0]
    dmod_all = rows[:, _ROW["dmod"][0]:_ROW["dmod"][0] + 6 * D]
    dmod_shard = jnp.pad(lax.dynamic_slice(dmod_all, (0, chip * n_ada), (8, n_ada)), ((0, 8), (0, 0)))
    grads["w_ada"] = _matmul(silu_c, dmod_shard, ta=True, name="ada_dw")

    delta, new_m, new_v = {}, {}, {}
    for n in ["w_ada"] + _BIG:
        delta[n], new_m[n], new_v[n] = _adamw(P[n], grads[n], M[n], V[n], name="adamw_" + n)
    cat = lambda T: jnp.concatenate([T[n] for n in _SMALL], axis=1)
    d_s, m_s, v_s = (t[0] for t in _adamw(cat(P)[None], cat(grads), cat(M)[None], cat(V)[None], name="adamw_small"))
    o = 0
    for n in _SMALL:
        wdt = P[n].shape[1]
        delta[n], new_m[n], new_v[n] = d_s[:, o:o + wdt], m_s[:, o:o + wdt], v_s[:, o:o + wdt]
        o += wdt

    def shaped(T, n):
        return T[n].reshape(P[n].shape)

    return (loss_out, grad_x[None], *[shaped(grads, n) for n in _ORDER], *[shaped(delta, n) for n in _ORDER],
            *[shaped(new_m, n) for n in _ORDER], *[shaped(new_v, n) for n in _ORDER])
```

```python
import functools
import math
from typing import Callable, NamedTuple

import jax
import jax.numpy as jnp
from jax import lax
from jax.experimental import pallas as pl
from jax.experimental.pallas import tpu as pltpu

F32 = jnp.float32
BF16 = jnp.bfloat16
_MXU = jnp.bfloat16

S = 2048
D = 1024
H = 8
HP = 128
FOX_HD = 64
MLA_NOPE = 64
MLA_ROPE = 32
MLA_V = 64
Q_LORA = 768
KV_LORA = 256
D_FF = 2816
NORM_EPS = 1e-6
ROPE_THETA = 10000.0
NEG = -1e30

ADAM_LR = 0.001
ADAM_B1 = 0.9
ADAM_B2 = 0.999
ADAM_EPS = 1e-08
ADAM_WD = 0.01
ADAM_STEP = 10

LANES = 128
VMEM_CAP = 60 * 1024 * 1024
MESH = pl.DeviceIdType.MESH

NQKV = 3 * H * HP
OFF_CQ = 0
OFF_CKV = Q_LORA
OFF_GFOX = 1024
OFF_GMLA = 2048
OFF_FLOG = 3072
OFF_KRIN = 3200
NREST = 3328
KRIN_LANE = 64
ROW_SUM_LANE = 64
COL_SUM_LANE = 65


def _limit(nbytes):
    return int(min(VMEM_CAP, nbytes * 1.25 + (4 << 20)))


def _nbytes(shape, dtype):
    n = 1
    for s in shape:
        n *= s
    return n * jnp.dtype(dtype).itemsize


def _pick(n, cap):
    best = None
    for t in range(LANES, min(n, cap) + 1, LANES):
        if n % t == 0:
            best = t
    return best if best is not None else n


def _pcall(body, *, out_shape, **kw):
    outs = jax.tree.map(lambda s: pltpu.HBM(s.shape, s.dtype), out_shape)
    call = pl.pallas_call(body, out_shape=outs, **kw)
    return lambda *args: call(*[pltpu.with_memory_space_constraint(a, pltpu.HBM) for a in args])


def _matmul(a, b, *, ta=False, tb=False, out_dtype=F32, name, tm_cap=1024, tn_cap=512, tk_cap=1024,
            b_shards=False, out_shards=False):
    if ta:
        K, M = a.shape
    else:
        M, K = a.shape
    if b_shards:
        _, R, cb = b.shape
        N, K2 = (R, 4 * cb) if tb else (4 * cb, R)
    elif tb:
        N, K2 = b.shape
    else:
        K2, N = b.shape
    assert K == K2, (a.shape, b.shape, ta, tb)
    tm = _pick(M, tm_cap)
    tn = _pick(N, tn_cap)
    tk = K if K <= tk_cap else _pick(K, tk_cap)
    nk = K // tk
    dims = (((0 if ta else 1,), (1 if tb else 0,)), ((), ()))

    def body(a_ref, b_ref, o_ref, acc_ref):
        k = pl.program_id(2)

        @pl.when(k == 0)
        def _():
            acc_ref[...] = jnp.zeros_like(acc_ref)

        acc_ref[...] += lax.dot_general(a_ref[...].astype(_MXU), b_ref[...].astype(_MXU), dims,
                                        preferred_element_type=F32)

        @pl.when(k == nk - 1)
        def _():
            o_ref[...] = acc_ref[...].astype(out_dtype)

    a_spec = pl.BlockSpec((tk, tm), lambda i, j, k: (k, i)) if ta else pl.BlockSpec((tm, tk), lambda i, j, k: (i, k))
    if b_shards and tb:
        assert cb % tk == 0
        per = cb // tk
        b_spec = pl.BlockSpec((None, tn, tk), lambda i, j, k: (k // per, j, k % per))
    elif b_shards:
        assert cb % tn == 0
        per = cb // tn
        b_spec = pl.BlockSpec((None, tk, tn), lambda i, j, k: (j // per, k, j % per))
    elif tb:
        b_spec = pl.BlockSpec((tn, tk), lambda i, j, k: (j, k))
    else:
        b_spec = pl.BlockSpec((tk, tn), lambda i, j, k: (k, j))
    if out_shards:
        assert (N // 4) % tn == 0
        pern = N // 4 // tn
        out_shape = jax.ShapeDtypeStruct((4, M, N // 4), out_dtype)
        out_spec = pl.BlockSpec((None, tm, tn), lambda i, j, k: (j // pern, i, j % pern))
    else:
        out_shape = jax.ShapeDtypeStruct((M, N), out_dtype)
        out_spec = pl.BlockSpec((tm, tn), lambda i, j, k: (i, j))
    need = (2 * _nbytes((tm, tk), a.dtype) + 2 * _nbytes((tk, tn), b.dtype) + 2 * _nbytes((tm, tn), out_dtype)
            + _nbytes((tm, tn), F32) * 2 + _nbytes((tm, tk), _MXU) + _nbytes((tk, tn), _MXU))
    return _pcall(
        body, name=name,
        out_shape=out_shape,
        grid=(M // tm, N // tn, nk),
        in_specs=[a_spec, b_spec],
        out_specs=out_spec,
        scratch_shapes=[pltpu.VMEM((tm, tn), F32)],
        compiler_params=pltpu.CompilerParams(dimension_semantics=("parallel", "parallel", "arbitrary"),
                                             vmem_limit_bytes=_limit(need)),
    )(a, b)


TM = 256


def _vec(w):
    return pl.BlockSpec((1, w), lambda i: (0, 0))


def _rows(w, col=0):
    return pl.BlockSpec((TM, w), lambda i: (i, col))


def _row_params(need, carried=False):
    return pltpu.CompilerParams(dimension_semantics=("arbitrary" if carried else "parallel",),
                                vmem_limit_bytes=_limit(need))


def _norm_mod(x, col, w, g, ops, sh, *, name):
    def body(x_ref, g_ref, ops_ref, sh_ref, o_ref):
        xv = x_ref[...]
        r = lax.rsqrt(jnp.mean(xv * xv, axis=-1, keepdims=True) + NORM_EPS)
        o_ref[...] = (((xv * r) * g_ref[...]) * ops_ref[...] + sh_ref[...]).astype(o_ref.dtype)

    return _pcall(
        body, name=name, out_shape=jax.ShapeDtypeStruct((S, w), _MXU), grid=(S // TM,),
        in_specs=[_rows(w, col), _vec(w), _vec(w), _vec(w)], out_specs=_rows(w),
        compiler_params=_row_params(8 * _nbytes((TM, w), F32)),
    )(x, g, ops, sh)


def _norm_mod_bwd(x, col, w, dh, g, ops, dres, *, name):
    has_res = dres is not None

    def body(*refs):
        if has_res:
            x_ref, dh_ref, g_ref, ops_ref, dres_ref, dx_ref, s1_ref, s2_ref = refs
        else:
            x_ref, dh_ref, g_ref, ops_ref, dx_ref, s1_ref, s2_ref = refs
        i = pl.program_id(0)

        @pl.when(i == 0)
        def _():
            s1_ref[...] = jnp.zeros_like(s1_ref)
            s2_ref[...] = jnp.zeros_like(s2_ref)

        xv = x_ref[...]
        dhv = dh_ref[...]
        r = lax.rsqrt(jnp.mean(xv * xv, axis=-1, keepdims=True) + NORM_EPS)
        xn = xv * r
        dxn = dhv * (g_ref[...] * ops_ref[...])
        dx = r * (dxn - xn * jnp.mean(dxn * xn, axis=-1, keepdims=True))
        if has_res:
            dx = dx + dres_ref[...]
        dx_ref[...] = dx
        s1_ref[...] += jnp.sum(dhv, axis=0, keepdims=True)
        s2_ref[...] += jnp.sum(dhv * xn, axis=0, keepdims=True)

    in_specs = [_rows(w, col), _rows(w), _vec(w), _vec(w)] + ([_rows(w)] if has_res else [])
    args = (x, dh, g, ops) + ((dres,) if has_res else ())
    return _pcall(
        body, name=name,
        out_shape=(jax.ShapeDtypeStruct((S, w), F32), jax.ShapeDtypeStruct((1, w), F32), jax.ShapeDtypeStruct((1, w), F32)),
        grid=(S // TM,), in_specs=in_specs, out_specs=(_rows(w), _vec(w), _vec(w)),
        compiler_params=_row_params(12 * _nbytes((TM, w), F32), carried=True),
    )(*args)


def _post_res(xres, y, g, gt, *, name):
    def body(x_ref, y_ref, g_ref, gt_ref, o_ref):
        yv = y_ref[...]
        r = lax.rsqrt(jnp.mean(yv * yv, axis=-1, keepdims=True) + NORM_EPS)
        o_ref[...] = x_ref[...] + gt_ref[...] * ((yv * r) * g_ref[...])

    return _pcall(
        body, name=name, out_shape=jax.ShapeDtypeStruct((S, D), F32), grid=(S // TM,),
        in_specs=[_rows(D), _rows(D), _vec(D), _vec(D)], out_specs=_rows(D),
        compiler_params=_row_params(8 * _nbytes((TM, D), F32)),
    )(xres, y, g, gt)


def _post_res_loss(xres, y, g, gt, target, *, name):
    def body(x_ref, y_ref, g_ref, gt_ref, t_ref, dout_ref, loss_ref):
        i = pl.program_id(0)

        @pl.when(i == 0)
        def _():
            loss_ref[...] = jnp.zeros_like(loss_ref)

        yv = y_ref[...]
        r = lax.rsqrt(jnp.mean(yv * yv, axis=-1, keepdims=True) + NORM_EPS)
        out = x_ref[...] + gt_ref[...] * ((yv * r) * g_ref[...])
        err = out - t_ref[...]
        dout_ref[...] = err * (1.0 / D)
        per_row = jnp.mean(err * err, axis=-1, keepdims=True)
        loss_ref[...] += 0.5 * jnp.sum(per_row, axis=0, keepdims=True)

    return _pcall(
        body, name=name,
        out_shape=(jax.ShapeDtypeStruct((S, D), F32), jax.ShapeDtypeStruct((1, 1), F32)), grid=(S // TM,),
        in_specs=[_rows(D), _rows(D), _vec(D), _vec(D), _rows(D)],
        out_specs=(_rows(D), pl.BlockSpec((1, 1), lambda i: (0, 0))),
        compiler_params=_row_params(10 * _nbytes((TM, D), F32), carried=True),
    )(xres, y, g, gt, target)


def _post_res_bwd(dxn, y, g, gt, *, name):
    def body(d_ref, y_ref, g_ref, gt_ref, dy_ref, sgt_ref, sg_ref):
        i = pl.program_id(0)

        @pl.when(i == 0)
        def _():
            sgt_ref[...] = jnp.zeros_like(sgt_ref)
            sg_ref[...] = jnp.zeros_like(sg_ref)

        yv = y_ref[...]
        dv = d_ref[...]
        r = lax.rsqrt(jnp.mean(yv * yv, axis=-1, keepdims=True) + NORM_EPS)
        yn = yv * r
        dn = dv * gt_ref[...]
        dyn = dn * g_ref[...]
        dy_ref[...] = (r * (dyn - yn * jnp.mean(dyn * yn, axis=-1, keepdims=True))).astype(dy_ref.dtype)
        sgt_ref[...] += jnp.sum(dv * (yn * g_ref[...]), axis=0, keepdims=True)
        sg_ref[...] += jnp.sum(dn * yn, axis=0, keepdims=True)

    return _pcall(
        body, name=name,
        out_shape=(jax.ShapeDtypeStruct((S, D), _MXU), jax.ShapeDtypeStruct((1, D), F32), jax.ShapeDtypeStruct((1, D), F32)),
        grid=(S // TM,), in_specs=[_rows(D), _rows(D), _vec(D), _vec(D)], out_specs=(_rows(D), _vec(D), _vec(D)),
        compiler_params=_row_params(10 * _nbytes((TM, D), F32), carried=True),
    )(dxn, y, g, gt)


def _swiglu(gu, *, name):
    def body(g_ref, u_ref, o_ref):
        gv = g_ref[...]
        o_ref[...] = ((gv * jax.nn.sigmoid(gv)) * u_ref[...]).astype(o_ref.dtype)

    return _pcall(
        body, name=name, out_shape=jax.ShapeDtypeStruct((S, D_FF), _MXU), grid=(S // TM,),
        in_specs=[_rows(D_FF, 0), _rows(D_FF, 1)], out_specs=_rows(D_FF),
        compiler_params=_row_params(8 * _nbytes((TM, D_FF), F32)),
    )(gu, gu)


def _swiglu_bwd(gu, dact, *, name):
    def body(g_ref, u_ref, d_ref, o_ref):
        gv = g_ref[...]
        dv = d_ref[...]
        sg = jax.nn.sigmoid(gv)
        o_ref[:, :D_FF] = (dv * u_ref[...] * (sg * (1.0 + gv * (1.0 - sg)))).astype(o_ref.dtype)
        o_ref[:, D_FF:] = (dv * (gv * sg)).astype(o_ref.dtype)

    return _pcall(
        body, name=name, out_shape=jax.ShapeDtypeStruct((S, 2 * D_FF), _MXU), grid=(S // TM,),
        in_specs=[_rows(D_FF, 0), _rows(D_FF, 1), _rows(D_FF)], out_specs=_rows(2 * D_FF),
        compiler_params=_row_params(12 * _nbytes((TM, D_FF), F32)),
    )(gu, gu, dact)


def _merge(rest, pa, pb, *, name):
    def body(ga_ref, gb_ref, pa_ref, pb_ref, o_ref):
        o_ref[...] = (jax.nn.sigmoid(ga_ref[...]) * pa_ref[...] + jax.nn.sigmoid(gb_ref[...]) * pb_ref[...]).astype(o_ref.dtype)

    return _pcall(
        body, name=name, out_shape=jax.ShapeDtypeStruct((S, D), _MXU), grid=(S // TM,),
        in_specs=[_rows(D, OFF_GFOX // D), _rows(D, OFF_GMLA // D), _rows(D), _rows(D)], out_specs=_rows(D),
        compiler_params=_row_params(10 * _nbytes((TM, D), F32)),
    )(rest, rest, pa, pb)


def _merge_bwd(rest, pa, pb, dm, *, name):
    def body(ga_ref, gb_ref, pa_ref, pb_ref, d_ref, dpa_ref, dpb_ref, dga_ref, dgb_ref):
        dv = d_ref[...]
        sa = jax.nn.sigmoid(ga_ref[...])
        sb = jax.nn.sigmoid(gb_ref[...])
        dpa_ref[...] = (dv * sa).astype(dpa_ref.dtype)
        dpb_ref[...] = (dv * sb).astype(dpb_ref.dtype)
        dga_ref[...] = (dv * pa_ref[...] * (sa * (1.0 - sa))).astype(dga_ref.dtype)
        dgb_ref[...] = (dv * pb_ref[...] * (sb * (1.0 - sb))).astype(dgb_ref.dtype)

    o = jax.ShapeDtypeStruct((S, D), _MXU)
    return _pcall(
        body, name=name, out_shape=(o, o, o, o), grid=(S // TM,),
        in_specs=[_rows(D, OFF_GFOX // D), _rows(D, OFF_GMLA // D), _rows(D), _rows(D), _rows(D)],
        out_specs=(_rows(D), _rows(D), _rows(D), _rows(D)),
        compiler_params=_row_params(16 * _nbytes((TM, D), F32)),
    )(rest, rest, pa, pb, dm)


SCAN = 256


def _split_dot(tri, x):
    hi = x.astype(_MXU)
    r1 = x - hi.astype(F32)
    mid = r1.astype(_MXU)
    lo = (r1 - mid.astype(F32)).astype(_MXU)
    dot = functools.partial(jnp.dot, preferred_element_type=F32)
    return dot(tri, hi) + dot(tri, mid) + dot(tri, lo)


def _fox_prep(rest, bf, *, name):
    def body(z_ref, b_ref, f_ref):
        lane = lax.broadcasted_iota(jnp.int32, (SCAN, LANES), 1)
        tri = (lax.broadcasted_iota(jnp.int32, (SCAN, SCAN), 1) <= lax.broadcasted_iota(jnp.int32, (SCAN, SCAN), 0)).astype(_MXU)
        carry = jnp.zeros((1, LANES), F32)
        for c in range(S // SCAN):
            z = z_ref[c * SCAN:(c + 1) * SCAN, :] + b_ref[...]
            lf = jnp.minimum(z, 0.0) - jnp.log(1.0 + jnp.exp(-jnp.abs(z)))
            lf = jnp.where(lane < H, lf, 0.0)
            cum = _split_dot(tri, lf) + carry
            f_ref[c * SCAN:(c + 1) * SCAN, :] = cum
            carry = cum[SCAN - 1:SCAN, :]

    return _pcall(
        body, name=name, out_shape=jax.ShapeDtypeStruct((S, LANES), F32), grid=(1,),
        in_specs=[pl.BlockSpec((S, LANES), lambda i: (0, OFF_FLOG // LANES)), pl.BlockSpec((1, LANES), lambda i: (0, 0))],
        out_specs=pl.BlockSpec((S, LANES), lambda i: (0, 0)),
        compiler_params=pltpu.CompilerParams(vmem_limit_bytes=_limit(8 * _nbytes((S, LANES), F32))),
    )(rest, bf)


def _fox_bwd_prep(rest, bf, dF, *, name):
    def body(z_ref, b_ref, d_ref, o_ref, db_ref):
        lane = lax.broadcasted_iota(jnp.int32, (SCAN, LANES), 1)
        tri = (lax.broadcasted_iota(jnp.int32, (SCAN, SCAN), 1) >= lax.broadcasted_iota(jnp.int32, (SCAN, SCAN), 0)).astype(_MXU)
        carry = jnp.zeros((1, LANES), F32)
        db = jnp.zeros((1, LANES), F32)
        for c in range(S // SCAN - 1, -1, -1):
            rc = _split_dot(tri, d_ref[c * SCAN:(c + 1) * SCAN, :]) + carry
            z = z_ref[c * SCAN:(c + 1) * SCAN, :] + b_ref[...]
            dz = jnp.where(lane < H, rc * jax.nn.sigmoid(-z), 0.0)
            o_ref[c * SCAN:(c + 1) * SCAN, :] = dz
            db = db + jnp.sum(dz, axis=0, keepdims=True)
            carry = rc[0:1, :]
        db_ref[...] = db

    return _pcall(
        body, name=name,
        out_shape=(jax.ShapeDtypeStruct((S, LANES), F32), jax.ShapeDtypeStruct((1, LANES), F32)), grid=(1,),
        in_specs=[pl.BlockSpec((S, LANES), lambda i: (0, OFF_FLOG // LANES)), pl.BlockSpec((1, LANES), lambda i: (0, 0)),
                  pl.BlockSpec((S, LANES), lambda i: (0, 0))],
        out_specs=(pl.BlockSpec((S, LANES), lambda i: (0, 0)), pl.BlockSpec((1, LANES), lambda i: (0, 0))),
        compiler_params=pltpu.CompilerParams(vmem_limit_bytes=_limit(10 * _nbytes((S, LANES), F32))),
    )(rest, bf, dF)


def _swap16(x):
    lane = lax.broadcasted_iota(jnp.int32, x.shape, 1)
    half = MLA_ROPE // 2
    sw = jnp.where(lane < KRIN_LANE + half, pltpu.roll(x, LANES - half, 1), pltpu.roll(x, half, 1))
    return jnp.where((lane >= KRIN_LANE) & (lane < KRIN_LANE + MLA_ROPE), sw, 0.0)


def _mla_assemble(qb, kvb, rest, ctab, stab, *, name):
    def body(q_ref, kk_ref, kv_ref, kr_ref, c_ref, s_ref, qo_ref, ko_ref, vo_ref):
        cv = c_ref[...]
        sv = s_ref[...]
        kr = kr_ref[...]
        kpe = kr * cv + _swap16(kr) * sv
        for h in range(H):
            sl = slice(h * HP, (h + 1) * HP)
            qh = q_ref[:, sl]
            qo_ref[:, sl] = (qh * cv + _swap16(qh) * sv).astype(qo_ref.dtype)
            ko_ref[:, sl] = (kk_ref[:, sl] + kpe).astype(ko_ref.dtype)
        vo_ref[...] = kv_ref[...].astype(vo_ref.dtype)

    o = jax.ShapeDtypeStruct((S, H * HP), _MXU)
    return _pcall(
        body, name=name, out_shape=(o, o, o), grid=(S // TM,),
        in_specs=[_rows(H * HP), _rows(H * HP, 0), _rows(H * HP, 1), _rows(LANES, OFF_KRIN // LANES), _rows(LANES), _rows(LANES)],
        out_specs=(_rows(H * HP), _rows(H * HP), _rows(H * HP)),
        compiler_params=_row_params(14 * _nbytes((TM, H * HP), F32)),
    )(qb, kvb, kvb, rest, ctab, stab)


def _mla_assemble_bwd(dq, dk, dv, ctab, stab, *, name):
    def body(dq_ref, dk_ref, dv_ref, c_ref, s_ref, dqo_ref, dkv_ref, dkr_ref):
        cv = c_ref[...]
        sv = s_ref[...]
        lane = lax.broadcasted_iota(jnp.int32, (TM, LANES), 1)
        dsum = jnp.zeros((TM, LANES), F32)
        for h in range(H):
            sl = slice(h * HP, (h + 1) * HP)
            dqh = dq_ref[:, sl]
            dqo_ref[:, sl] = (dqh * cv + _swap16(dqh * sv)).astype(dqo_ref.dtype)
            dsum = dsum + dk_ref[:, sl]
        dkv_ref[:, :H * HP] = dk_ref[...].astype(dkv_ref.dtype)
        dkv_ref[:, H * HP:] = dv_ref[...].astype(dkv_ref.dtype)
        dkr = dsum * cv + _swap16(dsum * sv)
        dkr_ref[...] = jnp.where((lane >= KRIN_LANE) & (lane < KRIN_LANE + MLA_ROPE), dkr, 0.0)

    return _pcall(
        body, name=name,
        out_shape=(jax.ShapeDtypeStruct((S, H * HP), _MXU), jax.ShapeDtypeStruct((S, 2 * H * HP), _MXU),
                   jax.ShapeDtypeStruct((S, LANES), F32)),
        grid=(S // TM,),
        in_specs=[_rows(H * HP), _rows(H * HP), _rows(H * HP), _rows(LANES), _rows(LANES)],
        out_specs=(_rows(H * HP), _rows(2 * H * HP), _rows(LANES)),
        compiler_params=_row_params(14 * _nbytes((TM, H * HP), F32)),
    )(dq, dk, dv, ctab, stab)


TQ = 256
TKF = 512
TKB = 256
TQB = 512
_NT = (((1,), (1,)), ((), ()))
_TN = (((0,), (0,)), ((), ()))


def _is_pow2(x):
    return math.frexp(x)[0] == 0.5


def _attn_fwd(q, k, v, frow, *, scale, name, side=None):
    has_decay = frow is not None
    fold = _is_pow2(scale)
    n_in = 4 if has_decay else 3
    n_side_in = len(side.inputs) if side else 0
    n_side_out = len(side.out_shapes) if side else 0

    def body(*refs):
        q_ref, k_ref, v_ref = refs[:3]
        fr_ref = refs[3] if has_decay else None
        side_in = refs[n_in:n_in + n_side_in]
        o_ref, lse_ref = refs[n_in + n_side_in:n_in + n_side_in + 2]
        side_out = refs[n_in + n_side_in + 2:n_in + n_side_in + 2 + n_side_out]
        side_scratch = refs[n_in + n_side_in + 2 + n_side_out:]
        i = pl.program_id(1)
        if side:
            @pl.when((pl.program_id(0) == 0) & (i == 0))
            def _():
                side.start(side_in, side_out, side_scratch)
        qv = q_ref[...]
        if fold:
            qv = (qv * scale).astype(qv.dtype)
        last = (i * TQ) // TKF

        def tile(j, carry, masked):
            m, l, acc = carry
            k0 = pl.multiple_of(j * TKF, TKF)
            kj = k_ref[pl.ds(k0, TKF), :]
            vj = v_ref[pl.ds(k0, TKF), :]
            s = lax.dot_general(qv, kj, _NT, preferred_element_type=F32)
            if not fold:
                s = s * scale
            if has_decay:
                s = s - fr_ref[0, j]
            if masked:
                rows = i * TQ + lax.broadcasted_iota(jnp.int32, (TQ, TKF), 0)
                cols = j * TKF + lax.broadcasted_iota(jnp.int32, (TQ, TKF), 1)
                s = jnp.where(cols <= rows, s, NEG)
            m_new = jnp.maximum(m, jnp.max(s, axis=-1, keepdims=True))
            alpha = jnp.exp(m - m_new)
            p = jnp.exp(s - m_new)
            l = alpha * l + jnp.sum(p, axis=-1, keepdims=True)
            acc = alpha * acc + jnp.dot(p.astype(_MXU), vj, preferred_element_type=F32)
            return m_new, l, acc

        init = (jnp.full((TQ, 1), NEG, F32), jnp.zeros((TQ, 1), F32), jnp.zeros((TQ, HP), F32))
        carry = lax.fori_loop(0, last, lambda j, c: tile(j, c, False), init)
        m, l, acc = tile(last, carry, True)
        o_ref[...] = acc / l
        lse_ref[0] = m + jnp.log(l)
        if side:
            @pl.when((pl.program_id(0) == H - 1) & (i == S // TQ - 1))
            def _():
                side.finish(side_in, side_out, side_scratch)

    in_specs = [pl.BlockSpec((TQ, HP), lambda h, i: (i, h)), pl.BlockSpec((S, HP), lambda h, i: (0, h)),
                pl.BlockSpec((S, HP), lambda h, i: (0, h))]
    args = (q, k, v)
    if has_decay:
        in_specs += [pl.BlockSpec((1, S // TKF, 1, TKF), lambda h, i: (h, 0, 0, 0))]
        args += (frow,)
    out_shape = (jax.ShapeDtypeStruct((S, H * HP), F32), jax.ShapeDtypeStruct((H, S, 1), F32))
    out_specs = (pl.BlockSpec((TQ, HP), lambda h, i: (i, h)), pl.BlockSpec((1, TQ, 1), lambda h, i: (h, i, 0)))
    extra = {}
    if side:
        anywhere = pl.BlockSpec(memory_space=pl.ANY)
        in_specs += [anywhere] * n_side_in
        args += tuple(side.inputs)
        out_shape += tuple(side.out_shapes)
        out_specs += (anywhere,) * n_side_out
        extra = dict(scratch_shapes=list(side.scratch),
                     input_output_aliases={n_in + a: 2 + b for a, b in side.aliases.items()})
    return _pcall(
        body, name=name, out_shape=out_shape, grid=(H, S // TQ), in_specs=in_specs, out_specs=out_specs,
        compiler_params=pltpu.CompilerParams(dimension_semantics=("arbitrary", "arbitrary") if side else ("parallel", "parallel"),
                                             vmem_limit_bytes=_limit(8 * _nbytes((S, HP), F32))),
        **extra,
    )(*args)


def _attn_delta(do, o, *, name):
    def body(do_ref, o_ref, dl_ref, dob_ref):
        for h in range(H):
            sl = slice(h * HP, (h + 1) * HP)
            dl_ref[h] = jnp.sum(do_ref[:, sl] * o_ref[:, sl], axis=-1, keepdims=True)
        dob_ref[...] = do_ref[...].astype(dob_ref.dtype)

    return _pcall(
        body, name=name,
        out_shape=(jax.ShapeDtypeStruct((H, S, 1), F32), jax.ShapeDtypeStruct((S, H * HP), _MXU)), grid=(S // TM,),
        in_specs=[_rows(H * HP), _rows(H * HP)],
        out_specs=(pl.BlockSpec((H, TM, 1), lambda i: (0, i, 0)), _rows(H * HP)),
        compiler_params=_row_params(8 * _nbytes((TM, H * HP), F32)),
    )(do, o)


def _attn_bwd(q, k, v, dob, lse_row, delta_row, fcol, *, scale, name, side=None):
    has_decay = fcol is not None
    fold = _is_pow2(scale)
    n_in = 7 if has_decay else 6
    n_side_in = len(side.inputs) if side else 0
    n_side_out = len(side.out_shapes) if side else 0

    def body(*refs):
        q_ref, k_ref, v_ref, do_ref, lse_ref, dl_ref = refs[:6]
        fc_ref = refs[6] if has_decay else None
        side_in = refs[n_in:n_in + n_side_in]
        dq_ref, dk_ref, dv_ref = refs[n_in + n_side_in:n_in + n_side_in + 3]
        side_out = refs[n_in + n_side_in + 3:n_in + n_side_in + 3 + n_side_out]
        dq_acc = refs[n_in + n_side_in + 3 + n_side_out]
        side_scratch = refs[n_in + n_side_in + 4 + n_side_out:]
        j = pl.program_id(1)
        if side:
            @pl.when((pl.program_id(0) == 0) & (j == 0))
            def _():
                side.start(side_in, side_out, side_scratch)

        @pl.when(j == 0)
        def _():
            dq_acc[...] = jnp.zeros_like(dq_acc)

        kj = k_ref[...]
        vj = v_ref[...]
        kjs = (kj * scale).astype(kj.dtype) if fold else kj
        if has_decay:
            klane = lax.broadcasted_iota(jnp.int32, (TKB, HP), 1)
            kj = jnp.where(klane == ROW_SUM_LANE, 1.0, kj).astype(kj.dtype)
        first = (j * TKB) // TQB

        def tile(t, carry, masked):
            dk, dv = carry
            r0 = pl.multiple_of(t * TQB, TQB)
            qi = q_ref[pl.ds(r0, TQB), :]
            doi = do_ref[pl.ds(r0, TQB), :]
            st = lax.dot_general(kjs, qi, _NT, preferred_element_type=F32)
            if not fold:
                st = st * scale
            if has_decay:
                st = st - fc_ref[0]
            if masked:
                keys = j * TKB + lax.broadcasted_iota(jnp.int32, (TKB, TQB), 0)
                qpos = t * TQB + lax.broadcasted_iota(jnp.int32, (TKB, TQB), 1)
                st = jnp.where(keys <= qpos, st, NEG)
            pt = jnp.exp(st - lse_ref[0, t])
            dv = dv + jnp.dot(pt.astype(_MXU), doi, preferred_element_type=F32)
            dpt = lax.dot_general(vj, doi, _NT, preferred_element_type=F32)
            dst = (pt * (dpt - dl_ref[0, t])).astype(_MXU)
            if has_decay:
                lane = lax.broadcasted_iota(jnp.int32, (TQB, HP), 1)
                qi = jnp.where(lane == COL_SUM_LANE, 1.0, qi).astype(qi.dtype)
            dk = dk + jnp.dot(dst, qi, preferred_element_type=F32)
            dq_acc[pl.ds(r0, TQB), :] += lax.dot_general(dst, kj, _TN, preferred_element_type=F32)
            return dk, dv

        zero = jnp.zeros((TKB, HP), F32)
        carry = tile(first, (zero, zero), True)
        dk, dv = lax.fori_loop(first + 1, S // TQB, lambda t, c: tile(t, c, False), carry)
        dk_ref[...] = dk * scale
        dv_ref[...] = dv

        @pl.when(j == S // TKB - 1)
        def _():
            dq_ref[...] = dq_acc[...] * scale

        if side:
            @pl.when((pl.program_id(0) == H - 1) & (j == S // TKB - 1))
            def _():
                side.finish(side_in, side_out, side_scratch)

    head = pl.BlockSpec((S, HP), lambda h, j: (0, h))
    kv = pl.BlockSpec((TKB, HP), lambda h, j: (j, h))
    stat = pl.BlockSpec((1, S // TQB, 1, TQB), lambda h, j: (h, 0, 0, 0))
    in_specs = [head, kv, kv, head, stat, stat]
    args = (q, k, v, dob, lse_row, delta_row)
    if has_decay:
        in_specs += [pl.BlockSpec((1, TKB, 1), lambda h, j: (h, j, 0))]
        args += (fcol,)
    o = jax.ShapeDtypeStruct((S, H * HP), F32)
    out_shape, out_specs, scratch, aliases = (o, o, o), (head, kv, kv), [pltpu.VMEM((S, HP), F32)], {}
    if side:
        anywhere = pl.BlockSpec(memory_space=pl.ANY)
        in_specs += [anywhere] * n_side_in
        args += tuple(side.inputs)
        out_shape += tuple(side.out_shapes)
        out_specs += (anywhere,) * n_side_out
        scratch += list(side.scratch)
        aliases = {n_in + a: 3 + b for a, b in side.aliases.items()}
    return _pcall(
        body, name=name, out_shape=out_shape, grid=(H, S // TKB), in_specs=in_specs,
        out_specs=out_specs, scratch_shapes=scratch, input_output_aliases=aliases,
        compiler_params=pltpu.CompilerParams(dimension_semantics=("arbitrary" if side else "parallel", "arbitrary"),
                                             vmem_limit_bytes=_limit(12 * _nbytes((S, HP), F32))),
    )(*args)


def _ada_mod(c_all, w_shard, b_shard, *, name):
    R = c_all.shape[0]
    N = w_shard.shape[1]
    tn = 512

    def body(c_ref, w_ref, b_ref, o_ref, sc_ref):
        cv = c_ref[...]
        sc = (cv * jax.nn.sigmoid(cv)).astype(_MXU)
        sc_ref[...] = sc
        o_ref[...] = jnp.dot(sc, w_ref[...].astype(_MXU), preferred_element_type=F32) + b_ref[...]

    return _pcall(
        body, name=name,
        out_shape=(jax.ShapeDtypeStruct((R, N), F32), jax.ShapeDtypeStruct((R, D), _MXU)), grid=(N // tn,),
        in_specs=[pl.BlockSpec((R, D), lambda j: (0, 0)), pl.BlockSpec((D, tn), lambda j: (0, j)), pl.BlockSpec((1, tn), lambda j: (0, j))],
        out_specs=(pl.BlockSpec((R, tn), lambda j: (0, j)), pl.BlockSpec((R, D), lambda j: (0, 0))),
        compiler_params=pltpu.CompilerParams(dimension_semantics=("arbitrary",), vmem_limit_bytes=_limit(6 * _nbytes((D, tn), F32))),
    )(c_all, w_shard, b_shard)


def _rowsum(x, *, name):
    R, L = x.shape

    def body(x_ref, o_ref):
        acc = x_ref[0:1, :]
        for r in range(1, R):
            acc = acc + x_ref[r:r + 1, :]
        o_ref[...] = acc

    return pl.pallas_call(body, name=name, out_shape=jax.ShapeDtypeStruct((1, L), F32),
                          in_specs=[pl.BlockSpec(memory_space=pltpu.VMEM)], out_specs=pl.BlockSpec(memory_space=pltpu.VMEM))(x)


def _adamw(w, g, m, v, *, name):
    _, R, C = w.shape
    tr = R
    for t in range(8, R + 1, 8):
        if R % t == 0 and t * C * 4 <= (1 << 20):
            tr = t

    def body(w_ref, g_ref, m_ref, v_ref, d_ref, mo_ref, vo_ref):
        gv = g_ref[...]
        m2 = ADAM_B1 * m_ref[...] + (1.0 - ADAM_B1) * gv
        v2 = ADAM_B2 * v_ref[...] + (1.0 - ADAM_B2) * (gv * gv)
        m_hat = m2 / (1.0 - ADAM_B1 ** ADAM_STEP)
        v_hat = v2 / (1.0 - ADAM_B2 ** ADAM_STEP)
        d_ref[...] = -ADAM_LR * (m_hat / (jnp.sqrt(v_hat) + ADAM_EPS) + ADAM_WD * w_ref[...])
        mo_ref[...] = m2
        vo_ref[...] = v2

    blk = pl.BlockSpec((None, tr, C), lambda i: (0, i, 0))
    o = jax.ShapeDtypeStruct((1, R, C), F32)
    return _pcall(
        body, name=name, out_shape=(o, o, o), grid=(R // tr,),
        in_specs=[blk, pl.BlockSpec((tr, C), lambda i: (i, 0)), blk, blk], out_specs=(blk, blk, blk),
        compiler_params=pltpu.CompilerParams(dimension_semantics=("parallel",), vmem_limit_bytes=_limit(20 * _nbytes((tr, C), F32))),
    )(w, g, m, v)


def _place():
    x, y, c = lax.axis_index("x"), lax.axis_index("y"), lax.axis_index("c")
    return x, y, c, [(1 - x, y), (x, 1 - y), (1 - x, 1 - y)]


def _two_level_gather(x_ref, out_ref, send_sems, recv_sems, local_sem):
    x, y, c, chips = _place()
    me, sibling = (x, y, c), (x, y, 1 - c)

    def blk(px, py, pc):
        return out_ref.at[4 * px + 2 * py + pc]

    def copy(k, block, to, src=None):
        return pltpu.make_async_remote_copy(
            src_ref=blk(*block) if src is None else src, dst_ref=blk(*block),
            send_sem=send_sems.at[k], recv_sem=recv_sems.at[k], device_id=to, device_id_type=MESH)

    mine = pltpu.make_async_copy(x_ref, blk(*me), local_sem)
    mine.start()
    first = [copy(0, me, sibling, src=x_ref)]
    first += [copy(1 + j, me, (*chip, c), src=x_ref) for j, chip in enumerate(chips)]
    for cp in first:
        cp.start()
    passed = [copy(4 + j, (*chip, c), sibling) for j, chip in enumerate(chips)]
    for j, chip in enumerate(chips):
        copy(1 + j, (*chip, c), me).wait_recv()
        passed[j].start()
    copy(0, sibling, me).wait_recv()
    for j, chip in enumerate(chips):
        copy(4 + j, (*chip, 1 - c), me).wait_recv()
    for cp in first + passed:
        cp.wait_send()
    mine.wait()


_GATHER_SEMS = [pltpu.SemaphoreType.DMA((7,)), pltpu.SemaphoreType.DMA((7,)), pltpu.SemaphoreType.DMA]


class _SideJob(NamedTuple):
    inputs: tuple
    out_shapes: tuple
    aliases: dict
    scratch: tuple
    start: Callable
    finish: Callable


def _block_index(px, py, pc):
    return 4 * px + 2 * py + pc


def _gather_level1_job(halves):
    nw = len(halves)

    def copies(ins, outs, scratch, n):
        sends, recvs, _ = scratch
        x, y, c, chips = _place()
        mine = outs[n].at[_block_index(x, y, c)]
        peers = [(x, y, 1 - c)] + [(*chip, c) for chip in chips]
        out = []
        for t, peer in enumerate(peers):
            sem = dict(send_sem=sends.at[4 * n + t], recv_sem=recvs.at[4 * n + t], device_id_type=MESH)
            landing = outs[n].at[_block_index(*peer)]
            out.append((pltpu.make_async_remote_copy(src_ref=ins[n], dst_ref=mine, device_id=peer, **sem),
                        pltpu.make_async_remote_copy(src_ref=landing, dst_ref=landing, device_id=peer, **sem)))
        local = pltpu.make_async_copy(ins[n], mine, scratch[2].at[n])
        return out, local

    def start(ins, outs, scratch):
        for n in range(nw):
            pairs, local = copies(ins, outs, scratch, n)
            local.start()
            for to, _ in pairs:
                to.start()

    def finish(ins, outs, scratch):
        for n in range(nw):
            pairs, local = copies(ins, outs, scratch, n)
            for to, frm in pairs:
                frm.wait_recv()
                to.wait_send()
            local.wait()

    return _SideJob(
        inputs=tuple(halves), out_shapes=tuple(jax.ShapeDtypeStruct((8,) + h.shape, h.dtype) for h in halves), aliases={},
        scratch=(pltpu.SemaphoreType.DMA((4 * nw,)), pltpu.SemaphoreType.DMA((4 * nw,)), pltpu.SemaphoreType.DMA((nw,))),
        start=start, finish=finish)


def _gather_level2_job(gathered):
    nw = len(gathered)

    def copies(outs, scratch, n):
        sends, recvs = scratch
        x, y, c, chips = _place()
        out = []
        for j, chip in enumerate(chips):
            sem = dict(send_sem=sends.at[3 * n + j], recv_sem=recvs.at[3 * n + j], device_id=(x, y, 1 - c), device_id_type=MESH)
            going = outs[n].at[_block_index(*chip, c)]
            landing = outs[n].at[_block_index(*chip, 1 - c)]
            out.append((pltpu.make_async_remote_copy(src_ref=going, dst_ref=going, **sem),
                        pltpu.make_async_remote_copy(src_ref=landing, dst_ref=landing, **sem)))
        return out

    def start(ins, outs, scratch):
        for n in range(nw):
            for to, _ in copies(outs, scratch, n):
                to.start()

    def finish(ins, outs, scratch):
        for n in range(nw):
            for to, frm in copies(outs, scratch, n):
                frm.wait_recv()
                to.wait_send()

    return _SideJob(
        inputs=tuple(gathered), out_shapes=tuple(jax.ShapeDtypeStruct(g.shape, g.dtype) for g in gathered),
        aliases={n: n for n in range(nw)},
        scratch=(pltpu.SemaphoreType.DMA((3 * nw,)), pltpu.SemaphoreType.DMA((3 * nw,))),
        start=start, finish=finish)


def _all_gather_rows(x, *, name):
    R, C = x.shape

    def body(x_ref, out_ref, send_sems, recv_sems, local_sem):
        _two_level_gather(x_ref, out_ref, send_sems, recv_sems, local_sem)

    return pl.pallas_call(
        body, name=name, out_shape=jax.ShapeDtypeStruct((8, R, C), x.dtype),
        in_specs=[pl.BlockSpec(memory_space=pltpu.VMEM)], out_specs=pl.BlockSpec(memory_space=pltpu.VMEM),
        scratch_shapes=list(_GATHER_SEMS),
        compiler_params=pltpu.CompilerParams(vmem_limit_bytes=_limit(10 * _nbytes((R, C), x.dtype))),
    )(x)


CAST_ROWS = 16
_FLIPS = [(fx, fy, fc) for fx in (0, 1) for fy in (0, 1) for fc in (0, 1)][1:]


def _flipped(v, bit):
    return 1 - v if bit else v


def _scatter_direct_job(pieces):
    nw = len(pieces)

    def copies(ins, outs, scratch, n):
        sends, recvs = scratch
        x, y, c, _ = _place()
        out = []
        for f, (fx, fy, fc) in enumerate(_FLIPS):
            peer = (_flipped(x, fx), _flipped(y, fy), _flipped(c, fc))
            sem = dict(send_sem=sends.at[7 * n + f], recv_sem=recvs.at[7 * n + f], device_id=peer, device_id_type=MESH)
            landing = outs[n].at[f]
            out.append((pltpu.make_async_remote_copy(src_ref=ins[n].at[_block_index(*peer)], dst_ref=landing, **sem),
                        pltpu.make_async_remote_copy(src_ref=landing, dst_ref=landing, **sem)))
        return out

    def start(ins, outs, scratch):
        for n in range(nw):
            for to, _ in copies(ins, outs, scratch, n):
                to.start()

    def finish(ins, outs, scratch):
        for n in range(nw):
            for to, frm in copies(ins, outs, scratch, n):
                frm.wait_recv()
                to.wait_send()

    return _SideJob(
        inputs=tuple(pieces), out_shapes=tuple(jax.ShapeDtypeStruct((7,) + p.shape[1:], p.dtype) for p in pieces), aliases={},
        scratch=(pltpu.SemaphoreType.DMA((7 * nw,)), pltpu.SemaphoreType.DMA((7 * nw,))),
        start=start, finish=finish)


def _scatter_finish(g4s, landed, *, name):
    nw = len(g4s)
    dims = [g.shape[1:] for g in g4s]

    def body(*refs):
        g_refs, l_refs, out_refs, own = refs[:nw], refs[nw:2 * nw], refs[2 * nw:3 * nw], refs[3 * nw:4 * nw]
        load_sems, send_sems, recv_sems = refs[4 * nw:]
        x, y, core, _ = _place()
        k = 2 * x + y
        loads = []
        for n, (r, c) in enumerate(dims):
            my0 = pl.multiple_of(core * (r // 2), CAST_ROWS)
            ld = pltpu.make_async_copy(g_refs[n].at[k, pl.ds(my0, r // 2), :], own[n], load_sems.at[n])
            ld.start()
            loads.append(ld)
        swaps = []
        for n, (r, c) in enumerate(dims):
            rh = r // 2
            my0 = pl.multiple_of(core * rh, CAST_ROWS)
            loads[n].wait()

            def fin(i, carry, n=n, my0=my0):
                r0 = pl.multiple_of(i * CAST_ROWS, CAST_ROWS)
                s = own[n][pl.ds(r0, CAST_ROWS), :]
                for f in range(7):
                    s = s + l_refs[n][f, pl.ds(r0, CAST_ROWS), :].astype(F32)
                out_refs[n][pl.ds(my0 + r0, CAST_ROWS), :] = s
                return carry

            lax.fori_loop(0, rh // CAST_ROWS, fin, 0)
            half = out_refs[n].at[pl.ds(my0, rh), :]
            sw = pltpu.make_async_remote_copy(src_ref=half, dst_ref=half, send_sem=send_sems.at[n], recv_sem=recv_sems.at[n],
                                              device_id=(x, y, 1 - core), device_id_type=MESH)
            sw.start()
            swaps.append(sw)
        for sw in swaps:
            sw.wait()

    need = sum(_nbytes((7, r // 2, c), BF16) + _nbytes((r // 2, c), F32) + _nbytes((r, c), F32) for r, c in dims)
    vmem = pl.BlockSpec(memory_space=pltpu.VMEM)
    return pl.pallas_call(
        body, name=name, out_shape=tuple(jax.ShapeDtypeStruct((r, c), F32) for r, c in dims),
        in_specs=[pl.BlockSpec(memory_space=pl.ANY)] * nw + [vmem] * nw, out_specs=(vmem,) * nw,
        scratch_shapes=[pltpu.VMEM((r // 2, c), F32) for r, c in dims]
        + [pltpu.SemaphoreType.DMA((nw,)), pltpu.SemaphoreType.DMA((nw,)), pltpu.SemaphoreType.DMA((nw,))],
        compiler_params=pltpu.CompilerParams(vmem_limit_bytes=_limit(need * 1.1)),
    )(*g4s, *landed)


def _gather_weight(w, *, name):
    r, c = w.shape
    rh = r // 2
    assert rh % CAST_ROWS == 0

    def body(w_hbm, out_ref, tmp, xb, send_sems, recv_sems, local_sem):
        core = lax.axis_index("c")
        ld = pltpu.make_async_copy(w_hbm.at[pl.ds(pl.multiple_of(core * rh, CAST_ROWS), rh), :], tmp, local_sem)
        ld.start()
        ld.wait()

        def cast(i, carry):
            r0 = pl.multiple_of(i * CAST_ROWS, CAST_ROWS)
            xb[pl.ds(r0, CAST_ROWS), :] = tmp[pl.ds(r0, CAST_ROWS), :].astype(BF16)
            return carry

        lax.fori_loop(0, rh // CAST_ROWS, cast, 0)
        _two_level_gather(xb, out_ref, send_sems, recv_sems, local_sem)

    need = _nbytes((8, rh, c), BF16) + _nbytes((rh, c), F32) + _nbytes((rh, c), BF16)
    out = pl.pallas_call(
        body, name=name, out_shape=jax.ShapeDtypeStruct((8, rh, c), BF16),
        in_specs=[pl.BlockSpec(memory_space=pl.ANY)], out_specs=pl.BlockSpec(memory_space=pltpu.VMEM),
        scratch_shapes=[pltpu.VMEM((rh, c), F32), pltpu.VMEM((rh, c), BF16)] + list(_GATHER_SEMS),
        compiler_params=pltpu.CompilerParams(vmem_limit_bytes=_limit(need * 1.3)),
    )(w)
    return out.reshape(4, r, c)


def _reduce_scatter_weight(g4, *, name):
    _, r, c = g4.shape
    rh = r // 2
    assert rh % CAST_ROWS == 0
    nsteps = rh // CAST_ROWS

    def body(g_hbm, out_ref, mine, tmp, sbuf, rbuf_a, rbuf_b, a_send, a_recv, b_send, b_recv, c_send, c_recv, lsem):
        x, y, core, chips = _place()
        sibling = (x, y, 1 - core)
        k = 2 * x + y
        my0 = pl.multiple_of(core * rh, CAST_ROWS)
        ot0 = pl.multiple_of((1 - core) * rh, CAST_ROWS)

        ld = pltpu.make_async_copy(g_hbm.at[:, pl.ds(my0, rh), :], mine, lsem)
        ld.start()
        ld.wait()
        for j in range(4):
            ldj = pltpu.make_async_copy(g_hbm.at[j, pl.ds(ot0, rh), :], tmp, lsem)
            ldj.start()
            ldj.wait()

            def cast(i, carry, j=j):
                r0 = pl.multiple_of(i * CAST_ROWS, CAST_ROWS)
                sbuf[j, pl.ds(r0, CAST_ROWS), :] = tmp[pl.ds(r0, CAST_ROWS), :].astype(BF16)
                return carry

            lax.fori_loop(0, nsteps, cast, 0)

        to_sib = pltpu.make_async_remote_copy(src_ref=sbuf, dst_ref=rbuf_a, send_sem=a_send, recv_sem=a_recv,
                                              device_id=sibling, device_id_type=MESH)
        to_sib.start()
        to_sib.wait()

        for j in range(4):
            def add(i, carry, j=j):
                r0 = pl.multiple_of(i * CAST_ROWS, CAST_ROWS)
                s = mine[j, pl.ds(r0, CAST_ROWS), :] + rbuf_a[j, pl.ds(r0, CAST_ROWS), :].astype(F32)
                mine[j, pl.ds(r0, CAST_ROWS), :] = s
                sbuf[j, pl.ds(r0, CAST_ROWS), :] = s.astype(BF16)
                return carry

            lax.fori_loop(0, nsteps, add, 0)

        sends = []
        for d, (px, py) in enumerate(chips):
            cp = pltpu.make_async_remote_copy(src_ref=sbuf.at[2 * px + py], dst_ref=rbuf_b.at[d], send_sem=b_send.at[d],
                                              recv_sem=b_recv.at[d], device_id=(px, py, core), device_id_type=MESH)
            cp.start()
            sends.append(cp)
        for cp in sends:
            cp.wait()

        def fin(i, carry):
            r0 = pl.multiple_of(i * CAST_ROWS, CAST_ROWS)
            s = mine[k, pl.ds(r0, CAST_ROWS), :]
            for d in range(3):
                s = s + rbuf_b[d, pl.ds(r0, CAST_ROWS), :].astype(F32)
            out_ref[pl.ds(my0 + r0, CAST_ROWS), :] = s
            return carry

        lax.fori_loop(0, nsteps, fin, 0)
        half = out_ref.at[pl.ds(my0, rh), :]
        swap = pltpu.make_async_remote_copy(src_ref=half, dst_ref=half, send_sem=c_send, recv_sem=c_recv,
                                            device_id=sibling, device_id_type=MESH)
        swap.start()
        swap.wait()

    need = (_nbytes((4, rh, c), F32) + _nbytes((rh, c), F32) + 2 * _nbytes((4, rh, c), BF16) + _nbytes((3, rh, c), BF16)
            + _nbytes((r, c), F32))
    return pl.pallas_call(
        body, name=name, out_shape=jax.ShapeDtypeStruct((r, c), F32),
        in_specs=[pl.BlockSpec(memory_space=pl.ANY)], out_specs=pl.BlockSpec(memory_space=pltpu.VMEM),
        scratch_shapes=[pltpu.VMEM((4, rh, c), F32), pltpu.VMEM((rh, c), F32), pltpu.VMEM((4, rh, c), BF16),
                        pltpu.VMEM((4, rh, c), BF16), pltpu.VMEM((3, rh, c), BF16),
                        pltpu.SemaphoreType.DMA, pltpu.SemaphoreType.DMA, pltpu.SemaphoreType.DMA((3,)),
                        pltpu.SemaphoreType.DMA((3,)), pltpu.SemaphoreType.DMA, pltpu.SemaphoreType.DMA,
                        pltpu.SemaphoreType.DMA],
        compiler_params=pltpu.CompilerParams(vmem_limit_bytes=_limit(need * 1.2)),
    )(g4)


def _cols_from_shards(g):
    n, K, c = g.shape
    return g.transpose(1, 0, 2).reshape(K, n * c)


def _cols_to_shards(w):
    K, N = w.shape
    return w.reshape(K, 4, N // 4).transpose(1, 0, 2)


def _pad_heads_cols(w, width, lane0=0):
    K = w.shape[0]
    w3 = w.reshape(K, H, width)
    return jnp.pad(w3, ((0, 0), (0, 0), (lane0, HP - lane0 - width))).reshape(K, H * HP)


def _unpad_heads_cols(w, width, lane0=0):
    K = w.shape[0]
    return w.reshape(K, H, HP)[:, :, lane0:lane0 + width].reshape(K, H * width)


def _pad_block(w, lane0=0):
    return jnp.pad(w, ((0, 0), (lane0, LANES - lane0 - w.shape[1])))


_IN_SPLITS = [512, 1024, 1536, 1544, 2312, 2568, 2600, 3624]


def _pad_w_in(w):
    fq, fk, fv, flog, cq, ckv, krin, gfox, gmla = jnp.split(w, _IN_SPLITS, axis=1)
    return jnp.concatenate([_pad_heads_cols(fq, FOX_HD), _pad_heads_cols(fk, FOX_HD), _pad_heads_cols(fv, FOX_HD),
                            cq, ckv, gfox, gmla, _pad_block(flog), _pad_block(krin, KRIN_LANE)], axis=1)


def _unpad_w_in(wp):
    qkv, rest = wp[:, :NQKV], wp[:, NQKV:]
    fq, fk, fv = (_unpad_heads_cols(qkv[:, i * H * HP:(i + 1) * H * HP], FOX_HD) for i in range(3))
    return jnp.concatenate([fq, fk, fv, rest[:, OFF_FLOG:OFF_FLOG + H], rest[:, OFF_CQ:OFF_CQ + Q_LORA],
                            rest[:, OFF_CKV:OFF_CKV + KV_LORA], rest[:, OFF_KRIN + KRIN_LANE:OFF_KRIN + KRIN_LANE + MLA_ROPE],
                            rest[:, OFF_GFOX:OFF_GFOX + D], rest[:, OFF_GMLA:OFF_GMLA + D]], axis=1)


def _pad_w_ukv(w):
    w3 = w.reshape(KV_LORA, H, MLA_NOPE + MLA_V)
    kp = jnp.pad(w3[:, :, :MLA_NOPE], ((0, 0), (0, 0), (0, HP - MLA_NOPE))).reshape(KV_LORA, H * HP)
    vp = jnp.pad(w3[:, :, MLA_NOPE:], ((0, 0), (0, 0), (0, HP - MLA_V))).reshape(KV_LORA, H * HP)
    return jnp.concatenate([kp, vp], axis=1)


def _unpad_w_ukv(wp):
    kp = wp[:, :H * HP].reshape(KV_LORA, H, HP)[:, :, :MLA_NOPE]
    vp = wp[:, H * HP:].reshape(KV_LORA, H, HP)[:, :, :MLA_V]
    return jnp.concatenate([kp, vp], axis=2).reshape(KV_LORA, H * (MLA_NOPE + MLA_V))


def _pad_heads_rows(w, width):
    N = w.shape[1]
    return jnp.pad(w.reshape(H, width, N), ((0, 0), (0, HP - width), (0, 0))).reshape(H * HP, N)


def _unpad_heads_rows(w, width):
    N = w.shape[1]
    return w.reshape(H, HP, N)[:, :width, :].reshape(H * width, N)


def _rope_tables(positions):
    inv_freq = 1.0 / (ROPE_THETA ** (jnp.arange(0, MLA_ROPE, 2, dtype=F32) / MLA_ROPE))
    ang = positions.reshape(S, 1).astype(F32) * inv_freq
    cos, sin = jnp.cos(ang), jnp.sin(ang)
    ones = jnp.ones((S, KRIN_LANE), F32)
    tail = jnp.zeros((S, LANES - KRIN_LANE - MLA_ROPE), F32)
    ctab = jnp.concatenate([ones, cos, cos, tail], axis=1)
    stab = jnp.concatenate([0.0 * ones, -sin, sin, tail], axis=1)
    return ctab, stab


def _local_step(x, target, mod, positions, gains, bf, W, late=None):
    W = dict(W)
    sh1, sc1, gt1, sh2, sc2, gt2 = (mod[:, i * D:(i + 1) * D] for i in range(6))
    ops1, ops2 = 1.0 + sc1, 1.0 + sc2
    ones = lambda w: jnp.ones((1, w), F32)
    zeros = lambda w: jnp.zeros((1, w), F32)
    bf_blk = _pad_block(bf)
    ctab, stab = _rope_tables(positions)
    fox_scale = 1.0 / math.sqrt(FOX_HD)
    mla_scale = 1.0 / math.sqrt(MLA_NOPE + MLA_ROPE)

    h1 = _norm_mod(x, 0, D, gains["g_pre_mix"], ops1, sh1, name="f_pre_mix")
    qkv = _matmul(h1, W["w_in_qkv"], out_dtype=_MXU, name="f_proj_qkv")
    rest = _matmul(h1, W["w_in_rest"], name="f_proj_rest", tn_cap=256)
    F = _fox_prep(rest, bf_blk, name="f_fox_prep")
    Ft = F[:, :H].T
    fcol, frow = Ft.reshape(H, S, 1), Ft.reshape(H, S // TKF, 1, TKF)
    qa, ka, va = qkv[:, :H * HP], qkv[:, H * HP:2 * H * HP], qkv[:, 2 * H * HP:]
    job = _gather_level1_job([late[n] for n in _LATE]) if late else None
    oa, lse_a, *landed = _attn_fwd(qa, ka, va, frow, scale=fox_scale, name="f_attn_fox", side=job)

    cqn = _norm_mod(rest, OFF_CQ // Q_LORA, Q_LORA, gains["g_q_lora"], ones(Q_LORA), zeros(Q_LORA), name="f_norm_cq")
    ckvn = _norm_mod(rest, OFF_CKV // KV_LORA, KV_LORA, gains["g_kv_lora"], ones(KV_LORA), zeros(KV_LORA), name="f_norm_ckv")
    qb = _matmul(cqn, W["w_uq"], name="f_uq")
    kvb = _matmul(ckvn, W["w_ukv"], name="f_ukv")
    qm, km, vm = _mla_assemble(qb, kvb, rest, ctab, stab, name="f_mla_assemble")
    job = _gather_level2_job(landed) if late else None
    ob, lse_b, *landed = _attn_fwd(qm, km, vm, None, scale=mla_scale, name="f_attn_mla", side=job)
    if late:
        W.update(_late_weights({n: g.reshape(4, 2 * g.shape[1], g.shape[2]) for n, g in zip(_LATE, landed)}))

    pa = _matmul(oa, W["w_proj_fox"], name="f_proj_fox")
    pb = _matmul(ob, W["w_proj_mla"], name="f_proj_mla")
    merged = _merge(rest, pa, pb, name="f_merge")
    y1 = _matmul(merged, W["w_out"], name="f_out")
    x2 = _post_res(x, y1, gains["g_post_mix"], gt1, name="f_post_mix")
    h2 = _norm_mod(x2, 0, D, gains["g_pre_ffn"], ops2, sh2, name="f_pre_ffn")
    gu = _matmul(h2, W["w_ffn_in"], name="f_ffn_in", b_shards=True, tn_cap=1408)
    act = _swiglu(gu, name="f_swiglu")
    y2 = _matmul(act, W["w_ffn_out"], name="f_ffn_out", tk_cap=1408)
    dout, loss = _post_res_loss(x2, y2, gains["g_post_ffn"], gt2, target, name="f_post_ffn_loss")

    dy2, s_gt2, s_gpost2 = _post_res_bwd(dout, y2, gains["g_post_ffn"], gt2, name="b_post_ffn")
    dact = _matmul(dy2, W["w_ffn_out"], tb=True, name="b_ffn_out_dx", tn_cap=1408)
    dW_ffn_out = _matmul(act, dy2, ta=True, name="b_ffn_out_dw", tm_cap=1408)
    dgu = _swiglu_bwd(gu, dact, name="b_swiglu")
    dh2 = _matmul(dgu, W["w_ffn_in"], tb=True, b_shards=True, name="b_ffn_in_dx", tk_cap=1408)
    dW_ffn_in = _matmul(h2, dgu, ta=True, name="b_ffn_in_dw", out_shards=True, tn_cap=1408)
    dx2, s_sh2, s_a2 = _norm_mod_bwd(x2, 0, D, dh2, gains["g_pre_ffn"], ops2, dout, name="b_pre_ffn")
    dy1, s_gt1, s_gpost1 = _post_res_bwd(dx2, y1, gains["g_post_mix"], gt1, name="b_post_mix")
    dmerged = _matmul(dy1, W["w_out"], tb=True, name="b_out_dx")
    dW_out = _matmul(merged, dy1, ta=True, name="b_out_dw")
    dpa, dpb, dgfox, dgmla = _merge_bwd(rest, pa, pb, dmerged, name="b_merge")
    doa = _matmul(dpa, W["w_proj_fox"], tb=True, name="b_proj_fox_dx")
    dW_proj_fox = _matmul(oa, dpa, ta=True, name="b_proj_fox_dw")
    dob = _matmul(dpb, W["w_proj_mla"], tb=True, name="b_proj_mla_dx")
    dW_proj_mla = _matmul(ob, dpb, ta=True, name="b_proj_mla_dw")

    delta_a, doa16 = _attn_delta(doa, oa, name="b_delta_fox")
    as_rows = lambda a: a.reshape(H, S // TQB, 1, TQB)
    dW = dict(w_proj_fox=dW_proj_fox, w_proj_mla=dW_proj_mla, w_out=dW_out, w_ffn_in=dW_ffn_in, w_ffn_out=dW_ffn_out)
    job_a = job_b = None
    if late:
        late_shards = _grad_shards(dW)
        pieces = {n: s.reshape(8, s.shape[1] // 2, s.shape[2]).astype(BF16) for n, s in late_shards.items()}
        job_a = _scatter_direct_job([pieces[n] for n in _SCATTER_A])
        job_b = _scatter_direct_job([pieces[n] for n in _SCATTER_B])
    dqa, dka, dva, *landed_a = _attn_bwd(qa, ka, va, doa16, as_rows(lse_a), as_rows(delta_a), fcol, scale=fox_scale,
                                         name="b_attn_fox", side=job_a)
    delta_b, dob16 = _attn_delta(dob, ob, name="b_delta_mla")
    dqm, dkm, dvm, *landed_b = _attn_bwd(qm, km, vm, dob16, as_rows(lse_b), as_rows(delta_b), None, scale=mla_scale,
                                         name="b_attn_mla", side=job_b)
    reduced = {}
    if late:
        order = _SCATTER_A + _SCATTER_B
        done = _scatter_finish([late_shards[n] for n in order], landed_a + landed_b, name="scatter_late")
        reduced = dict(zip(order, done))
        dW = {}

    dF = (dqa[:, ROW_SUM_LANE::HP] - dka[:, COL_SUM_LANE::HP]) * (1.0 / fox_scale)
    dflog, s_bf = _fox_bwd_prep(rest, bf_blk, _pad_block(dF), name="b_fox_prep")

    dqb, dkvb, dkrin = _mla_assemble_bwd(dqm, dkm, dvm, ctab, stab, name="b_mla_assemble")
    dcqn = _matmul(dqb, W["w_uq"], tb=True, name="b_uq_dx")
    dW_uq = _matmul(cqn, dqb, ta=True, name="b_uq_dw")
    dckvn = _matmul(dkvb, W["w_ukv"], tb=True, name="b_ukv_dx")
    dW_ukv = _matmul(ckvn, dkvb, ta=True, name="b_ukv_dw")
    dcq, _, s_gq = _norm_mod_bwd(rest, OFF_CQ // Q_LORA, Q_LORA, dcqn, gains["g_q_lora"], ones(Q_LORA), None, name="b_norm_cq")
    dckv, _, s_gkv = _norm_mod_bwd(rest, OFF_CKV // KV_LORA, KV_LORA, dckvn, gains["g_kv_lora"], ones(KV_LORA), None, name="b_norm_ckv")

    c16 = lambda a: a.astype(_MXU)
    dproj = jnp.concatenate([c16(dqa), c16(dka), c16(dva), c16(dcq), c16(dckv), dgfox, dgmla, c16(dflog), c16(dkrin)], axis=1)
    w_in_full = jnp.concatenate([W["w_in_qkv"], W["w_in_rest"]], axis=1)
    dh1 = _matmul(dproj, w_in_full, tb=True, name="b_in_dx", tk_cap=1280)
    dW_in = _matmul(h1, dproj, ta=True, name="b_in_dw", tn_cap=640)
    grad_x, s_sh1, s_a1 = _norm_mod_bwd(x, 0, D, dh1, gains["g_pre_mix"], ops1, dx2, name="b_pre_mix")

    dmod = jnp.concatenate([s_sh1, s_a1 * gains["g_pre_mix"], s_gt1, s_sh2, s_a2 * gains["g_pre_ffn"], s_gt2], axis=1)
    small = dict(dmod=dmod, g_pre_mix=s_a1 * ops1, g_post_mix=s_gpost1, g_pre_ffn=s_a2 * ops2, g_post_ffn=s_gpost2,
                 g_q_lora=s_gq, g_kv_lora=s_gkv, b_forget=s_bf)
    dW = dict(dW, w_in=dW_in, w_uq=dW_uq, w_ukv=dW_ukv)
    return loss, grad_x, dW, reduced, small


_BIG = ["w_in", "w_uq", "w_ukv", "w_proj_fox", "w_proj_mla", "w_out", "w_ffn_in", "w_ffn_out"]
_COL_SHARDED = {"w_in", "w_ukv", "w_proj_fox", "w_proj_mla", "w_ffn_in"}
_SMALL = ["b_ada", "g_pre_mix", "g_post_mix", "g_pre_ffn", "g_post_ffn", "b_forget", "g_q_lora", "g_kv_lora"]
_ORDER = ["w_ada", "b_ada", "g_pre_mix", "g_post_mix", "g_pre_ffn", "g_post_ffn", "w_in", "b_forget", "g_q_lora", "w_uq",
          "g_kv_lora", "w_ukv", "w_proj_fox", "w_proj_mla", "w_out", "w_ffn_in", "w_ffn_out"]
_ROW = {}
_off = 0
for _n, _w in [("dmod", 6 * D), ("g_pre_mix", D), ("g_post_mix", D), ("g_pre_ffn", D), ("g_post_ffn", D), ("g_q_lora", Q_LORA),
               ("g_kv_lora", KV_LORA), ("b_forget", LANES), ("loss", LANES)]:
    _ROW[_n] = (_off, _w)
    _off += _w
_ROW_LEN = _off


_EARLY = ["w_in", "w_uq", "w_ukv"]
_LATE = ["w_proj_fox", "w_proj_mla", "w_out", "w_ffn_in", "w_ffn_out"]
_SCATTER_A = ["w_ffn_in"]
_SCATTER_B = ["w_ffn_out", "w_out", "w_proj_fox", "w_proj_mla"]


def _rows_from_shards(g):
    return g.reshape(-1, g.shape[2])


def _early_weights(G):
    w_in = _pad_w_in(_cols_from_shards(G["w_in"]))
    return dict(
        w_in_qkv=w_in[:, :NQKV], w_in_rest=w_in[:, NQKV:],
        w_uq=_pad_heads_cols(_rows_from_shards(G["w_uq"]), MLA_NOPE + MLA_ROPE),
        w_ukv=_pad_w_ukv(_cols_from_shards(G["w_ukv"])))


def _late_weights(G):
    return dict(
        w_proj_fox=_pad_heads_rows(_cols_from_shards(G["w_proj_fox"]), FOX_HD),
        w_proj_mla=_pad_heads_rows(_cols_from_shards(G["w_proj_mla"]), MLA_V),
        w_out=_rows_from_shards(G["w_out"]), w_ffn_in=G["w_ffn_in"], w_ffn_out=_rows_from_shards(G["w_ffn_out"]))


def _full_weights(G):
    return {**_early_weights(G), **_late_weights(G)}


_UNPAD = dict(
    w_in=_unpad_w_in, w_uq=lambda g: _unpad_heads_cols(g, MLA_NOPE + MLA_ROPE), w_ukv=_unpad_w_ukv,
    w_proj_fox=lambda g: _unpad_heads_rows(g, FOX_HD), w_proj_mla=lambda g: _unpad_heads_rows(g, MLA_V),
    w_out=lambda g: g, w_ffn_out=lambda g: g)


def _grad_shards(dW):
    out = {}
    for n, g in dW.items():
        if n == "w_ffn_in":
            out[n] = g
        else:
            nat = _UNPAD[n](g)
            out[n] = _cols_to_shards(nat) if n in _COL_SHARDED else nat.reshape(4, nat.shape[0] // 4, nat.shape[1])
    return out


def kernel(x, c, positions, w_ada, b_ada, g_pre_mix, g_post_mix, g_pre_ffn, g_post_ffn, w_in, b_forget, g_q_lora, w_uq, g_kv_lora, w_ukv, w_proj_fox, w_proj_mla, w_out, w_ffn_in, w_ffn_out, loss_target, m_w_ada, m_b_ada, m_g_pre_mix, m_g_post_mix, m_g_pre_ffn, m_g_post_ffn, m_w_in, m_b_forget, m_g_q_lora, m_w_uq, m_g_kv_lora, m_w_ukv, m_w_proj_fox, m_w_proj_mla, m_w_out, m_w_ffn_in, m_w_ffn_out, v_w_ada, v_b_ada, v_g_pre_mix, v_g_post_mix, v_g_pre_ffn, v_g_post_ffn, v_w_in, v_b_forget, v_g_q_lora, v_w_uq, v_g_kv_lora, v_w_ukv, v_w_proj_fox, v_w_proj_mla, v_w_out, v_w_ffn_in, v_w_ffn_out):
    P = dict(w_ada=w_ada, b_ada=b_ada, g_pre_mix=g_pre_mix, g_post_mix=g_post_mix, g_pre_ffn=g_pre_ffn, g_post_ffn=g_post_ffn,
             w_in=w_in, b_forget=b_forget, g_q_lora=g_q_lora, w_uq=w_uq, g_kv_lora=g_kv_lora, w_ukv=w_ukv,
             w_proj_fox=w_proj_fox, w_proj_mla=w_proj_mla, w_out=w_out, w_ffn_in=w_ffn_in, w_ffn_out=w_ffn_out)
    M = dict(w_ada=m_w_ada, b_ada=m_b_ada, g_pre_mix=m_g_pre_mix, g_post_mix=m_g_post_mix, g_pre_ffn=m_g_pre_ffn,
             g_post_ffn=m_g_post_ffn, w_in=m_w_in, b_forget=m_b_forget, g_q_lora=m_g_q_lora, w_uq=m_w_uq, g_kv_lora=m_g_kv_lora,
             w_ukv=m_w_ukv, w_proj_fox=m_w_proj_fox, w_proj_mla=m_w_proj_mla, w_out=m_w_out, w_ffn_in=m_w_ffn_in,
             w_ffn_out=m_w_ffn_out)
    V = dict(w_ada=v_w_ada, b_ada=v_b_ada, g_pre_mix=v_g_pre_mix, g_post_mix=v_g_post_mix, g_pre_ffn=v_g_pre_ffn,
             g_post_ffn=v_g_post_ffn, w_in=v_w_in, b_forget=v_b_forget, g_q_lora=v_g_q_lora, w_uq=v_w_uq, g_kv_lora=v_g_kv_lora,
             w_ukv=v_w_ukv, w_proj_fox=v_w_proj_fox, w_proj_mla=v_w_proj_mla, w_out=v_w_out, w_ffn_in=v_w_ffn_in,
             w_ffn_out=v_w_ffn_out)
    ax, ay, ac = lax.axis_index("x"), lax.axis_index("y"), lax.axis_index("c")
    chip = 2 * ax + ay
    me = 4 * ax + 2 * ay + ac
    n_ada = w_ada.shape[2]

    c_all = _all_gather_rows(jnp.pad(c, ((0, 7), (0, 0))), name="gather_c")[:, 0, :]
    c_all = jnp.pad(c_all, ((0, 8), (0, 0)))
    b_shard = lax.dynamic_slice(b_ada, (0, chip * n_ada), (1, n_ada))
    mod_blk, silu_c = _ada_mod(c_all, w_ada[0], b_shard, name="ada_mod")
    mod_all = _all_gather_rows(mod_blk, name="gather_mod")
    mod_mine = lax.dynamic_index_in_dim(mod_all, me, axis=1, keepdims=False)
    mod = lax.dynamic_index_in_dim(mod_mine.reshape(4, 2, n_ada), ac, axis=1, keepdims=False).reshape(1, 6 * D)

    W = _early_weights({n: _gather_weight(P[n][0], name="gather_" + n) for n in _EARLY})
    late = {}
    for n in _LATE:
        rh = P[n].shape[1] // 2
        late[n] = lax.dynamic_slice_in_dim(P[n][0], ac * rh, rh, axis=0).astype(BF16)

    gains = {n: P[n] for n in ["g_pre_mix", "g_post_mix", "g_pre_ffn", "g_post_ffn", "g_q_lora", "g_kv_lora"]}
    loss, grad_x, dW, grads, small = _local_step(x[0], loss_target[0], mod, positions, gains, b_forget, W, late)

    shards = _grad_shards(dW)
    grads.update({n: _reduce_scatter_weight(shards[n], name="scatter_" + n) for n in shards})

    small = dict(small, loss=_pad_block(loss))
    row = jnp.concatenate([small[n] for n in _ROW], axis=1)
    rows = _all_gather_rows(jnp.pad(row, ((0, 7), (0, 0))), name="gather_small")[:, 0, :]
    tot = _rowsum(rows, name="sum_small")
    piece = lambda n: tot[:, _ROW[n][0]:_ROW[n][0] + _ROW[n][1]]
    grads["b_ada"] = piece("dmod")
    for n in ["g_pre_mix", "g_post_mix", "g_pre_ffn", "g_post_ffn", "g_q_lora", "g_kv_lora"]:
        grads[n] = piece(n)
    grads["b_forget"] = piece("b_forget")[:, :H]
    loss_out = piece("loss")[0, 0]
    dmod_all = rows[:, _ROW["dmod"][0]:_ROW["dmod"][0] + 6 * D]
    dmod_shard = jnp.pad(lax.dynamic_slice(dmod_all, (0, chip * n_ada), (8, n_ada)), ((0, 8), (0, 0)))
    grads["w_ada"] = _matmul(silu_c, dmod_shard, ta=True, name="ada_dw")

    delta, new_m, new_v = {}, {}, {}
    for n in ["w_ada"] + _BIG:
        delta[n], new_m[n], new_v[n] = _adamw(P[n], grads[n], M[n], V[n], name="adamw_" + n)
    cat = lambda T: jnp.concatenate([T[n] for n in _SMALL], axis=1)
    d_s, m_s, v_s = (t[0] for t in _adamw(cat(P)[None], cat(grads), cat(M)[None], cat(V)[None], name="adamw_small"))
    o = 0
    for n in _SMALL:
        wdt = P[n].shape[1]
        delta[n], new_m[n], new_v[n] = d_s[:, o:o + wdt], m_s[:, o:o + wdt], v_s[:, o:o + wdt]
        o += wdt

    def shaped(T, n):
        return T[n].reshape(P[n].shape)

    return (loss_out, grad_x[None], *[shaped(grads, n) for n in _ORDER], *[shaped(delta, n) for n in _ORDER],
            *[shaped(new_m, n) for n in _ORDER], *[shaped(new_v, n) for n in _ORDER])
```

```python
import functools
import math
from typing import Callable, NamedTuple

import jax
import jax.numpy as jnp
from jax import lax
from jax.experimental import pallas as pl
from jax.experimental.pallas import tpu as pltpu

F32 = jnp.float32
BF16 = jnp.bfloat16
_MXU = jnp.bfloat16

S = 2048
D = 1024
H = 8
HP = 128
FOX_HD = 64
MLA_NOPE = 64
MLA_ROPE = 32
MLA_V = 64
Q_LORA = 768
KV_LORA = 256
D_FF = 2816
NORM_EPS = 1e-6
ROPE_THETA = 10000.0
NEG = -1e30

ADAM_LR = 0.001
ADAM_B1 = 0.9
ADAM_B2 = 0.999
ADAM_EPS = 1e-08
ADAM_WD = 0.01
ADAM_STEP = 10

LANES = 128
VMEM_CAP = 60 * 1024 * 1024
MESH = pl.DeviceIdType.MESH

NQKV = 3 * H * HP
OFF_CQ = 0
OFF_CKV = Q_LORA
OFF_GFOX = 1024
OFF_GMLA = 2048
OFF_FLOG = 3072
OFF_KRIN = 3200
NREST = 3328
KRIN_LANE = 64
ROW_SUM_LANE = 64
COL_SUM_LANE = 65


def _limit(nbytes):
    return int(min(VMEM_CAP, nbytes * 1.25 + (4 << 20)))


def _nbytes(shape, dtype):
    n = 1
    for s in shape:
        n *= s
    return n * jnp.dtype(dtype).itemsize


def _pick(n, cap):
    best = None
    for t in range(LANES, min(n, cap) + 1, LANES):
        if n % t == 0:
            best = t
    return best if best is not None else n


def _pcall(body, *, out_shape, **kw):
    outs = jax.tree.map(lambda s: pltpu.HBM(s.shape, s.dtype), out_shape)
    call = pl.pallas_call(body, out_shape=outs, **kw)
    return lambda *args: call(*[pltpu.with_memory_space_constraint(a, pltpu.HBM) for a in args])


def _matmul(a, b, *, ta=False, tb=False, out_dtype=F32, name, tm_cap=1024, tn_cap=512, tk_cap=1024,
            b_shards=False, out_shards=False):
    if ta:
        K, M = a.shape
    else:
        M, K = a.shape
    if b_shards:
        _, R, cb = b.shape
        N, K2 = (R, 4 * cb) if tb else (4 * cb, R)
    elif tb:
        N, K2 = b.shape
    else:
        K2, N = b.shape
    assert K == K2, (a.shape, b.shape, ta, tb)
    tm = _pick(M, tm_cap)
    tn = _pick(N, tn_cap)
    tk = K if K <= tk_cap else _pick(K, tk_cap)
    nk = K // tk
    dims = (((0 if ta else 1,), (1 if tb else 0,)), ((), ()))

    def body(a_ref, b_ref, o_ref, acc_ref):
        k = pl.program_id(2)

        @pl.when(k == 0)
        def _():
            acc_ref[...] = jnp.zeros_like(acc_ref)

        acc_ref[...] += lax.dot_general(a_ref[...].astype(_MXU), b_ref[...].astype(_MXU), dims,
                                        preferred_element_type=F32)

        @pl.when(k == nk - 1)
        def _():
            o_ref[...] = acc_ref[...].astype(out_dtype)

    a_spec = pl.BlockSpec((tk, tm), lambda i, j, k: (k, i)) if ta else pl.BlockSpec((tm, tk), lambda i, j, k: (i, k))
    if b_shards and tb:
        assert cb % tk == 0
        per = cb // tk
        b_spec = pl.BlockSpec((None, tn, tk), lambda i, j, k: (k // per, j, k % per))
    elif b_shards:
        assert cb % tn == 0
        per = cb // tn
        b_spec = pl.BlockSpec((None, tk, tn), lambda i, j, k: (j // per, k, j % per))
    elif tb:
        b_spec = pl.BlockSpec((tn, tk), lambda i, j, k: (j, k))
    else:
        b_spec = pl.BlockSpec((tk, tn), lambda i, j, k: (k, j))
    if out_shards:
        assert (N // 4) % tn == 0
        pern = N // 4 // tn
        out_shape = jax.ShapeDtypeStruct((4, M, N // 4), out_dtype)
        out_spec = pl.BlockSpec((None, tm, tn), lambda i, j, k: (j // pern, i, j % pern))
    else:
        out_shape = jax.ShapeDtypeStruct((M, N), out_dtype)
        out_spec = pl.BlockSpec((tm, tn), lambda i, j, k: (i, j))
    need = (2 * _nbytes((tm, tk), a.dtype) + 2 * _nbytes((tk, tn), b.dtype) + 2 * _nbytes((tm, tn), out_dtype)
            + _nbytes((tm, tn), F32) * 2 + _nbytes((tm, tk), _MXU) + _nbytes((tk, tn), _MXU))
    return _pcall(
        body, name=name,
        out_shape=out_shape,
        grid=(M // tm, N // tn, nk),
        in_specs=[a_spec, b_spec],
        out_specs=out_spec,
        scratch_shapes=[pltpu.VMEM((tm, tn), F32)],
        compiler_params=pltpu.CompilerParams(dimension_semantics=("parallel", "parallel", "arbitrary"),
                                             vmem_limit_bytes=_limit(need)),
    )(a, b)


TM = 256


def _vec(w):
    return pl.BlockSpec((1, w), lambda i: (0, 0))


def _rows(w, col=0):
    return pl.BlockSpec((TM, w), lambda i: (i, col))


def _row_params(need, carried=False):
    return pltpu.CompilerParams(dimension_semantics=("arbitrary" if carried else "parallel",),
                                vmem_limit_bytes=_limit(need))


def _norm_mod(x, col, w, g, ops, sh, *, name):
    def body(x_ref, g_ref, ops_ref, sh_ref, o_ref):
        xv = x_ref[...]
        r = lax.rsqrt(jnp.mean(xv * xv, axis=-1, keepdims=True) + NORM_EPS)
        o_ref[...] = (((xv * r) * g_ref[...]) * ops_ref[...] + sh_ref[...]).astype(o_ref.dtype)

    return _pcall(
        body, name=name, out_shape=jax.ShapeDtypeStruct((S, w), _MXU), grid=(S // TM,),
        in_specs=[_rows(w, col), _vec(w), _vec(w), _vec(w)], out_specs=_rows(w),
        compiler_params=_row_params(8 * _nbytes((TM, w), F32)),
    )(x, g, ops, sh)


def _norm_mod_bwd(x, col, w, dh, g, ops, dres, *, name):
    has_res = dres is not None

    def body(*refs):
        if has_res:
            x_ref, dh_ref, g_ref, ops_ref, dres_ref, dx_ref, s1_ref, s2_ref = refs
        else:
            x_ref, dh_ref, g_ref, ops_ref, dx_ref, s1_ref, s2_ref = refs
        i = pl.program_id(0)

        @pl.when(i == 0)
        def _():
            s1_ref[...] = jnp.zeros_like(s1_ref)
            s2_ref[...] = jnp.zeros_like(s2_ref)

        xv = x_ref[...]
        dhv = dh_ref[...]
        r = lax.rsqrt(jnp.mean(xv * xv, axis=-1, keepdims=True) + NORM_EPS)
        xn = xv * r
        dxn = dhv * (g_ref[...] * ops_ref[...])
        dx = r * (dxn - xn * jnp.mean(dxn * xn, axis=-1, keepdims=True))
        if has_res:
            dx = dx + dres_ref[...]
        dx_ref[...] = dx
        s1_ref[...] += jnp.sum(dhv, axis=0, keepdims=True)
        s2_ref[...] += jnp.sum(dhv * xn, axis=0, keepdims=True)

    in_specs = [_rows(w, col), _rows(w), _vec(w), _vec(w)] + ([_rows(w)] if has_res else [])
    args = (x, dh, g, ops) + ((dres,) if has_res else ())
    return _pcall(
        body, name=name,
        out_shape=(jax.ShapeDtypeStruct((S, w), F32), jax.ShapeDtypeStruct((1, w), F32), jax.ShapeDtypeStruct((1, w), F32)),
        grid=(S // TM,), in_specs=in_specs, out_specs=(_rows(w), _vec(w), _vec(w)),
        compiler_params=_row_params(12 * _nbytes((TM, w), F32), carried=True),
    )(*args)


def _post_res(xres, y, g, gt, *, name):
    def body(x_ref, y_ref, g_ref, gt_ref, o_ref):
        yv = y_ref[...]
        r = lax.rsqrt(jnp.mean(yv * yv, axis=-1, keepdims=True) + NORM_EPS)
        o_ref[...] = x_ref[...] + gt_ref[...] * ((yv * r) * g_ref[...])

    return _pcall(
        body, name=name, out_shape=jax.ShapeDtypeStruct((S, D), F32), grid=(S // TM,),
        in_specs=[_rows(D), _rows(D), _vec(D), _vec(D)], out_specs=_rows(D),
        compiler_params=_row_params(8 * _nbytes((TM, D), F32)),
    )(xres, y, g, gt)


def _post_res_loss(xres, y, g, gt, target, *, name):
    def body(x_ref, y_ref, g_ref, gt_ref, t_ref, dout_ref, loss_ref):
        i = pl.program_id(0)

        @pl.when(i == 0)
        def _():
            loss_ref[...] = jnp.zeros_like(loss_ref)

        yv = y_ref[...]
        r = lax.rsqrt(jnp.mean(yv * yv, axis=-1, keepdims=True) + NORM_EPS)
        out = x_ref[...] + gt_ref[...] * ((yv * r) * g_ref[...])
        err = out - t_ref[...]
        dout_ref[...] = err * (1.0 / D)
        per_row = jnp.mean(err * err, axis=-1, keepdims=True)
        loss_ref[...] += 0.5 * jnp.sum(per_row, axis=0, keepdims=True)

    return _pcall(
        body, name=name,
        out_shape=(jax.ShapeDtypeStruct((S, D), F32), jax.ShapeDtypeStruct((1, 1), F32)), grid=(S // TM,),
        in_specs=[_rows(D), _rows(D), _vec(D), _vec(D), _rows(D)],
        out_specs=(_rows(D), pl.BlockSpec((1, 1), lambda i: (0, 0))),
        compiler_params=_row_params(10 * _nbytes((TM, D), F32), carried=True),
    )(xres, y, g, gt, target)


def _post_res_bwd(dxn, y, g, gt, *, name):
    def body(d_ref, y_ref, g_ref, gt_ref, dy_ref, sgt_ref, sg_ref):
        i = pl.program_id(0)

        @pl.when(i == 0)
        def _():
            sgt_ref[...] = jnp.zeros_like(sgt_ref)
            sg_ref[...] = jnp.zeros_like(sg_ref)

        yv = y_ref[...]
        dv = d_ref[...]
        r = lax.rsqrt(jnp.mean(yv * yv, axis=-1, keepdims=True) + NORM_EPS)
        yn = yv * r
        dn = dv * gt_ref[...]
        dyn = dn * g_ref[...]
        dy_ref[...] = (r * (dyn - yn * jnp.mean(dyn * yn, axis=-1, keepdims=True))).astype(dy_ref.dtype)
        sgt_ref[...] += jnp.sum(dv * (yn * g_ref[...]), axis=0, keepdims=True)
        sg_ref[...] += jnp.sum(dn * yn, axis=0, keepdims=True)

    return _pcall(
        body, name=name,
        out_shape=(jax.ShapeDtypeStruct((S, D), _MXU), jax.ShapeDtypeStruct((1, D), F32), jax.ShapeDtypeStruct((1, D), F32)),
        grid=(S // TM,), in_specs=[_rows(D), _rows(D), _vec(D), _vec(D)], out_specs=(_rows(D), _vec(D), _vec(D)),
        compiler_params=_row_params(10 * _nbytes((TM, D), F32), carried=True),
    )(dxn, y, g, gt)


def _swiglu(gu, *, name):
    def body(g_ref, u_ref, o_ref):
        gv = g_ref[...]
        o_ref[...] = ((gv * jax.nn.sigmoid(gv)) * u_ref[...]).astype(o_ref.dtype)

    return _pcall(
        body, name=name, out_shape=jax.ShapeDtypeStruct((S, D_FF), _MXU), grid=(S // TM,),
        in_specs=[_rows(D_FF, 0), _rows(D_FF, 1)], out_specs=_rows(D_FF),
        compiler_params=_row_params(8 * _nbytes((TM, D_FF), F32)),
    )(gu, gu)


def _swiglu_bwd(gu, dact, *, name):
    def body(g_ref, u_ref, d_ref, o_ref):
        gv = g_ref[...]
        dv = d_ref[...]
        sg = jax.nn.sigmoid(gv)
        o_ref[:, :D_FF] = (dv * u_ref[...] * (sg * (1.0 + gv * (1.0 - sg)))).astype(o_ref.dtype)
        o_ref[:, D_FF:] = (dv * (gv * sg)).astype(o_ref.dtype)

    return _pcall(
        body, name=name, out_shape=jax.ShapeDtypeStruct((S, 2 * D_FF), _MXU), grid=(S // TM,),
        in_specs=[_rows(D_FF, 0), _rows(D_FF, 1), _rows(D_FF)], out_specs=_rows(2 * D_FF),
        compiler_params=_row_params(12 * _nbytes((TM, D_FF), F32)),
    )(gu, gu, dact)


def _merge(rest, pa, pb, *, name):
    def body(ga_ref, gb_ref, pa_ref, pb_ref, o_ref):
        o_ref[...] = (jax.nn.sigmoid(ga_ref[...]) * pa_ref[...] + jax.nn.sigmoid(gb_ref[...]) * pb_ref[...]).astype(o_ref.dtype)

    return _pcall(
        body, name=name, out_shape=jax.ShapeDtypeStruct((S, D), _MXU), grid=(S // TM,),
        in_specs=[_rows(D, OFF_GFOX // D), _rows(D, OFF_GMLA // D), _rows(D), _rows(D)], out_specs=_rows(D),
        compiler_params=_row_params(10 * _nbytes((TM, D), F32)),
    )(rest, rest, pa, pb)


def _merge_bwd(rest, pa, pb, dm, *, name):
    def body(ga_ref, gb_ref, pa_ref, pb_ref, d_ref, dpa_ref, dpb_ref, dga_ref, dgb_ref):
        dv = d_ref[...]
        sa = jax.nn.sigmoid(ga_ref[...])
        sb = jax.nn.sigmoid(gb_ref[...])
        dpa_ref[...] = (dv * sa).astype(dpa_ref.dtype)
        dpb_ref[...] = (dv * sb).astype(dpb_ref.dtype)
        dga_ref[...] = (dv * pa_ref[...] * (sa * (1.0 - sa))).astype(dga_ref.dtype)
        dgb_ref[...] = (dv * pb_ref[...] * (sb * (1.0 - sb))).astype(dgb_ref.dtype)

    o = jax.ShapeDtypeStruct((S, D), _MXU)
    return _pcall(
        body, name=name, out_shape=(o, o, o, o), grid=(S // TM,),
        in_specs=[_rows(D, OFF_GFOX // D), _rows(D, OFF_GMLA // D), _rows(D), _rows(D), _rows(D)],
        out_specs=(_rows(D), _rows(D), _rows(D), _rows(D)),
        compiler_params=_row_params(16 * _nbytes((TM, D), F32)),
    )(rest, rest, pa, pb, dm)


SCAN = 256


def _split_dot(tri, x):
    hi = x.astype(_MXU)
    r1 = x - hi.astype(F32)
    mid = r1.astype(_MXU)
    lo = (r1 - mid.astype(F32)).astype(_MXU)
    dot = functools.partial(jnp.dot, preferred_element_type=F32)
    return dot(tri, hi) + dot(tri, mid) + dot(tri, lo)


def _fox_prep(rest, bf, *, name):
    def body(z_ref, b_ref, f_ref):
        lane = lax.broadcasted_iota(jnp.int32, (SCAN, LANES), 1)
        tri = (lax.broadcasted_iota(jnp.int32, (SCAN, SCAN), 1) <= lax.broadcasted_iota(jnp.int32, (SCAN, SCAN), 0)).astype(_MXU)
        carry = jnp.zeros((1, LANES), F32)
        for c in range(S // SCAN):
            z = z_ref[c * SCAN:(c + 1) * SCAN, :] + b_ref[...]
            lf = jnp.minimum(z, 0.0) - jnp.log(1.0 + jnp.exp(-jnp.abs(z)))
            lf = jnp.where(lane < H, lf, 0.0)
            cum = _split_dot(tri, lf) + carry
            f_ref[c * SCAN:(c + 1) * SCAN, :] = cum
            carry = cum[SCAN - 1:SCAN, :]

    return _pcall(
        body, name=name, out_shape=jax.ShapeDtypeStruct((S, LANES), F32), grid=(1,),
        in_specs=[pl.BlockSpec((S, LANES), lambda i: (0, OFF_FLOG // LANES)), pl.BlockSpec((1, LANES), lambda i: (0, 0))],
        out_specs=pl.BlockSpec((S, LANES), lambda i: (0, 0)),
        compiler_params=pltpu.CompilerParams(vmem_limit_bytes=_limit(8 * _nbytes((S, LANES), F32))),
    )(rest, bf)


def _fox_bwd_prep(rest, bf, dF, *, name):
    def body(z_ref, b_ref, d_ref, o_ref, db_ref):
        lane = lax.broadcasted_iota(jnp.int32, (SCAN, LANES), 1)
        tri = (lax.broadcasted_iota(jnp.int32, (SCAN, SCAN), 1) >= lax.broadcasted_iota(jnp.int32, (SCAN, SCAN), 0)).astype(_MXU)
        carry = jnp.zeros((1, LANES), F32)
        db = jnp.zeros((1, LANES), F32)
        for c in range(S // SCAN - 1, -1, -1):
            rc = _split_dot(tri, d_ref[c * SCAN:(c + 1) * SCAN, :]) + carry
            z = z_ref[c * SCAN:(c + 1) * SCAN, :] + b_ref[...]
            dz = jnp.where(lane < H, rc * jax.nn.sigmoid(-z), 0.0)
            o_ref[c * SCAN:(c + 1) * SCAN, :] = dz
            db = db + jnp.sum(dz, axis=0, keepdims=True)
            carry = rc[0:1, :]
        db_ref[...] = db

    return _pcall(
        body, name=name,
        out_shape=(jax.ShapeDtypeStruct((S, LANES), F32), jax.ShapeDtypeStruct((1, LANES), F32)), grid=(1,),
        in_specs=[pl.BlockSpec((S, LANES), lambda i: (0, OFF_FLOG // LANES)), pl.BlockSpec((1, LANES), lambda i: (0, 0)),
                  pl.BlockSpec((S, LANES), lambda i: (0, 0))],
        out_specs=(pl.BlockSpec((S, LANES), lambda i: (0, 0)), pl.BlockSpec((1, LANES), lambda i: (0, 0))),
        compiler_params=pltpu.CompilerParams(vmem_limit_bytes=_limit(10 * _nbytes((S, LANES), F32))),
    )(rest, bf, dF)


def _swap16(x):
    lane = lax.broadcasted_iota(jnp.int32, x.shape, 1)
    half = MLA_ROPE // 2
    sw = jnp.where(lane < KRIN_LANE + half, pltpu.roll(x, LANES - half, 1), pltpu.roll(x, half, 1))
    return jnp.where((lane >= KRIN_LANE) & (lane < KRIN_LANE + MLA_ROPE), sw, 0.0)


def _mla_assemble(qb, kvb, rest, ctab, stab, *, name):
    def body(q_ref, kk_ref, kv_ref, kr_ref, c_ref, s_ref, qo_ref, ko_ref, vo_ref):
        cv = c_ref[...]
        sv = s_ref[...]
        kr = kr_ref[...]
        kpe = kr * cv + _swap16(kr) * sv
        for h in range(H):
            sl = slice(h * HP, (h + 1) * HP)
            qh = q_ref[:, sl]
            qo_ref[:, sl] = (qh * cv + _swap16(qh) * sv).astype(qo_ref.dtype)
            ko_ref[:, sl] = (kk_ref[:, sl] + kpe).astype(ko_ref.dtype)
        vo_ref[...] = kv_ref[...].astype(vo_ref.dtype)

    o = jax.ShapeDtypeStruct((S, H * HP), _MXU)
    return _pcall(
        body, name=name, out_shape=(o, o, o), grid=(S // TM,),
        in_specs=[_rows(H * HP), _rows(H * HP, 0), _rows(H * HP, 1), _rows(LANES, OFF_KRIN // LANES), _rows(LANES), _rows(LANES)],
        out_specs=(_rows(H * HP), _rows(H * HP), _rows(H * HP)),
        compiler_params=_row_params(14 * _nbytes((TM, H * HP), F32)),
    )(qb, kvb, kvb, rest, ctab, stab)


def _mla_assemble_bwd(dq, dk, dv, ctab, stab, *, name):
    def body(dq_ref, dk_ref, dv_ref, c_ref, s_ref, dqo_ref, dkv_ref, dkr_ref):
        cv = c_ref[...]
        sv = s_ref[...]
        lane = lax.broadcasted_iota(jnp.int32, (TM, LANES), 1)
        dsum = jnp.zeros((TM, LANES), F32)
        for h in range(H):
            sl = slice(h * HP, (h + 1) * HP)
            dqh = dq_ref[:, sl]
            dqo_ref[:, sl] = (dqh * cv + _swap16(dqh * sv)).astype(dqo_ref.dtype)
            dsum = dsum + dk_ref[:, sl]
        dkv_ref[:, :H * HP] = dk_ref[...].astype(dkv_ref.dtype)
        dkv_ref[:, H * HP:] = dv_ref[...].astype(dkv_ref.dtype)
        dkr = dsum * cv + _swap16(dsum * sv)
        dkr_ref[...] = jnp.where((lane >= KRIN_LANE) & (lane < KRIN_LANE + MLA_ROPE), dkr, 0.0)

    return _pcall(
        body, name=name,
        out_shape=(jax.ShapeDtypeStruct((S, H * HP), _MXU), jax.ShapeDtypeStruct((S, 2 * H * HP), _MXU),
                   jax.ShapeDtypeStruct((S, LANES), F32)),
        grid=(S // TM,),
        in_specs=[_rows(H * HP), _rows(H * HP), _rows(H * HP), _rows(LANES), _rows(LANES)],
        out_specs=(_rows(H * HP), _rows(2 * H * HP), _rows(LANES)),
        compiler_params=_row_params(14 * _nbytes((TM, H * HP), F32)),
    )(dq, dk, dv, ctab, stab)


TQ = 256
TKF = 512
TKB = 256
TQB = 512
_NT = (((1,), (1,)), ((), ()))
_TN = (((0,), (0,)), ((), ()))


def _is_pow2(x):
    return math.frexp(x)[0] == 0.5


def _attn_fwd(q, k, v, frow, *, scale, name, side=None, heads0=(0, 0, 0)):
    has_decay = frow is not None
    fold = _is_pow2(scale)
    n_in = 4 if has_decay else 3
    n_side_in = len(side.inputs) if side else 0
    n_side_out = len(side.out_shapes) if side else 0

    def body(*refs):
        q_ref, k_ref, v_ref = refs[:3]
        fr_ref = refs[3] if has_decay else None
        side_in = refs[n_in:n_in + n_side_in]
        o_ref, lse_ref = refs[n_in + n_side_in:n_in + n_side_in + 2]
        side_out = refs[n_in + n_side_in + 2:n_in + n_side_in + 2 + n_side_out]
        side_scratch = refs[n_in + n_side_in + 2 + n_side_out:]
        i = pl.program_id(1)
        if side:
            @pl.when((pl.program_id(0) == 0) & (i == 0))
            def _():
                side.start(side_in, side_out, side_scratch)
        qv = q_ref[...]
        if fold:
            qv = (qv * scale).astype(qv.dtype)
        last = (i * TQ) // TKF

        def tile(j, carry, masked):
            m, l, acc = carry
            k0 = pl.multiple_of(j * TKF, TKF)
            kj = k_ref[pl.ds(k0, TKF), :]
            vj = v_ref[pl.ds(k0, TKF), :]
            s = lax.dot_general(qv, kj, _NT, preferred_element_type=F32)
            if not fold:
                s = s * scale
            if has_decay:
                s = s - fr_ref[0, j]
            if masked:
                rows = i * TQ + lax.broadcasted_iota(jnp.int32, (TQ, TKF), 0)
                cols = j * TKF + lax.broadcasted_iota(jnp.int32, (TQ, TKF), 1)
                s = jnp.where(cols <= rows, s, NEG)
            m_new = jnp.maximum(m, jnp.max(s, axis=-1, keepdims=True))
            alpha = jnp.exp(m - m_new)
            p = jnp.exp(s - m_new)
            l = alpha * l + jnp.sum(p, axis=-1, keepdims=True)
            acc = alpha * acc + jnp.dot(p.astype(_MXU), vj, preferred_element_type=F32)
            return m_new, l, acc

        init = (jnp.full((TQ, 1), NEG, F32), jnp.zeros((TQ, 1), F32), jnp.zeros((TQ, HP), F32))
        carry = lax.fori_loop(0, last, lambda j, c: tile(j, c, False), init)
        m, l, acc = tile(last, carry, True)
        o_ref[...] = acc / l
        lse_ref[0] = m + jnp.log(l)
        if side:
            @pl.when((pl.program_id(0) == H - 1) & (i == S // TQ - 1))
            def _():
                side.finish(side_in, side_out, side_scratch)

    q0, k0, v0 = heads0
    in_specs = [pl.BlockSpec((TQ, HP), lambda h, i: (i, h + q0)), pl.BlockSpec((S, HP), lambda h, i: (0, h + k0)),
                pl.BlockSpec((S, HP), lambda h, i: (0, h + v0))]
    args = (q, k, v)
    if has_decay:
        in_specs += [pl.BlockSpec((1, S // TKF, 1, TKF), lambda h, i: (h, 0, 0, 0))]
        args += (frow,)
    out_shape = (jax.ShapeDtypeStruct((S, H * HP), F32), jax.ShapeDtypeStruct((H, S, 1), F32))
    out_specs = (pl.BlockSpec((TQ, HP), lambda h, i: (i, h)), pl.BlockSpec((1, TQ, 1), lambda h, i: (h, i, 0)))
    extra = {}
    if side:
        anywhere = pl.BlockSpec(memory_space=pl.ANY)
        in_specs += [anywhere] * n_side_in
        args += tuple(side.inputs)
        out_shape += tuple(side.out_shapes)
        out_specs += (anywhere,) * n_side_out
        extra = dict(scratch_shapes=list(side.scratch),
                     input_output_aliases={n_in + a: 2 + b for a, b in side.aliases.items()})
    return _pcall(
        body, name=name, out_shape=out_shape, grid=(H, S // TQ), in_specs=in_specs, out_specs=out_specs,
        compiler_params=pltpu.CompilerParams(dimension_semantics=("arbitrary", "arbitrary") if side else ("parallel", "parallel"),
                                             vmem_limit_bytes=_limit(8 * _nbytes((S, HP), F32))),
        **extra,
    )(*args)


def _attn_delta(do, o, *, name):
    def body(do_ref, o_ref, dl_ref, dob_ref):
        for h in range(H):
            sl = slice(h * HP, (h + 1) * HP)
            dl_ref[h] = jnp.sum(do_ref[:, sl] * o_ref[:, sl], axis=-1, keepdims=True)
        dob_ref[...] = do_ref[...].astype(dob_ref.dtype)

    return _pcall(
        body, name=name,
        out_shape=(jax.ShapeDtypeStruct((H, S, 1), F32), jax.ShapeDtypeStruct((S, H * HP), _MXU)), grid=(S // TM,),
        in_specs=[_rows(H * HP), _rows(H * HP)],
        out_specs=(pl.BlockSpec((H, TM, 1), lambda i: (0, i, 0)), _rows(H * HP)),
        compiler_params=_row_params(8 * _nbytes((TM, H * HP), F32)),
    )(do, o)


def _attn_bwd(q, k, v, dob, lse_row, delta_row, fcol, *, scale, name, side=None, heads0=(0, 0, 0)):
    has_decay = fcol is not None
    fold = _is_pow2(scale)
    n_in = 7 if has_decay else 6
    n_side_in = len(side.inputs) if side else 0
    n_side_out = len(side.out_shapes) if side else 0

    def body(*refs):
        q_ref, k_ref, v_ref, do_ref, lse_ref, dl_ref = refs[:6]
        fc_ref = refs[6] if has_decay else None
        side_in = refs[n_in:n_in + n_side_in]
        dq_ref, dk_ref, dv_ref = refs[n_in + n_side_in:n_in + n_side_in + 3]
        side_out = refs[n_in + n_side_in + 3:n_in + n_side_in + 3 + n_side_out]
        dq_acc = refs[n_in + n_side_in + 3 + n_side_out]
        side_scratch = refs[n_in + n_side_in + 4 + n_side_out:]
        j = pl.program_id(1)
        if side:
            @pl.when((pl.program_id(0) == 0) & (j == 0))
            def _():
                side.start(side_in, side_out, side_scratch)

        @pl.when(j == 0)
        def _():
            dq_acc[...] = jnp.zeros_like(dq_acc)

        kj = k_ref[...]
        vj = v_ref[...]
        kjs = (kj * scale).astype(kj.dtype) if fold else kj
        if has_decay:
            klane = lax.broadcasted_iota(jnp.int32, (TKB, HP), 1)
            kj = jnp.where(klane == ROW_SUM_LANE, 1.0, kj).astype(kj.dtype)
        first = (j * TKB) // TQB

        def tile(t, carry, masked):
            dk, dv = carry
            r0 = pl.multiple_of(t * TQB, TQB)
            qi = q_ref[pl.ds(r0, TQB), :]
            doi = do_ref[pl.ds(r0, TQB), :]
            st = lax.dot_general(kjs, qi, _NT, preferred_element_type=F32)
            if not fold:
                st = st * scale
            if has_decay:
                st = st - fc_ref[0]
            if masked:
                keys = j * TKB + lax.broadcasted_iota(jnp.int32, (TKB, TQB), 0)
                qpos = t * TQB + lax.broadcasted_iota(jnp.int32, (TKB, TQB), 1)
                st = jnp.where(keys <= qpos, st, NEG)
            pt = jnp.exp(st - lse_ref[0, t])
            dv = dv + jnp.dot(pt.astype(_MXU), doi, preferred_element_type=F32)
            dpt = lax.dot_general(vj, doi, _NT, preferred_element_type=F32)
            dst = (pt * (dpt - dl_ref[0, t])).astype(_MXU)
            if has_decay:
                lane = lax.broadcasted_iota(jnp.int32, (TQB, HP), 1)
                qi = jnp.where(lane == COL_SUM_LANE, 1.0, qi).astype(qi.dtype)
            dk = dk + jnp.dot(dst, qi, preferred_element_type=F32)
            dq_acc[pl.ds(r0, TQB), :] += lax.dot_general(dst, kj, _TN, preferred_element_type=F32)
            return dk, dv

        zero = jnp.zeros((TKB, HP), F32)
        carry = tile(first, (zero, zero), True)
        dk, dv = lax.fori_loop(first + 1, S // TQB, lambda t, c: tile(t, c, False), carry)
        dk_ref[...] = dk * scale
        dv_ref[...] = dv

        @pl.when(j == S // TKB - 1)
        def _():
            dq_ref[...] = dq_acc[...] * scale

        if side:
            @pl.when((pl.program_id(0) == H - 1) & (j == S // TKB - 1))
            def _():
                side.finish(side_in, side_out, side_scratch)

    q0, k0, v0 = heads0
    head = pl.BlockSpec((S, HP), lambda h, j: (0, h))
    kv = pl.BlockSpec((TKB, HP), lambda h, j: (j, h))
    stat = pl.BlockSpec((1, S // TQB, 1, TQB), lambda h, j: (h, 0, 0, 0))
    in_specs = [pl.BlockSpec((S, HP), lambda h, j: (0, h + q0)), pl.BlockSpec((TKB, HP), lambda h, j: (j, h + k0)),
                pl.BlockSpec((TKB, HP), lambda h, j: (j, h + v0)), head, stat, stat]
    args = (q, k, v, dob, lse_row, delta_row)
    if has_decay:
        in_specs += [pl.BlockSpec((1, TKB, 1), lambda h, j: (h, j, 0))]
        args += (fcol,)
    o = jax.ShapeDtypeStruct((S, H * HP), F32)
    out_shape, out_specs, scratch, aliases = (o, o, o), (head, kv, kv), [pltpu.VMEM((S, HP), F32)], {}
    if side:
        anywhere = pl.BlockSpec(memory_space=pl.ANY)
        in_specs += [anywhere] * n_side_in
        args += tuple(side.inputs)
        out_shape += tuple(side.out_shapes)
        out_specs += (anywhere,) * n_side_out
        scratch += list(side.scratch)
        aliases = {n_in + a: 3 + b for a, b in side.aliases.items()}
    return _pcall(
        body, name=name, out_shape=out_shape, grid=(H, S // TKB), in_specs=in_specs,
        out_specs=out_specs, scratch_shapes=scratch, input_output_aliases=aliases,
        compiler_params=pltpu.CompilerParams(dimension_semantics=("arbitrary" if side else "parallel", "arbitrary"),
                                             vmem_limit_bytes=_limit(12 * _nbytes((S, HP), F32))),
    )(*args)


def _ada_mod(c_all, w_shard, b_shard, *, name):
    R = c_all.shape[0]
    N = w_shard.shape[1]
    tn = 512

    def body(c_ref, w_ref, b_ref, o_ref, sc_ref):
        cv = c_ref[...]
        sc = (cv * jax.nn.sigmoid(cv)).astype(_MXU)
        sc_ref[...] = sc
        o_ref[...] = jnp.dot(sc, w_ref[...].astype(_MXU), preferred_element_type=F32) + b_ref[...]

    return _pcall(
        body, name=name,
        out_shape=(jax.ShapeDtypeStruct((R, N), F32), jax.ShapeDtypeStruct((R, D), _MXU)), grid=(N // tn,),
        in_specs=[pl.BlockSpec((R, D), lambda j: (0, 0)), pl.BlockSpec((D, tn), lambda j: (0, j)), pl.BlockSpec((1, tn), lambda j: (0, j))],
        out_specs=(pl.BlockSpec((R, tn), lambda j: (0, j)), pl.BlockSpec((R, D), lambda j: (0, 0))),
        compiler_params=pltpu.CompilerParams(dimension_semantics=("arbitrary",), vmem_limit_bytes=_limit(6 * _nbytes((D, tn), F32))),
    )(c_all, w_shard, b_shard)


def _rowsum(x, *, name):
    R, L = x.shape

    def body(x_ref, o_ref):
        acc = x_ref[0:1, :]
        for r in range(1, R):
            acc = acc + x_ref[r:r + 1, :]
        o_ref[...] = acc

    return pl.pallas_call(body, name=name, out_shape=jax.ShapeDtypeStruct((1, L), F32),
                          in_specs=[pl.BlockSpec(memory_space=pltpu.VMEM)], out_specs=pl.BlockSpec(memory_space=pltpu.VMEM))(x)


def _adamw(w, g, m, v, *, name):
    _, R, C = w.shape
    tr = R
    for t in range(8, R + 1, 8):
        if R % t == 0 and t * C * 4 <= (1 << 20):
            tr = t

    def body(w_ref, g_ref, m_ref, v_ref, d_ref, mo_ref, vo_ref):
        gv = g_ref[...]
        m2 = ADAM_B1 * m_ref[...] + (1.0 - ADAM_B1) * gv
        v2 = ADAM_B2 * v_ref[...] + (1.0 - ADAM_B2) * (gv * gv)
        m_hat = m2 / (1.0 - ADAM_B1 ** ADAM_STEP)
        v_hat = v2 / (1.0 - ADAM_B2 ** ADAM_STEP)
        d_ref[...] = -ADAM_LR * (m_hat / (jnp.sqrt(v_hat) + ADAM_EPS) + ADAM_WD * w_ref[...])
        mo_ref[...] = m2
        vo_ref[...] = v2

    blk = pl.BlockSpec((None, tr, C), lambda i: (0, i, 0))
    o = jax.ShapeDtypeStruct((1, R, C), F32)
    return _pcall(
        body, name=name, out_shape=(o, o, o), grid=(R // tr,),
        in_specs=[blk, pl.BlockSpec((tr, C), lambda i: (i, 0)), blk, blk], out_specs=(blk, blk, blk),
        compiler_params=pltpu.CompilerParams(dimension_semantics=("parallel",), vmem_limit_bytes=_limit(20 * _nbytes((tr, C), F32))),
    )(w, g, m, v)


def _place():
    x, y, c = lax.axis_index("x"), lax.axis_index("y"), lax.axis_index("c")
    return x, y, c, [(1 - x, y), (x, 1 - y), (1 - x, 1 - y)]


def _two_level_gather(x_ref, out_ref, send_sems, recv_sems, local_sem):
    x, y, c, chips = _place()
    me, sibling = (x, y, c), (x, y, 1 - c)

    def blk(px, py, pc):
        return out_ref.at[4 * px + 2 * py + pc]

    def copy(k, block, to, src=None):
        return pltpu.make_async_remote_copy(
            src_ref=blk(*block) if src is None else src, dst_ref=blk(*block),
            send_sem=send_sems.at[k], recv_sem=recv_sems.at[k], device_id=to, device_id_type=MESH)

    mine = pltpu.make_async_copy(x_ref, blk(*me), local_sem)
    mine.start()
    first = [copy(0, me, sibling, src=x_ref)]
    first += [copy(1 + j, me, (*chip, c), src=x_ref) for j, chip in enumerate(chips)]
    for cp in first:
        cp.start()
    passed = [copy(4 + j, (*chip, c), sibling) for j, chip in enumerate(chips)]
    for j, chip in enumerate(chips):
        copy(1 + j, (*chip, c), me).wait_recv()
        passed[j].start()
    copy(0, sibling, me).wait_recv()
    for j, chip in enumerate(chips):
        copy(4 + j, (*chip, 1 - c), me).wait_recv()
    for cp in first + passed:
        cp.wait_send()
    mine.wait()


_GATHER_SEMS = [pltpu.SemaphoreType.DMA((7,)), pltpu.SemaphoreType.DMA((7,)), pltpu.SemaphoreType.DMA]


class _SideJob(NamedTuple):
    inputs: tuple
    out_shapes: tuple
    aliases: dict
    scratch: tuple
    start: Callable
    finish: Callable


def _block_index(px, py, pc):
    return 4 * px + 2 * py + pc


def _gather_level1_job(halves):
    nw = len(halves)

    def copies(ins, outs, scratch, n):
        sends, recvs, _ = scratch
        x, y, c, chips = _place()
        mine = outs[n].at[_block_index(x, y, c)]
        peers = [(x, y, 1 - c)] + [(*chip, c) for chip in chips]
        out = []
        for t, peer in enumerate(peers):
            sem = dict(send_sem=sends.at[4 * n + t], recv_sem=recvs.at[4 * n + t], device_id_type=MESH)
            landing = outs[n].at[_block_index(*peer)]
            out.append((pltpu.make_async_remote_copy(src_ref=ins[n], dst_ref=mine, device_id=peer, **sem),
                        pltpu.make_async_remote_copy(src_ref=landing, dst_ref=landing, device_id=peer, **sem)))
        local = pltpu.make_async_copy(ins[n], mine, scratch[2].at[n])
        return out, local

    def start(ins, outs, scratch):
        for n in range(nw):
            pairs, local = copies(ins, outs, scratch, n)
            local.start()
            for to, _ in pairs:
                to.start()

    def finish(ins, outs, scratch):
        for n in range(nw):
            pairs, local = copies(ins, outs, scratch, n)
            for to, frm in pairs:
                frm.wait_recv()
                to.wait_send()
            local.wait()

    return _SideJob(
        inputs=tuple(halves), out_shapes=tuple(jax.ShapeDtypeStruct((8,) + h.shape, h.dtype) for h in halves), aliases={},
        scratch=(pltpu.SemaphoreType.DMA((4 * nw,)), pltpu.SemaphoreType.DMA((4 * nw,)), pltpu.SemaphoreType.DMA((nw,))),
        start=start, finish=finish)


def _gather_level2_job(gathered):
    nw = len(gathered)

    def copies(outs, scratch, n):
        sends, recvs = scratch
        x, y, c, chips = _place()
        out = []
        for j, chip in enumerate(chips):
            sem = dict(send_sem=sends.at[3 * n + j], recv_sem=recvs.at[3 * n + j], device_id=(x, y, 1 - c), device_id_type=MESH)
            going = outs[n].at[_block_index(*chip, c)]
            landing = outs[n].at[_block_index(*chip, 1 - c)]
            out.append((pltpu.make_async_remote_copy(src_ref=going, dst_ref=going, **sem),
                        pltpu.make_async_remote_copy(src_ref=landing, dst_ref=landing, **sem)))
        return out

    def start(ins, outs, scratch):
        for n in range(nw):
            for to, _ in copies(outs, scratch, n):
                to.start()

    def finish(ins, outs, scratch):
        for n in range(nw):
            for to, frm in copies(outs, scratch, n):
                frm.wait_recv()
                to.wait_send()

    return _SideJob(
        inputs=tuple(gathered), out_shapes=tuple(jax.ShapeDtypeStruct(g.shape, g.dtype) for g in gathered),
        aliases={n: n for n in range(nw)},
        scratch=(pltpu.SemaphoreType.DMA((3 * nw,)), pltpu.SemaphoreType.DMA((3 * nw,))),
        start=start, finish=finish)


def _all_gather_rows(x, *, name):
    R, C = x.shape

    def body(x_ref, out_ref, send_sems, recv_sems, local_sem):
        _two_level_gather(x_ref, out_ref, send_sems, recv_sems, local_sem)

    return pl.pallas_call(
        body, name=name, out_shape=jax.ShapeDtypeStruct((8, R, C), x.dtype),
        in_specs=[pl.BlockSpec(memory_space=pltpu.VMEM)], out_specs=pl.BlockSpec(memory_space=pltpu.VMEM),
        scratch_shapes=list(_GATHER_SEMS),
        compiler_params=pltpu.CompilerParams(vmem_limit_bytes=_limit(10 * _nbytes((R, C), x.dtype))),
    )(x)


CAST_ROWS = 16
_FLIPS = [(fx, fy, fc) for fx in (0, 1) for fy in (0, 1) for fc in (0, 1)][1:]


def _flipped(v, bit):
    return 1 - v if bit else v


def _scatter_direct_job(pieces):
    nw = len(pieces)

    def copies(ins, outs, scratch, n):
        sends, recvs = scratch
        x, y, c, _ = _place()
        out = []
        for f, (fx, fy, fc) in enumerate(_FLIPS):
            peer = (_flipped(x, fx), _flipped(y, fy), _flipped(c, fc))
            sem = dict(send_sem=sends.at[7 * n + f], recv_sem=recvs.at[7 * n + f], device_id=peer, device_id_type=MESH)
            landing = outs[n].at[f]
            out.append((pltpu.make_async_remote_copy(src_ref=ins[n].at[_block_index(*peer)], dst_ref=landing, **sem),
                        pltpu.make_async_remote_copy(src_ref=landing, dst_ref=landing, **sem)))
        return out

    def start(ins, outs, scratch):
        for n in range(nw):
            for to, _ in copies(ins, outs, scratch, n):
                to.start()

    def finish(ins, outs, scratch):
        for n in range(nw):
            for to, frm in copies(ins, outs, scratch, n):
                frm.wait_recv()
                to.wait_send()

    return _SideJob(
        inputs=tuple(pieces), out_shapes=tuple(jax.ShapeDtypeStruct((7,) + p.shape[1:], p.dtype) for p in pieces), aliases={},
        scratch=(pltpu.SemaphoreType.DMA((7 * nw,)), pltpu.SemaphoreType.DMA((7 * nw,))),
        start=start, finish=finish)


def _scatter_finish(g4s, landed, *, name):
    nw = len(g4s)
    dims = [g.shape[1:] for g in g4s]

    def body(*refs):
        g_refs, l_refs, out_refs, own = refs[:nw], refs[nw:2 * nw], refs[2 * nw:3 * nw], refs[3 * nw:4 * nw]
        load_sems, send_sems, recv_sems = refs[4 * nw:]
        x, y, core, _ = _place()
        k = 2 * x + y
        loads = []
        for n, (r, c) in enumerate(dims):
            my0 = pl.multiple_of(core * (r // 2), CAST_ROWS)
            ld = pltpu.make_async_copy(g_refs[n].at[k, pl.ds(my0, r // 2), :], own[n], load_sems.at[n])
            ld.start()
            loads.append(ld)
        swaps = []
        for n, (r, c) in enumerate(dims):
            rh = r // 2
            my0 = pl.multiple_of(core * rh, CAST_ROWS)
            loads[n].wait()

            def fin(i, carry, n=n, my0=my0):
                r0 = pl.multiple_of(i * CAST_ROWS, CAST_ROWS)
                s = own[n][pl.ds(r0, CAST_ROWS), :]
                for f in range(7):
                    s = s + l_refs[n][f, pl.ds(r0, CAST_ROWS), :].astype(F32)
                out_refs[n][pl.ds(my0 + r0, CAST_ROWS), :] = s
                return carry

            lax.fori_loop(0, rh // CAST_ROWS, fin, 0)
            half = out_refs[n].at[pl.ds(my0, rh), :]
            sw = pltpu.make_async_remote_copy(src_ref=half, dst_ref=half, send_sem=send_sems.at[n], recv_sem=recv_sems.at[n],
                                              device_id=(x, y, 1 - core), device_id_type=MESH)
            sw.start()
            swaps.append(sw)
        for sw in swaps:
            sw.wait()

    need = sum(_nbytes((7, r // 2, c), BF16) + _nbytes((r // 2, c), F32) + _nbytes((r, c), F32) for r, c in dims)
    vmem = pl.BlockSpec(memory_space=pltpu.VMEM)
    return pl.pallas_call(
        body, name=name, out_shape=tuple(jax.ShapeDtypeStruct((r, c), F32) for r, c in dims),
        in_specs=[pl.BlockSpec(memory_space=pl.ANY)] * nw + [vmem] * nw, out_specs=(vmem,) * nw,
        scratch_shapes=[pltpu.VMEM((r // 2, c), F32) for r, c in dims]
        + [pltpu.SemaphoreType.DMA((nw,)), pltpu.SemaphoreType.DMA((nw,)), pltpu.SemaphoreType.DMA((nw,))],
        compiler_params=pltpu.CompilerParams(vmem_limit_bytes=_limit(need * 1.1)),
    )(*g4s, *landed)


def _gather_weight(w, *, name):
    r, c = w.shape
    rh = r // 2
    assert rh % CAST_ROWS == 0

    def body(w_hbm, out_ref, tmp, xb, send_sems, recv_sems, local_sem):
        core = lax.axis_index("c")
        ld = pltpu.make_async_copy(w_hbm.at[pl.ds(pl.multiple_of(core * rh, CAST_ROWS), rh), :], tmp, local_sem)
        ld.start()
        ld.wait()

        def cast(i, carry):
            r0 = pl.multiple_of(i * CAST_ROWS, CAST_ROWS)
            xb[pl.ds(r0, CAST_ROWS), :] = tmp[pl.ds(r0, CAST_ROWS), :].astype(BF16)
            return carry

        lax.fori_loop(0, rh // CAST_ROWS, cast, 0)
        _two_level_gather(xb, out_ref, send_sems, recv_sems, local_sem)

    need = _nbytes((8, rh, c), BF16) + _nbytes((rh, c), F32) + _nbytes((rh, c), BF16)
    out = pl.pallas_call(
        body, name=name, out_shape=jax.ShapeDtypeStruct((8, rh, c), BF16),
        in_specs=[pl.BlockSpec(memory_space=pl.ANY)], out_specs=pl.BlockSpec(memory_space=pltpu.VMEM),
        scratch_shapes=[pltpu.VMEM((rh, c), F32), pltpu.VMEM((rh, c), BF16)] + list(_GATHER_SEMS),
        compiler_params=pltpu.CompilerParams(vmem_limit_bytes=_limit(need * 1.3)),
    )(w)
    return out.reshape(4, r, c)


def _reduce_scatter_weight(g4, *, name):
    _, r, c = g4.shape
    rh = r // 2
    assert rh % CAST_ROWS == 0
    nsteps = rh // CAST_ROWS

    def body(g_hbm, out_ref, mine, tmp, sbuf, rbuf_a, rbuf_b, a_send, a_recv, b_send, b_recv, c_send, c_recv, lsem):
        x, y, core, chips = _place()
        sibling = (x, y, 1 - core)
        k = 2 * x + y
        my0 = pl.multiple_of(core * rh, CAST_ROWS)
        ot0 = pl.multiple_of((1 - core) * rh, CAST_ROWS)

        ld = pltpu.make_async_copy(g_hbm.at[:, pl.ds(my0, rh), :], mine, lsem)
        ld.start()
        ld.wait()
        for j in range(4):
            ldj = pltpu.make_async_copy(g_hbm.at[j, pl.ds(ot0, rh), :], tmp, lsem)
            ldj.start()
            ldj.wait()

            def cast(i, carry, j=j):
                r0 = pl.multiple_of(i * CAST_ROWS, CAST_ROWS)
                sbuf[j, pl.ds(r0, CAST_ROWS), :] = tmp[pl.ds(r0, CAST_ROWS), :].astype(BF16)
                return carry

            lax.fori_loop(0, nsteps, cast, 0)

        to_sib = pltpu.make_async_remote_copy(src_ref=sbuf, dst_ref=rbuf_a, send_sem=a_send, recv_sem=a_recv,
                                              device_id=sibling, device_id_type=MESH)
        to_sib.start()
        to_sib.wait()

        for j in range(4):
            def add(i, carry, j=j):
                r0 = pl.multiple_of(i * CAST_ROWS, CAST_ROWS)
                s = mine[j, pl.ds(r0, CAST_ROWS), :] + rbuf_a[j, pl.ds(r0, CAST_ROWS), :].astype(F32)
                mine[j, pl.ds(r0, CAST_ROWS), :] = s
                sbuf[j, pl.ds(r0, CAST_ROWS), :] = s.astype(BF16)
                return carry

            lax.fori_loop(0, nsteps, add, 0)

        sends = []
        for d, (px, py) in enumerate(chips):
            cp = pltpu.make_async_remote_copy(src_ref=sbuf.at[2 * px + py], dst_ref=rbuf_b.at[d], send_sem=b_send.at[d],
                                              recv_sem=b_recv.at[d], device_id=(px, py, core), device_id_type=MESH)
            cp.start()
            sends.append(cp)
        for cp in sends:
            cp.wait()

        def fin(i, carry):
            r0 = pl.multiple_of(i * CAST_ROWS, CAST_ROWS)
            s = mine[k, pl.ds(r0, CAST_ROWS), :]
            for d in range(3):
                s = s + rbuf_b[d, pl.ds(r0, CAST_ROWS), :].astype(F32)
            out_ref[pl.ds(my0 + r0, CAST_ROWS), :] = s
            return carry

        lax.fori_loop(0, nsteps, fin, 0)
        half = out_ref.at[pl.ds(my0, rh), :]
        swap = pltpu.make_async_remote_copy(src_ref=half, dst_ref=half, send_sem=c_send, recv_sem=c_recv,
                                            device_id=sibling, device_id_type=MESH)
        swap.start()
        swap.wait()

    need = (_nbytes((4, rh, c), F32) + _nbytes((rh, c), F32) + 2 * _nbytes((4, rh, c), BF16) + _nbytes((3, rh, c), BF16)
            + _nbytes((r, c), F32))
    return pl.pallas_call(
        body, name=name, out_shape=jax.ShapeDtypeStruct((r, c), F32),
        in_specs=[pl.BlockSpec(memory_space=pl.ANY)], out_specs=pl.BlockSpec(memory_space=pltpu.VMEM),
        scratch_shapes=[pltpu.VMEM((4, rh, c), F32), pltpu.VMEM((rh, c), F32), pltpu.VMEM((4, rh, c), BF16),
                        pltpu.VMEM((4, rh, c), BF16), pltpu.VMEM((3, rh, c), BF16),
                        pltpu.SemaphoreType.DMA, pltpu.SemaphoreType.DMA, pltpu.SemaphoreType.DMA((3,)),
                        pltpu.SemaphoreType.DMA((3,)), pltpu.SemaphoreType.DMA, pltpu.SemaphoreType.DMA,
                        pltpu.SemaphoreType.DMA],
        compiler_params=pltpu.CompilerParams(vmem_limit_bytes=_limit(need * 1.2)),
    )(g4)


def _cols_from_shards(g):
    n, K, c = g.shape
    return g.transpose(1, 0, 2).reshape(K, n * c)


def _cols_to_shards(w):
    K, N = w.shape
    return w.reshape(K, 4, N // 4).transpose(1, 0, 2)


def _pad_heads_cols(w, width, lane0=0):
    K = w.shape[0]
    w3 = w.reshape(K, H, width)
    return jnp.pad(w3, ((0, 0), (0, 0), (lane0, HP - lane0 - width))).reshape(K, H * HP)


def _unpad_heads_cols(w, width, lane0=0):
    K = w.shape[0]
    return w.reshape(K, H, HP)[:, :, lane0:lane0 + width].reshape(K, H * width)


def _pad_block(w, lane0=0):
    return jnp.pad(w, ((0, 0), (lane0, LANES - lane0 - w.shape[1])))


_IN_SPLITS = [512, 1024, 1536, 1544, 2312, 2568, 2600, 3624]


def _pad_w_in(w):
    fq, fk, fv, flog, cq, ckv, krin, gfox, gmla = jnp.split(w, _IN_SPLITS, axis=1)
    return jnp.concatenate([_pad_heads_cols(fq, FOX_HD), _pad_heads_cols(fk, FOX_HD), _pad_heads_cols(fv, FOX_HD),
                            cq, ckv, gfox, gmla, _pad_block(flog), _pad_block(krin, KRIN_LANE)], axis=1)


def _unpad_w_in(wp):
    qkv, rest = wp[:, :NQKV], wp[:, NQKV:]
    fq, fk, fv = (_unpad_heads_cols(qkv[:, i * H * HP:(i + 1) * H * HP], FOX_HD) for i in range(3))
    return jnp.concatenate([fq, fk, fv, rest[:, OFF_FLOG:OFF_FLOG + H], rest[:, OFF_CQ:OFF_CQ + Q_LORA],
                            rest[:, OFF_CKV:OFF_CKV + KV_LORA], rest[:, OFF_KRIN + KRIN_LANE:OFF_KRIN + KRIN_LANE + MLA_ROPE],
                            rest[:, OFF_GFOX:OFF_GFOX + D], rest[:, OFF_GMLA:OFF_GMLA + D]], axis=1)


def _pad_w_ukv(w):
    w3 = w.reshape(KV_LORA, H, MLA_NOPE + MLA_V)
    kp = jnp.pad(w3[:, :, :MLA_NOPE], ((0, 0), (0, 0), (0, HP - MLA_NOPE))).reshape(KV_LORA, H * HP)
    vp = jnp.pad(w3[:, :, MLA_NOPE:], ((0, 0), (0, 0), (0, HP - MLA_V))).reshape(KV_LORA, H * HP)
    return jnp.concatenate([kp, vp], axis=1)


def _unpad_w_ukv(wp):
    kp = wp[:, :H * HP].reshape(KV_LORA, H, HP)[:, :, :MLA_NOPE]
    vp = wp[:, H * HP:].reshape(KV_LORA, H, HP)[:, :, :MLA_V]
    return jnp.concatenate([kp, vp], axis=2).reshape(KV_LORA, H * (MLA_NOPE + MLA_V))


def _pad_heads_rows(w, width):
    N = w.shape[1]
    return jnp.pad(w.reshape(H, width, N), ((0, 0), (0, HP - width), (0, 0))).reshape(H * HP, N)


def _unpad_heads_rows(w, width):
    N = w.shape[1]
    return w.reshape(H, HP, N)[:, :width, :].reshape(H * width, N)


def _rope_tables(positions):
    inv_freq = 1.0 / (ROPE_THETA ** (jnp.arange(0, MLA_ROPE, 2, dtype=F32) / MLA_ROPE))
    ang = positions.reshape(S, 1).astype(F32) * inv_freq
    cos, sin = jnp.cos(ang), jnp.sin(ang)
    ones = jnp.ones((S, KRIN_LANE), F32)
    tail = jnp.zeros((S, LANES - KRIN_LANE - MLA_ROPE), F32)
    ctab = jnp.concatenate([ones, cos, cos, tail], axis=1)
    stab = jnp.concatenate([0.0 * ones, -sin, sin, tail], axis=1)
    return ctab, stab


def _local_step(x, target, mod, positions, gains, bf, W, late=None):
    W = dict(W)
    sh1, sc1, gt1, sh2, sc2, gt2 = (mod[:, i * D:(i + 1) * D] for i in range(6))
    ops1, ops2 = 1.0 + sc1, 1.0 + sc2
    ones = lambda w: jnp.ones((1, w), F32)
    zeros = lambda w: jnp.zeros((1, w), F32)
    bf_blk = _pad_block(bf)
    ctab, stab = _rope_tables(positions)
    fox_scale = 1.0 / math.sqrt(FOX_HD)
    mla_scale = 1.0 / math.sqrt(MLA_NOPE + MLA_ROPE)

    h1 = _norm_mod(x, 0, D, gains["g_pre_mix"], ops1, sh1, name="f_pre_mix")
    qkv = _matmul(h1, W["w_in_qkv"], out_dtype=_MXU, name="f_proj_qkv", tm_cap=2048)
    rest = _matmul(h1, W["w_in_rest"], name="f_proj_rest", tm_cap=2048, tn_cap=256)
    F = _fox_prep(rest, bf_blk, name="f_fox_prep")
    Ft = F[:, :H].T
    fcol, frow = Ft.reshape(H, S, 1), Ft.reshape(H, S // TKF, 1, TKF)
    fox_heads = (0, H, 2 * H)
    job = _gather_level1_job([late[n] for n in _LATE]) if late else None
    oa, lse_a, *landed = _attn_fwd(qkv, qkv, qkv, frow, scale=fox_scale, name="f_attn_fox", side=job, heads0=fox_heads)

    cqn = _norm_mod(rest, OFF_CQ // Q_LORA, Q_LORA, gains["g_q_lora"], ones(Q_LORA), zeros(Q_LORA), name="f_norm_cq")
    ckvn = _norm_mod(rest, OFF_CKV // KV_LORA, KV_LORA, gains["g_kv_lora"], ones(KV_LORA), zeros(KV_LORA), name="f_norm_ckv")
    qb = _matmul(cqn, W["w_uq"], name="f_uq")
    kvb = _matmul(ckvn, W["w_ukv"], name="f_ukv")
    qm, km, vm = _mla_assemble(qb, kvb, rest, ctab, stab, name="f_mla_assemble")
    job = _gather_level2_job(landed) if late else None
    ob, lse_b, *landed = _attn_fwd(qm, km, vm, None, scale=mla_scale, name="f_attn_mla", side=job)
    if late:
        W.update(_late_weights({n: g.reshape(4, 2 * g.shape[1], g.shape[2]) for n, g in zip(_LATE, landed)}))

    pa = _matmul(oa, W["w_proj_fox"], name="f_proj_fox")
    pb = _matmul(ob, W["w_proj_mla"], name="f_proj_mla")
    merged = _merge(rest, pa, pb, name="f_merge")
    y1 = _matmul(merged, W["w_out"], name="f_out")
    x2 = _post_res(x, y1, gains["g_post_mix"], gt1, name="f_post_mix")
    h2 = _norm_mod(x2, 0, D, gains["g_pre_ffn"], ops2, sh2, name="f_pre_ffn")
    gu = _matmul(h2, W["w_ffn_in"], name="f_ffn_in", b_shards=True, tn_cap=1408)
    act = _swiglu(gu, name="f_swiglu")
    y2 = _matmul(act, W["w_ffn_out"], name="f_ffn_out", tk_cap=1408, tn_cap=1024)
    dout, loss = _post_res_loss(x2, y2, gains["g_post_ffn"], gt2, target, name="f_post_ffn_loss")

    dy2, s_gt2, s_gpost2 = _post_res_bwd(dout, y2, gains["g_post_ffn"], gt2, name="b_post_ffn")
    dact = _matmul(dy2, W["w_ffn_out"], tb=True, name="b_ffn_out_dx", tn_cap=1408)
    dW_ffn_out = _matmul(act, dy2, ta=True, name="b_ffn_out_dw", tm_cap=1408, tk_cap=2048)
    dgu = _swiglu_bwd(gu, dact, name="b_swiglu")
    dh2 = _matmul(dgu, W["w_ffn_in"], tb=True, b_shards=True, name="b_ffn_in_dx", tk_cap=1408, tn_cap=1024)
    dW_ffn_in = _matmul(h2, dgu, ta=True, name="b_ffn_in_dw", out_shards=True, tn_cap=1408, tk_cap=2048)
    dx2, s_sh2, s_a2 = _norm_mod_bwd(x2, 0, D, dh2, gains["g_pre_ffn"], ops2, dout, name="b_pre_ffn")
    dy1, s_gt1, s_gpost1 = _post_res_bwd(dx2, y1, gains["g_post_mix"], gt1, name="b_post_mix")
    dmerged = _matmul(dy1, W["w_out"], tb=True, name="b_out_dx")
    dW_out = _matmul(merged, dy1, ta=True, name="b_out_dw")
    dpa, dpb, dgfox, dgmla = _merge_bwd(rest, pa, pb, dmerged, name="b_merge")
    doa = _matmul(dpa, W["w_proj_fox"], tb=True, name="b_proj_fox_dx")
    dW_proj_fox = _matmul(oa, dpa, ta=True, name="b_proj_fox_dw")
    dob = _matmul(dpb, W["w_proj_mla"], tb=True, name="b_proj_mla_dx")
    dW_proj_mla = _matmul(ob, dpb, ta=True, name="b_proj_mla_dw")

    delta_a, doa16 = _attn_delta(doa, oa, name="b_delta_fox")
    as_rows = lambda a: a.reshape(H, S // TQB, 1, TQB)
    dW = dict(w_proj_fox=dW_proj_fox, w_proj_mla=dW_proj_mla, w_out=dW_out, w_ffn_in=dW_ffn_in, w_ffn_out=dW_ffn_out)
    job_a = job_b = None
    if late:
        late_shards = _grad_shards(dW)
        pieces = {n: s.reshape(8, s.shape[1] // 2, s.shape[2]).astype(BF16) for n, s in late_shards.items()}
        job_a = _scatter_direct_job([pieces[n] for n in _SCATTER_A])
        job_b = _scatter_direct_job([pieces[n] for n in _SCATTER_B])
    dqa, dka, dva, *landed_a = _attn_bwd(qkv, qkv, qkv, doa16, as_rows(lse_a), as_rows(delta_a), fcol, scale=fox_scale,
                                         name="b_attn_fox", side=job_a, heads0=fox_heads)
    delta_b, dob16 = _attn_delta(dob, ob, name="b_delta_mla")
    dqm, dkm, dvm, *landed_b = _attn_bwd(qm, km, vm, dob16, as_rows(lse_b), as_rows(delta_b), None, scale=mla_scale,
                                         name="b_attn_mla", side=job_b)
    reduced = {}
    if late:
        order = _SCATTER_A + _SCATTER_B
        done = _scatter_finish([late_shards[n] for n in order], landed_a + landed_b, name="scatter_late")
        reduced = dict(zip(order, done))
        dW = {}

    dF = (dqa[:, ROW_SUM_LANE::HP] - dka[:, COL_SUM_LANE::HP]) * (1.0 / fox_scale)
    dflog, s_bf = _fox_bwd_prep(rest, bf_blk, _pad_block(dF), name="b_fox_prep")

    dqb, dkvb, dkrin = _mla_assemble_bwd(dqm, dkm, dvm, ctab, stab, name="b_mla_assemble")
    dcqn = _matmul(dqb, W["w_uq"], tb=True, name="b_uq_dx")
    dW_uq = _matmul(cqn, dqb, ta=True, name="b_uq_dw")
    dckvn = _matmul(dkvb, W["w_ukv"], tb=True, name="b_ukv_dx")
    dW_ukv = _matmul(ckvn, dkvb, ta=True, name="b_ukv_dw")
    dcq, _, s_gq = _norm_mod_bwd(rest, OFF_CQ // Q_LORA, Q_LORA, dcqn, gains["g_q_lora"], ones(Q_LORA), None, name="b_norm_cq")
    dckv, _, s_gkv = _norm_mod_bwd(rest, OFF_CKV // KV_LORA, KV_LORA, dckvn, gains["g_kv_lora"], ones(KV_LORA), None, name="b_norm_ckv")

    c16 = lambda a: a.astype(_MXU)
    dproj = jnp.concatenate([c16(dqa), c16(dka), c16(dva), c16(dcq), c16(dckv), dgfox, dgmla, c16(dflog), c16(dkrin)], axis=1)
    w_in_full = jnp.concatenate([W["w_in_qkv"], W["w_in_rest"]], axis=1)
    dh1 = _matmul(dproj, w_in_full, tb=True, name="b_in_dx", tk_cap=1280, tn_cap=1024)
    dW_in = _matmul(h1, dproj, ta=True, name="b_in_dw", tn_cap=640, tk_cap=2048)
    grad_x, s_sh1, s_a1 = _norm_mod_bwd(x, 0, D, dh1, gains["g_pre_mix"], ops1, dx2, name="b_pre_mix")

    dmod = jnp.concatenate([s_sh1, s_a1 * gains["g_pre_mix"], s_gt1, s_sh2, s_a2 * gains["g_pre_ffn"], s_gt2], axis=1)
    small = dict(dmod=dmod, g_pre_mix=s_a1 * ops1, g_post_mix=s_gpost1, g_pre_ffn=s_a2 * ops2, g_post_ffn=s_gpost2,
                 g_q_lora=s_gq, g_kv_lora=s_gkv, b_forget=s_bf)
    dW = dict(dW, w_in=dW_in, w_uq=dW_uq, w_ukv=dW_ukv)
    return loss, grad_x, dW, reduced, small


_BIG = ["w_in", "w_uq", "w_ukv", "w_proj_fox", "w_proj_mla", "w_out", "w_ffn_in", "w_ffn_out"]
_COL_SHARDED = {"w_in", "w_ukv", "w_proj_fox", "w_proj_mla", "w_ffn_in"}
_SMALL = ["b_ada", "g_pre_mix", "g_post_mix", "g_pre_ffn", "g_post_ffn", "b_forget", "g_q_lora", "g_kv_lora"]
_ORDER = ["w_ada", "b_ada", "g_pre_mix", "g_post_mix", "g_pre_ffn", "g_post_ffn", "w_in", "b_forget", "g_q_lora", "w_uq",
          "g_kv_lora", "w_ukv", "w_proj_fox", "w_proj_mla", "w_out", "w_ffn_in", "w_ffn_out"]
_ROW = {}
_off = 0
for _n, _w in [("dmod", 6 * D), ("g_pre_mix", D), ("g_post_mix", D), ("g_pre_ffn", D), ("g_post_ffn", D), ("g_q_lora", Q_LORA),
               ("g_kv_lora", KV_LORA), ("b_forget", LANES), ("loss", LANES)]:
    _ROW[_n] = (_off, _w)
    _off += _w
_ROW_LEN = _off


_EARLY = ["w_in", "w_uq", "w_ukv"]
_LATE = ["w_proj_fox", "w_proj_mla", "w_out", "w_ffn_in", "w_ffn_out"]
_SCATTER_A = ["w_ffn_in"]
_SCATTER_B = ["w_ffn_out", "w_out", "w_proj_fox", "w_proj_mla"]


def _rows_from_shards(g):
    return g.reshape(-1, g.shape[2])


def _early_weights(G):
    w_in = _pad_w_in(_cols_from_shards(G["w_in"]))
    return dict(
        w_in_qkv=w_in[:, :NQKV], w_in_rest=w_in[:, NQKV:],
        w_uq=_pad_heads_cols(_rows_from_shards(G["w_uq"]), MLA_NOPE + MLA_ROPE),
        w_ukv=_pad_w_ukv(_cols_from_shards(G["w_ukv"])))


def _late_weights(G):
    return dict(
        w_proj_fox=_pad_heads_rows(_cols_from_shards(G["w_proj_fox"]), FOX_HD),
        w_proj_mla=_pad_heads_rows(_cols_from_shards(G["w_proj_mla"]), MLA_V),
        w_out=_rows_from_shards(G["w_out"]), w_ffn_in=G["w_ffn_in"], w_ffn_out=_rows_from_shards(G["w_ffn_out"]))


def _full_weights(G):
    return {**_early_weights(G), **_late_weights(G)}


_UNPAD = dict(
    w_in=_unpad_w_in, w_uq=lambda g: _unpad_heads_cols(g, MLA_NOPE + MLA_ROPE), w_ukv=_unpad_w_ukv,
    w_proj_fox=lambda g: _unpad_heads_rows(g, FOX_HD), w_proj_mla=lambda g: _unpad_heads_rows(g, MLA_V),
    w_out=lambda g: g, w_ffn_out=lambda g: g)


def _grad_shards(dW):
    out = {}
    for n, g in dW.items():
        if n == "w_ffn_in":
            out[n] = g
        else:
            nat = _UNPAD[n](g)
            out[n] = _cols_to_shards(nat) if n in _COL_SHARDED else nat.reshape(4, nat.shape[0] // 4, nat.shape[1])
    return out


def kernel(x, c, positions, w_ada, b_ada, g_pre_mix, g_post_mix, g_pre_ffn, g_post_ffn, w_in, b_forget, g_q_lora, w_uq, g_kv_lora, w_ukv, w_proj_fox, w_proj_mla, w_out, w_ffn_in, w_ffn_out, loss_target, m_w_ada, m_b_ada, m_g_pre_mix, m_g_post_mix, m_g_pre_ffn, m_g_post_ffn, m_w_in, m_b_forget, m_g_q_lora, m_w_uq, m_g_kv_lora, m_w_ukv, m_w_proj_fox, m_w_proj_mla, m_w_out, m_w_ffn_in, m_w_ffn_out, v_w_ada, v_b_ada, v_g_pre_mix, v_g_post_mix, v_g_pre_ffn, v_g_post_ffn, v_w_in, v_b_forget, v_g_q_lora, v_w_uq, v_g_kv_lora, v_w_ukv, v_w_proj_fox, v_w_proj_mla, v_w_out, v_w_ffn_in, v_w_ffn_out):
    P = dict(w_ada=w_ada, b_ada=b_ada, g_pre_mix=g_pre_mix, g_post_mix=g_post_mix, g_pre_ffn=g_pre_ffn, g_post_ffn=g_post_ffn,
             w_in=w_in, b_forget=b_forget, g_q_lora=g_q_lora, w_uq=w_uq, g_kv_lora=g_kv_lora, w_ukv=w_ukv,
             w_proj_fox=w_proj_fox, w_proj_mla=w_proj_mla, w_out=w_out, w_ffn_in=w_ffn_in, w_ffn_out=w_ffn_out)
    M = dict(w_ada=m_w_ada, b_ada=m_b_ada, g_pre_mix=m_g_pre_mix, g_post_mix=m_g_post_mix, g_pre_ffn=m_g_pre_ffn,
             g_post_ffn=m_g_post_ffn, w_in=m_w_in, b_forget=m_b_forget, g_q_lora=m_g_q_lora, w_uq=m_w_uq, g_kv_lora=m_g_kv_lora,
             w_ukv=m_w_ukv, w_proj_fox=m_w_proj_fox, w_proj_mla=m_w_proj_mla, w_out=m_w_out, w_ffn_in=m_w_ffn_in,
             w_ffn_out=m_w_ffn_out)
    V = dict(w_ada=v_w_ada, b_ada=v_b_ada, g_pre_mix=v_g_pre_mix, g_post_mix=v_g_post_mix, g_pre_ffn=v_g_pre_ffn,
             g_post_ffn=v_g_post_ffn, w_in=v_w_in, b_forget=v_b_forget, g_q_lora=v_g_q_lora, w_uq=v_w_uq, g_kv_lora=v_g_kv_lora,
             w_ukv=v_w_ukv, w_proj_fox=v_w_proj_fox, w_proj_mla=v_w_proj_mla, w_out=v_w_out, w_ffn_in=v_w_ffn_in,
             w_ffn_out=v_w_ffn_out)
    ax, ay, ac = lax.axis_index("x"), lax.axis_index("y"), lax.axis_index("c")
    chip = 2 * ax + ay
    me = 4 * ax + 2 * ay + ac
    n_ada = w_ada.shape[2]

    c_all = _all_gather_rows(jnp.pad(c, ((0, 7), (0, 0))), name="gather_c")[:, 0, :]
    c_all = jnp.pad(c_all, ((0, 8), (0, 0)))
    b_shard = lax.dynamic_slice(b_ada, (0, chip * n_ada), (1, n_ada))
    mod_blk, silu_c = _ada_mod(c_all, w_ada[0], b_shard, name="ada_mod")
    mod_all = _all_gather_rows(mod_blk, name="gather_mod")
    mod_mine = lax.dynamic_index_in_dim(mod_all, me, axis=1, keepdims=False)
    mod = lax.dynamic_index_in_dim(mod_mine.reshape(4, 2, n_ada), ac, axis=1, keepdims=False).reshape(1, 6 * D)

    W = _early_weights({n: _gather_weight(P[n][0], name="gather_" + n) for n in _EARLY})
    late = {}
    for n in _LATE:
        rh = P[n].shape[1] // 2
        late[n] = lax.dynamic_slice_in_dim(P[n][0], ac * rh, rh, axis=0).astype(BF16)

    gains = {n: P[n] for n in ["g_pre_mix", "g_post_mix", "g_pre_ffn", "g_post_ffn", "g_q_lora", "g_kv_lora"]}
    loss, grad_x, dW, grads, small = _local_step(x[0], loss_target[0], mod, positions, gains, b_forget, W, late)

    shards = _grad_shards(dW)
    grads.update({n: _reduce_scatter_weight(shards[n], name="scatter_" + n) for n in shards})

    small = dict(small, loss=_pad_block(loss))
    row = jnp.concatenate([small[n] for n in _ROW], axis=1)
    rows = _all_gather_rows(jnp.pad(row, ((0, 7), (0, 0))), name="gather_small")[:, 0, :]
    tot = _rowsum(rows, name="sum_small")
    piece = lambda n: tot[:, _ROW[n][0]:_ROW[n][0] + _ROW[n][1]]
    grads["b_ada"] = piece("dmod")
    for n in ["g_pre_mix", "g_post_mix", "g_pre_ffn", "g_post_ffn", "g_q_lora", "g_kv_lora"]:
        grads[n] = piece(n)
    grads["b_forget"] = piece("b_forget")[:, :H]
    loss_out = piece("loss")[0, 0]
    dmod_all = rows[:, _ROW["dmod"][0]:_ROW["dmod"][0] + 6 * D]
    dmod_shard = jnp.pad(lax.dynamic_slice(dmod_all, (0, chip * n_ada), (8, n_ada)), ((0, 8), (0, 0)))
    grads["w_ada"] = _matmul(silu_c, dmod_shard, ta=True, name="ada_dw")

    delta, new_m, new_v = {}, {}, {}
    for n in ["w_ada"] + _BIG:
        delta[n], new_m[n], new_v[n] = _adamw(P[n], grads[n], M[n], V[n], name="adamw_" + n)
    cat = lambda T: jnp.concatenate([T[n] for n in _SMALL], axis=1)
    d_s, m_s, v_s = (t[0] for t in _adamw(cat(P)[None], cat(grads), cat(M)[None], cat(V)[None], name="adamw_small"))
    o = 0
    for n in _SMALL:
        wdt = P[n].shape[1]
        delta[n], new_m[n], new_v[n] = d_s[:, o:o + wdt], m_s[:, o:o + wdt], v_s[:, o:o + wdt]
        o += wdt

    def shaped(T, n):
        return T[n].reshape(P[n].shape)

    return (loss_out, grad_x[None], *[shaped(grads, n) for n in _ORDER], *[shaped(delta, n) for n in _ORDER],
            *[shaped(new_m, n) for n in _ORDER], *[shaped(new_v, n) for n in _ORDER])
```

```python
import functools
import math
from typing import Callable, NamedTuple

import jax
import jax.numpy as jnp
from jax import lax
from jax.experimental import pallas as pl
from jax.experimental.pallas import tpu as pltpu

F32 = jnp.float32
BF16 = jnp.bfloat16
_MXU = jnp.bfloat16

S = 2048
D = 1024
H = 8
HP = 128
FOX_HD = 64
MLA_NOPE = 64
MLA_ROPE = 32
MLA_V = 64
Q_LORA = 768
KV_LORA = 256
D_FF = 2816
NORM_EPS = 1e-6
ROPE_THETA = 10000.0
NEG = -1e30

ADAM_LR = 0.001
ADAM_B1 = 0.9
ADAM_B2 = 0.999
ADAM_EPS = 1e-08
ADAM_WD = 0.01
ADAM_STEP = 10

LANES = 128
VMEM_CAP = 60 * 1024 * 1024
MESH = pl.DeviceIdType.MESH

NQKV = 3 * H * HP
OFF_CQ = 0
OFF_CKV = Q_LORA
OFF_GFOX = 1024
OFF_GMLA = 2048
OFF_FLOG = 3072
OFF_KRIN = 3200
NREST = 3328
KRIN_LANE = 64
ROW_SUM_LANE = 64
COL_SUM_LANE = 65


def _limit(nbytes):
    return int(min(VMEM_CAP, nbytes * 1.25 + (4 << 20)))


def _nbytes(shape, dtype):
    n = 1
    for s in shape:
        n *= s
    return n * jnp.dtype(dtype).itemsize


def _pick(n, cap):
    best = None
    for t in range(LANES, min(n, cap) + 1, LANES):
        if n % t == 0:
            best = t
    return best if best is not None else n


def _pcall(body, *, out_shape, **kw):
    outs = jax.tree.map(lambda s: pltpu.HBM(s.shape, s.dtype), out_shape)
    call = pl.pallas_call(body, out_shape=outs, **kw)
    return lambda *args: call(*[pltpu.with_memory_space_constraint(a, pltpu.HBM) for a in args])


def _matmul(a, b, *, ta=False, tb=False, out_dtype=F32, name, tm_cap=1024, tn_cap=512, tk_cap=1024,
            b_shards=False, out_shards=False):
    if ta:
        K, M = a.shape
    else:
        M, K = a.shape
    if b_shards:
        _, R, cb = b.shape
        N, K2 = (R, 4 * cb) if tb else (4 * cb, R)
    elif tb:
        N, K2 = b.shape
    else:
        K2, N = b.shape
    assert K == K2, (a.shape, b.shape, ta, tb)
    tm = _pick(M, tm_cap)
    tn = _pick(N, tn_cap)
    tk = K if K <= tk_cap else _pick(K, tk_cap)
    nk = K // tk
    dims = (((0 if ta else 1,), (1 if tb else 0,)), ((), ()))

    def body(a_ref, b_ref, o_ref, acc_ref):
        k = pl.program_id(2)

        @pl.when(k == 0)
        def _():
            acc_ref[...] = jnp.zeros_like(acc_ref)

        acc_ref[...] += lax.dot_general(a_ref[...].astype(_MXU), b_ref[...].astype(_MXU), dims,
                                        preferred_element_type=F32)

        @pl.when(k == nk - 1)
        def _():
            o_ref[...] = acc_ref[...].astype(out_dtype)

    a_spec = pl.BlockSpec((tk, tm), lambda i, j, k: (k, i)) if ta else pl.BlockSpec((tm, tk), lambda i, j, k: (i, k))
    if b_shards and tb:
        assert cb % tk == 0
        per = cb // tk
        b_spec = pl.BlockSpec((None, tn, tk), lambda i, j, k: (k // per, j, k % per))
    elif b_shards:
        assert cb % tn == 0
        per = cb // tn
        b_spec = pl.BlockSpec((None, tk, tn), lambda i, j, k: (j // per, k, j % per))
    elif tb:
        b_spec = pl.BlockSpec((tn, tk), lambda i, j, k: (j, k))
    else:
        b_spec = pl.BlockSpec((tk, tn), lambda i, j, k: (k, j))
    if out_shards:
        assert (N // 4) % tn == 0
        pern = N // 4 // tn
        out_shape = jax.ShapeDtypeStruct((4, M, N // 4), out_dtype)
        out_spec = pl.BlockSpec((None, tm, tn), lambda i, j, k: (j // pern, i, j % pern))
    else:
        out_shape = jax.ShapeDtypeStruct((M, N), out_dtype)
        out_spec = pl.BlockSpec((tm, tn), lambda i, j, k: (i, j))
    need = (2 * _nbytes((tm, tk), a.dtype) + 2 * _nbytes((tk, tn), b.dtype) + 2 * _nbytes((tm, tn), out_dtype)
            + _nbytes((tm, tn), F32) * 2 + _nbytes((tm, tk), _MXU) + _nbytes((tk, tn), _MXU))
    return _pcall(
        body, name=name,
        out_shape=out_shape,
        grid=(M // tm, N // tn, nk),
        in_specs=[a_spec, b_spec],
        out_specs=out_spec,
        scratch_shapes=[pltpu.VMEM((tm, tn), F32)],
        compiler_params=pltpu.CompilerParams(dimension_semantics=("parallel", "parallel", "arbitrary"),
                                             vmem_limit_bytes=_limit(need)),
    )(a, b)


TM = 512
TM_FF = 256


def _vec(w):
    return pl.BlockSpec((1, w), lambda i: (0, 0))


def _rows(w, col=0, tm=TM):
    return pl.BlockSpec((tm, w), lambda i: (i, col))


def _row_params(need, carried=False):
    return pltpu.CompilerParams(dimension_semantics=("arbitrary" if carried else "parallel",),
                                vmem_limit_bytes=_limit(need))


def _norm_mod(x, col, w, g, ops, sh, *, name):
    def body(x_ref, g_ref, ops_ref, sh_ref, o_ref):
        xv = x_ref[...]
        r = lax.rsqrt(jnp.mean(xv * xv, axis=-1, keepdims=True) + NORM_EPS)
        o_ref[...] = (((xv * r) * g_ref[...]) * ops_ref[...] + sh_ref[...]).astype(o_ref.dtype)

    return _pcall(
        body, name=name, out_shape=jax.ShapeDtypeStruct((S, w), _MXU), grid=(S // TM,),
        in_specs=[_rows(w, col), _vec(w), _vec(w), _vec(w)], out_specs=_rows(w),
        compiler_params=_row_params(8 * _nbytes((TM, w), F32)),
    )(x, g, ops, sh)


def _norm_mod_bwd(x, col, w, dh, g, ops, dres, *, name):
    has_res = dres is not None

    def body(*refs):
        if has_res:
            x_ref, dh_ref, g_ref, ops_ref, dres_ref, dx_ref, s1_ref, s2_ref = refs
        else:
            x_ref, dh_ref, g_ref, ops_ref, dx_ref, s1_ref, s2_ref = refs
        i = pl.program_id(0)

        @pl.when(i == 0)
        def _():
            s1_ref[...] = jnp.zeros_like(s1_ref)
            s2_ref[...] = jnp.zeros_like(s2_ref)

        xv = x_ref[...]
        dhv = dh_ref[...]
        r = lax.rsqrt(jnp.mean(xv * xv, axis=-1, keepdims=True) + NORM_EPS)
        xn = xv * r
        dxn = dhv * (g_ref[...] * ops_ref[...])
        dx = r * (dxn - xn * jnp.mean(dxn * xn, axis=-1, keepdims=True))
        if has_res:
            dx = dx + dres_ref[...]
        dx_ref[...] = dx
        s1_ref[...] += jnp.sum(dhv, axis=0, keepdims=True)
        s2_ref[...] += jnp.sum(dhv * xn, axis=0, keepdims=True)

    in_specs = [_rows(w, col), _rows(w), _vec(w), _vec(w)] + ([_rows(w)] if has_res else [])
    args = (x, dh, g, ops) + ((dres,) if has_res else ())
    return _pcall(
        body, name=name,
        out_shape=(jax.ShapeDtypeStruct((S, w), F32), jax.ShapeDtypeStruct((1, w), F32), jax.ShapeDtypeStruct((1, w), F32)),
        grid=(S // TM,), in_specs=in_specs, out_specs=(_rows(w), _vec(w), _vec(w)),
        compiler_params=_row_params(12 * _nbytes((TM, w), F32), carried=True),
    )(*args)


def _post_res(xres, y, g, gt, *, name):
    def body(x_ref, y_ref, g_ref, gt_ref, o_ref):
        yv = y_ref[...]
        r = lax.rsqrt(jnp.mean(yv * yv, axis=-1, keepdims=True) + NORM_EPS)
        o_ref[...] = x_ref[...] + gt_ref[...] * ((yv * r) * g_ref[...])

    return _pcall(
        body, name=name, out_shape=jax.ShapeDtypeStruct((S, D), F32), grid=(S // TM,),
        in_specs=[_rows(D), _rows(D), _vec(D), _vec(D)], out_specs=_rows(D),
        compiler_params=_row_params(8 * _nbytes((TM, D), F32)),
    )(xres, y, g, gt)


def _post_res_loss(xres, y, g, gt, target, *, name):
    def body(x_ref, y_ref, g_ref, gt_ref, t_ref, dout_ref, loss_ref):
        i = pl.program_id(0)

        @pl.when(i == 0)
        def _():
            loss_ref[...] = jnp.zeros_like(loss_ref)

        yv = y_ref[...]
        r = lax.rsqrt(jnp.mean(yv * yv, axis=-1, keepdims=True) + NORM_EPS)
        out = x_ref[...] + gt_ref[...] * ((yv * r) * g_ref[...])
        err = out - t_ref[...]
        dout_ref[...] = err * (1.0 / D)
        per_row = jnp.mean(err * err, axis=-1, keepdims=True)
        loss_ref[...] += 0.5 * jnp.sum(per_row, axis=0, keepdims=True)

    return _pcall(
        body, name=name,
        out_shape=(jax.ShapeDtypeStruct((S, D), F32), jax.ShapeDtypeStruct((1, 1), F32)), grid=(S // TM,),
        in_specs=[_rows(D), _rows(D), _vec(D), _vec(D), _rows(D)],
        out_specs=(_rows(D), pl.BlockSpec((1, 1), lambda i: (0, 0))),
        compiler_params=_row_params(10 * _nbytes((TM, D), F32), carried=True),
    )(xres, y, g, gt, target)


def _post_res_bwd(dxn, y, g, gt, *, name):
    def body(d_ref, y_ref, g_ref, gt_ref, dy_ref, sgt_ref, sg_ref):
        i = pl.program_id(0)

        @pl.when(i == 0)
        def _():
            sgt_ref[...] = jnp.zeros_like(sgt_ref)
            sg_ref[...] = jnp.zeros_like(sg_ref)

        yv = y_ref[...]
        dv = d_ref[...]
        r = lax.rsqrt(jnp.mean(yv * yv, axis=-1, keepdims=True) + NORM_EPS)
        yn = yv * r
        dn = dv * gt_ref[...]
        dyn = dn * g_ref[...]
        dy_ref[...] = (r * (dyn - yn * jnp.mean(dyn * yn, axis=-1, keepdims=True))).astype(dy_ref.dtype)
        sgt_ref[...] += jnp.sum(dv * (yn * g_ref[...]), axis=0, keepdims=True)
        sg_ref[...] += jnp.sum(dn * yn, axis=0, keepdims=True)

    return _pcall(
        body, name=name,
        out_shape=(jax.ShapeDtypeStruct((S, D), _MXU), jax.ShapeDtypeStruct((1, D), F32), jax.ShapeDtypeStruct((1, D), F32)),
        grid=(S // TM,), in_specs=[_rows(D), _rows(D), _vec(D), _vec(D)], out_specs=(_rows(D), _vec(D), _vec(D)),
        compiler_params=_row_params(10 * _nbytes((TM, D), F32), carried=True),
    )(dxn, y, g, gt)


def _swiglu(gu, *, name):
    def body(g_ref, u_ref, o_ref):
        gv = g_ref[...]
        o_ref[...] = ((gv * jax.nn.sigmoid(gv)) * u_ref[...]).astype(o_ref.dtype)

    return _pcall(
        body, name=name, out_shape=jax.ShapeDtypeStruct((S, D_FF), _MXU), grid=(S // TM_FF,),
        in_specs=[_rows(D_FF, 0, TM_FF), _rows(D_FF, 1, TM_FF)], out_specs=_rows(D_FF, 0, TM_FF),
        compiler_params=_row_params(8 * _nbytes((TM_FF, D_FF), F32)),
    )(gu, gu)


def _swiglu_bwd(gu, dact, *, name):
    def body(g_ref, u_ref, d_ref, o_ref):
        gv = g_ref[...]
        dv = d_ref[...]
        sg = jax.nn.sigmoid(gv)
        o_ref[:, :D_FF] = (dv * u_ref[...] * (sg * (1.0 + gv * (1.0 - sg)))).astype(o_ref.dtype)
        o_ref[:, D_FF:] = (dv * (gv * sg)).astype(o_ref.dtype)

    return _pcall(
        body, name=name, out_shape=jax.ShapeDtypeStruct((S, 2 * D_FF), _MXU), grid=(S // TM_FF,),
        in_specs=[_rows(D_FF, 0, TM_FF), _rows(D_FF, 1, TM_FF), _rows(D_FF, 0, TM_FF)], out_specs=_rows(2 * D_FF, 0, TM_FF),
        compiler_params=_row_params(12 * _nbytes((TM_FF, D_FF), F32)),
    )(gu, gu, dact)


def _merge(rest, pa, pb, *, name):
    def body(ga_ref, gb_ref, pa_ref, pb_ref, o_ref):
        o_ref[...] = (jax.nn.sigmoid(ga_ref[...]) * pa_ref[...] + jax.nn.sigmoid(gb_ref[...]) * pb_ref[...]).astype(o_ref.dtype)

    return _pcall(
        body, name=name, out_shape=jax.ShapeDtypeStruct((S, D), _MXU), grid=(S // TM,),
        in_specs=[_rows(D, OFF_GFOX // D), _rows(D, OFF_GMLA // D), _rows(D), _rows(D)], out_specs=_rows(D),
        compiler_params=_row_params(10 * _nbytes((TM, D), F32)),
    )(rest, rest, pa, pb)


def _merge_bwd(rest, pa, pb, dm, *, name):
    def body(ga_ref, gb_ref, pa_ref, pb_ref, d_ref, dpa_ref, dpb_ref, dga_ref, dgb_ref):
        dv = d_ref[...]
        sa = jax.nn.sigmoid(ga_ref[...])
        sb = jax.nn.sigmoid(gb_ref[...])
        dpa_ref[...] = (dv * sa).astype(dpa_ref.dtype)
        dpb_ref[...] = (dv * sb).astype(dpb_ref.dtype)
        dga_ref[...] = (dv * pa_ref[...] * (sa * (1.0 - sa))).astype(dga_ref.dtype)
        dgb_ref[...] = (dv * pb_ref[...] * (sb * (1.0 - sb))).astype(dgb_ref.dtype)

    o = jax.ShapeDtypeStruct((S, D), _MXU)
    return _pcall(
        body, name=name, out_shape=(o, o, o, o), grid=(S // TM,),
        in_specs=[_rows(D, OFF_GFOX // D), _rows(D, OFF_GMLA // D), _rows(D), _rows(D), _rows(D)],
        out_specs=(_rows(D), _rows(D), _rows(D), _rows(D)),
        compiler_params=_row_params(16 * _nbytes((TM, D), F32)),
    )(rest, rest, pa, pb, dm)


SCAN = 256


def _split_dot(tri, x):
    hi = x.astype(_MXU)
    r1 = x - hi.astype(F32)
    mid = r1.astype(_MXU)
    lo = (r1 - mid.astype(F32)).astype(_MXU)
    dot = functools.partial(jnp.dot, preferred_element_type=F32)
    return dot(tri, hi) + dot(tri, mid) + dot(tri, lo)


def _fox_prep(rest, bf, *, name):
    def body(z_ref, b_ref, f_ref):
        lane = lax.broadcasted_iota(jnp.int32, (SCAN, LANES), 1)
        tri = (lax.broadcasted_iota(jnp.int32, (SCAN, SCAN), 1) <= lax.broadcasted_iota(jnp.int32, (SCAN, SCAN), 0)).astype(_MXU)
        carry = jnp.zeros((1, LANES), F32)
        for c in range(S // SCAN):
            z = z_ref[c * SCAN:(c + 1) * SCAN, :] + b_ref[...]
            lf = jnp.minimum(z, 0.0) - jnp.log(1.0 + jnp.exp(-jnp.abs(z)))
            lf = jnp.where(lane < H, lf, 0.0)
            cum = _split_dot(tri, lf) + carry
            f_ref[c * SCAN:(c + 1) * SCAN, :] = cum
            carry = cum[SCAN - 1:SCAN, :]

    return _pcall(
        body, name=name, out_shape=jax.ShapeDtypeStruct((S, LANES), F32), grid=(1,),
        in_specs=[pl.BlockSpec((S, LANES), lambda i: (0, OFF_FLOG // LANES)), pl.BlockSpec((1, LANES), lambda i: (0, 0))],
        out_specs=pl.BlockSpec((S, LANES), lambda i: (0, 0)),
        compiler_params=pltpu.CompilerParams(vmem_limit_bytes=_limit(8 * _nbytes((S, LANES), F32))),
    )(rest, bf)


def _fox_bwd_prep(rest, bf, dF, *, name):
    def body(z_ref, b_ref, d_ref, o_ref, db_ref):
        lane = lax.broadcasted_iota(jnp.int32, (SCAN, LANES), 1)
        tri = (lax.broadcasted_iota(jnp.int32, (SCAN, SCAN), 1) >= lax.broadcasted_iota(jnp.int32, (SCAN, SCAN), 0)).astype(_MXU)
        carry = jnp.zeros((1, LANES), F32)
        db = jnp.zeros((1, LANES), F32)
        for c in range(S // SCAN - 1, -1, -1):
            rc = _split_dot(tri, d_ref[c * SCAN:(c + 1) * SCAN, :]) + carry
            z = z_ref[c * SCAN:(c + 1) * SCAN, :] + b_ref[...]
            dz = jnp.where(lane < H, rc * jax.nn.sigmoid(-z), 0.0)
            o_ref[c * SCAN:(c + 1) * SCAN, :] = dz
            db = db + jnp.sum(dz, axis=0, keepdims=True)
            carry = rc[0:1, :]
        db_ref[...] = db

    return _pcall(
        body, name=name,
        out_shape=(jax.ShapeDtypeStruct((S, LANES), F32), jax.ShapeDtypeStruct((1, LANES), F32)), grid=(1,),
        in_specs=[pl.BlockSpec((S, LANES), lambda i: (0, OFF_FLOG // LANES)), pl.BlockSpec((1, LANES), lambda i: (0, 0)),
                  pl.BlockSpec((S, LANES), lambda i: (0, 0))],
        out_specs=(pl.BlockSpec((S, LANES), lambda i: (0, 0)), pl.BlockSpec((1, LANES), lambda i: (0, 0))),
        compiler_params=pltpu.CompilerParams(vmem_limit_bytes=_limit(10 * _nbytes((S, LANES), F32))),
    )(rest, bf, dF)


def _swap16(x):
    lane = lax.broadcasted_iota(jnp.int32, x.shape, 1)
    half = MLA_ROPE // 2
    sw = jnp.where(lane < KRIN_LANE + half, pltpu.roll(x, LANES - half, 1), pltpu.roll(x, half, 1))
    return jnp.where((lane >= KRIN_LANE) & (lane < KRIN_LANE + MLA_ROPE), sw, 0.0)


def _mla_assemble(qb, kvb, rest, ctab, stab, *, name):
    def body(q_ref, kk_ref, kv_ref, kr_ref, c_ref, s_ref, qo_ref, ko_ref, vo_ref):
        cv = c_ref[...]
        sv = s_ref[...]
        kr = kr_ref[...]
        kpe = kr * cv + _swap16(kr) * sv
        for h in range(H):
            sl = slice(h * HP, (h + 1) * HP)
            qh = q_ref[:, sl]
            qo_ref[:, sl] = (qh * cv + _swap16(qh) * sv).astype(qo_ref.dtype)
            ko_ref[:, sl] = (kk_ref[:, sl] + kpe).astype(ko_ref.dtype)
        vo_ref[...] = kv_ref[...].astype(vo_ref.dtype)

    o = jax.ShapeDtypeStruct((S, H * HP), _MXU)
    return _pcall(
        body, name=name, out_shape=(o, o, o), grid=(S // TM,),
        in_specs=[_rows(H * HP), _rows(H * HP, 0), _rows(H * HP, 1), _rows(LANES, OFF_KRIN // LANES), _rows(LANES), _rows(LANES)],
        out_specs=(_rows(H * HP), _rows(H * HP), _rows(H * HP)),
        compiler_params=_row_params(14 * _nbytes((TM, H * HP), F32)),
    )(qb, kvb, kvb, rest, ctab, stab)


def _mla_assemble_bwd(dq, dk, dv, ctab, stab, *, name):
    def body(dq_ref, dk_ref, dv_ref, c_ref, s_ref, dqo_ref, dkv_ref, dkr_ref):
        cv = c_ref[...]
        sv = s_ref[...]
        lane = lax.broadcasted_iota(jnp.int32, (TM, LANES), 1)
        dsum = jnp.zeros((TM, LANES), F32)
        for h in range(H):
            sl = slice(h * HP, (h + 1) * HP)
            dqh = dq_ref[:, sl]
            dqo_ref[:, sl] = (dqh * cv + _swap16(dqh * sv)).astype(dqo_ref.dtype)
            dsum = dsum + dk_ref[:, sl]
        dkv_ref[:, :H * HP] = dk_ref[...].astype(dkv_ref.dtype)
        dkv_ref[:, H * HP:] = dv_ref[...].astype(dkv_ref.dtype)
        dkr = dsum * cv + _swap16(dsum * sv)
        dkr_ref[...] = jnp.where((lane >= KRIN_LANE) & (lane < KRIN_LANE + MLA_ROPE), dkr, 0.0)

    return _pcall(
        body, name=name,
        out_shape=(jax.ShapeDtypeStruct((S, H * HP), _MXU), jax.ShapeDtypeStruct((S, 2 * H * HP), _MXU),
                   jax.ShapeDtypeStruct((S, LANES), F32)),
        grid=(S // TM,),
        in_specs=[_rows(H * HP), _rows(H * HP), _rows(H * HP), _rows(LANES), _rows(LANES)],
        out_specs=(_rows(H * HP), _rows(2 * H * HP), _rows(LANES)),
        compiler_params=_row_params(14 * _nbytes((TM, H * HP), F32)),
    )(dq, dk, dv, ctab, stab)


TQ = 256
TKF = 512
TKB = 256
TQB = 512
_NT = (((1,), (1,)), ((), ()))
_TN = (((0,), (0,)), ((), ()))


def _is_pow2(x):
    return math.frexp(x)[0] == 0.5


def _attn_fwd(q, k, v, frow, *, scale, name, side=None, heads0=(0, 0, 0)):
    has_decay = frow is not None
    fold = _is_pow2(scale)
    n_in = 4 if has_decay else 3
    n_side_in = len(side.inputs) if side else 0
    n_side_out = len(side.out_shapes) if side else 0

    def body(*refs):
        q_ref, k_ref, v_ref = refs[:3]
        fr_ref = refs[3] if has_decay else None
        side_in = refs[n_in:n_in + n_side_in]
        o_ref, lse_ref = refs[n_in + n_side_in:n_in + n_side_in + 2]
        side_out = refs[n_in + n_side_in + 2:n_in + n_side_in + 2 + n_side_out]
        side_scratch = refs[n_in + n_side_in + 2 + n_side_out:]
        i = pl.program_id(1)
        if side:
            @pl.when((pl.program_id(0) == 0) & (i == 0))
            def _():
                side.start(side_in, side_out, side_scratch)
        qv = q_ref[...]
        if fold:
            qv = (qv * scale).astype(qv.dtype)
        last = (i * TQ) // TKF

        def tile(j, carry, masked):
            m, l, acc = carry
            k0 = pl.multiple_of(j * TKF, TKF)
            kj = k_ref[pl.ds(k0, TKF), :]
            vj = v_ref[pl.ds(k0, TKF), :]
            s = lax.dot_general(qv, kj, _NT, preferred_element_type=F32)
            if not fold:
                s = s * scale
            if has_decay:
                s = s - fr_ref[0, j]
            if masked:
                rows = i * TQ + lax.broadcasted_iota(jnp.int32, (TQ, TKF), 0)
                cols = j * TKF + lax.broadcasted_iota(jnp.int32, (TQ, TKF), 1)
                s = jnp.where(cols <= rows, s, NEG)
            m_new = jnp.maximum(m, jnp.max(s, axis=-1, keepdims=True))
            alpha = jnp.exp(m - m_new)
            p = jnp.exp(s - m_new)
            l = alpha * l + jnp.sum(p, axis=-1, keepdims=True)
            acc = alpha * acc + jnp.dot(p.astype(_MXU), vj, preferred_element_type=F32)
            return m_new, l, acc

        init = (jnp.full((TQ, 1), NEG, F32), jnp.zeros((TQ, 1), F32), jnp.zeros((TQ, HP), F32))
        carry = lax.fori_loop(0, last, lambda j, c: tile(j, c, False), init)
        m, l, acc = tile(last, carry, True)
        o_ref[...] = acc / l
        lse_ref[0] = m + jnp.log(l)
        if side:
            @pl.when((pl.program_id(0) == H - 1) & (i == S // TQ - 1))
            def _():
                side.finish(side_in, side_out, side_scratch)

    q0, k0, v0 = heads0
    in_specs = [pl.BlockSpec((TQ, HP), lambda h, i: (i, h + q0)), pl.BlockSpec((S, HP), lambda h, i: (0, h + k0)),
                pl.BlockSpec((S, HP), lambda h, i: (0, h + v0))]
    args = (q, k, v)
    if has_decay:
        in_specs += [pl.BlockSpec((1, S // TKF, 1, TKF), lambda h, i: (h, 0, 0, 0))]
        args += (frow,)
    out_shape = (jax.ShapeDtypeStruct((S, H * HP), F32), jax.ShapeDtypeStruct((H, S, 1), F32))
    out_specs = (pl.BlockSpec((TQ, HP), lambda h, i: (i, h)), pl.BlockSpec((1, TQ, 1), lambda h, i: (h, i, 0)))
    extra = {}
    if side:
        anywhere = pl.BlockSpec(memory_space=pl.ANY)
        in_specs += [anywhere] * n_side_in
        args += tuple(side.inputs)
        out_shape += tuple(side.out_shapes)
        out_specs += (anywhere,) * n_side_out
        extra = dict(scratch_shapes=list(side.scratch),
                     input_output_aliases={n_in + a: 2 + b for a, b in side.aliases.items()})
    return _pcall(
        body, name=name, out_shape=out_shape, grid=(H, S // TQ), in_specs=in_specs, out_specs=out_specs,
        compiler_params=pltpu.CompilerParams(dimension_semantics=("arbitrary", "arbitrary") if side else ("parallel", "parallel"),
                                             vmem_limit_bytes=_limit(8 * _nbytes((S, HP), F32))),
        **extra,
    )(*args)


def _attn_delta(do, o, *, name):
    def body(do_ref, o_ref, dl_ref, dob_ref):
        for h in range(H):
            sl = slice(h * HP, (h + 1) * HP)
            dl_ref[h] = jnp.sum(do_ref[:, sl] * o_ref[:, sl], axis=-1, keepdims=True)
        dob_ref[...] = do_ref[...].astype(dob_ref.dtype)

    return _pcall(
        body, name=name,
        out_shape=(jax.ShapeDtypeStruct((H, S, 1), F32), jax.ShapeDtypeStruct((S, H * HP), _MXU)), grid=(S // TM,),
        in_specs=[_rows(H * HP), _rows(H * HP)],
        out_specs=(pl.BlockSpec((H, TM, 1), lambda i: (0, i, 0)), _rows(H * HP)),
        compiler_params=_row_params(8 * _nbytes((TM, H * HP), F32)),
    )(do, o)


def _attn_bwd(q, k, v, dob, lse_row, delta_row, fcol, *, scale, name, side=None, heads0=(0, 0, 0)):
    has_decay = fcol is not None
    fold = _is_pow2(scale)
    n_in = 7 if has_decay else 6
    n_side_in = len(side.inputs) if side else 0
    n_side_out = len(side.out_shapes) if side else 0

    def body(*refs):
        q_ref, k_ref, v_ref, do_ref, lse_ref, dl_ref = refs[:6]
        fc_ref = refs[6] if has_decay else None
        side_in = refs[n_in:n_in + n_side_in]
        dq_ref, dk_ref, dv_ref = refs[n_in + n_side_in:n_in + n_side_in + 3]
        side_out = refs[n_in + n_side_in + 3:n_in + n_side_in + 3 + n_side_out]
        dq_acc = refs[n_in + n_side_in + 3 + n_side_out]
        side_scratch = refs[n_in + n_side_in + 4 + n_side_out:]
        j = pl.program_id(1)
        if side:
            @pl.when((pl.program_id(0) == 0) & (j == 0))
            def _():
                side.start(side_in, side_out, side_scratch)

        @pl.when(j == 0)
        def _():
            dq_acc[...] = jnp.zeros_like(dq_acc)

        kj = k_ref[...]
        vj = v_ref[...]
        kjs = (kj * scale).astype(kj.dtype) if fold else kj
        if has_decay:
            klane = lax.broadcasted_iota(jnp.int32, (TKB, HP), 1)
            kj = jnp.where(klane == ROW_SUM_LANE, 1.0, kj).astype(kj.dtype)
        first = (j * TKB) // TQB

        def tile(t, carry, masked):
            dk, dv = carry
            r0 = pl.multiple_of(t * TQB, TQB)
            qi = q_ref[pl.ds(r0, TQB), :]
            doi = do_ref[pl.ds(r0, TQB), :]
            st = lax.dot_general(kjs, qi, _NT, preferred_element_type=F32)
            if not fold:
                st = st * scale
            if has_decay:
                st = st - fc_ref[0]
            if masked:
                keys = j * TKB + lax.broadcasted_iota(jnp.int32, (TKB, TQB), 0)
                qpos = t * TQB + lax.broadcasted_iota(jnp.int32, (TKB, TQB), 1)
                st = jnp.where(keys <= qpos, st, NEG)
            pt = jnp.exp(st - lse_ref[0, t])
            dv = dv + jnp.dot(pt.astype(_MXU), doi, preferred_element_type=F32)
            dpt = lax.dot_general(vj, doi, _NT, preferred_element_type=F32)
            dst = (pt * (dpt - dl_ref[0, t])).astype(_MXU)
            if has_decay:
                lane = lax.broadcasted_iota(jnp.int32, (TQB, HP), 1)
                qi = jnp.where(lane == COL_SUM_LANE, 1.0, qi).astype(qi.dtype)
            dk = dk + jnp.dot(dst, qi, preferred_element_type=F32)
            dq_acc[pl.ds(r0, TQB), :] += lax.dot_general(dst, kj, _TN, preferred_element_type=F32)
            return dk, dv

        zero = jnp.zeros((TKB, HP), F32)
        carry = tile(first, (zero, zero), True)
        dk, dv = lax.fori_loop(first + 1, S // TQB, lambda t, c: tile(t, c, False), carry)
        dk_ref[...] = dk * scale
        dv_ref[...] = dv

        @pl.when(j == S // TKB - 1)
        def _():
            dq_ref[...] = dq_acc[...] * scale

        if side:
            @pl.when((pl.program_id(0) == H - 1) & (j == S // TKB - 1))
            def _():
                side.finish(side_in, side_out, side_scratch)

    q0, k0, v0 = heads0
    head = pl.BlockSpec((S, HP), lambda h, j: (0, h))
    kv = pl.BlockSpec((TKB, HP), lambda h, j: (j, h))
    stat = pl.BlockSpec((1, S // TQB, 1, TQB), lambda h, j: (h, 0, 0, 0))
    in_specs = [pl.BlockSpec((S, HP), lambda h, j: (0, h + q0)), pl.BlockSpec((TKB, HP), lambda h, j: (j, h + k0)),
                pl.BlockSpec((TKB, HP), lambda h, j: (j, h + v0)), head, stat, stat]
    args = (q, k, v, dob, lse_row, delta_row)
    if has_decay:
        in_specs += [pl.BlockSpec((1, TKB, 1), lambda h, j: (h, j, 0))]
        args += (fcol,)
    o = jax.ShapeDtypeStruct((S, H * HP), F32)
    out_shape, out_specs, scratch, aliases = (o, o, o), (head, kv, kv), [pltpu.VMEM((S, HP), F32)], {}
    if side:
        anywhere = pl.BlockSpec(memory_space=pl.ANY)
        in_specs += [anywhere] * n_side_in
        args += tuple(side.inputs)
        out_shape += tuple(side.out_shapes)
        out_specs += (anywhere,) * n_side_out
        scratch += list(side.scratch)
        aliases = {n_in + a: 3 + b for a, b in side.aliases.items()}
    return _pcall(
        body, name=name, out_shape=out_shape, grid=(H, S // TKB), in_specs=in_specs,
        out_specs=out_specs, scratch_shapes=scratch, input_output_aliases=aliases,
        compiler_params=pltpu.CompilerParams(dimension_semantics=("arbitrary" if side else "parallel", "arbitrary"),
                                             vmem_limit_bytes=_limit(12 * _nbytes((S, HP), F32))),
    )(*args)


def _ada_mod(c_all, w_shard, b_shard, *, name):
    R = c_all.shape[0]
    N = w_shard.shape[1]
    tn = 512

    def body(c_ref, w_ref, b_ref, o_ref, sc_ref):
        cv = c_ref[...]
        sc = (cv * jax.nn.sigmoid(cv)).astype(_MXU)
        sc_ref[...] = sc
        o_ref[...] = jnp.dot(sc, w_ref[...].astype(_MXU), preferred_element_type=F32) + b_ref[...]

    return _pcall(
        body, name=name,
        out_shape=(jax.ShapeDtypeStruct((R, N), F32), jax.ShapeDtypeStruct((R, D), _MXU)), grid=(N // tn,),
        in_specs=[pl.BlockSpec((R, D), lambda j: (0, 0)), pl.BlockSpec((D, tn), lambda j: (0, j)), pl.BlockSpec((1, tn), lambda j: (0, j))],
        out_specs=(pl.BlockSpec((R, tn), lambda j: (0, j)), pl.BlockSpec((R, D), lambda j: (0, 0))),
        compiler_params=pltpu.CompilerParams(dimension_semantics=("arbitrary",), vmem_limit_bytes=_limit(6 * _nbytes((D, tn), F32))),
    )(c_all, w_shard, b_shard)


def _rowsum(x, *, name):
    R, L = x.shape

    def body(x_ref, o_ref):
        acc = x_ref[0:1, :]
        for r in range(1, R):
            acc = acc + x_ref[r:r + 1, :]
        o_ref[...] = acc

    return pl.pallas_call(body, name=name, out_shape=jax.ShapeDtypeStruct((1, L), F32),
                          in_specs=[pl.BlockSpec(memory_space=pltpu.VMEM)], out_specs=pl.BlockSpec(memory_space=pltpu.VMEM))(x)


def _adamw(w, g, m, v, *, name):
    _, R, C = w.shape
    tr = R
    for t in range(8, R + 1, 8):
        if R % t == 0 and t * C * 4 <= (1 << 20):
            tr = t

    def body(w_ref, g_ref, m_ref, v_ref, d_ref, mo_ref, vo_ref):
        gv = g_ref[...]
        m2 = ADAM_B1 * m_ref[...] + (1.0 - ADAM_B1) * gv
        v2 = ADAM_B2 * v_ref[...] + (1.0 - ADAM_B2) * (gv * gv)
        m_hat = m2 / (1.0 - ADAM_B1 ** ADAM_STEP)
        v_hat = v2 / (1.0 - ADAM_B2 ** ADAM_STEP)
        d_ref[...] = -ADAM_LR * (m_hat / (jnp.sqrt(v_hat) + ADAM_EPS) + ADAM_WD * w_ref[...])
        mo_ref[...] = m2
        vo_ref[...] = v2

    blk = pl.BlockSpec((None, tr, C), lambda i: (0, i, 0))
    o = jax.ShapeDtypeStruct((1, R, C), F32)
    return _pcall(
        body, name=name, out_shape=(o, o, o), grid=(R // tr,),
        in_specs=[blk, pl.BlockSpec((tr, C), lambda i: (i, 0)), blk, blk], out_specs=(blk, blk, blk),
        compiler_params=pltpu.CompilerParams(dimension_semantics=("parallel",), vmem_limit_bytes=_limit(20 * _nbytes((tr, C), F32))),
    )(w, g, m, v)


def _place():
    x, y, c = lax.axis_index("x"), lax.axis_index("y"), lax.axis_index("c")
    return x, y, c, [(1 - x, y), (x, 1 - y), (1 - x, 1 - y)]


def _two_level_gather(x_ref, out_ref, send_sems, recv_sems, local_sem):
    x, y, c, chips = _place()
    me, sibling = (x, y, c), (x, y, 1 - c)

    def blk(px, py, pc):
        return out_ref.at[4 * px + 2 * py + pc]

    def copy(k, block, to, src=None):
        return pltpu.make_async_remote_copy(
            src_ref=blk(*block) if src is None else src, dst_ref=blk(*block),
            send_sem=send_sems.at[k], recv_sem=recv_sems.at[k], device_id=to, device_id_type=MESH)

    mine = pltpu.make_async_copy(x_ref, blk(*me), local_sem)
    mine.start()
    first = [copy(0, me, sibling, src=x_ref)]
    first += [copy(1 + j, me, (*chip, c), src=x_ref) for j, chip in enumerate(chips)]
    for cp in first:
        cp.start()
    passed = [copy(4 + j, (*chip, c), sibling) for j, chip in enumerate(chips)]
    for j, chip in enumerate(chips):
        copy(1 + j, (*chip, c), me).wait_recv()
        passed[j].start()
    copy(0, sibling, me).wait_recv()
    for j, chip in enumerate(chips):
        copy(4 + j, (*chip, 1 - c), me).wait_recv()
    for cp in first + passed:
        cp.wait_send()
    mine.wait()


_GATHER_SEMS = [pltpu.SemaphoreType.DMA((7,)), pltpu.SemaphoreType.DMA((7,)), pltpu.SemaphoreType.DMA]


class _SideJob(NamedTuple):
    inputs: tuple
    out_shapes: tuple
    aliases: dict
    scratch: tuple
    start: Callable
    finish: Callable


def _block_index(px, py, pc):
    return 4 * px + 2 * py + pc


def _gather_level1_job(halves):
    nw = len(halves)

    def copies(ins, outs, scratch, n):
        sends, recvs, _ = scratch
        x, y, c, chips = _place()
        mine = outs[n].at[_block_index(x, y, c)]
        peers = [(x, y, 1 - c)] + [(*chip, c) for chip in chips]
        out = []
        for t, peer in enumerate(peers):
            sem = dict(send_sem=sends.at[4 * n + t], recv_sem=recvs.at[4 * n + t], device_id_type=MESH)
            landing = outs[n].at[_block_index(*peer)]
            out.append((pltpu.make_async_remote_copy(src_ref=ins[n], dst_ref=mine, device_id=peer, **sem),
                        pltpu.make_async_remote_copy(src_ref=landing, dst_ref=landing, device_id=peer, **sem)))
        local = pltpu.make_async_copy(ins[n], mine, scratch[2].at[n])
        return out, local

    def start(ins, outs, scratch):
        for n in range(nw):
            pairs, local = copies(ins, outs, scratch, n)
            local.start()
            for to, _ in pairs:
                to.start()

    def finish(ins, outs, scratch):
        for n in range(nw):
            pairs, local = copies(ins, outs, scratch, n)
            for to, frm in pairs:
                frm.wait_recv()
                to.wait_send()
            local.wait()

    return _SideJob(
        inputs=tuple(halves), out_shapes=tuple(jax.ShapeDtypeStruct((8,) + h.shape, h.dtype) for h in halves), aliases={},
        scratch=(pltpu.SemaphoreType.DMA((4 * nw,)), pltpu.SemaphoreType.DMA((4 * nw,)), pltpu.SemaphoreType.DMA((nw,))),
        start=start, finish=finish)


def _gather_direct_job(halves):
    nw = len(halves)

    def copies(ins, outs, scratch, n):
        sends, recvs, _ = scratch
        x, y, c, _ = _place()
        mine = outs[n].at[_block_index(x, y, c)]
        out = []
        for f, (fx, fy, fc) in enumerate(_FLIPS):
            peer = (_flipped(x, fx), _flipped(y, fy), _flipped(c, fc))
            sem = dict(send_sem=sends.at[7 * n + f], recv_sem=recvs.at[7 * n + f], device_id=peer, device_id_type=MESH)
            landing = outs[n].at[_block_index(*peer)]
            out.append((pltpu.make_async_remote_copy(src_ref=ins[n], dst_ref=mine, **sem),
                        pltpu.make_async_remote_copy(src_ref=landing, dst_ref=landing, **sem)))
        return out, pltpu.make_async_copy(ins[n], mine, scratch[2].at[n])

    def start(ins, outs, scratch):
        for n in range(nw):
            pairs, local = copies(ins, outs, scratch, n)
            local.start()
            for to, _ in pairs:
                to.start()

    def finish(ins, outs, scratch):
        for n in range(nw):
            pairs, local = copies(ins, outs, scratch, n)
            for to, frm in pairs:
                frm.wait_recv()
                to.wait_send()
            local.wait()

    return _SideJob(
        inputs=tuple(halves), out_shapes=tuple(jax.ShapeDtypeStruct((8,) + h.shape, h.dtype) for h in halves), aliases={},
        scratch=(pltpu.SemaphoreType.DMA((7 * nw,)), pltpu.SemaphoreType.DMA((7 * nw,)), pltpu.SemaphoreType.DMA((nw,))),
        start=start, finish=finish)


def _join_jobs(a, b):
    cut = (len(a.inputs), len(a.out_shapes), len(a.scratch))

    def parts(ins, outs, scratch):
        return ((ins[:cut[0]], outs[:cut[1]], scratch[:cut[2]]), (ins[cut[0]:], outs[cut[1]:], scratch[cut[2]:]))

    def start(ins, outs, scratch):
        pa, pb = parts(ins, outs, scratch)
        a.start(*pa)
        b.start(*pb)

    def finish(ins, outs, scratch):
        pa, pb = parts(ins, outs, scratch)
        a.finish(*pa)
        b.finish(*pb)

    aliases = dict(a.aliases)
    aliases.update({cut[0] + i: cut[1] + o for i, o in b.aliases.items()})
    return _SideJob(inputs=a.inputs + b.inputs, out_shapes=a.out_shapes + b.out_shapes, aliases=aliases,
                    scratch=a.scratch + b.scratch, start=start, finish=finish)


def _gather_level2_job(gathered):
    nw = len(gathered)

    def copies(outs, scratch, n):
        sends, recvs = scratch
        x, y, c, chips = _place()
        out = []
        for j, chip in enumerate(chips):
            sem = dict(send_sem=sends.at[3 * n + j], recv_sem=recvs.at[3 * n + j], device_id=(x, y, 1 - c), device_id_type=MESH)
            going = outs[n].at[_block_index(*chip, c)]
            landing = outs[n].at[_block_index(*chip, 1 - c)]
            out.append((pltpu.make_async_remote_copy(src_ref=going, dst_ref=going, **sem),
                        pltpu.make_async_remote_copy(src_ref=landing, dst_ref=landing, **sem)))
        return out

    def start(ins, outs, scratch):
        for n in range(nw):
            for to, _ in copies(outs, scratch, n):
                to.start()

    def finish(ins, outs, scratch):
        for n in range(nw):
            for to, frm in copies(outs, scratch, n):
                frm.wait_recv()
                to.wait_send()

    return _SideJob(
        inputs=tuple(gathered), out_shapes=tuple(jax.ShapeDtypeStruct(g.shape, g.dtype) for g in gathered),
        aliases={n: n for n in range(nw)},
        scratch=(pltpu.SemaphoreType.DMA((3 * nw,)), pltpu.SemaphoreType.DMA((3 * nw,))),
        start=start, finish=finish)


def _all_gather_rows(x, *, name):
    R, C = x.shape

    def body(x_ref, out_ref, send_sems, recv_sems, local_sem):
        _two_level_gather(x_ref, out_ref, send_sems, recv_sems, local_sem)

    return pl.pallas_call(
        body, name=name, out_shape=jax.ShapeDtypeStruct((8, R, C), x.dtype),
        in_specs=[pl.BlockSpec(memory_space=pltpu.VMEM)], out_specs=pl.BlockSpec(memory_space=pltpu.VMEM),
        scratch_shapes=list(_GATHER_SEMS),
        compiler_params=pltpu.CompilerParams(vmem_limit_bytes=_limit(10 * _nbytes((R, C), x.dtype))),
    )(x)


CAST_ROWS = 16
_FLIPS = [(fx, fy, fc) for fx in (0, 1) for fy in (0, 1) for fc in (0, 1)][1:]


def _flipped(v, bit):
    return 1 - v if bit else v


def _scatter_direct_job(pieces):
    nw = len(pieces)

    def copies(ins, outs, scratch, n):
        sends, recvs = scratch
        x, y, c, _ = _place()
        out = []
        for f, (fx, fy, fc) in enumerate(_FLIPS):
            peer = (_flipped(x, fx), _flipped(y, fy), _flipped(c, fc))
            sem = dict(send_sem=sends.at[7 * n + f], recv_sem=recvs.at[7 * n + f], device_id=peer, device_id_type=MESH)
            landing = outs[n].at[f]
            out.append((pltpu.make_async_remote_copy(src_ref=ins[n].at[_block_index(*peer)], dst_ref=landing, **sem),
                        pltpu.make_async_remote_copy(src_ref=landing, dst_ref=landing, **sem)))
        return out

    def start(ins, outs, scratch):
        for n in range(nw):
            for to, _ in copies(ins, outs, scratch, n):
                to.start()

    def finish(ins, outs, scratch):
        for n in range(nw):
            for to, frm in copies(ins, outs, scratch, n):
                frm.wait_recv()
                to.wait_send()

    return _SideJob(
        inputs=tuple(pieces), out_shapes=tuple(jax.ShapeDtypeStruct((7,) + p.shape[1:], p.dtype) for p in pieces), aliases={},
        scratch=(pltpu.SemaphoreType.DMA((7 * nw,)), pltpu.SemaphoreType.DMA((7 * nw,))),
        start=start, finish=finish)


def _scatter_finish(g4s, landed, *, name):
    nw = len(g4s)
    dims = [g.shape[1:] for g in g4s]

    def body(*refs):
        g_refs, l_refs, out_refs, own = refs[:nw], refs[nw:2 * nw], refs[2 * nw:3 * nw], refs[3 * nw:4 * nw]
        load_sems, send_sems, recv_sems = refs[4 * nw:]
        x, y, core, _ = _place()
        k = 2 * x + y
        loads = []
        for n, (r, c) in enumerate(dims):
            my0 = pl.multiple_of(core * (r // 2), CAST_ROWS)
            ld = pltpu.make_async_copy(g_refs[n].at[k, pl.ds(my0, r // 2), :], own[n], load_sems.at[n])
            ld.start()
            loads.append(ld)
        swaps = []
        for n, (r, c) in enumerate(dims):
            rh = r // 2
            my0 = pl.multiple_of(core * rh, CAST_ROWS)
            loads[n].wait()

            def fin(i, carry, n=n, my0=my0):
                r0 = pl.multiple_of(i * CAST_ROWS, CAST_ROWS)
                s = own[n][pl.ds(r0, CAST_ROWS), :]
                for f in range(7):
                    s = s + l_refs[n][f, pl.ds(r0, CAST_ROWS), :].astype(F32)
                out_refs[n][pl.ds(my0 + r0, CAST_ROWS), :] = s
                return carry

            lax.fori_loop(0, rh // CAST_ROWS, fin, 0)
            half = out_refs[n].at[pl.ds(my0, rh), :]
            sw = pltpu.make_async_remote_copy(src_ref=half, dst_ref=half, send_sem=send_sems.at[n], recv_sem=recv_sems.at[n],
                                              device_id=(x, y, 1 - core), device_id_type=MESH)
            sw.start()
            swaps.append(sw)
        for sw in swaps:
            sw.wait()

    need = sum(_nbytes((7, r // 2, c), BF16) + _nbytes((r // 2, c), F32) + _nbytes((r, c), F32) for r, c in dims)
    vmem = pl.BlockSpec(memory_space=pltpu.VMEM)
    return pl.pallas_call(
        body, name=name, out_shape=tuple(jax.ShapeDtypeStruct((r, c), F32) for r, c in dims),
        in_specs=[pl.BlockSpec(memory_space=pl.ANY)] * nw + [vmem] * nw, out_specs=(vmem,) * nw,
        scratch_shapes=[pltpu.VMEM((r // 2, c), F32) for r, c in dims]
        + [pltpu.SemaphoreType.DMA((nw,)), pltpu.SemaphoreType.DMA((nw,)), pltpu.SemaphoreType.DMA((nw,))],
        compiler_params=pltpu.CompilerParams(vmem_limit_bytes=_limit(need * 1.1)),
    )(*g4s, *landed)


def _gather_weight(w, *, name):
    r, c = w.shape
    rh = r // 2
    assert rh % CAST_ROWS == 0

    def body(w_hbm, out_ref, tmp, xb, send_sems, recv_sems, local_sem):
        core = lax.axis_index("c")
        ld = pltpu.make_async_copy(w_hbm.at[pl.ds(pl.multiple_of(core * rh, CAST_ROWS), rh), :], tmp, local_sem)
        ld.start()
        ld.wait()

        def cast(i, carry):
            r0 = pl.multiple_of(i * CAST_ROWS, CAST_ROWS)
            xb[pl.ds(r0, CAST_ROWS), :] = tmp[pl.ds(r0, CAST_ROWS), :].astype(BF16)
            return carry

        lax.fori_loop(0, rh // CAST_ROWS, cast, 0)
        _two_level_gather(xb, out_ref, send_sems, recv_sems, local_sem)

    need = _nbytes((8, rh, c), BF16) + _nbytes((rh, c), F32) + _nbytes((rh, c), BF16)
    out = pl.pallas_call(
        body, name=name, out_shape=jax.ShapeDtypeStruct((8, rh, c), BF16),
        in_specs=[pl.BlockSpec(memory_space=pl.ANY)], out_specs=pl.BlockSpec(memory_space=pltpu.VMEM),
        scratch_shapes=[pltpu.VMEM((rh, c), F32), pltpu.VMEM((rh, c), BF16)] + list(_GATHER_SEMS),
        compiler_params=pltpu.CompilerParams(vmem_limit_bytes=_limit(need * 1.3)),
    )(w)
    return out.reshape(4, r, c)


def _reduce_scatter_weight(g4, *, name):
    _, r, c = g4.shape
    rh = r // 2
    assert rh % CAST_ROWS == 0
    nsteps = rh // CAST_ROWS

    def body(g_hbm, out_ref, mine, tmp, sbuf, rbuf_a, rbuf_b, a_send, a_recv, b_send, b_recv, c_send, c_recv, lsem):
        x, y, core, chips = _place()
        sibling = (x, y, 1 - core)
        k = 2 * x + y
        my0 = pl.multiple_of(core * rh, CAST_ROWS)
        ot0 = pl.multiple_of((1 - core) * rh, CAST_ROWS)

        ld = pltpu.make_async_copy(g_hbm.at[:, pl.ds(my0, rh), :], mine, lsem)
        ld.start()
        ld.wait()
        for j in range(4):
            ldj = pltpu.make_async_copy(g_hbm.at[j, pl.ds(ot0, rh), :], tmp, lsem)
            ldj.start()
            ldj.wait()

            def cast(i, carry, j=j):
                r0 = pl.multiple_of(i * CAST_ROWS, CAST_ROWS)
                sbuf[j, pl.ds(r0, CAST_ROWS), :] = tmp[pl.ds(r0, CAST_ROWS), :].astype(BF16)
                return carry

            lax.fori_loop(0, nsteps, cast, 0)

        to_sib = pltpu.make_async_remote_copy(src_ref=sbuf, dst_ref=rbuf_a, send_sem=a_send, recv_sem=a_recv,
                                              device_id=sibling, device_id_type=MESH)
        to_sib.start()
        to_sib.wait()

        for j in range(4):
            def add(i, carry, j=j):
                r0 = pl.multiple_of(i * CAST_ROWS, CAST_ROWS)
                s = mine[j, pl.ds(r0, CAST_ROWS), :] + rbuf_a[j, pl.ds(r0, CAST_ROWS), :].astype(F32)
                mine[j, pl.ds(r0, CAST_ROWS), :] = s
                sbuf[j, pl.ds(r0, CAST_ROWS), :] = s.astype(BF16)
                return carry

            lax.fori_loop(0, nsteps, add, 0)

        sends = []
        for d, (px, py) in enumerate(chips):
            cp = pltpu.make_async_remote_copy(src_ref=sbuf.at[2 * px + py], dst_ref=rbuf_b.at[d], send_sem=b_send.at[d],
                                              recv_sem=b_recv.at[d], device_id=(px, py, core), device_id_type=MESH)
            cp.start()
            sends.append(cp)
        for cp in sends:
            cp.wait()

        def fin(i, carry):
            r0 = pl.multiple_of(i * CAST_ROWS, CAST_ROWS)
            s = mine[k, pl.ds(r0, CAST_ROWS), :]
            for d in range(3):
                s = s + rbuf_b[d, pl.ds(r0, CAST_ROWS), :].astype(F32)
            out_ref[pl.ds(my0 + r0, CAST_ROWS), :] = s
            return carry

        lax.fori_loop(0, nsteps, fin, 0)
        half = out_ref.at[pl.ds(my0, rh), :]
        swap = pltpu.make_async_remote_copy(src_ref=half, dst_ref=half, send_sem=c_send, recv_sem=c_recv,
                                            device_id=sibling, device_id_type=MESH)
        swap.start()
        swap.wait()

    need = (_nbytes((4, rh, c), F32) + _nbytes((rh, c), F32) + 2 * _nbytes((4, rh, c), BF16) + _nbytes((3, rh, c), BF16)
            + _nbytes((r, c), F32))
    return pl.pallas_call(
        body, name=name, out_shape=jax.ShapeDtypeStruct((r, c), F32),
        in_specs=[pl.BlockSpec(memory_space=pl.ANY)], out_specs=pl.BlockSpec(memory_space=pltpu.VMEM),
        scratch_shapes=[pltpu.VMEM((4, rh, c), F32), pltpu.VMEM((rh, c), F32), pltpu.VMEM((4, rh, c), BF16),
                        pltpu.VMEM((4, rh, c), BF16), pltpu.VMEM((3, rh, c), BF16),
                        pltpu.SemaphoreType.DMA, pltpu.SemaphoreType.DMA, pltpu.SemaphoreType.DMA((3,)),
                        pltpu.SemaphoreType.DMA((3,)), pltpu.SemaphoreType.DMA, pltpu.SemaphoreType.DMA,
                        pltpu.SemaphoreType.DMA],
        compiler_params=pltpu.CompilerParams(vmem_limit_bytes=_limit(need * 1.2)),
    )(g4)


def _cols_from_shards(g):
    n, K, c = g.shape
    return g.transpose(1, 0, 2).reshape(K, n * c)


def _cols_to_shards(w):
    K, N = w.shape
    return w.reshape(K, 4, N // 4).transpose(1, 0, 2)


def _pad_heads_cols(w, width, lane0=0):
    K = w.shape[0]
    w3 = w.reshape(K, H, width)
    return jnp.pad(w3, ((0, 0), (0, 0), (lane0, HP - lane0 - width))).reshape(K, H * HP)


def _unpad_heads_cols(w, width, lane0=0):
    K = w.shape[0]
    return w.reshape(K, H, HP)[:, :, lane0:lane0 + width].reshape(K, H * width)


def _pad_block(w, lane0=0):
    return jnp.pad(w, ((0, 0), (lane0, LANES - lane0 - w.shape[1])))


_IN_SPLITS = [512, 1024, 1536, 1544, 2312, 2568, 2600, 3624]


def _pad_w_in(w):
    fq, fk, fv, flog, cq, ckv, krin, gfox, gmla = jnp.split(w, _IN_SPLITS, axis=1)
    return jnp.concatenate([_pad_heads_cols(fq, FOX_HD), _pad_heads_cols(fk, FOX_HD), _pad_heads_cols(fv, FOX_HD),
                            cq, ckv, gfox, gmla, _pad_block(flog), _pad_block(krin, KRIN_LANE)], axis=1)


def _unpad_w_in(wp):
    qkv, rest = wp[:, :NQKV], wp[:, NQKV:]
    fq, fk, fv = (_unpad_heads_cols(qkv[:, i * H * HP:(i + 1) * H * HP], FOX_HD) for i in range(3))
    return jnp.concatenate([fq, fk, fv, rest[:, OFF_FLOG:OFF_FLOG + H], rest[:, OFF_CQ:OFF_CQ + Q_LORA],
                            rest[:, OFF_CKV:OFF_CKV + KV_LORA], rest[:, OFF_KRIN + KRIN_LANE:OFF_KRIN + KRIN_LANE + MLA_ROPE],
                            rest[:, OFF_GFOX:OFF_GFOX + D], rest[:, OFF_GMLA:OFF_GMLA + D]], axis=1)


def _pad_w_ukv(w):
    w3 = w.reshape(KV_LORA, H, MLA_NOPE + MLA_V)
    kp = jnp.pad(w3[:, :, :MLA_NOPE], ((0, 0), (0, 0), (0, HP - MLA_NOPE))).reshape(KV_LORA, H * HP)
    vp = jnp.pad(w3[:, :, MLA_NOPE:], ((0, 0), (0, 0), (0, HP - MLA_V))).reshape(KV_LORA, H * HP)
    return jnp.concatenate([kp, vp], axis=1)


def _unpad_w_ukv(wp):
    kp = wp[:, :H * HP].reshape(KV_LORA, H, HP)[:, :, :MLA_NOPE]
    vp = wp[:, H * HP:].reshape(KV_LORA, H, HP)[:, :, :MLA_V]
    return jnp.concatenate([kp, vp], axis=2).reshape(KV_LORA, H * (MLA_NOPE + MLA_V))


def _pad_heads_rows(w, width):
    N = w.shape[1]
    return jnp.pad(w.reshape(H, width, N), ((0, 0), (0, HP - width), (0, 0))).reshape(H * HP, N)


def _unpad_heads_rows(w, width):
    N = w.shape[1]
    return w.reshape(H, HP, N)[:, :width, :].reshape(H * width, N)


def _rope_tables(positions):
    inv_freq = 1.0 / (ROPE_THETA ** (jnp.arange(0, MLA_ROPE, 2, dtype=F32) / MLA_ROPE))
    ang = positions.reshape(S, 1).astype(F32) * inv_freq
    cos, sin = jnp.cos(ang), jnp.sin(ang)
    ones = jnp.ones((S, KRIN_LANE), F32)
    tail = jnp.zeros((S, LANES - KRIN_LANE - MLA_ROPE), F32)
    ctab = jnp.concatenate([ones, cos, cos, tail], axis=1)
    stab = jnp.concatenate([0.0 * ones, -sin, sin, tail], axis=1)
    return ctab, stab


def _local_step(x, target, mod, positions, gains, bf, W, late=None):
    W = dict(W)
    sh1, sc1, gt1, sh2, sc2, gt2 = (mod[:, i * D:(i + 1) * D] for i in range(6))
    ops1, ops2 = 1.0 + sc1, 1.0 + sc2
    ones = lambda w: jnp.ones((1, w), F32)
    zeros = lambda w: jnp.zeros((1, w), F32)
    bf_blk = _pad_block(bf)
    ctab, stab = _rope_tables(positions)
    fox_scale = 1.0 / math.sqrt(FOX_HD)
    mla_scale = 1.0 / math.sqrt(MLA_NOPE + MLA_ROPE)

    h1 = _norm_mod(x, 0, D, gains["g_pre_mix"], ops1, sh1, name="f_pre_mix")
    qkv = _matmul(h1, W["w_in_qkv"], out_dtype=_MXU, name="f_proj_qkv", tm_cap=2048)
    rest = _matmul(h1, W["w_in_rest"], name="f_proj_rest", tm_cap=2048, tn_cap=256)
    F = _fox_prep(rest, bf_blk, name="f_fox_prep")
    Ft = F[:, :H].T
    fcol, frow = Ft.reshape(H, S, 1), Ft.reshape(H, S // TKF, 1, TKF)
    fox_heads = (0, H, 2 * H)
    job = _gather_level1_job([late[n] for n in _GATHER_A]) if late else None
    oa, lse_a, *landed = _attn_fwd(qkv, qkv, qkv, frow, scale=fox_scale, name="f_attn_fox", side=job, heads0=fox_heads)

    cqn = _norm_mod(rest, OFF_CQ // Q_LORA, Q_LORA, gains["g_q_lora"], ones(Q_LORA), zeros(Q_LORA), name="f_norm_cq")
    ckvn = _norm_mod(rest, OFF_CKV // KV_LORA, KV_LORA, gains["g_kv_lora"], ones(KV_LORA), zeros(KV_LORA), name="f_norm_ckv")
    qb = _matmul(cqn, W["w_uq"], name="f_uq")
    kvb = _matmul(ckvn, W["w_ukv"], name="f_ukv")
    qm, km, vm = _mla_assemble(qb, kvb, rest, ctab, stab, name="f_mla_assemble")
    job = _join_jobs(_gather_level2_job(landed), _gather_direct_job([late[n] for n in _GATHER_B])) if late else None
    ob, lse_b, *landed = _attn_fwd(qm, km, vm, None, scale=mla_scale, name="f_attn_mla", side=job)
    if late:
        W.update(_late_weights({n: g.reshape(4, 2 * g.shape[1], g.shape[2]) for n, g in zip(_GATHER_A + _GATHER_B, landed)}))

    pa = _matmul(oa, W["w_proj_fox"], name="f_proj_fox")
    pb = _matmul(ob, W["w_proj_mla"], name="f_proj_mla")
    merged = _merge(rest, pa, pb, name="f_merge")
    y1 = _matmul(merged, W["w_out"], name="f_out")
    x2 = _post_res(x, y1, gains["g_post_mix"], gt1, name="f_post_mix")
    h2 = _norm_mod(x2, 0, D, gains["g_pre_ffn"], ops2, sh2, name="f_pre_ffn")
    gu = _matmul(h2, W["w_ffn_in"], name="f_ffn_in", b_shards=True, tn_cap=1408)
    act = _swiglu(gu, name="f_swiglu")
    y2 = _matmul(act, W["w_ffn_out"], name="f_ffn_out", tk_cap=1408, tn_cap=1024)
    dout, loss = _post_res_loss(x2, y2, gains["g_post_ffn"], gt2, target, name="f_post_ffn_loss")

    dy2, s_gt2, s_gpost2 = _post_res_bwd(dout, y2, gains["g_post_ffn"], gt2, name="b_post_ffn")
    dact = _matmul(dy2, W["w_ffn_out"], tb=True, name="b_ffn_out_dx", tn_cap=1408)
    dW_ffn_out = _matmul(act, dy2, ta=True, name="b_ffn_out_dw", tm_cap=1408, tk_cap=2048)
    dgu = _swiglu_bwd(gu, dact, name="b_swiglu")
    dh2 = _matmul(dgu, W["w_ffn_in"], tb=True, b_shards=True, name="b_ffn_in_dx", tk_cap=1408, tn_cap=1024)
    dW_ffn_in = _matmul(h2, dgu, ta=True, name="b_ffn_in_dw", out_shards=True, tn_cap=1408, tk_cap=2048)
    dx2, s_sh2, s_a2 = _norm_mod_bwd(x2, 0, D, dh2, gains["g_pre_ffn"], ops2, dout, name="b_pre_ffn")
    dy1, s_gt1, s_gpost1 = _post_res_bwd(dx2, y1, gains["g_post_mix"], gt1, name="b_post_mix")
    dmerged = _matmul(dy1, W["w_out"], tb=True, name="b_out_dx")
    dW_out = _matmul(merged, dy1, ta=True, name="b_out_dw", tk_cap=2048)
    dpa, dpb, dgfox, dgmla = _merge_bwd(rest, pa, pb, dmerged, name="b_merge")
    doa = _matmul(dpa, W["w_proj_fox"], tb=True, name="b_proj_fox_dx")
    dW_proj_fox = _matmul(oa, dpa, ta=True, name="b_proj_fox_dw")
    dob = _matmul(dpb, W["w_proj_mla"], tb=True, name="b_proj_mla_dx")
    dW_proj_mla = _matmul(ob, dpb, ta=True, name="b_proj_mla_dw")

    delta_a, doa16 = _attn_delta(doa, oa, name="b_delta_fox")
    as_rows = lambda a: a.reshape(H, S // TQB, 1, TQB)
    dW = dict(w_proj_fox=dW_proj_fox, w_proj_mla=dW_proj_mla, w_out=dW_out, w_ffn_in=dW_ffn_in, w_ffn_out=dW_ffn_out)
    job_a = job_b = None
    if late:
        late_shards = _grad_shards(dW)
        pieces = {n: s.reshape(8, s.shape[1] // 2, s.shape[2]).astype(BF16) for n, s in late_shards.items()}
        job_a = _scatter_direct_job([pieces[n] for n in _SCATTER_A])
        job_b = _scatter_direct_job([pieces[n] for n in _SCATTER_B])
    dqa, dka, dva, *landed_a = _attn_bwd(qkv, qkv, qkv, doa16, as_rows(lse_a), as_rows(delta_a), fcol, scale=fox_scale,
                                         name="b_attn_fox", side=job_a, heads0=fox_heads)
    delta_b, dob16 = _attn_delta(dob, ob, name="b_delta_mla")
    dqm, dkm, dvm, *landed_b = _attn_bwd(qm, km, vm, dob16, as_rows(lse_b), as_rows(delta_b), None, scale=mla_scale,
                                         name="b_attn_mla", side=job_b)
    reduced = {}
    if late:
        order = _SCATTER_A + _SCATTER_B
        done = _scatter_finish([late_shards[n] for n in order], landed_a + landed_b, name="scatter_late")
        reduced = dict(zip(order, done))
        dW = {}

    dF = (dqa[:, ROW_SUM_LANE::HP] - dka[:, COL_SUM_LANE::HP]) * (1.0 / fox_scale)
    dflog, s_bf = _fox_bwd_prep(rest, bf_blk, _pad_block(dF), name="b_fox_prep")

    dqb, dkvb, dkrin = _mla_assemble_bwd(dqm, dkm, dvm, ctab, stab, name="b_mla_assemble")
    dcqn = _matmul(dqb, W["w_uq"], tb=True, name="b_uq_dx")
    dW_uq = _matmul(cqn, dqb, ta=True, name="b_uq_dw", tk_cap=2048)
    dckvn = _matmul(dkvb, W["w_ukv"], tb=True, name="b_ukv_dx")
    dW_ukv = _matmul(ckvn, dkvb, ta=True, name="b_ukv_dw", tk_cap=2048)
    dcq, _, s_gq = _norm_mod_bwd(rest, OFF_CQ // Q_LORA, Q_LORA, dcqn, gains["g_q_lora"], ones(Q_LORA), None, name="b_norm_cq")
    dckv, _, s_gkv = _norm_mod_bwd(rest, OFF_CKV // KV_LORA, KV_LORA, dckvn, gains["g_kv_lora"], ones(KV_LORA), None, name="b_norm_ckv")

    c16 = lambda a: a.astype(_MXU)
    dproj = jnp.concatenate([c16(dqa), c16(dka), c16(dva), c16(dcq), c16(dckv), dgfox, dgmla, c16(dflog), c16(dkrin)], axis=1)
    w_in_full = jnp.concatenate([W["w_in_qkv"], W["w_in_rest"]], axis=1)
    dh1 = _matmul(dproj, w_in_full, tb=True, name="b_in_dx", tk_cap=1280, tn_cap=1024)
    dW_in = _matmul(h1, dproj, ta=True, name="b_in_dw", tn_cap=640, tk_cap=2048)
    grad_x, s_sh1, s_a1 = _norm_mod_bwd(x, 0, D, dh1, gains["g_pre_mix"], ops1, dx2, name="b_pre_mix")

    dmod = jnp.concatenate([s_sh1, s_a1 * gains["g_pre_mix"], s_gt1, s_sh2, s_a2 * gains["g_pre_ffn"], s_gt2], axis=1)
    small = dict(dmod=dmod, g_pre_mix=s_a1 * ops1, g_post_mix=s_gpost1, g_pre_ffn=s_a2 * ops2, g_post_ffn=s_gpost2,
                 g_q_lora=s_gq, g_kv_lora=s_gkv, b_forget=s_bf)
    dW = dict(dW, w_in=dW_in, w_uq=dW_uq, w_ukv=dW_ukv)
    return loss, grad_x, dW, reduced, small


_BIG = ["w_in", "w_uq", "w_ukv", "w_proj_fox", "w_proj_mla", "w_out", "w_ffn_in", "w_ffn_out"]
_COL_SHARDED = {"w_in", "w_ukv", "w_proj_fox", "w_proj_mla", "w_ffn_in"}
_SMALL = ["b_ada", "g_pre_mix", "g_post_mix", "g_pre_ffn", "g_post_ffn", "b_forget", "g_q_lora", "g_kv_lora"]
_ORDER = ["w_ada", "b_ada", "g_pre_mix", "g_post_mix", "g_pre_ffn", "g_post_ffn", "w_in", "b_forget", "g_q_lora", "w_uq",
          "g_kv_lora", "w_ukv", "w_proj_fox", "w_proj_mla", "w_out", "w_ffn_in", "w_ffn_out"]
_ROW = {}
_off = 0
for _n, _w in [("dmod", 6 * D), ("g_pre_mix", D), ("g_post_mix", D), ("g_pre_ffn", D), ("g_post_ffn", D), ("g_q_lora", Q_LORA),
               ("g_kv_lora", KV_LORA), ("b_forget", LANES), ("loss", LANES)]:
    _ROW[_n] = (_off, _w)
    _off += _w
_ROW_LEN = _off


_EARLY = ["w_in", "w_uq", "w_ukv"]
_LATE = ["w_proj_fox", "w_proj_mla", "w_out", "w_ffn_in", "w_ffn_out"]
_GATHER_A = ["w_proj_fox", "w_proj_mla", "w_out", "w_ffn_in"]
_GATHER_B = ["w_ffn_out"]
_SCATTER_A = ["w_ffn_in"]
_SCATTER_B = ["w_ffn_out", "w_out", "w_proj_fox", "w_proj_mla"]


def _rows_from_shards(g):
    return g.reshape(-1, g.shape[2])


def _early_weights(G):
    w_in = _pad_w_in(_cols_from_shards(G["w_in"]))
    return dict(
        w_in_qkv=w_in[:, :NQKV], w_in_rest=w_in[:, NQKV:],
        w_uq=_pad_heads_cols(_rows_from_shards(G["w_uq"]), MLA_NOPE + MLA_ROPE),
        w_ukv=_pad_w_ukv(_cols_from_shards(G["w_ukv"])))


def _late_weights(G):
    return dict(
        w_proj_fox=_pad_heads_rows(_cols_from_shards(G["w_proj_fox"]), FOX_HD),
        w_proj_mla=_pad_heads_rows(_cols_from_shards(G["w_proj_mla"]), MLA_V),
        w_out=_rows_from_shards(G["w_out"]), w_ffn_in=G["w_ffn_in"], w_ffn_out=_rows_from_shards(G["w_ffn_out"]))


def _full_weights(G):
    return {**_early_weights(G), **_late_weights(G)}


_UNPAD = dict(
    w_in=_unpad_w_in, w_uq=lambda g: _unpad_heads_cols(g, MLA_NOPE + MLA_ROPE), w_ukv=_unpad_w_ukv,
    w_proj_fox=lambda g: _unpad_heads_rows(g, FOX_HD), w_proj_mla=lambda g: _unpad_heads_rows(g, MLA_V),
    w_out=lambda g: g, w_ffn_out=lambda g: g)


def _grad_shards(dW):
    out = {}
    for n, g in dW.items():
        if n == "w_ffn_in":
            out[n] = g
        else:
            nat = _UNPAD[n](g)
            out[n] = _cols_to_shards(nat) if n in _COL_SHARDED else nat.reshape(4, nat.shape[0] // 4, nat.shape[1])
    return out


def kernel(x, c, positions, w_ada, b_ada, g_pre_mix, g_post_mix, g_pre_ffn, g_post_ffn, w_in, b_forget, g_q_lora, w_uq, g_kv_lora, w_ukv, w_proj_fox, w_proj_mla, w_out, w_ffn_in, w_ffn_out, loss_target, m_w_ada, m_b_ada, m_g_pre_mix, m_g_post_mix, m_g_pre_ffn, m_g_post_ffn, m_w_in, m_b_forget, m_g_q_lora, m_w_uq, m_g_kv_lora, m_w_ukv, m_w_proj_fox, m_w_proj_mla, m_w_out, m_w_ffn_in, m_w_ffn_out, v_w_ada, v_b_ada, v_g_pre_mix, v_g_post_mix, v_g_pre_ffn, v_g_post_ffn, v_w_in, v_b_forget, v_g_q_lora, v_w_uq, v_g_kv_lora, v_w_ukv, v_w_proj_fox, v_w_proj_mla, v_w_out, v_w_ffn_in, v_w_ffn_out):
    P = dict(w_ada=w_ada, b_ada=b_ada, g_pre_mix=g_pre_mix, g_post_mix=g_post_mix, g_pre_ffn=g_pre_ffn, g_post_ffn=g_post_ffn,
             w_in=w_in, b_forget=b_forget, g_q_lora=g_q_lora, w_uq=w_uq, g_kv_lora=g_kv_lora, w_ukv=w_ukv,
             w_proj_fox=w_proj_fox, w_proj_mla=w_proj_mla, w_out=w_out, w_ffn_in=w_ffn_in, w_ffn_out=w_ffn_out)
    M = dict(w_ada=m_w_ada, b_ada=m_b_ada, g_pre_mix=m_g_pre_mix, g_post_mix=m_g_post_mix, g_pre_ffn=m_g_pre_ffn,
             g_post_ffn=m_g_post_ffn, w_in=m_w_in, b_forget=m_b_forget, g_q_lora=m_g_q_lora, w_uq=m_w_uq, g_kv_lora=m_g_kv_lora,
             w_ukv=m_w_ukv, w_proj_fox=m_w_proj_fox, w_proj_mla=m_w_proj_mla, w_out=m_w_out, w_ffn_in=m_w_ffn_in,
             w_ffn_out=m_w_ffn_out)
    V = dict(w_ada=v_w_ada, b_ada=v_b_ada, g_pre_mix=v_g_pre_mix, g_post_mix=v_g_post_mix, g_pre_ffn=v_g_pre_ffn,
             g_post_ffn=v_g_post_ffn, w_in=v_w_in, b_forget=v_b_forget, g_q_lora=v_g_q_lora, w_uq=v_w_uq, g_kv_lora=v_g_kv_lora,
             w_ukv=v_w_ukv, w_proj_fox=v_w_proj_fox, w_proj_mla=v_w_proj_mla, w_out=v_w_out, w_ffn_in=v_w_ffn_in,
             w_ffn_out=v_w_ffn_out)
    ax, ay, ac = lax.axis_index("x"), lax.axis_index("y"), lax.axis_index("c")
    chip = 2 * ax + ay
    me = 4 * ax + 2 * ay + ac
    n_ada = w_ada.shape[2]

    c_all = _all_gather_rows(jnp.pad(c, ((0, 7), (0, 0))), name="gather_c")[:, 0, :]
    c_all = jnp.pad(c_all, ((0, 8), (0, 0)))
    b_shard = lax.dynamic_slice(b_ada, (0, chip * n_ada), (1, n_ada))
    mod_blk, silu_c = _ada_mod(c_all, w_ada[0], b_shard, name="ada_mod")
    mod_all = _all_gather_rows(mod_blk, name="gather_mod")
    mod_mine = lax.dynamic_index_in_dim(mod_all, me, axis=1, keepdims=False)
    mod = lax.dynamic_index_in_dim(mod_mine.reshape(4, 2, n_ada), ac, axis=1, keepdims=False).reshape(1, 6 * D)

    W = _early_weights({n: _gather_weight(P[n][0], name="gather_" + n) for n in _EARLY})
    late = {}
    for n in _LATE:
        rh = P[n].shape[1] // 2
        late[n] = lax.dynamic_slice_in_dim(P[n][0], ac * rh, rh, axis=0).astype(BF16)

    gains = {n: P[n] for n in ["g_pre_mix", "g_post_mix", "g_pre_ffn", "g_post_ffn", "g_q_lora", "g_kv_lora"]}
    loss, grad_x, dW, grads, small = _local_step(x[0], loss_target[0], mod, positions, gains, b_forget, W, late)

    shards = _grad_shards(dW)
    grads.update({n: _reduce_scatter_weight(shards[n], name="scatter_" + n) for n in shards})

    small = dict(small, loss=_pad_block(loss))
    row = jnp.concatenate([small[n] for n in _ROW], axis=1)
    rows = _all_gather_rows(jnp.pad(row, ((0, 7), (0, 0))), name="gather_small")[:, 0, :]
    tot = _rowsum(rows, name="sum_small")
    piece = lambda n: tot[:, _ROW[n][0]:_ROW[n][0] + _ROW[n][1]]
    grads["b_ada"] = piece("dmod")
    for n in ["g_pre_mix", "g_post_mix", "g_pre_ffn", "g_post_ffn", "g_q_lora", "g_kv_lora"]:
        grads[n] = piece(n)
    grads["b_forget"] = piece("b_forget")[:, :H]
    loss_out = piece("loss")[0, 0]
    dmod_all = rows[:, _ROW["dmod"][0]:_ROW["dmod"][0] + 6 * D]
    dmod_shard = jnp.pad(lax.dynamic_slice(dmod_all, (0, chip * n_ada), (8, n_ada)), ((0, 8), (0, 0)))
    grads["w_ada"] = _matmul(silu_c, dmod_shard, ta=True, name="ada_dw")

    delta, new_m, new_v = {}, {}, {}
    for n in ["w_ada"] + _BIG:
        delta[n], new_m[n], new_v[n] = _adamw(P[n], grads[n], M[n], V[n], name="adamw_" + n)
    cat = lambda T: jnp.concatenate([T[n] for n in _SMALL], axis=1)
    d_s, m_s, v_s = (t[0] for t in _adamw(cat(P)[None], cat(grads), cat(M)[None], cat(V)[None], name="adamw_small"))
    o = 0
    for n in _SMALL:
        wdt = P[n].shape[1]
        delta[n], new_m[n], new_v[n] = d_s[:, o:o + wdt], m_s[:, o:o + wdt], v_s[:, o:o + wdt]
        o += wdt

    def shaped(T, n):
        return T[n].reshape(P[n].shape)

    return (loss_out, grad_x[None], *[shaped(grads, n) for n in _ORDER], *[shaped(delta, n) for n in _ORDER],
            *[shaped(new_m, n) for n in _ORDER], *[shaped(new_v, n) for n in _ORDER])
```

```python
import functools
import math
from typing import Callable, NamedTuple

import jax
import jax.numpy as jnp
from jax import lax
from jax.experimental import pallas as pl
from jax.experimental.pallas import tpu as pltpu

F32 = jnp.float32
BF16 = jnp.bfloat16
_MXU = jnp.bfloat16

S = 2048
D = 1024
H = 8
HP = 128
FOX_HD = 64
MLA_NOPE = 64
MLA_ROPE = 32
MLA_V = 64
Q_LORA = 768
KV_LORA = 256
D_FF = 2816
NORM_EPS = 1e-6
ROPE_THETA = 10000.0
NEG = -1e30

ADAM_LR = 0.001
ADAM_B1 = 0.9
ADAM_B2 = 0.999
ADAM_EPS = 1e-08
ADAM_WD = 0.01
ADAM_STEP = 10

LANES = 128
VMEM_CAP = 60 * 1024 * 1024
MESH = pl.DeviceIdType.MESH

NQKV = 3 * H * HP
OFF_CQ = 0
OFF_CKV = Q_LORA
OFF_GFOX = 1024
OFF_GMLA = 2048
OFF_FLOG = 3072
OFF_KRIN = 3200
NREST = 3328
KRIN_LANE = 64
ROW_SUM_LANE = 64
COL_SUM_LANE = 65


def _limit(nbytes):
    return int(min(VMEM_CAP, nbytes * 1.25 + (4 << 20)))


def _nbytes(shape, dtype):
    n = 1
    for s in shape:
        n *= s
    return n * jnp.dtype(dtype).itemsize


def _pick(n, cap):
    best = None
    for t in range(LANES, min(n, cap) + 1, LANES):
        if n % t == 0:
            best = t
    return best if best is not None else n


def _pcall(body, *, out_shape, **kw):
    outs = jax.tree.map(lambda s: pltpu.HBM(s.shape, s.dtype), out_shape)
    call = pl.pallas_call(body, out_shape=outs, **kw)
    return lambda *args: call(*[pltpu.with_memory_space_constraint(a, pltpu.HBM) for a in args])


def _matmul(a, b, *, ta=False, tb=False, out_dtype=F32, name, tm_cap=1024, tn_cap=512, tk_cap=1024,
            b_shards=False, out_shards=False):
    if ta:
        K, M = a.shape
    else:
        M, K = a.shape
    if b_shards:
        _, R, cb = b.shape
        N, K2 = (R, 4 * cb) if tb else (4 * cb, R)
    elif tb:
        N, K2 = b.shape
    else:
        K2, N = b.shape
    assert K == K2, (a.shape, b.shape, ta, tb)
    tm = _pick(M, tm_cap)
    tn = _pick(N, tn_cap)
    tk = K if K <= tk_cap else _pick(K, tk_cap)
    nk = K // tk
    dims = (((0 if ta else 1,), (1 if tb else 0,)), ((), ()))

    def body(a_ref, b_ref, o_ref, acc_ref):
        k = pl.program_id(2)

        @pl.when(k == 0)
        def _():
            acc_ref[...] = jnp.zeros_like(acc_ref)

        acc_ref[...] += lax.dot_general(a_ref[...].astype(_MXU), b_ref[...].astype(_MXU), dims,
                                        preferred_element_type=F32)

        @pl.when(k == nk - 1)
        def _():
            o_ref[...] = acc_ref[...].astype(out_dtype)

    a_spec = pl.BlockSpec((tk, tm), lambda i, j, k: (k, i)) if ta else pl.BlockSpec((tm, tk), lambda i, j, k: (i, k))
    if b_shards and tb:
        assert cb % tk == 0
        per = cb // tk
        b_spec = pl.BlockSpec((None, tn, tk), lambda i, j, k: (k // per, j, k % per))
    elif b_shards:
        assert cb % tn == 0
        per = cb // tn
        b_spec = pl.BlockSpec((None, tk, tn), lambda i, j, k: (j // per, k, j % per))
    elif tb:
        b_spec = pl.BlockSpec((tn, tk), lambda i, j, k: (j, k))
    else:
        b_spec = pl.BlockSpec((tk, tn), lambda i, j, k: (k, j))
    if out_shards:
        assert (N // 4) % tn == 0
        pern = N // 4 // tn
        out_shape = jax.ShapeDtypeStruct((4, M, N // 4), out_dtype)
        out_spec = pl.BlockSpec((None, tm, tn), lambda i, j, k: (j // pern, i, j % pern))
    else:
        out_shape = jax.ShapeDtypeStruct((M, N), out_dtype)
        out_spec = pl.BlockSpec((tm, tn), lambda i, j, k: (i, j))
    need = (2 * _nbytes((tm, tk), a.dtype) + 2 * _nbytes((tk, tn), b.dtype) + 2 * _nbytes((tm, tn), out_dtype)
            + _nbytes((tm, tn), F32) * 2 + _nbytes((tm, tk), _MXU) + _nbytes((tk, tn), _MXU))
    return _pcall(
        body, name=name,
        out_shape=out_shape,
        grid=(M // tm, N // tn, nk),
        in_specs=[a_spec, b_spec],
        out_specs=out_spec,
        scratch_shapes=[pltpu.VMEM((tm, tn), F32)],
        compiler_params=pltpu.CompilerParams(dimension_semantics=("parallel", "parallel", "arbitrary"),
                                             vmem_limit_bytes=_limit(need)),
    )(a, b)


TM = 512
TM_FF = 256


def _vec(w):
    return pl.BlockSpec((1, w), lambda i: (0, 0))


def _rows(w, col=0, tm=TM):
    return pl.BlockSpec((tm, w), lambda i: (i, col))


def _row_params(need, carried=False):
    return pltpu.CompilerParams(dimension_semantics=("arbitrary" if carried else "parallel",),
                                vmem_limit_bytes=_limit(need))


def _norm_mod(x, col, w, g, ops, sh, *, name):
    def body(x_ref, g_ref, ops_ref, sh_ref, o_ref):
        xv = x_ref[...]
        r = lax.rsqrt(jnp.mean(xv * xv, axis=-1, keepdims=True) + NORM_EPS)
        o_ref[...] = (((xv * r) * g_ref[...]) * ops_ref[...] + sh_ref[...]).astype(o_ref.dtype)

    return _pcall(
        body, name=name, out_shape=jax.ShapeDtypeStruct((S, w), _MXU), grid=(S // TM,),
        in_specs=[_rows(w, col), _vec(w), _vec(w), _vec(w)], out_specs=_rows(w),
        compiler_params=_row_params(8 * _nbytes((TM, w), F32)),
    )(x, g, ops, sh)


def _norm_mod_bwd(x, col, w, dh, g, ops, dres, *, name):
    has_res = dres is not None

    def body(*refs):
        if has_res:
            x_ref, dh_ref, g_ref, ops_ref, dres_ref, dx_ref, s1_ref, s2_ref = refs
        else:
            x_ref, dh_ref, g_ref, ops_ref, dx_ref, s1_ref, s2_ref = refs
        i = pl.program_id(0)

        @pl.when(i == 0)
        def _():
            s1_ref[...] = jnp.zeros_like(s1_ref)
            s2_ref[...] = jnp.zeros_like(s2_ref)

        xv = x_ref[...]
        dhv = dh_ref[...]
        r = lax.rsqrt(jnp.mean(xv * xv, axis=-1, keepdims=True) + NORM_EPS)
        xn = xv * r
        dxn = dhv * (g_ref[...] * ops_ref[...])
        dx = r * (dxn - xn * jnp.mean(dxn * xn, axis=-1, keepdims=True))
        if has_res:
            dx = dx + dres_ref[...]
        dx_ref[...] = dx
        s1_ref[...] += jnp.sum(dhv, axis=0, keepdims=True)
        s2_ref[...] += jnp.sum(dhv * xn, axis=0, keepdims=True)

    in_specs = [_rows(w, col), _rows(w), _vec(w), _vec(w)] + ([_rows(w)] if has_res else [])
    args = (x, dh, g, ops) + ((dres,) if has_res else ())
    return _pcall(
        body, name=name,
        out_shape=(jax.ShapeDtypeStruct((S, w), F32), jax.ShapeDtypeStruct((1, w), F32), jax.ShapeDtypeStruct((1, w), F32)),
        grid=(S // TM,), in_specs=in_specs, out_specs=(_rows(w), _vec(w), _vec(w)),
        compiler_params=_row_params(12 * _nbytes((TM, w), F32), carried=True),
    )(*args)


def _post_res(xres, y, g, gt, *, name):
    def body(x_ref, y_ref, g_ref, gt_ref, o_ref):
        yv = y_ref[...]
        r = lax.rsqrt(jnp.mean(yv * yv, axis=-1, keepdims=True) + NORM_EPS)
        o_ref[...] = x_ref[...] + gt_ref[...] * ((yv * r) * g_ref[...])

    return _pcall(
        body, name=name, out_shape=jax.ShapeDtypeStruct((S, D), F32), grid=(S // TM,),
        in_specs=[_rows(D), _rows(D), _vec(D), _vec(D)], out_specs=_rows(D),
        compiler_params=_row_params(8 * _nbytes((TM, D), F32)),
    )(xres, y, g, gt)


def _post_res_loss(xres, y, g, gt, target, *, name):
    def body(x_ref, y_ref, g_ref, gt_ref, t_ref, dout_ref, loss_ref):
        i = pl.program_id(0)

        @pl.when(i == 0)
        def _():
            loss_ref[...] = jnp.zeros_like(loss_ref)

        yv = y_ref[...]
        r = lax.rsqrt(jnp.mean(yv * yv, axis=-1, keepdims=True) + NORM_EPS)
        out = x_ref[...] + gt_ref[...] * ((yv * r) * g_ref[...])
        err = out - t_ref[...]
        dout_ref[...] = err * (1.0 / D)
        per_row = jnp.mean(err * err, axis=-1, keepdims=True)
        loss_ref[...] += 0.5 * jnp.sum(per_row, axis=0, keepdims=True)

    return _pcall(
        body, name=name,
        out_shape=(jax.ShapeDtypeStruct((S, D), F32), jax.ShapeDtypeStruct((1, 1), F32)), grid=(S // TM,),
        in_specs=[_rows(D), _rows(D), _vec(D), _vec(D), _rows(D)],
        out_specs=(_rows(D), pl.BlockSpec((1, 1), lambda i: (0, 0))),
        compiler_params=_row_params(10 * _nbytes((TM, D), F32), carried=True),
    )(xres, y, g, gt, target)


def _post_res_bwd(dxn, y, g, gt, *, name):
    def body(d_ref, y_ref, g_ref, gt_ref, dy_ref, sgt_ref, sg_ref):
        i = pl.program_id(0)

        @pl.when(i == 0)
        def _():
            sgt_ref[...] = jnp.zeros_like(sgt_ref)
            sg_ref[...] = jnp.zeros_like(sg_ref)

        yv = y_ref[...]
        dv = d_ref[...]
        r = lax.rsqrt(jnp.mean(yv * yv, axis=-1, keepdims=True) + NORM_EPS)
        yn = yv * r
        dn = dv * gt_ref[...]
        dyn = dn * g_ref[...]
        dy_ref[...] = (r * (dyn - yn * jnp.mean(dyn * yn, axis=-1, keepdims=True))).astype(dy_ref.dtype)
        sgt_ref[...] += jnp.sum(dv * (yn * g_ref[...]), axis=0, keepdims=True)
        sg_ref[...] += jnp.sum(dn * yn, axis=0, keepdims=True)

    return _pcall(
        body, name=name,
        out_shape=(jax.ShapeDtypeStruct((S, D), _MXU), jax.ShapeDtypeStruct((1, D), F32), jax.ShapeDtypeStruct((1, D), F32)),
        grid=(S // TM,), in_specs=[_rows(D), _rows(D), _vec(D), _vec(D)], out_specs=(_rows(D), _vec(D), _vec(D)),
        compiler_params=_row_params(10 * _nbytes((TM, D), F32), carried=True),
    )(dxn, y, g, gt)


def _swiglu(gu, *, name):
    def body(g_ref, u_ref, o_ref):
        gv = g_ref[...]
        o_ref[...] = ((gv * jax.nn.sigmoid(gv)) * u_ref[...]).astype(o_ref.dtype)

    return _pcall(
        body, name=name, out_shape=jax.ShapeDtypeStruct((S, D_FF), _MXU), grid=(S // TM_FF,),
        in_specs=[_rows(D_FF, 0, TM_FF), _rows(D_FF, 1, TM_FF)], out_specs=_rows(D_FF, 0, TM_FF),
        compiler_params=_row_params(8 * _nbytes((TM_FF, D_FF), F32)),
    )(gu, gu)


def _swiglu_bwd(gu, dact, *, name):
    def body(g_ref, u_ref, d_ref, o_ref):
        gv = g_ref[...]
        dv = d_ref[...]
        sg = jax.nn.sigmoid(gv)
        o_ref[:, :D_FF] = (dv * u_ref[...] * (sg * (1.0 + gv * (1.0 - sg)))).astype(o_ref.dtype)
        o_ref[:, D_FF:] = (dv * (gv * sg)).astype(o_ref.dtype)

    return _pcall(
        body, name=name, out_shape=jax.ShapeDtypeStruct((S, 2 * D_FF), _MXU), grid=(S // TM_FF,),
        in_specs=[_rows(D_FF, 0, TM_FF), _rows(D_FF, 1, TM_FF), _rows(D_FF, 0, TM_FF)], out_specs=_rows(2 * D_FF, 0, TM_FF),
        compiler_params=_row_params(12 * _nbytes((TM_FF, D_FF), F32)),
    )(gu, gu, dact)


def _merge(rest, pa, pb, *, name):
    def body(ga_ref, gb_ref, pa_ref, pb_ref, o_ref):
        o_ref[...] = (jax.nn.sigmoid(ga_ref[...]) * pa_ref[...] + jax.nn.sigmoid(gb_ref[...]) * pb_ref[...]).astype(o_ref.dtype)

    return _pcall(
        body, name=name, out_shape=jax.ShapeDtypeStruct((S, D), _MXU), grid=(S // TM,),
        in_specs=[_rows(D, OFF_GFOX // D), _rows(D, OFF_GMLA // D), _rows(D), _rows(D)], out_specs=_rows(D),
        compiler_params=_row_params(10 * _nbytes((TM, D), F32)),
    )(rest, rest, pa, pb)


def _merge_bwd(rest, pa, pb, dm, *, name):
    def body(ga_ref, gb_ref, pa_ref, pb_ref, d_ref, dpa_ref, dpb_ref, dga_ref, dgb_ref):
        dv = d_ref[...]
        sa = jax.nn.sigmoid(ga_ref[...])
        sb = jax.nn.sigmoid(gb_ref[...])
        dpa_ref[...] = (dv * sa).astype(dpa_ref.dtype)
        dpb_ref[...] = (dv * sb).astype(dpb_ref.dtype)
        dga_ref[...] = (dv * pa_ref[...] * (sa * (1.0 - sa))).astype(dga_ref.dtype)
        dgb_ref[...] = (dv * pb_ref[...] * (sb * (1.0 - sb))).astype(dgb_ref.dtype)

    o = jax.ShapeDtypeStruct((S, D), _MXU)
    return _pcall(
        body, name=name, out_shape=(o, o, o, o), grid=(S // TM,),
        in_specs=[_rows(D, OFF_GFOX // D), _rows(D, OFF_GMLA // D), _rows(D), _rows(D), _rows(D)],
        out_specs=(_rows(D), _rows(D), _rows(D), _rows(D)),
        compiler_params=_row_params(16 * _nbytes((TM, D), F32)),
    )(rest, rest, pa, pb, dm)


SCAN = 256


def _split_dot(tri, x):
    hi = x.astype(_MXU)
    r1 = x - hi.astype(F32)
    mid = r1.astype(_MXU)
    lo = (r1 - mid.astype(F32)).astype(_MXU)
    dot = functools.partial(jnp.dot, preferred_element_type=F32)
    return dot(tri, hi) + dot(tri, mid) + dot(tri, lo)


def _fox_prep(rest, bf, *, name):
    def body(z_ref, b_ref, f_ref):
        lane = lax.broadcasted_iota(jnp.int32, (SCAN, LANES), 1)
        tri = (lax.broadcasted_iota(jnp.int32, (SCAN, SCAN), 1) <= lax.broadcasted_iota(jnp.int32, (SCAN, SCAN), 0)).astype(_MXU)
        carry = jnp.zeros((1, LANES), F32)
        for c in range(S // SCAN):
            z = z_ref[c * SCAN:(c + 1) * SCAN, :] + b_ref[...]
            lf = jnp.minimum(z, 0.0) - jnp.log(1.0 + jnp.exp(-jnp.abs(z)))
            lf = jnp.where(lane < H, lf, 0.0)
            cum = _split_dot(tri, lf) + carry
            f_ref[c * SCAN:(c + 1) * SCAN, :] = cum
            carry = cum[SCAN - 1:SCAN, :]

    return _pcall(
        body, name=name, out_shape=jax.ShapeDtypeStruct((S, LANES), F32), grid=(1,),
        in_specs=[pl.BlockSpec((S, LANES), lambda i: (0, OFF_FLOG // LANES)), pl.BlockSpec((1, LANES), lambda i: (0, 0))],
        out_specs=pl.BlockSpec((S, LANES), lambda i: (0, 0)),
        compiler_params=pltpu.CompilerParams(vmem_limit_bytes=_limit(8 * _nbytes((S, LANES), F32))),
    )(rest, bf)


def _fox_bwd_prep(rest, bf, dF, *, name):
    def body(z_ref, b_ref, d_ref, o_ref, db_ref):
        lane = lax.broadcasted_iota(jnp.int32, (SCAN, LANES), 1)
        tri = (lax.broadcasted_iota(jnp.int32, (SCAN, SCAN), 1) >= lax.broadcasted_iota(jnp.int32, (SCAN, SCAN), 0)).astype(_MXU)
        carry = jnp.zeros((1, LANES), F32)
        db = jnp.zeros((1, LANES), F32)
        for c in range(S // SCAN - 1, -1, -1):
            rc = _split_dot(tri, d_ref[c * SCAN:(c + 1) * SCAN, :]) + carry
            z = z_ref[c * SCAN:(c + 1) * SCAN, :] + b_ref[...]
            dz = jnp.where(lane < H, rc * jax.nn.sigmoid(-z), 0.0)
            o_ref[c * SCAN:(c + 1) * SCAN, :] = dz
            db = db + jnp.sum(dz, axis=0, keepdims=True)
            carry = rc[0:1, :]
        db_ref[...] = db

    return _pcall(
        body, name=name,
        out_shape=(jax.ShapeDtypeStruct((S, LANES), F32), jax.ShapeDtypeStruct((1, LANES), F32)), grid=(1,),
        in_specs=[pl.BlockSpec((S, LANES), lambda i: (0, OFF_FLOG // LANES)), pl.BlockSpec((1, LANES), lambda i: (0, 0)),
                  pl.BlockSpec((S, LANES), lambda i: (0, 0))],
        out_specs=(pl.BlockSpec((S, LANES), lambda i: (0, 0)), pl.BlockSpec((1, LANES), lambda i: (0, 0))),
        compiler_params=pltpu.CompilerParams(vmem_limit_bytes=_limit(10 * _nbytes((S, LANES), F32))),
    )(rest, bf, dF)


def _swap16(x):
    lane = lax.broadcasted_iota(jnp.int32, x.shape, 1)
    half = MLA_ROPE // 2
    sw = jnp.where(lane < KRIN_LANE + half, pltpu.roll(x, LANES - half, 1), pltpu.roll(x, half, 1))
    return jnp.where((lane >= KRIN_LANE) & (lane < KRIN_LANE + MLA_ROPE), sw, 0.0)


def _mla_assemble(qb, kvb, rest, ctab, stab, *, name):
    def body(q_ref, kk_ref, kv_ref, kr_ref, c_ref, s_ref, qo_ref, ko_ref, vo_ref):
        cv = c_ref[...]
        sv = s_ref[...]
        kr = kr_ref[...]
        kpe = kr * cv + _swap16(kr) * sv
        for h in range(H):
            sl = slice(h * HP, (h + 1) * HP)
            qh = q_ref[:, sl]
            qo_ref[:, sl] = (qh * cv + _swap16(qh) * sv).astype(qo_ref.dtype)
            ko_ref[:, sl] = (kk_ref[:, sl] + kpe).astype(ko_ref.dtype)
        vo_ref[...] = kv_ref[...].astype(vo_ref.dtype)

    o = jax.ShapeDtypeStruct((S, H * HP), _MXU)
    return _pcall(
        body, name=name, out_shape=(o, o, o), grid=(S // TM,),
        in_specs=[_rows(H * HP), _rows(H * HP, 0), _rows(H * HP, 1), _rows(LANES, OFF_KRIN // LANES), _rows(LANES), _rows(LANES)],
        out_specs=(_rows(H * HP), _rows(H * HP), _rows(H * HP)),
        compiler_params=_row_params(14 * _nbytes((TM, H * HP), F32)),
    )(qb, kvb, kvb, rest, ctab, stab)


def _mla_assemble_bwd(dq, dk, dv, ctab, stab, *, name):
    def body(dq_ref, dk_ref, dv_ref, c_ref, s_ref, dqo_ref, dkv_ref, dkr_ref):
        cv = c_ref[...]
        sv = s_ref[...]
        lane = lax.broadcasted_iota(jnp.int32, (TM, LANES), 1)
        dsum = jnp.zeros((TM, LANES), F32)
        for h in range(H):
            sl = slice(h * HP, (h + 1) * HP)
            dqh = dq_ref[:, sl]
            dqo_ref[:, sl] = (dqh * cv + _swap16(dqh * sv)).astype(dqo_ref.dtype)
            dsum = dsum + dk_ref[:, sl]
        dkv_ref[:, :H * HP] = dk_ref[...].astype(dkv_ref.dtype)
        dkv_ref[:, H * HP:] = dv_ref[...].astype(dkv_ref.dtype)
        dkr = dsum * cv + _swap16(dsum * sv)
        dkr_ref[...] = jnp.where((lane >= KRIN_LANE) & (lane < KRIN_LANE + MLA_ROPE), dkr, 0.0)

    return _pcall(
        body, name=name,
        out_shape=(jax.ShapeDtypeStruct((S, H * HP), _MXU), jax.ShapeDtypeStruct((S, 2 * H * HP), _MXU),
                   jax.ShapeDtypeStruct((S, LANES), F32)),
        grid=(S // TM,),
        in_specs=[_rows(H * HP), _rows(H * HP), _rows(H * HP), _rows(LANES), _rows(LANES)],
        out_specs=(_rows(H * HP), _rows(2 * H * HP), _rows(LANES)),
        compiler_params=_row_params(14 * _nbytes((TM, H * HP), F32)),
    )(dq, dk, dv, ctab, stab)


TQ = 512
TKF = 512
TKB = 512
TQB = 512
_NT = (((1,), (1,)), ((), ()))
_TN = (((0,), (0,)), ((), ()))


def _is_pow2(x):
    return math.frexp(x)[0] == 0.5


def _attn_fwd(q, k, v, frow, *, scale, name, side=None, heads0=(0, 0, 0)):
    has_decay = frow is not None
    fold = _is_pow2(scale)
    n_in = 4 if has_decay else 3
    n_side_in = len(side.inputs) if side else 0
    n_side_out = len(side.out_shapes) if side else 0

    def body(*refs):
        q_ref, k_ref, v_ref = refs[:3]
        fr_ref = refs[3] if has_decay else None
        side_in = refs[n_in:n_in + n_side_in]
        o_ref, lse_ref = refs[n_in + n_side_in:n_in + n_side_in + 2]
        side_out = refs[n_in + n_side_in + 2:n_in + n_side_in + 2 + n_side_out]
        side_scratch = refs[n_in + n_side_in + 2 + n_side_out:]
        i = pl.program_id(1)
        if side:
            @pl.when((pl.program_id(0) == 0) & (i == 0))
            def _():
                side.start(side_in, side_out, side_scratch)
        qv = q_ref[...]
        if fold:
            qv = (qv * scale).astype(qv.dtype)
        last = (i * TQ) // TKF

        def tile(j, carry, masked):
            m, l, acc = carry
            k0 = pl.multiple_of(j * TKF, TKF)
            kj = k_ref[pl.ds(k0, TKF), :]
            vj = v_ref[pl.ds(k0, TKF), :]
            s = lax.dot_general(qv, kj, _NT, preferred_element_type=F32)
            if not fold:
                s = s * scale
            if has_decay:
                s = s - fr_ref[0, j]
            if masked:
                rows = i * TQ + lax.broadcasted_iota(jnp.int32, (TQ, TKF), 0)
                cols = j * TKF + lax.broadcasted_iota(jnp.int32, (TQ, TKF), 1)
                s = jnp.where(cols <= rows, s, NEG)
            m_new = jnp.maximum(m, jnp.max(s, axis=-1, keepdims=True))
            alpha = jnp.exp(m - m_new)
            p = jnp.exp(s - m_new)
            l = alpha * l + jnp.sum(p, axis=-1, keepdims=True)
            acc = alpha * acc + jnp.dot(p.astype(_MXU), vj, preferred_element_type=F32)
            return m_new, l, acc

        init = (jnp.full((TQ, 1), NEG, F32), jnp.zeros((TQ, 1), F32), jnp.zeros((TQ, HP), F32))
        carry = lax.fori_loop(0, last, lambda j, c: tile(j, c, False), init)
        m, l, acc = tile(last, carry, True)
        o_ref[...] = acc / l
        lse_ref[0] = m + jnp.log(l)
        if side:
            @pl.when((pl.program_id(0) == H - 1) & (i == S // TQ - 1))
            def _():
                side.finish(side_in, side_out, side_scratch)

    q0, k0, v0 = heads0
    in_specs = [pl.BlockSpec((TQ, HP), lambda h, i: (i, h + q0)), pl.BlockSpec((S, HP), lambda h, i: (0, h + k0)),
                pl.BlockSpec((S, HP), lambda h, i: (0, h + v0))]
    args = (q, k, v)
    if has_decay:
        in_specs += [pl.BlockSpec((1, S // TKF, 1, TKF), lambda h, i: (h, 0, 0, 0))]
        args += (frow,)
    out_shape = (jax.ShapeDtypeStruct((S, H * HP), F32), jax.ShapeDtypeStruct((H, S, 1), F32))
    out_specs = (pl.BlockSpec((TQ, HP), lambda h, i: (i, h)), pl.BlockSpec((1, TQ, 1), lambda h, i: (h, i, 0)))
    extra = {}
    if side:
        anywhere = pl.BlockSpec(memory_space=pl.ANY)
        in_specs += [anywhere] * n_side_in
        args += tuple(side.inputs)
        out_shape += tuple(side.out_shapes)
        out_specs += (anywhere,) * n_side_out
        extra = dict(scratch_shapes=list(side.scratch),
                     input_output_aliases={n_in + a: 2 + b for a, b in side.aliases.items()})
    return _pcall(
        body, name=name, out_shape=out_shape, grid=(H, S // TQ), in_specs=in_specs, out_specs=out_specs,
        compiler_params=pltpu.CompilerParams(dimension_semantics=("arbitrary", "arbitrary") if side else ("parallel", "parallel"),
                                             vmem_limit_bytes=_limit(8 * _nbytes((S, HP), F32))),
        **extra,
    )(*args)


def _attn_delta(do, o, *, name):
    def body(do_ref, o_ref, dl_ref, dob_ref):
        for h in range(H):
            sl = slice(h * HP, (h + 1) * HP)
            dl_ref[h] = jnp.sum(do_ref[:, sl] * o_ref[:, sl], axis=-1, keepdims=True)
        dob_ref[...] = do_ref[...].astype(dob_ref.dtype)

    return _pcall(
        body, name=name,
        out_shape=(jax.ShapeDtypeStruct((H, S, 1), F32), jax.ShapeDtypeStruct((S, H * HP), _MXU)), grid=(S // TM,),
        in_specs=[_rows(H * HP), _rows(H * HP)],
        out_specs=(pl.BlockSpec((H, TM, 1), lambda i: (0, i, 0)), _rows(H * HP)),
        compiler_params=_row_params(8 * _nbytes((TM, H * HP), F32)),
    )(do, o)


def _attn_bwd(q, k, v, dob, lse_row, delta_row, fcol, *, scale, name, side=None, heads0=(0, 0, 0)):
    has_decay = fcol is not None
    fold = _is_pow2(scale)
    n_in = 7 if has_decay else 6
    n_side_in = len(side.inputs) if side else 0
    n_side_out = len(side.out_shapes) if side else 0

    def body(*refs):
        q_ref, k_ref, v_ref, do_ref, lse_ref, dl_ref = refs[:6]
        fc_ref = refs[6] if has_decay else None
        side_in = refs[n_in:n_in + n_side_in]
        dq_ref, dk_ref, dv_ref = refs[n_in + n_side_in:n_in + n_side_in + 3]
        side_out = refs[n_in + n_side_in + 3:n_in + n_side_in + 3 + n_side_out]
        dq_acc = refs[n_in + n_side_in + 3 + n_side_out]
        side_scratch = refs[n_in + n_side_in + 4 + n_side_out:]
        j = pl.program_id(1)
        if side:
            @pl.when((pl.program_id(0) == 0) & (j == 0))
            def _():
                side.start(side_in, side_out, side_scratch)

        @pl.when(j == 0)
        def _():
            dq_acc[...] = jnp.zeros_like(dq_acc)

        kj = k_ref[...]
        vj = v_ref[...]
        kjs = (kj * scale).astype(kj.dtype) if fold else kj
        if has_decay:
            klane = lax.broadcasted_iota(jnp.int32, (TKB, HP), 1)
            kj = jnp.where(klane == ROW_SUM_LANE, 1.0, kj).astype(kj.dtype)
        first = (j * TKB) // TQB

        def tile(t, carry, masked):
            dk, dv = carry
            r0 = pl.multiple_of(t * TQB, TQB)
            qi = q_ref[pl.ds(r0, TQB), :]
            doi = do_ref[pl.ds(r0, TQB), :]
            st = lax.dot_general(kjs, qi, _NT, preferred_element_type=F32)
            if not fold:
                st = st * scale
            if has_decay:
                st = st - fc_ref[0]
            if masked:
                keys = j * TKB + lax.broadcasted_iota(jnp.int32, (TKB, TQB), 0)
                qpos = t * TQB + lax.broadcasted_iota(jnp.int32, (TKB, TQB), 1)
                st = jnp.where(keys <= qpos, st, NEG)
            pt = jnp.exp(st - lse_ref[0, t])
            dv = dv + jnp.dot(pt.astype(_MXU), doi, preferred_element_type=F32)
            dpt = lax.dot_general(vj, doi, _NT, preferred_element_type=F32)
            dst = (pt * (dpt - dl_ref[0, t])).astype(_MXU)
            if has_decay:
                lane = lax.broadcasted_iota(jnp.int32, (TQB, HP), 1)
                qi = jnp.where(lane == COL_SUM_LANE, 1.0, qi).astype(qi.dtype)
            dk = dk + jnp.dot(dst, qi, preferred_element_type=F32)
            dq_acc[pl.ds(r0, TQB), :] += lax.dot_general(dst, kj, _TN, preferred_element_type=F32)
            return dk, dv

        zero = jnp.zeros((TKB, HP), F32)
        carry = tile(first, (zero, zero), True)
        dk, dv = lax.fori_loop(first + 1, S // TQB, lambda t, c: tile(t, c, False), carry)
        dk_ref[...] = dk * scale
        dv_ref[...] = dv

        @pl.when(j == S // TKB - 1)
        def _():
            dq_ref[...] = dq_acc[...] * scale

        if side:
            @pl.when((pl.program_id(0) == H - 1) & (j == S // TKB - 1))
            def _():
                side.finish(side_in, side_out, side_scratch)

    q0, k0, v0 = heads0
    head = pl.BlockSpec((S, HP), lambda h, j: (0, h))
    kv = pl.BlockSpec((TKB, HP), lambda h, j: (j, h))
    stat = pl.BlockSpec((1, S // TQB, 1, TQB), lambda h, j: (h, 0, 0, 0))
    in_specs = [pl.BlockSpec((S, HP), lambda h, j: (0, h + q0)), pl.BlockSpec((TKB, HP), lambda h, j: (j, h + k0)),
                pl.BlockSpec((TKB, HP), lambda h, j: (j, h + v0)), head, stat, stat]
    args = (q, k, v, dob, lse_row, delta_row)
    if has_decay:
        in_specs += [pl.BlockSpec((1, TKB, 1), lambda h, j: (h, j, 0))]
        args += (fcol,)
    o = jax.ShapeDtypeStruct((S, H * HP), F32)
    out_shape, out_specs, scratch, aliases = (o, o, o), (head, kv, kv), [pltpu.VMEM((S, HP), F32)], {}
    if side:
        anywhere = pl.BlockSpec(memory_space=pl.ANY)
        in_specs += [anywhere] * n_side_in
        args += tuple(side.inputs)
        out_shape += tuple(side.out_shapes)
        out_specs += (anywhere,) * n_side_out
        scratch += list(side.scratch)
        aliases = {n_in + a: 3 + b for a, b in side.aliases.items()}
    return _pcall(
        body, name=name, out_shape=out_shape, grid=(H, S // TKB), in_specs=in_specs,
        out_specs=out_specs, scratch_shapes=scratch, input_output_aliases=aliases,
        compiler_params=pltpu.CompilerParams(dimension_semantics=("arbitrary" if side else "parallel", "arbitrary"),
                                             vmem_limit_bytes=_limit(12 * _nbytes((S, HP), F32))),
    )(*args)


def _ada_mod(c_all, w_shard, b_shard, *, name):
    R = c_all.shape[0]
    N = w_shard.shape[1]
    tn = 512

    def body(c_ref, w_ref, b_ref, o_ref, sc_ref):
        cv = c_ref[...]
        sc = (cv * jax.nn.sigmoid(cv)).astype(_MXU)
        sc_ref[...] = sc
        o_ref[...] = jnp.dot(sc, w_ref[...].astype(_MXU), preferred_element_type=F32) + b_ref[...]

    return _pcall(
        body, name=name,
        out_shape=(jax.ShapeDtypeStruct((R, N), F32), jax.ShapeDtypeStruct((R, D), _MXU)), grid=(N // tn,),
        in_specs=[pl.BlockSpec((R, D), lambda j: (0, 0)), pl.BlockSpec((D, tn), lambda j: (0, j)), pl.BlockSpec((1, tn), lambda j: (0, j))],
        out_specs=(pl.BlockSpec((R, tn), lambda j: (0, j)), pl.BlockSpec((R, D), lambda j: (0, 0))),
        compiler_params=pltpu.CompilerParams(dimension_semantics=("arbitrary",), vmem_limit_bytes=_limit(6 * _nbytes((D, tn), F32))),
    )(c_all, w_shard, b_shard)


def _rowsum(x, *, name):
    R, L = x.shape

    def body(x_ref, o_ref):
        acc = x_ref[0:1, :]
        for r in range(1, R):
            acc = acc + x_ref[r:r + 1, :]
        o_ref[...] = acc

    return pl.pallas_call(body, name=name, out_shape=jax.ShapeDtypeStruct((1, L), F32),
                          in_specs=[pl.BlockSpec(memory_space=pltpu.VMEM)], out_specs=pl.BlockSpec(memory_space=pltpu.VMEM))(x)


def _adamw(w, g, m, v, *, name):
    _, R, C = w.shape
    tr = R
    for t in range(8, R + 1, 8):
        if R % t == 0 and t * C * 4 <= (1 << 20):
            tr = t

    def body(w_ref, g_ref, m_ref, v_ref, d_ref, mo_ref, vo_ref):
        gv = g_ref[...]
        m2 = ADAM_B1 * m_ref[...] + (1.0 - ADAM_B1) * gv
        v2 = ADAM_B2 * v_ref[...] + (1.0 - ADAM_B2) * (gv * gv)
        m_hat = m2 / (1.0 - ADAM_B1 ** ADAM_STEP)
        v_hat = v2 / (1.0 - ADAM_B2 ** ADAM_STEP)
        d_ref[...] = -ADAM_LR * (m_hat / (jnp.sqrt(v_hat) + ADAM_EPS) + ADAM_WD * w_ref[...])
        mo_ref[...] = m2
        vo_ref[...] = v2

    blk = pl.BlockSpec((None, tr, C), lambda i: (0, i, 0))
    o = jax.ShapeDtypeStruct((1, R, C), F32)
    return _pcall(
        body, name=name, out_shape=(o, o, o), grid=(R // tr,),
        in_specs=[blk, pl.BlockSpec((tr, C), lambda i: (i, 0)), blk, blk], out_specs=(blk, blk, blk),
        compiler_params=pltpu.CompilerParams(dimension_semantics=("parallel",), vmem_limit_bytes=_limit(20 * _nbytes((tr, C), F32))),
    )(w, g, m, v)


def _place():
    x, y, c = lax.axis_index("x"), lax.axis_index("y"), lax.axis_index("c")
    return x, y, c, [(1 - x, y), (x, 1 - y), (1 - x, 1 - y)]


def _two_level_gather(x_ref, out_ref, send_sems, recv_sems, local_sem):
    x, y, c, chips = _place()
    me, sibling = (x, y, c), (x, y, 1 - c)

    def blk(px, py, pc):
        return out_ref.at[4 * px + 2 * py + pc]

    def copy(k, block, to, src=None):
        return pltpu.make_async_remote_copy(
            src_ref=blk(*block) if src is None else src, dst_ref=blk(*block),
            send_sem=send_sems.at[k], recv_sem=recv_sems.at[k], device_id=to, device_id_type=MESH)

    mine = pltpu.make_async_copy(x_ref, blk(*me), local_sem)
    mine.start()
    first = [copy(0, me, sibling, src=x_ref)]
    first += [copy(1 + j, me, (*chip, c), src=x_ref) for j, chip in enumerate(chips)]
    for cp in first:
        cp.start()
    passed = [copy(4 + j, (*chip, c), sibling) for j, chip in enumerate(chips)]
    for j, chip in enumerate(chips):
        copy(1 + j, (*chip, c), me).wait_recv()
        passed[j].start()
    copy(0, sibling, me).wait_recv()
    for j, chip in enumerate(chips):
        copy(4 + j, (*chip, 1 - c), me).wait_recv()
    for cp in first + passed:
        cp.wait_send()
    mine.wait()


_GATHER_SEMS = [pltpu.SemaphoreType.DMA((7,)), pltpu.SemaphoreType.DMA((7,)), pltpu.SemaphoreType.DMA]


class _SideJob(NamedTuple):
    inputs: tuple
    out_shapes: tuple
    aliases: dict
    scratch: tuple
    start: Callable
    finish: Callable


def _block_index(px, py, pc):
    return 4 * px + 2 * py + pc


def _gather_level1_job(halves):
    nw = len(halves)

    def copies(ins, outs, scratch, n):
        sends, recvs, _ = scratch
        x, y, c, chips = _place()
        mine = outs[n].at[_block_index(x, y, c)]
        peers = [(x, y, 1 - c)] + [(*chip, c) for chip in chips]
        out = []
        for t, peer in enumerate(peers):
            sem = dict(send_sem=sends.at[4 * n + t], recv_sem=recvs.at[4 * n + t], device_id_type=MESH)
            landing = outs[n].at[_block_index(*peer)]
            out.append((pltpu.make_async_remote_copy(src_ref=ins[n], dst_ref=mine, device_id=peer, **sem),
                        pltpu.make_async_remote_copy(src_ref=landing, dst_ref=landing, device_id=peer, **sem)))
        local = pltpu.make_async_copy(ins[n], mine, scratch[2].at[n])
        return out, local

    def start(ins, outs, scratch):
        for n in range(nw):
            pairs, local = copies(ins, outs, scratch, n)
            local.start()
            for to, _ in pairs:
                to.start()

    def finish(ins, outs, scratch):
        for n in range(nw):
            pairs, local = copies(ins, outs, scratch, n)
            for to, frm in pairs:
                frm.wait_recv()
                to.wait_send()
            local.wait()

    return _SideJob(
        inputs=tuple(halves), out_shapes=tuple(jax.ShapeDtypeStruct((8,) + h.shape, h.dtype) for h in halves), aliases={},
        scratch=(pltpu.SemaphoreType.DMA((4 * nw,)), pltpu.SemaphoreType.DMA((4 * nw,)), pltpu.SemaphoreType.DMA((nw,))),
        start=start, finish=finish)


def _gather_direct_job(halves):
    nw = len(halves)

    def copies(ins, outs, scratch, n):
        sends, recvs, _ = scratch
        x, y, c, _ = _place()
        mine = outs[n].at[_block_index(x, y, c)]
        out = []
        for f, (fx, fy, fc) in enumerate(_FLIPS):
            peer = (_flipped(x, fx), _flipped(y, fy), _flipped(c, fc))
            sem = dict(send_sem=sends.at[7 * n + f], recv_sem=recvs.at[7 * n + f], device_id=peer, device_id_type=MESH)
            landing = outs[n].at[_block_index(*peer)]
            out.append((pltpu.make_async_remote_copy(src_ref=ins[n], dst_ref=mine, **sem),
                        pltpu.make_async_remote_copy(src_ref=landing, dst_ref=landing, **sem)))
        return out, pltpu.make_async_copy(ins[n], mine, scratch[2].at[n])

    def start(ins, outs, scratch):
        for n in range(nw):
            pairs, local = copies(ins, outs, scratch, n)
            local.start()
            for to, _ in pairs:
                to.start()

    def finish(ins, outs, scratch):
        for n in range(nw):
            pairs, local = copies(ins, outs, scratch, n)
            for to, frm in pairs:
                frm.wait_recv()
                to.wait_send()
            local.wait()

    return _SideJob(
        inputs=tuple(halves), out_shapes=tuple(jax.ShapeDtypeStruct((8,) + h.shape, h.dtype) for h in halves), aliases={},
        scratch=(pltpu.SemaphoreType.DMA((7 * nw,)), pltpu.SemaphoreType.DMA((7 * nw,)), pltpu.SemaphoreType.DMA((nw,))),
        start=start, finish=finish)


def _join_jobs(a, b):
    cut = (len(a.inputs), len(a.out_shapes), len(a.scratch))

    def parts(ins, outs, scratch):
        return ((ins[:cut[0]], outs[:cut[1]], scratch[:cut[2]]), (ins[cut[0]:], outs[cut[1]:], scratch[cut[2]:]))

    def start(ins, outs, scratch):
        pa, pb = parts(ins, outs, scratch)
        a.start(*pa)
        b.start(*pb)

    def finish(ins, outs, scratch):
        pa, pb = parts(ins, outs, scratch)
        a.finish(*pa)
        b.finish(*pb)

    aliases = dict(a.aliases)
    aliases.update({cut[0] + i: cut[1] + o for i, o in b.aliases.items()})
    return _SideJob(inputs=a.inputs + b.inputs, out_shapes=a.out_shapes + b.out_shapes, aliases=aliases,
                    scratch=a.scratch + b.scratch, start=start, finish=finish)


def _gather_level2_job(gathered):
    nw = len(gathered)

    def copies(outs, scratch, n):
        sends, recvs = scratch
        x, y, c, chips = _place()
        out = []
        for j, chip in enumerate(chips):
            sem = dict(send_sem=sends.at[3 * n + j], recv_sem=recvs.at[3 * n + j], device_id=(x, y, 1 - c), device_id_type=MESH)
            going = outs[n].at[_block_index(*chip, c)]
            landing = outs[n].at[_block_index(*chip, 1 - c)]
            out.append((pltpu.make_async_remote_copy(src_ref=going, dst_ref=going, **sem),
                        pltpu.make_async_remote_copy(src_ref=landing, dst_ref=landing, **sem)))
        return out

    def start(ins, outs, scratch):
        for n in range(nw):
            for to, _ in copies(outs, scratch, n):
                to.start()

    def finish(ins, outs, scratch):
        for n in range(nw):
            for to, frm in copies(outs, scratch, n):
                frm.wait_recv()
                to.wait_send()

    return _SideJob(
        inputs=tuple(gathered), out_shapes=tuple(jax.ShapeDtypeStruct(g.shape, g.dtype) for g in gathered),
        aliases={n: n for n in range(nw)},
        scratch=(pltpu.SemaphoreType.DMA((3 * nw,)), pltpu.SemaphoreType.DMA((3 * nw,))),
        start=start, finish=finish)


def _all_gather_rows(x, *, name):
    R, C = x.shape

    def body(x_ref, out_ref, send_sems, recv_sems, local_sem):
        x_, y_, c_, _ = _place()
        mine = out_ref.at[_block_index(x_, y_, c_)]
        local = pltpu.make_async_copy(x_ref, mine, local_sem)
        local.start()
        pairs = []
        for f, (fx, fy, fc) in enumerate(_FLIPS):
            peer = (_flipped(x_, fx), _flipped(y_, fy), _flipped(c_, fc))
            sem = dict(send_sem=send_sems.at[f], recv_sem=recv_sems.at[f], device_id=peer, device_id_type=MESH)
            landing = out_ref.at[_block_index(*peer)]
            to = pltpu.make_async_remote_copy(src_ref=x_ref, dst_ref=mine, **sem)
            to.start()
            pairs.append((to, pltpu.make_async_remote_copy(src_ref=landing, dst_ref=landing, **sem)))
        for to, frm in pairs:
            frm.wait_recv()
            to.wait_send()
        local.wait()

    return pl.pallas_call(
        body, name=name, out_shape=jax.ShapeDtypeStruct((8, R, C), x.dtype),
        in_specs=[pl.BlockSpec(memory_space=pltpu.VMEM)], out_specs=pl.BlockSpec(memory_space=pltpu.VMEM),
        scratch_shapes=list(_GATHER_SEMS),
        compiler_params=pltpu.CompilerParams(vmem_limit_bytes=_limit(10 * _nbytes((R, C), x.dtype))),
    )(x)


CAST_ROWS = 16
_FLIPS = [(fx, fy, fc) for fx in (0, 1) for fy in (0, 1) for fc in (0, 1)][1:]


def _flipped(v, bit):
    return 1 - v if bit else v


def _scatter_direct_job(pieces):
    nw = len(pieces)

    def copies(ins, outs, scratch, n):
        sends, recvs = scratch
        x, y, c, _ = _place()
        out = []
        for f, (fx, fy, fc) in enumerate(_FLIPS):
            peer = (_flipped(x, fx), _flipped(y, fy), _flipped(c, fc))
            sem = dict(send_sem=sends.at[7 * n + f], recv_sem=recvs.at[7 * n + f], device_id=peer, device_id_type=MESH)
            landing = outs[n].at[f]
            out.append((pltpu.make_async_remote_copy(src_ref=ins[n].at[_block_index(*peer)], dst_ref=landing, **sem),
                        pltpu.make_async_remote_copy(src_ref=landing, dst_ref=landing, **sem)))
        return out

    def start(ins, outs, scratch):
        for n in range(nw):
            for to, _ in copies(ins, outs, scratch, n):
                to.start()

    def finish(ins, outs, scratch):
        for n in range(nw):
            for to, frm in copies(ins, outs, scratch, n):
                frm.wait_recv()
                to.wait_send()

    return _SideJob(
        inputs=tuple(pieces), out_shapes=tuple(jax.ShapeDtypeStruct((7,) + p.shape[1:], p.dtype) for p in pieces), aliases={},
        scratch=(pltpu.SemaphoreType.DMA((7 * nw,)), pltpu.SemaphoreType.DMA((7 * nw,))),
        start=start, finish=finish)


def _scatter_finish(g4s, landed, *, name):
    nw = len(g4s)
    dims = [g.shape[1:] for g in g4s]

    def body(*refs):
        g_refs, l_refs, out_refs, own = refs[:nw], refs[nw:2 * nw], refs[2 * nw:3 * nw], refs[3 * nw:4 * nw]
        load_sems, send_sems, recv_sems = refs[4 * nw:]
        x, y, core, _ = _place()
        k = 2 * x + y
        loads = []
        for n, (r, c) in enumerate(dims):
            my0 = pl.multiple_of(core * (r // 2), CAST_ROWS)
            ld = pltpu.make_async_copy(g_refs[n].at[k, pl.ds(my0, r // 2), :], own[n], load_sems.at[n])
            ld.start()
            loads.append(ld)
        swaps = []
        for n, (r, c) in enumerate(dims):
            rh = r // 2
            my0 = pl.multiple_of(core * rh, CAST_ROWS)
            loads[n].wait()

            def fin(i, carry, n=n, my0=my0):
                r0 = pl.multiple_of(i * CAST_ROWS, CAST_ROWS)
                s = own[n][pl.ds(r0, CAST_ROWS), :]
                for f in range(7):
                    s = s + l_refs[n][f, pl.ds(r0, CAST_ROWS), :].astype(F32)
                out_refs[n][pl.ds(my0 + r0, CAST_ROWS), :] = s
                return carry

            lax.fori_loop(0, rh // CAST_ROWS, fin, 0)
            half = out_refs[n].at[pl.ds(my0, rh), :]
            sw = pltpu.make_async_remote_copy(src_ref=half, dst_ref=half, send_sem=send_sems.at[n], recv_sem=recv_sems.at[n],
                                              device_id=(x, y, 1 - core), device_id_type=MESH)
            sw.start()
            swaps.append(sw)
        for sw in swaps:
            sw.wait()

    need = sum(_nbytes((7, r // 2, c), BF16) + _nbytes((r // 2, c), F32) + _nbytes((r, c), F32) for r, c in dims)
    vmem = pl.BlockSpec(memory_space=pltpu.VMEM)
    return pl.pallas_call(
        body, name=name, out_shape=tuple(jax.ShapeDtypeStruct((r, c), F32) for r, c in dims),
        in_specs=[pl.BlockSpec(memory_space=pl.ANY)] * nw + [vmem] * nw, out_specs=(vmem,) * nw,
        scratch_shapes=[pltpu.VMEM((r // 2, c), F32) for r, c in dims]
        + [pltpu.SemaphoreType.DMA((nw,)), pltpu.SemaphoreType.DMA((nw,)), pltpu.SemaphoreType.DMA((nw,))],
        compiler_params=pltpu.CompilerParams(vmem_limit_bytes=_limit(need * 1.1)),
    )(*g4s, *landed)


def _gather_weight(w, *, name):
    r, c = w.shape
    rh = r // 2
    assert rh % CAST_ROWS == 0

    def body(w_hbm, out_ref, tmp, xb, send_sems, recv_sems, local_sem):
        core = lax.axis_index("c")
        ld = pltpu.make_async_copy(w_hbm.at[pl.ds(pl.multiple_of(core * rh, CAST_ROWS), rh), :], tmp, local_sem)
        ld.start()
        ld.wait()

        def cast(i, carry):
            r0 = pl.multiple_of(i * CAST_ROWS, CAST_ROWS)
            xb[pl.ds(r0, CAST_ROWS), :] = tmp[pl.ds(r0, CAST_ROWS), :].astype(BF16)
            return carry

        lax.fori_loop(0, rh // CAST_ROWS, cast, 0)
        _two_level_gather(xb, out_ref, send_sems, recv_sems, local_sem)

    need = _nbytes((8, rh, c), BF16) + _nbytes((rh, c), F32) + _nbytes((rh, c), BF16)
    out = pl.pallas_call(
        body, name=name, out_shape=jax.ShapeDtypeStruct((8, rh, c), BF16),
        in_specs=[pl.BlockSpec(memory_space=pl.ANY)], out_specs=pl.BlockSpec(memory_space=pltpu.VMEM),
        scratch_shapes=[pltpu.VMEM((rh, c), F32), pltpu.VMEM((rh, c), BF16)] + list(_GATHER_SEMS),
        compiler_params=pltpu.CompilerParams(vmem_limit_bytes=_limit(need * 1.3)),
    )(w)
    return out.reshape(4, r, c)


def _reduce_scatter_weight(g4, *, name):
    _, r, c = g4.shape
    rh = r // 2
    assert rh % CAST_ROWS == 0
    nsteps = rh // CAST_ROWS

    def body(g_hbm, out_ref, mine, tmp, sbuf, rbuf_a, rbuf_b, a_send, a_recv, b_send, b_recv, c_send, c_recv, lsem):
        x, y, core, chips = _place()
        sibling = (x, y, 1 - core)
        k = 2 * x + y
        my0 = pl.multiple_of(core * rh, CAST_ROWS)
        ot0 = pl.multiple_of((1 - core) * rh, CAST_ROWS)

        ld = pltpu.make_async_copy(g_hbm.at[:, pl.ds(my0, rh), :], mine, lsem)
        ld.start()
        ld.wait()
        for j in range(4):
            ldj = pltpu.make_async_copy(g_hbm.at[j, pl.ds(ot0, rh), :], tmp, lsem)
            ldj.start()
            ldj.wait()

            def cast(i, carry, j=j):
                r0 = pl.multiple_of(i * CAST_ROWS, CAST_ROWS)
                sbuf[j, pl.ds(r0, CAST_ROWS), :] = tmp[pl.ds(r0, CAST_ROWS), :].astype(BF16)
                return carry

            lax.fori_loop(0, nsteps, cast, 0)

        to_sib = pltpu.make_async_remote_copy(src_ref=sbuf, dst_ref=rbuf_a, send_sem=a_send, recv_sem=a_recv,
                                              device_id=sibling, device_id_type=MESH)
        to_sib.start()
        to_sib.wait()

        for j in range(4):
            def add(i, carry, j=j):
                r0 = pl.multiple_of(i * CAST_ROWS, CAST_ROWS)
                s = mine[j, pl.ds(r0, CAST_ROWS), :] + rbuf_a[j, pl.ds(r0, CAST_ROWS), :].astype(F32)
                mine[j, pl.ds(r0, CAST_ROWS), :] = s
                sbuf[j, pl.ds(r0, CAST_ROWS), :] = s.astype(BF16)
                return carry

            lax.fori_loop(0, nsteps, add, 0)

        sends = []
        for d, (px, py) in enumerate(chips):
            cp = pltpu.make_async_remote_copy(src_ref=sbuf.at[2 * px + py], dst_ref=rbuf_b.at[d], send_sem=b_send.at[d],
                                              recv_sem=b_recv.at[d], device_id=(px, py, core), device_id_type=MESH)
            cp.start()
            sends.append(cp)
        for cp in sends:
            cp.wait()

        def fin(i, carry):
            r0 = pl.multiple_of(i * CAST_ROWS, CAST_ROWS)
            s = mine[k, pl.ds(r0, CAST_ROWS), :]
            for d in range(3):
                s = s + rbuf_b[d, pl.ds(r0, CAST_ROWS), :].astype(F32)
            out_ref[pl.ds(my0 + r0, CAST_ROWS), :] = s
            return carry

        lax.fori_loop(0, nsteps, fin, 0)
        half = out_ref.at[pl.ds(my0, rh), :]
        swap = pltpu.make_async_remote_copy(src_ref=half, dst_ref=half, send_sem=c_send, recv_sem=c_recv,
                                            device_id=sibling, device_id_type=MESH)
        swap.start()
        swap.wait()

    need = (_nbytes((4, rh, c), F32) + _nbytes((rh, c), F32) + 2 * _nbytes((4, rh, c), BF16) + _nbytes((3, rh, c), BF16)
            + _nbytes((r, c), F32))
    return pl.pallas_call(
        body, name=name, out_shape=jax.ShapeDtypeStruct((r, c), F32),
        in_specs=[pl.BlockSpec(memory_space=pl.ANY)], out_specs=pl.BlockSpec(memory_space=pltpu.VMEM),
        scratch_shapes=[pltpu.VMEM((4, rh, c), F32), pltpu.VMEM((rh, c), F32), pltpu.VMEM((4, rh, c), BF16),
                        pltpu.VMEM((4, rh, c), BF16), pltpu.VMEM((3, rh, c), BF16),
                        pltpu.SemaphoreType.DMA, pltpu.SemaphoreType.DMA, pltpu.SemaphoreType.DMA((3,)),
                        pltpu.SemaphoreType.DMA((3,)), pltpu.SemaphoreType.DMA, pltpu.SemaphoreType.DMA,
                        pltpu.SemaphoreType.DMA],
        compiler_params=pltpu.CompilerParams(vmem_limit_bytes=_limit(need * 1.2)),
    )(g4)


def _cols_from_shards(g):
    n, K, c = g.shape
    return g.transpose(1, 0, 2).reshape(K, n * c)


def _cols_to_shards(w):
    K, N = w.shape
    return w.reshape(K, 4, N // 4).transpose(1, 0, 2)


def _pad_heads_cols(w, width, lane0=0):
    K = w.shape[0]
    w3 = w.reshape(K, H, width)
    return jnp.pad(w3, ((0, 0), (0, 0), (lane0, HP - lane0 - width))).reshape(K, H * HP)


def _unpad_heads_cols(w, width, lane0=0):
    K = w.shape[0]
    return w.reshape(K, H, HP)[:, :, lane0:lane0 + width].reshape(K, H * width)


def _pad_block(w, lane0=0):
    return jnp.pad(w, ((0, 0), (lane0, LANES - lane0 - w.shape[1])))


_IN_SPLITS = [512, 1024, 1536, 1544, 2312, 2568, 2600, 3624]


def _pad_w_in(w):
    fq, fk, fv, flog, cq, ckv, krin, gfox, gmla = jnp.split(w, _IN_SPLITS, axis=1)
    return jnp.concatenate([_pad_heads_cols(fq, FOX_HD), _pad_heads_cols(fk, FOX_HD), _pad_heads_cols(fv, FOX_HD),
                            cq, ckv, gfox, gmla, _pad_block(flog), _pad_block(krin, KRIN_LANE)], axis=1)


def _unpad_w_in(wp):
    qkv, rest = wp[:, :NQKV], wp[:, NQKV:]
    fq, fk, fv = (_unpad_heads_cols(qkv[:, i * H * HP:(i + 1) * H * HP], FOX_HD) for i in range(3))
    return jnp.concatenate([fq, fk, fv, rest[:, OFF_FLOG:OFF_FLOG + H], rest[:, OFF_CQ:OFF_CQ + Q_LORA],
                            rest[:, OFF_CKV:OFF_CKV + KV_LORA], rest[:, OFF_KRIN + KRIN_LANE:OFF_KRIN + KRIN_LANE + MLA_ROPE],
                            rest[:, OFF_GFOX:OFF_GFOX + D], rest[:, OFF_GMLA:OFF_GMLA + D]], axis=1)


def _pad_w_ukv(w):
    w3 = w.reshape(KV_LORA, H, MLA_NOPE + MLA_V)
    kp = jnp.pad(w3[:, :, :MLA_NOPE], ((0, 0), (0, 0), (0, HP - MLA_NOPE))).reshape(KV_LORA, H * HP)
    vp = jnp.pad(w3[:, :, MLA_NOPE:], ((0, 0), (0, 0), (0, HP - MLA_V))).reshape(KV_LORA, H * HP)
    return jnp.concatenate([kp, vp], axis=1)


def _unpad_w_ukv(wp):
    kp = wp[:, :H * HP].reshape(KV_LORA, H, HP)[:, :, :MLA_NOPE]
    vp = wp[:, H * HP:].reshape(KV_LORA, H, HP)[:, :, :MLA_V]
    return jnp.concatenate([kp, vp], axis=2).reshape(KV_LORA, H * (MLA_NOPE + MLA_V))


def _pad_heads_rows(w, width):
    N = w.shape[1]
    return jnp.pad(w.reshape(H, width, N), ((0, 0), (0, HP - width), (0, 0))).reshape(H * HP, N)


def _unpad_heads_rows(w, width):
    N = w.shape[1]
    return w.reshape(H, HP, N)[:, :width, :].reshape(H * width, N)


def _rope_tables(positions):
    inv_freq = 1.0 / (ROPE_THETA ** (jnp.arange(0, MLA_ROPE, 2, dtype=F32) / MLA_ROPE))
    ang = positions.reshape(S, 1).astype(F32) * inv_freq
    cos, sin = jnp.cos(ang), jnp.sin(ang)
    ones = jnp.ones((S, KRIN_LANE), F32)
    tail = jnp.zeros((S, LANES - KRIN_LANE - MLA_ROPE), F32)
    ctab = jnp.concatenate([ones, cos, cos, tail], axis=1)
    stab = jnp.concatenate([0.0 * ones, -sin, sin, tail], axis=1)
    return ctab, stab


def _local_step(x, target, mod, positions, gains, bf, W, late=None):
    W = dict(W)
    sh1, sc1, gt1, sh2, sc2, gt2 = (mod[:, i * D:(i + 1) * D] for i in range(6))
    ops1, ops2 = 1.0 + sc1, 1.0 + sc2
    ones = lambda w: jnp.ones((1, w), F32)
    zeros = lambda w: jnp.zeros((1, w), F32)
    bf_blk = _pad_block(bf)
    ctab, stab = _rope_tables(positions)
    fox_scale = 1.0 / math.sqrt(FOX_HD)
    mla_scale = 1.0 / math.sqrt(MLA_NOPE + MLA_ROPE)

    h1 = _norm_mod(x, 0, D, gains["g_pre_mix"], ops1, sh1, name="f_pre_mix")
    qkv = _matmul(h1, W["w_in_qkv"], out_dtype=_MXU, name="f_proj_qkv", tm_cap=2048)
    rest = _matmul(h1, W["w_in_rest"], name="f_proj_rest", tm_cap=2048, tn_cap=256)
    F = _fox_prep(rest, bf_blk, name="f_fox_prep")
    Ft = F[:, :H].T
    fcol, frow = Ft.reshape(H, S, 1), Ft.reshape(H, S // TKF, 1, TKF)
    fox_heads = (0, H, 2 * H)
    job = _gather_level1_job([late[n] for n in _GATHER_A]) if late else None
    oa, lse_a, *landed = _attn_fwd(qkv, qkv, qkv, frow, scale=fox_scale, name="f_attn_fox", side=job, heads0=fox_heads)

    cqn = _norm_mod(rest, OFF_CQ // Q_LORA, Q_LORA, gains["g_q_lora"], ones(Q_LORA), zeros(Q_LORA), name="f_norm_cq")
    ckvn = _norm_mod(rest, OFF_CKV // KV_LORA, KV_LORA, gains["g_kv_lora"], ones(KV_LORA), zeros(KV_LORA), name="f_norm_ckv")
    qb = _matmul(cqn, W["w_uq"], name="f_uq")
    kvb = _matmul(ckvn, W["w_ukv"], name="f_ukv")
    qm, km, vm = _mla_assemble(qb, kvb, rest, ctab, stab, name="f_mla_assemble")
    job = _join_jobs(_gather_level2_job(landed), _gather_direct_job([late[n] for n in _GATHER_B])) if late else None
    ob, lse_b, *landed = _attn_fwd(qm, km, vm, None, scale=mla_scale, name="f_attn_mla", side=job)
    if late:
        W.update(_late_weights({n: g.reshape(4, 2 * g.shape[1], g.shape[2]) for n, g in zip(_GATHER_A + _GATHER_B, landed)}))

    pa = _matmul(oa, W["w_proj_fox"], name="f_proj_fox")
    pb = _matmul(ob, W["w_proj_mla"], name="f_proj_mla")
    merged = _merge(rest, pa, pb, name="f_merge")
    y1 = _matmul(merged, W["w_out"], name="f_out")
    x2 = _post_res(x, y1, gains["g_post_mix"], gt1, name="f_post_mix")
    h2 = _norm_mod(x2, 0, D, gains["g_pre_ffn"], ops2, sh2, name="f_pre_ffn")
    gu = _matmul(h2, W["w_ffn_in"], name="f_ffn_in", b_shards=True, tn_cap=1408)
    act = _swiglu(gu, name="f_swiglu")
    y2 = _matmul(act, W["w_ffn_out"], name="f_ffn_out", tk_cap=1408, tn_cap=1024)
    dout, loss = _post_res_loss(x2, y2, gains["g_post_ffn"], gt2, target, name="f_post_ffn_loss")

    dy2, s_gt2, s_gpost2 = _post_res_bwd(dout, y2, gains["g_post_ffn"], gt2, name="b_post_ffn")
    dact = _matmul(dy2, W["w_ffn_out"], tb=True, name="b_ffn_out_dx", tn_cap=1408)
    dW_ffn_out = _matmul(act, dy2, ta=True, name="b_ffn_out_dw", tm_cap=1408, tk_cap=2048)
    dgu = _swiglu_bwd(gu, dact, name="b_swiglu")
    dh2 = _matmul(dgu, W["w_ffn_in"], tb=True, b_shards=True, name="b_ffn_in_dx", tk_cap=1408, tn_cap=1024)
    dW_ffn_in = _matmul(h2, dgu, ta=True, name="b_ffn_in_dw", out_shards=True, tn_cap=1408, tk_cap=2048)
    dx2, s_sh2, s_a2 = _norm_mod_bwd(x2, 0, D, dh2, gains["g_pre_ffn"], ops2, dout, name="b_pre_ffn")
    dy1, s_gt1, s_gpost1 = _post_res_bwd(dx2, y1, gains["g_post_mix"], gt1, name="b_post_mix")
    dmerged = _matmul(dy1, W["w_out"], tb=True, name="b_out_dx")
    dW_out = _matmul(merged, dy1, ta=True, name="b_out_dw", tk_cap=2048)
    dpa, dpb, dgfox, dgmla = _merge_bwd(rest, pa, pb, dmerged, name="b_merge")
    doa = _matmul(dpa, W["w_proj_fox"], tb=True, name="b_proj_fox_dx")
    dW_proj_fox = _matmul(oa, dpa, ta=True, name="b_proj_fox_dw")
    dob = _matmul(dpb, W["w_proj_mla"], tb=True, name="b_proj_mla_dx")
    dW_proj_mla = _matmul(ob, dpb, ta=True, name="b_proj_mla_dw")

    delta_a, doa16 = _attn_delta(doa, oa, name="b_delta_fox")
    as_rows = lambda a: a.reshape(H, S // TQB, 1, TQB)
    dW = dict(w_proj_fox=dW_proj_fox, w_proj_mla=dW_proj_mla, w_out=dW_out, w_ffn_in=dW_ffn_in, w_ffn_out=dW_ffn_out)
    job_a = job_b = None
    if late:
        late_shards = _grad_shards(dW)
        pieces = {n: s.reshape(8, s.shape[1] // 2, s.shape[2]).astype(BF16) for n, s in late_shards.items()}
        job_a = _scatter_direct_job([pieces[n] for n in _SCATTER_A])
        job_b = _scatter_direct_job([pieces[n] for n in _SCATTER_B])
    dqa, dka, dva, *landed_a = _attn_bwd(qkv, qkv, qkv, doa16, as_rows(lse_a), as_rows(delta_a), fcol, scale=fox_scale,
                                         name="b_attn_fox", side=job_a, heads0=fox_heads)
    delta_b, dob16 = _attn_delta(dob, ob, name="b_delta_mla")
    dqm, dkm, dvm, *landed_b = _attn_bwd(qm, km, vm, dob16, as_rows(lse_b), as_rows(delta_b), None, scale=mla_scale,
                                         name="b_attn_mla", side=job_b)
    reduced = {}
    if late:
        order = _SCATTER_A + _SCATTER_B
        done = _scatter_finish([late_shards[n] for n in order], landed_a + landed_b, name="scatter_late")
        reduced = dict(zip(order, done))
        dW = {}

    dF = (dqa[:, ROW_SUM_LANE::HP] - dka[:, COL_SUM_LANE::HP]) * (1.0 / fox_scale)
    dflog, s_bf = _fox_bwd_prep(rest, bf_blk, _pad_block(dF), name="b_fox_prep")

    dqb, dkvb, dkrin = _mla_assemble_bwd(dqm, dkm, dvm, ctab, stab, name="b_mla_assemble")
    dcqn = _matmul(dqb, W["w_uq"], tb=True, name="b_uq_dx")
    dW_uq = _matmul(cqn, dqb, ta=True, name="b_uq_dw", tk_cap=2048)
    dckvn = _matmul(dkvb, W["w_ukv"], tb=True, name="b_ukv_dx")
    dW_ukv = _matmul(ckvn, dkvb, ta=True, name="b_ukv_dw", tk_cap=2048)
    dcq, _, s_gq = _norm_mod_bwd(rest, OFF_CQ // Q_LORA, Q_LORA, dcqn, gains["g_q_lora"], ones(Q_LORA), None, name="b_norm_cq")
    dckv, _, s_gkv = _norm_mod_bwd(rest, OFF_CKV // KV_LORA, KV_LORA, dckvn, gains["g_kv_lora"], ones(KV_LORA), None, name="b_norm_ckv")

    c16 = lambda a: a.astype(_MXU)
    dproj = jnp.concatenate([c16(dqa), c16(dka), c16(dva), c16(dcq), c16(dckv), dgfox, dgmla, c16(dflog), c16(dkrin)], axis=1)
    w_in_full = jnp.concatenate([W["w_in_qkv"], W["w_in_rest"]], axis=1)
    dh1 = _matmul(dproj, w_in_full, tb=True, name="b_in_dx", tk_cap=1280, tn_cap=1024)
    dW_in = _matmul(h1, dproj, ta=True, name="b_in_dw", tn_cap=640, tk_cap=2048)
    grad_x, s_sh1, s_a1 = _norm_mod_bwd(x, 0, D, dh1, gains["g_pre_mix"], ops1, dx2, name="b_pre_mix")

    dmod = jnp.concatenate([s_sh1, s_a1 * gains["g_pre_mix"], s_gt1, s_sh2, s_a2 * gains["g_pre_ffn"], s_gt2], axis=1)
    small = dict(dmod=dmod, g_pre_mix=s_a1 * ops1, g_post_mix=s_gpost1, g_pre_ffn=s_a2 * ops2, g_post_ffn=s_gpost2,
                 g_q_lora=s_gq, g_kv_lora=s_gkv, b_forget=s_bf)
    dW = dict(dW, w_in=dW_in, w_uq=dW_uq, w_ukv=dW_ukv)
    return loss, grad_x, dW, reduced, small


_BIG = ["w_in", "w_uq", "w_ukv", "w_proj_fox", "w_proj_mla", "w_out", "w_ffn_in", "w_ffn_out"]
_COL_SHARDED = {"w_in", "w_ukv", "w_proj_fox", "w_proj_mla", "w_ffn_in"}
_SMALL = ["b_ada", "g_pre_mix", "g_post_mix", "g_pre_ffn", "g_post_ffn", "b_forget", "g_q_lora", "g_kv_lora"]
_ORDER = ["w_ada", "b_ada", "g_pre_mix", "g_post_mix", "g_pre_ffn", "g_post_ffn", "w_in", "b_forget", "g_q_lora", "w_uq",
          "g_kv_lora", "w_ukv", "w_proj_fox", "w_proj_mla", "w_out", "w_ffn_in", "w_ffn_out"]
_ROW = {}
_off = 0
for _n, _w in [("dmod", 6 * D), ("g_pre_mix", D), ("g_post_mix", D), ("g_pre_ffn", D), ("g_post_ffn", D), ("g_q_lora", Q_LORA),
               ("g_kv_lora", KV_LORA), ("b_forget", LANES), ("loss", LANES)]:
    _ROW[_n] = (_off, _w)
    _off += _w
_ROW_LEN = _off


_EARLY = ["w_in", "w_uq", "w_ukv"]
_LATE = ["w_proj_fox", "w_proj_mla", "w_out", "w_ffn_in", "w_ffn_out"]
_GATHER_A = ["w_proj_fox", "w_proj_mla", "w_out", "w_ffn_in"]
_GATHER_B = ["w_ffn_out"]
_SCATTER_A = ["w_ffn_in"]
_SCATTER_B = ["w_ffn_out", "w_out", "w_proj_fox", "w_proj_mla"]


def _rows_from_shards(g):
    return g.reshape(-1, g.shape[2])


def _early_weights(G):
    w_in = _pad_w_in(_cols_from_shards(G["w_in"]))
    return dict(
        w_in_qkv=w_in[:, :NQKV], w_in_rest=w_in[:, NQKV:],
        w_uq=_pad_heads_cols(_rows_from_shards(G["w_uq"]), MLA_NOPE + MLA_ROPE),
        w_ukv=_pad_w_ukv(_cols_from_shards(G["w_ukv"])))


def _late_weights(G):
    return dict(
        w_proj_fox=_pad_heads_rows(_cols_from_shards(G["w_proj_fox"]), FOX_HD),
        w_proj_mla=_pad_heads_rows(_cols_from_shards(G["w_proj_mla"]), MLA_V),
        w_out=_rows_from_shards(G["w_out"]), w_ffn_in=G["w_ffn_in"], w_ffn_out=_rows_from_shards(G["w_ffn_out"]))


def _full_weights(G):
    return {**_early_weights(G), **_late_weights(G)}


_UNPAD = dict(
    w_in=_unpad_w_in, w_uq=lambda g: _unpad_heads_cols(g, MLA_NOPE + MLA_ROPE), w_ukv=_unpad_w_ukv,
    w_proj_fox=lambda g: _unpad_heads_rows(g, FOX_HD), w_proj_mla=lambda g: _unpad_heads_rows(g, MLA_V),
    w_out=lambda g: g, w_ffn_out=lambda g: g)


def _grad_shards(dW):
    out = {}
    for n, g in dW.items():
        if n == "w_ffn_in":
            out[n] = g
        else:
            nat = _UNPAD[n](g)
            out[n] = _cols_to_shards(nat) if n in _COL_SHARDED else nat.reshape(4, nat.shape[0] // 4, nat.shape[1])
    return out


def kernel(x, c, positions, w_ada, b_ada, g_pre_mix, g_post_mix, g_pre_ffn, g_post_ffn, w_in, b_forget, g_q_lora, w_uq, g_kv_lora, w_ukv, w_proj_fox, w_proj_mla, w_out, w_ffn_in, w_ffn_out, loss_target, m_w_ada, m_b_ada, m_g_pre_mix, m_g_post_mix, m_g_pre_ffn, m_g_post_ffn, m_w_in, m_b_forget, m_g_q_lora, m_w_uq, m_g_kv_lora, m_w_ukv, m_w_proj_fox, m_w_proj_mla, m_w_out, m_w_ffn_in, m_w_ffn_out, v_w_ada, v_b_ada, v_g_pre_mix, v_g_post_mix, v_g_pre_ffn, v_g_post_ffn, v_w_in, v_b_forget, v_g_q_lora, v_w_uq, v_g_kv_lora, v_w_ukv, v_w_proj_fox, v_w_proj_mla, v_w_out, v_w_ffn_in, v_w_ffn_out):
    P = dict(w_ada=w_ada, b_ada=b_ada, g_pre_mix=g_pre_mix, g_post_mix=g_post_mix, g_pre_ffn=g_pre_ffn, g_post_ffn=g_post_ffn,
             w_in=w_in, b_forget=b_forget, g_q_lora=g_q_lora, w_uq=w_uq, g_kv_lora=g_kv_lora, w_ukv=w_ukv,
             w_proj_fox=w_proj_fox, w_proj_mla=w_proj_mla, w_out=w_out, w_ffn_in=w_ffn_in, w_ffn_out=w_ffn_out)
    M = dict(w_ada=m_w_ada, b_ada=m_b_ada, g_pre_mix=m_g_pre_mix, g_post_mix=m_g_post_mix, g_pre_ffn=m_g_pre_ffn,
             g_post_ffn=m_g_post_ffn, w_in=m_w_in, b_forget=m_b_forget, g_q_lora=m_g_q_lora, w_uq=m_w_uq, g_kv_lora=m_g_kv_lora,
             w_ukv=m_w_ukv, w_proj_fox=m_w_proj_fox, w_proj_mla=m_w_proj_mla, w_out=m_w_out, w_ffn_in=m_w_ffn_in,
             w_ffn_out=m_w_ffn_out)
    V = dict(w_ada=v_w_ada, b_ada=v_b_ada, g_pre_mix=v_g_pre_mix, g_post_mix=v_g_post_mix, g_pre_ffn=v_g_pre_ffn,
             g_post_ffn=v_g_post_ffn, w_in=v_w_in, b_forget=v_b_forget, g_q_lora=v_g_q_lora, w_uq=v_w_uq, g_kv_lora=v_g_kv_lora,
             w_ukv=v_w_ukv, w_proj_fox=v_w_proj_fox, w_proj_mla=v_w_proj_mla, w_out=v_w_out, w_ffn_in=v_w_ffn_in,
             w_ffn_out=v_w_ffn_out)
    ax, ay, ac = lax.axis_index("x"), lax.axis_index("y"), lax.axis_index("c")
    chip = 2 * ax + ay
    me = 4 * ax + 2 * ay + ac
    n_ada = w_ada.shape[2]

    c_all = _all_gather_rows(jnp.pad(c, ((0, 7), (0, 0))), name="gather_c")[:, 0, :]
    c_all = jnp.pad(c_all, ((0, 8), (0, 0)))
    b_shard = lax.dynamic_slice(b_ada, (0, chip * n_ada), (1, n_ada))
    mod_blk, silu_c = _ada_mod(c_all, w_ada[0], b_shard, name="ada_mod")
    mod_all = _all_gather_rows(mod_blk, name="gather_mod")
    mod_mine = lax.dynamic_index_in_dim(mod_all, me, axis=1, keepdims=False)
    mod = lax.dynamic_index_in_dim(mod_mine.reshape(4, 2, n_ada), ac, axis=1, keepdims=False).reshape(1, 6 * D)

    W = _early_weights({n: _gather_weight(P[n][0], name="gather_" + n) for n in _EARLY})
    late = {}
    for n in _LATE:
        rh = P[n].shape[1] // 2
        late[n] = lax.dynamic_slice_in_dim(P[n][0], ac * rh, rh, axis=0).astype(BF16)

    gains = {n: P[n] for n in ["g_pre_mix", "g_post_mix", "g_pre_ffn", "g_post_ffn", "g_q_lora", "g_kv_lora"]}
    loss, grad_x, dW, grads, small = _local_step(x[0], loss_target[0], mod, positions, gains, b_forget, W, late)

    shards = _grad_shards(dW)
    grads.update({n: _reduce_scatter_weight(shards[n], name="scatter_" + n) for n in shards})

    small = dict(small, loss=_pad_block(loss))
    row = jnp.concatenate([small[n] for n in _ROW], axis=1)
    rows = _all_gather_rows(jnp.pad(row, ((0, 7), (0, 0))), name="gather_small")[:, 0, :]
    tot = _rowsum(rows, name="sum_small")
    piece = lambda n: tot[:, _ROW[n][0]:_ROW[n][0] + _ROW[n][1]]
    grads["b_ada"] = piece("dmod")
    for n in ["g_pre_mix", "g_post_mix", "g_pre_ffn", "g_post_ffn", "g_q_lora", "g_kv_lora"]:
        grads[n] = piece(n)
    grads["b_forget"] = piece("b_forget")[:, :H]
    loss_out = piece("loss")[0, 0]
    dmod_all = rows[:, _ROW["dmod"][0]:_ROW["dmod"][0] + 6 * D]
    dmod_shard = jnp.pad(lax.dynamic_slice(dmod_all, (0, chip * n_ada), (8, n_ada)), ((0, 8), (0, 0)))
    grads["w_ada"] = _matmul(silu_c, dmod_shard, ta=True, name="ada_dw")

    delta, new_m, new_v = {}, {}, {}
    for n in ["w_ada"] + _BIG:
        delta[n], new_m[n], new_v[n] = _adamw(P[n], grads[n], M[n], V[n], name="adamw_" + n)
    cat = lambda T: jnp.concatenate([T[n] for n in _SMALL], axis=1)
    d_s, m_s, v_s = (t[0] for t in _adamw(cat(P)[None], cat(grads), cat(M)[None], cat(V)[None], name="adamw_small"))
    o = 0
    for n in _SMALL:
        wdt = P[n].shape[1]
        delta[n], new_m[n], new_v[n] = d_s[:, o:o + wdt], m_s[:, o:o + wdt], v_s[:, o:o + wdt]
        o += wdt

    def shaped(T, n):
        return T[n].reshape(P[n].shape)

    return (loss_out, grad_x[None], *[shaped(grads, n) for n in _ORDER], *[shaped(delta, n) for n in _ORDER],
            *[shaped(new_m, n) for n in _ORDER], *[shaped(new_v, n) for n in _ORDER])
```

```python
import functools
import math
from typing import Callable, NamedTuple

import jax
import jax.numpy as jnp
from jax import lax
from jax.experimental import pallas as pl
from jax.experimental.pallas import tpu as pltpu

F32 = jnp.float32
BF16 = jnp.bfloat16
_MXU = jnp.bfloat16

S = 2048
D = 1024
H = 8
HP = 128
FOX_HD = 64
MLA_NOPE = 64
MLA_ROPE = 32
MLA_V = 64
Q_LORA = 768
KV_LORA = 256
D_FF = 2816
NORM_EPS = 1e-6
ROPE_THETA = 10000.0
NEG = -1e30

ADAM_LR = 0.001
ADAM_B1 = 0.9
ADAM_B2 = 0.999
ADAM_EPS = 1e-08
ADAM_WD = 0.01
ADAM_STEP = 10

LANES = 128
VMEM_CAP = 60 * 1024 * 1024
MESH = pl.DeviceIdType.MESH

NQKV = 3 * H * HP
OFF_CQ = 0
OFF_CKV = Q_LORA
OFF_GFOX = 1024
OFF_GMLA = 2048
OFF_FLOG = 3072
OFF_KRIN = 3200
NREST = 3328
KRIN_LANE = 64
ROW_SUM_LANE = 64
COL_SUM_LANE = 65


def _limit(nbytes):
    return int(min(VMEM_CAP, nbytes * 1.25 + (4 << 20)))


def _nbytes(shape, dtype):
    n = 1
    for s in shape:
        n *= s
    return n * jnp.dtype(dtype).itemsize


def _pick(n, cap):
    best = None
    for t in range(LANES, min(n, cap) + 1, LANES):
        if n % t == 0:
            best = t
    return best if best is not None else n


def _pcall(body, *, out_shape, **kw):
    outs = jax.tree.map(lambda s: pltpu.HBM(s.shape, s.dtype), out_shape)
    call = pl.pallas_call(body, out_shape=outs, **kw)
    return lambda *args: call(*[pltpu.with_memory_space_constraint(a, pltpu.HBM) for a in args])


def _matmul(a, b, *, ta=False, tb=False, out_dtype=F32, name, tm_cap=1024, tn_cap=512, tk_cap=1024,
            b_shards=False, out_shards=False):
    if ta:
        K, M = a.shape
    else:
        M, K = a.shape
    if b_shards:
        _, R, cb = b.shape
        N, K2 = (R, 4 * cb) if tb else (4 * cb, R)
    elif tb:
        N, K2 = b.shape
    else:
        K2, N = b.shape
    assert K == K2, (a.shape, b.shape, ta, tb)
    tm = _pick(M, tm_cap)
    tn = _pick(N, tn_cap)
    tk = K if K <= tk_cap else _pick(K, tk_cap)
    nk = K // tk
    dims = (((0 if ta else 1,), (1 if tb else 0,)), ((), ()))

    def body(a_ref, b_ref, o_ref, acc_ref):
        k = pl.program_id(2)

        @pl.when(k == 0)
        def _():
            acc_ref[...] = jnp.zeros_like(acc_ref)

        acc_ref[...] += lax.dot_general(a_ref[...].astype(_MXU), b_ref[...].astype(_MXU), dims,
                                        preferred_element_type=F32)

        @pl.when(k == nk - 1)
        def _():
            o_ref[...] = acc_ref[...].astype(out_dtype)

    a_spec = pl.BlockSpec((tk, tm), lambda i, j, k: (k, i)) if ta else pl.BlockSpec((tm, tk), lambda i, j, k: (i, k))
    if b_shards and tb:
        assert cb % tk == 0
        per = cb // tk
        b_spec = pl.BlockSpec((None, tn, tk), lambda i, j, k: (k // per, j, k % per))
    elif b_shards:
        assert cb % tn == 0
        per = cb // tn
        b_spec = pl.BlockSpec((None, tk, tn), lambda i, j, k: (j // per, k, j % per))
    elif tb:
        b_spec = pl.BlockSpec((tn, tk), lambda i, j, k: (j, k))
    else:
        b_spec = pl.BlockSpec((tk, tn), lambda i, j, k: (k, j))
    if out_shards:
        assert (N // 4) % tn == 0
        pern = N // 4 // tn
        out_shape = jax.ShapeDtypeStruct((4, M, N // 4), out_dtype)
        out_spec = pl.BlockSpec((None, tm, tn), lambda i, j, k: (j // pern, i, j % pern))
    else:
        out_shape = jax.ShapeDtypeStruct((M, N), out_dtype)
        out_spec = pl.BlockSpec((tm, tn), lambda i, j, k: (i, j))
    need = (2 * _nbytes((tm, tk), a.dtype) + 2 * _nbytes((tk, tn), b.dtype) + 2 * _nbytes((tm, tn), out_dtype)
            + _nbytes((tm, tn), F32) * 2 + _nbytes((tm, tk), _MXU) + _nbytes((tk, tn), _MXU))
    return _pcall(
        body, name=name,
        out_shape=out_shape,
        grid=(M // tm, N // tn, nk),
        in_specs=[a_spec, b_spec],
        out_specs=out_spec,
        scratch_shapes=[pltpu.VMEM((tm, tn), F32)],
        compiler_params=pltpu.CompilerParams(dimension_semantics=("parallel", "parallel", "arbitrary"),
                                             vmem_limit_bytes=_limit(need)),
    )(a, b)


TM = 512
TM_FF = 256


def _vec(w):
    return pl.BlockSpec((1, w), lambda i: (0, 0))


def _rows(w, col=0, tm=TM):
    return pl.BlockSpec((tm, w), lambda i: (i, col))


def _row_params(need, carried=False):
    return pltpu.CompilerParams(dimension_semantics=("arbitrary" if carried else "parallel",),
                                vmem_limit_bytes=_limit(need))


def _norm_mod(x, col, w, g, ops, sh, *, name):
    def body(x_ref, g_ref, ops_ref, sh_ref, o_ref):
        xv = x_ref[...]
        r = lax.rsqrt(jnp.mean(xv * xv, axis=-1, keepdims=True) + NORM_EPS)
        o_ref[...] = (((xv * r) * g_ref[...]) * ops_ref[...] + sh_ref[...]).astype(o_ref.dtype)

    return _pcall(
        body, name=name, out_shape=jax.ShapeDtypeStruct((S, w), _MXU), grid=(S // TM,),
        in_specs=[_rows(w, col), _vec(w), _vec(w), _vec(w)], out_specs=_rows(w),
        compiler_params=_row_params(8 * _nbytes((TM, w), F32)),
    )(x, g, ops, sh)


def _norm_mod_bwd(x, col, w, dh, g, ops, dres, *, name):
    has_res = dres is not None

    def body(*refs):
        if has_res:
            x_ref, dh_ref, g_ref, ops_ref, dres_ref, dx_ref, s1_ref, s2_ref = refs
        else:
            x_ref, dh_ref, g_ref, ops_ref, dx_ref, s1_ref, s2_ref = refs
        i = pl.program_id(0)

        @pl.when(i == 0)
        def _():
            s1_ref[...] = jnp.zeros_like(s1_ref)
            s2_ref[...] = jnp.zeros_like(s2_ref)

        xv = x_ref[...]
        dhv = dh_ref[...]
        r = lax.rsqrt(jnp.mean(xv * xv, axis=-1, keepdims=True) + NORM_EPS)
        xn = xv * r
        dxn = dhv * (g_ref[...] * ops_ref[...])
        dx = r * (dxn - xn * jnp.mean(dxn * xn, axis=-1, keepdims=True))
        if has_res:
            dx = dx + dres_ref[...]
        dx_ref[...] = dx
        s1_ref[...] += jnp.sum(dhv, axis=0, keepdims=True)
        s2_ref[...] += jnp.sum(dhv * xn, axis=0, keepdims=True)

    in_specs = [_rows(w, col), _rows(w), _vec(w), _vec(w)] + ([_rows(w)] if has_res else [])
    args = (x, dh, g, ops) + ((dres,) if has_res else ())
    return _pcall(
        body, name=name,
        out_shape=(jax.ShapeDtypeStruct((S, w), F32), jax.ShapeDtypeStruct((1, w), F32), jax.ShapeDtypeStruct((1, w), F32)),
        grid=(S // TM,), in_specs=in_specs, out_specs=(_rows(w), _vec(w), _vec(w)),
        compiler_params=_row_params(12 * _nbytes((TM, w), F32), carried=True),
    )(*args)


def _post_res(xres, y, g, gt, *, name):
    def body(x_ref, y_ref, g_ref, gt_ref, o_ref):
        yv = y_ref[...]
        r = lax.rsqrt(jnp.mean(yv * yv, axis=-1, keepdims=True) + NORM_EPS)
        o_ref[...] = x_ref[...] + gt_ref[...] * ((yv * r) * g_ref[...])

    return _pcall(
        body, name=name, out_shape=jax.ShapeDtypeStruct((S, D), F32), grid=(S // TM,),
        in_specs=[_rows(D), _rows(D), _vec(D), _vec(D)], out_specs=_rows(D),
        compiler_params=_row_params(8 * _nbytes((TM, D), F32)),
    )(xres, y, g, gt)


def _post_res_loss(xres, y, g, gt, target, *, name):
    def body(x_ref, y_ref, g_ref, gt_ref, t_ref, dout_ref, loss_ref):
        i = pl.program_id(0)

        @pl.when(i == 0)
        def _():
            loss_ref[...] = jnp.zeros_like(loss_ref)

        yv = y_ref[...]
        r = lax.rsqrt(jnp.mean(yv * yv, axis=-1, keepdims=True) + NORM_EPS)
        out = x_ref[...] + gt_ref[...] * ((yv * r) * g_ref[...])
        err = out - t_ref[...]
        dout_ref[...] = err * (1.0 / D)
        per_row = jnp.mean(err * err, axis=-1, keepdims=True)
        loss_ref[...] += 0.5 * jnp.sum(per_row, axis=0, keepdims=True)

    return _pcall(
        body, name=name,
        out_shape=(jax.ShapeDtypeStruct((S, D), F32), jax.ShapeDtypeStruct((1, 1), F32)), grid=(S // TM,),
        in_specs=[_rows(D), _rows(D), _vec(D), _vec(D), _rows(D)],
        out_specs=(_rows(D), pl.BlockSpec((1, 1), lambda i: (0, 0))),
        compiler_params=_row_params(10 * _nbytes((TM, D), F32), carried=True),
    )(xres, y, g, gt, target)


def _post_res_bwd(dxn, y, g, gt, *, name):
    def body(d_ref, y_ref, g_ref, gt_ref, dy_ref, sgt_ref, sg_ref):
        i = pl.program_id(0)

        @pl.when(i == 0)
        def _():
            sgt_ref[...] = jnp.zeros_like(sgt_ref)
            sg_ref[...] = jnp.zeros_like(sg_ref)

        yv = y_ref[...]
        dv = d_ref[...]
        r = lax.rsqrt(jnp.mean(yv * yv, axis=-1, keepdims=True) + NORM_EPS)
        yn = yv * r
        dn = dv * gt_ref[...]
        dyn = dn * g_ref[...]
        dy_ref[...] = (r * (dyn - yn * jnp.mean(dyn * yn, axis=-1, keepdims=True))).astype(dy_ref.dtype)
        sgt_ref[...] += jnp.sum(dv * (yn * g_ref[...]), axis=0, keepdims=True)
        sg_ref[...] += jnp.sum(dn * yn, axis=0, keepdims=True)

    return _pcall(
        body, name=name,
        out_shape=(jax.ShapeDtypeStruct((S, D), _MXU), jax.ShapeDtypeStruct((1, D), F32), jax.ShapeDtypeStruct((1, D), F32)),
        grid=(S // TM,), in_specs=[_rows(D), _rows(D), _vec(D), _vec(D)], out_specs=(_rows(D), _vec(D), _vec(D)),
        compiler_params=_row_params(10 * _nbytes((TM, D), F32), carried=True),
    )(dxn, y, g, gt)


def _swiglu(gu, *, name):
    def body(g_ref, u_ref, o_ref):
        gv = g_ref[...]
        o_ref[...] = ((gv * jax.nn.sigmoid(gv)) * u_ref[...]).astype(o_ref.dtype)

    return _pcall(
        body, name=name, out_shape=jax.ShapeDtypeStruct((S, D_FF), _MXU), grid=(S // TM_FF,),
        in_specs=[_rows(D_FF, 0, TM_FF), _rows(D_FF, 1, TM_FF)], out_specs=_rows(D_FF, 0, TM_FF),
        compiler_params=_row_params(8 * _nbytes((TM_FF, D_FF), F32)),
    )(gu, gu)


def _swiglu_bwd(gu, dact, *, name):
    def body(g_ref, u_ref, d_ref, o_ref):
        gv = g_ref[...]
        dv = d_ref[...]
        sg = jax.nn.sigmoid(gv)
        o_ref[:, :D_FF] = (dv * u_ref[...] * (sg * (1.0 + gv * (1.0 - sg)))).astype(o_ref.dtype)
        o_ref[:, D_FF:] = (dv * (gv * sg)).astype(o_ref.dtype)

    return _pcall(
        body, name=name, out_shape=jax.ShapeDtypeStruct((S, 2 * D_FF), _MXU), grid=(S // TM_FF,),
        in_specs=[_rows(D_FF, 0, TM_FF), _rows(D_FF, 1, TM_FF), _rows(D_FF, 0, TM_FF)], out_specs=_rows(2 * D_FF, 0, TM_FF),
        compiler_params=_row_params(12 * _nbytes((TM_FF, D_FF), F32)),
    )(gu, gu, dact)


def _merge(rest, pa, pb, *, name):
    def body(ga_ref, gb_ref, pa_ref, pb_ref, o_ref):
        o_ref[...] = (jax.nn.sigmoid(ga_ref[...]) * pa_ref[...] + jax.nn.sigmoid(gb_ref[...]) * pb_ref[...]).astype(o_ref.dtype)

    return _pcall(
        body, name=name, out_shape=jax.ShapeDtypeStruct((S, D), _MXU), grid=(S // TM,),
        in_specs=[_rows(D, OFF_GFOX // D), _rows(D, OFF_GMLA // D), _rows(D), _rows(D)], out_specs=_rows(D),
        compiler_params=_row_params(10 * _nbytes((TM, D), F32)),
    )(rest, rest, pa, pb)


def _merge_bwd(rest, pa, pb, dm, *, name):
    def body(ga_ref, gb_ref, pa_ref, pb_ref, d_ref, dpa_ref, dpb_ref, dga_ref, dgb_ref):
        dv = d_ref[...]
        sa = jax.nn.sigmoid(ga_ref[...])
        sb = jax.nn.sigmoid(gb_ref[...])
        dpa_ref[...] = (dv * sa).astype(dpa_ref.dtype)
        dpb_ref[...] = (dv * sb).astype(dpb_ref.dtype)
        dga_ref[...] = (dv * pa_ref[...] * (sa * (1.0 - sa))).astype(dga_ref.dtype)
        dgb_ref[...] = (dv * pb_ref[...] * (sb * (1.0 - sb))).astype(dgb_ref.dtype)

    o = jax.ShapeDtypeStruct((S, D), _MXU)
    return _pcall(
        body, name=name, out_shape=(o, o, o, o), grid=(S // TM,),
        in_specs=[_rows(D, OFF_GFOX // D), _rows(D, OFF_GMLA // D), _rows(D), _rows(D), _rows(D)],
        out_specs=(_rows(D), _rows(D), _rows(D), _rows(D)),
        compiler_params=_row_params(16 * _nbytes((TM, D), F32)),
    )(rest, rest, pa, pb, dm)


SCAN = 256


def _split_dot(tri, x):
    hi = x.astype(_MXU)
    r1 = x - hi.astype(F32)
    mid = r1.astype(_MXU)
    lo = (r1 - mid.astype(F32)).astype(_MXU)
    dot = functools.partial(jnp.dot, preferred_element_type=F32)
    return dot(tri, hi) + dot(tri, mid) + dot(tri, lo)


def _fox_prep(rest, bf, *, name):
    def body(z_ref, b_ref, f_ref):
        lane = lax.broadcasted_iota(jnp.int32, (SCAN, LANES), 1)
        tri = (lax.broadcasted_iota(jnp.int32, (SCAN, SCAN), 1) <= lax.broadcasted_iota(jnp.int32, (SCAN, SCAN), 0)).astype(_MXU)
        carry = jnp.zeros((1, LANES), F32)
        for c in range(S // SCAN):
            z = z_ref[c * SCAN:(c + 1) * SCAN, :] + b_ref[...]
            lf = jnp.minimum(z, 0.0) - jnp.log(1.0 + jnp.exp(-jnp.abs(z)))
            lf = jnp.where(lane < H, lf, 0.0)
            cum = _split_dot(tri, lf) + carry
            f_ref[c * SCAN:(c + 1) * SCAN, :] = cum
            carry = cum[SCAN - 1:SCAN, :]

    return _pcall(
        body, name=name, out_shape=jax.ShapeDtypeStruct((S, LANES), F32), grid=(1,),
        in_specs=[pl.BlockSpec((S, LANES), lambda i: (0, OFF_FLOG // LANES)), pl.BlockSpec((1, LANES), lambda i: (0, 0))],
        out_specs=pl.BlockSpec((S, LANES), lambda i: (0, 0)),
        compiler_params=pltpu.CompilerParams(vmem_limit_bytes=_limit(8 * _nbytes((S, LANES), F32))),
    )(rest, bf)


def _fox_bwd_prep(rest, bf, dF, *, name):
    def body(z_ref, b_ref, d_ref, o_ref, db_ref):
        lane = lax.broadcasted_iota(jnp.int32, (SCAN, LANES), 1)
        tri = (lax.broadcasted_iota(jnp.int32, (SCAN, SCAN), 1) >= lax.broadcasted_iota(jnp.int32, (SCAN, SCAN), 0)).astype(_MXU)
        carry = jnp.zeros((1, LANES), F32)
        db = jnp.zeros((1, LANES), F32)
        for c in range(S // SCAN - 1, -1, -1):
            rc = _split_dot(tri, d_ref[c * SCAN:(c + 1) * SCAN, :]) + carry
            z = z_ref[c * SCAN:(c + 1) * SCAN, :] + b_ref[...]
            dz = jnp.where(lane < H, rc * jax.nn.sigmoid(-z), 0.0)
            o_ref[c * SCAN:(c + 1) * SCAN, :] = dz
            db = db + jnp.sum(dz, axis=0, keepdims=True)
            carry = rc[0:1, :]
        db_ref[...] = db

    return _pcall(
        body, name=name,
        out_shape=(jax.ShapeDtypeStruct((S, LANES), F32), jax.ShapeDtypeStruct((1, LANES), F32)), grid=(1,),
        in_specs=[pl.BlockSpec((S, LANES), lambda i: (0, OFF_FLOG // LANES)), pl.BlockSpec((1, LANES), lambda i: (0, 0)),
                  pl.BlockSpec((S, LANES), lambda i: (0, 0))],
        out_specs=(pl.BlockSpec((S, LANES), lambda i: (0, 0)), pl.BlockSpec((1, LANES), lambda i: (0, 0))),
        compiler_params=pltpu.CompilerParams(vmem_limit_bytes=_limit(10 * _nbytes((S, LANES), F32))),
    )(rest, bf, dF)


def _swap16(x):
    lane = lax.broadcasted_iota(jnp.int32, x.shape, 1)
    half = MLA_ROPE // 2
    sw = jnp.where(lane < KRIN_LANE + half, pltpu.roll(x, LANES - half, 1), pltpu.roll(x, half, 1))
    return jnp.where((lane >= KRIN_LANE) & (lane < KRIN_LANE + MLA_ROPE), sw, 0.0)


def _mla_assemble(qb, kvb, rest, ctab, stab, *, name):
    def body(q_ref, kk_ref, kv_ref, kr_ref, c_ref, s_ref, qo_ref, ko_ref, vo_ref):
        cv = c_ref[...]
        sv = s_ref[...]
        kr = kr_ref[...]
        kpe = kr * cv + _swap16(kr) * sv
        for h in range(H):
            sl = slice(h * HP, (h + 1) * HP)
            qh = q_ref[:, sl]
            qo_ref[:, sl] = (qh * cv + _swap16(qh) * sv).astype(qo_ref.dtype)
            ko_ref[:, sl] = (kk_ref[:, sl] + kpe).astype(ko_ref.dtype)
        vo_ref[...] = kv_ref[...].astype(vo_ref.dtype)

    o = jax.ShapeDtypeStruct((S, H * HP), _MXU)
    return _pcall(
        body, name=name, out_shape=(o, o, o), grid=(S // TM,),
        in_specs=[_rows(H * HP), _rows(H * HP, 0), _rows(H * HP, 1), _rows(LANES, OFF_KRIN // LANES), _rows(LANES), _rows(LANES)],
        out_specs=(_rows(H * HP), _rows(H * HP), _rows(H * HP)),
        compiler_params=_row_params(14 * _nbytes((TM, H * HP), F32)),
    )(qb, kvb, kvb, rest, ctab, stab)


def _mla_assemble_bwd(dq, dk, dv, ctab, stab, *, name):
    def body(dq_ref, dk_ref, dv_ref, c_ref, s_ref, dqo_ref, dkv_ref, dkr_ref):
        cv = c_ref[...]
        sv = s_ref[...]
        lane = lax.broadcasted_iota(jnp.int32, (TM, LANES), 1)
        dsum = jnp.zeros((TM, LANES), F32)
        for h in range(H):
            sl = slice(h * HP, (h + 1) * HP)
            dqh = dq_ref[:, sl]
            dqo_ref[:, sl] = (dqh * cv + _swap16(dqh * sv)).astype(dqo_ref.dtype)
            dsum = dsum + dk_ref[:, sl]
        dkv_ref[:, :H * HP] = dk_ref[...].astype(dkv_ref.dtype)
        dkv_ref[:, H * HP:] = dv_ref[...].astype(dkv_ref.dtype)
        dkr = dsum * cv + _swap16(dsum * sv)
        dkr_ref[...] = jnp.where((lane >= KRIN_LANE) & (lane < KRIN_LANE + MLA_ROPE), dkr, 0.0)

    return _pcall(
        body, name=name,
        out_shape=(jax.ShapeDtypeStruct((S, H * HP), _MXU), jax.ShapeDtypeStruct((S, 2 * H * HP), _MXU),
                   jax.ShapeDtypeStruct((S, LANES), F32)),
        grid=(S // TM,),
        in_specs=[_rows(H * HP), _rows(H * HP), _rows(H * HP), _rows(LANES), _rows(LANES)],
        out_specs=(_rows(H * HP), _rows(2 * H * HP), _rows(LANES)),
        compiler_params=_row_params(14 * _nbytes((TM, H * HP), F32)),
    )(dq, dk, dv, ctab, stab)


TQ = 512
TKF = 512
TKB = 512
TQB = 512
_NT = (((1,), (1,)), ((), ()))
_TN = (((0,), (0,)), ((), ()))


def _is_pow2(x):
    return math.frexp(x)[0] == 0.5


def _attn_fwd(q, k, v, frow, *, scale, name, side=None, heads0=(0, 0, 0)):
    has_decay = frow is not None
    fold = _is_pow2(scale)
    n_in = 4 if has_decay else 3
    n_side_in = len(side.inputs) if side else 0
    n_side_out = len(side.out_shapes) if side else 0

    def body(*refs):
        q_ref, k_ref, v_ref = refs[:3]
        fr_ref = refs[3] if has_decay else None
        side_in = refs[n_in:n_in + n_side_in]
        o_ref, lse_ref = refs[n_in + n_side_in:n_in + n_side_in + 2]
        side_out = refs[n_in + n_side_in + 2:n_in + n_side_in + 2 + n_side_out]
        side_scratch = refs[n_in + n_side_in + 2 + n_side_out:]
        i = pl.program_id(1)
        if side:
            @pl.when((pl.program_id(0) == 0) & (i == 0))
            def _():
                side.start(side_in, side_out, side_scratch)
        qv = q_ref[...]
        if fold:
            qv = (qv * scale).astype(qv.dtype)
        last = (i * TQ) // TKF

        def tile(j, carry, masked):
            m, l, acc = carry
            k0 = pl.multiple_of(j * TKF, TKF)
            kj = k_ref[pl.ds(k0, TKF), :]
            vj = v_ref[pl.ds(k0, TKF), :]
            s = lax.dot_general(qv, kj, _NT, preferred_element_type=F32)
            if not fold:
                s = s * scale
            if has_decay:
                s = s - fr_ref[0, j]
            if masked:
                rows = i * TQ + lax.broadcasted_iota(jnp.int32, (TQ, TKF), 0)
                cols = j * TKF + lax.broadcasted_iota(jnp.int32, (TQ, TKF), 1)
                s = jnp.where(cols <= rows, s, NEG)
            m_new = jnp.maximum(m, jnp.max(s, axis=-1, keepdims=True))
            alpha = jnp.exp(m - m_new)
            p = jnp.exp(s - m_new)
            l = alpha * l + jnp.sum(p, axis=-1, keepdims=True)
            acc = alpha * acc + jnp.dot(p.astype(_MXU), vj, preferred_element_type=F32)
            return m_new, l, acc

        init = (jnp.full((TQ, 1), NEG, F32), jnp.zeros((TQ, 1), F32), jnp.zeros((TQ, HP), F32))
        carry = lax.fori_loop(0, last, lambda j, c: tile(j, c, False), init)
        m, l, acc = tile(last, carry, True)
        o_ref[...] = acc / l
        lse_ref[0] = m + jnp.log(l)
        if side:
            @pl.when((pl.program_id(0) == H - 1) & (i == S // TQ - 1))
            def _():
                side.finish(side_in, side_out, side_scratch)

    q0, k0, v0 = heads0
    in_specs = [pl.BlockSpec((TQ, HP), lambda h, i: (i, h + q0)), pl.BlockSpec((S, HP), lambda h, i: (0, h + k0)),
                pl.BlockSpec((S, HP), lambda h, i: (0, h + v0))]
    args = (q, k, v)
    if has_decay:
        in_specs += [pl.BlockSpec((1, S // TKF, 1, TKF), lambda h, i: (h, 0, 0, 0))]
        args += (frow,)
    out_shape = (jax.ShapeDtypeStruct((S, H * HP), F32), jax.ShapeDtypeStruct((H, S, 1), F32))
    out_specs = (pl.BlockSpec((TQ, HP), lambda h, i: (i, h)), pl.BlockSpec((1, TQ, 1), lambda h, i: (h, i, 0)))
    extra = {}
    if side:
        anywhere = pl.BlockSpec(memory_space=pl.ANY)
        in_specs += [anywhere] * n_side_in
        args += tuple(side.inputs)
        out_shape += tuple(side.out_shapes)
        out_specs += (anywhere,) * n_side_out
        extra = dict(scratch_shapes=list(side.scratch),
                     input_output_aliases={n_in + a: 2 + b for a, b in side.aliases.items()})
    return _pcall(
        body, name=name, out_shape=out_shape, grid=(H, S // TQ), in_specs=in_specs, out_specs=out_specs,
        compiler_params=pltpu.CompilerParams(dimension_semantics=("arbitrary", "arbitrary") if side else ("parallel", "parallel"),
                                             vmem_limit_bytes=_limit(8 * _nbytes((S, HP), F32))),
        **extra,
    )(*args)


def _attn_delta(do, o, *, name):
    def body(do_ref, o_ref, dl_ref, dob_ref):
        for h in range(H):
            sl = slice(h * HP, (h + 1) * HP)
            dl_ref[h] = jnp.sum(do_ref[:, sl] * o_ref[:, sl], axis=-1, keepdims=True)
        dob_ref[...] = do_ref[...].astype(dob_ref.dtype)

    return _pcall(
        body, name=name,
        out_shape=(jax.ShapeDtypeStruct((H, S, 1), F32), jax.ShapeDtypeStruct((S, H * HP), _MXU)), grid=(S // TM,),
        in_specs=[_rows(H * HP), _rows(H * HP)],
        out_specs=(pl.BlockSpec((H, TM, 1), lambda i: (0, i, 0)), _rows(H * HP)),
        compiler_params=_row_params(8 * _nbytes((TM, H * HP), F32)),
    )(do, o)


def _attn_bwd(q, k, v, dob, lse_row, delta_row, fcol, *, scale, name, side=None, heads0=(0, 0, 0)):
    has_decay = fcol is not None
    fold = _is_pow2(scale)
    n_in = 7 if has_decay else 6
    n_side_in = len(side.inputs) if side else 0
    n_side_out = len(side.out_shapes) if side else 0

    def body(*refs):
        q_ref, k_ref, v_ref, do_ref, lse_ref, dl_ref = refs[:6]
        fc_ref = refs[6] if has_decay else None
        side_in = refs[n_in:n_in + n_side_in]
        dq_ref, dk_ref, dv_ref = refs[n_in + n_side_in:n_in + n_side_in + 3]
        side_out = refs[n_in + n_side_in + 3:n_in + n_side_in + 3 + n_side_out]
        dq_acc = refs[n_in + n_side_in + 3 + n_side_out]
        side_scratch = refs[n_in + n_side_in + 4 + n_side_out:]
        j = pl.program_id(1)
        if side:
            @pl.when((pl.program_id(0) == 0) & (j == 0))
            def _():
                side.start(side_in, side_out, side_scratch)

        @pl.when(j == 0)
        def _():
            dq_acc[...] = jnp.zeros_like(dq_acc)

        kj = k_ref[...]
        vj = v_ref[...]
        kjs = (kj * scale).astype(kj.dtype) if fold else kj
        if has_decay:
            klane = lax.broadcasted_iota(jnp.int32, (TKB, HP), 1)
            kj = jnp.where(klane == ROW_SUM_LANE, 1.0, kj).astype(kj.dtype)
        first = (j * TKB) // TQB

        def tile(t, carry, masked):
            dk, dv = carry
            r0 = pl.multiple_of(t * TQB, TQB)
            qi = q_ref[pl.ds(r0, TQB), :]
            doi = do_ref[pl.ds(r0, TQB), :]
            st = lax.dot_general(kjs, qi, _NT, preferred_element_type=F32)
            if not fold:
                st = st * scale
            if has_decay:
                st = st - fc_ref[0]
            if masked:
                keys = j * TKB + lax.broadcasted_iota(jnp.int32, (TKB, TQB), 0)
                qpos = t * TQB + lax.broadcasted_iota(jnp.int32, (TKB, TQB), 1)
                st = jnp.where(keys <= qpos, st, NEG)
            pt = jnp.exp(st - lse_ref[0, t])
            dv = dv + jnp.dot(pt.astype(_MXU), doi, preferred_element_type=F32)
            dpt = lax.dot_general(vj, doi, _NT, preferred_element_type=F32)
            dst = (pt * (dpt - dl_ref[0, t])).astype(_MXU)
            if has_decay:
                lane = lax.broadcasted_iota(jnp.int32, (TQB, HP), 1)
                qi = jnp.where(lane == COL_SUM_LANE, 1.0, qi).astype(qi.dtype)
            dk = dk + jnp.dot(dst, qi, preferred_element_type=F32)
            dq_acc[pl.ds(r0, TQB), :] += lax.dot_general(dst, kj, _TN, preferred_element_type=F32)
            return dk, dv

        zero = jnp.zeros((TKB, HP), F32)
        carry = tile(first, (zero, zero), True)
        dk, dv = lax.fori_loop(first + 1, S // TQB, lambda t, c: tile(t, c, False), carry)
        dk_ref[...] = dk * scale
        dv_ref[...] = dv

        @pl.when(j == S // TKB - 1)
        def _():
            dq_ref[...] = dq_acc[...] * scale

        if side:
            @pl.when((pl.program_id(0) == H - 1) & (j == S // TKB - 1))
            def _():
                side.finish(side_in, side_out, side_scratch)

    q0, k0, v0 = heads0
    head = pl.BlockSpec((S, HP), lambda h, j: (0, h))
    kv = pl.BlockSpec((TKB, HP), lambda h, j: (j, h))
    stat = pl.BlockSpec((1, S // TQB, 1, TQB), lambda h, j: (h, 0, 0, 0))
    in_specs = [pl.BlockSpec((S, HP), lambda h, j: (0, h + q0)), pl.BlockSpec((TKB, HP), lambda h, j: (j, h + k0)),
                pl.BlockSpec((TKB, HP), lambda h, j: (j, h + v0)), head, stat, stat]
    args = (q, k, v, dob, lse_row, delta_row)
    if has_decay:
        in_specs += [pl.BlockSpec((1, TKB, 1), lambda h, j: (h, j, 0))]
        args += (fcol,)
    o = jax.ShapeDtypeStruct((S, H * HP), F32)
    out_shape, out_specs, scratch, aliases = (o, o, o), (head, kv, kv), [pltpu.VMEM((S, HP), F32)], {}
    if side:
        anywhere = pl.BlockSpec(memory_space=pl.ANY)
        in_specs += [anywhere] * n_side_in
        args += tuple(side.inputs)
        out_shape += tuple(side.out_shapes)
        out_specs += (anywhere,) * n_side_out
        scratch += list(side.scratch)
        aliases = {n_in + a: 3 + b for a, b in side.aliases.items()}
    return _pcall(
        body, name=name, out_shape=out_shape, grid=(H, S // TKB), in_specs=in_specs,
        out_specs=out_specs, scratch_shapes=scratch, input_output_aliases=aliases,
        compiler_params=pltpu.CompilerParams(dimension_semantics=("arbitrary" if side else "parallel", "arbitrary"),
                                             vmem_limit_bytes=_limit(12 * _nbytes((S, HP), F32))),
    )(*args)


def _ada_mod(c_all, w_shard, b_shard, *, name):
    R = c_all.shape[0]
    N = w_shard.shape[1]
    tn = 512

    def body(c_ref, w_ref, b_ref, o_ref, sc_ref):
        cv = c_ref[...]
        sc = (cv * jax.nn.sigmoid(cv)).astype(_MXU)
        sc_ref[...] = sc
        o_ref[...] = jnp.dot(sc, w_ref[...].astype(_MXU), preferred_element_type=F32) + b_ref[...]

    return _pcall(
        body, name=name,
        out_shape=(jax.ShapeDtypeStruct((R, N), F32), jax.ShapeDtypeStruct((R, D), _MXU)), grid=(N // tn,),
        in_specs=[pl.BlockSpec((R, D), lambda j: (0, 0)), pl.BlockSpec((D, tn), lambda j: (0, j)), pl.BlockSpec((1, tn), lambda j: (0, j))],
        out_specs=(pl.BlockSpec((R, tn), lambda j: (0, j)), pl.BlockSpec((R, D), lambda j: (0, 0))),
        compiler_params=pltpu.CompilerParams(dimension_semantics=("arbitrary",), vmem_limit_bytes=_limit(6 * _nbytes((D, tn), F32))),
    )(c_all, w_shard, b_shard)


def _rowsum(x, *, name):
    R, L = x.shape

    def body(x_ref, o_ref):
        acc = x_ref[0:1, :]
        for r in range(1, R):
            acc = acc + x_ref[r:r + 1, :]
        o_ref[...] = acc

    return pl.pallas_call(body, name=name, out_shape=jax.ShapeDtypeStruct((1, L), F32),
                          in_specs=[pl.BlockSpec(memory_space=pltpu.VMEM)], out_specs=pl.BlockSpec(memory_space=pltpu.VMEM))(x)


def _adamw(w, g, m, v, *, name):
    _, R, C = w.shape
    tr = R
    for t in range(8, R + 1, 8):
        if R % t == 0 and t * C * 4 <= (1 << 20):
            tr = t

    def body(w_ref, g_ref, m_ref, v_ref, d_ref, mo_ref, vo_ref):
        gv = g_ref[...]
        m2 = ADAM_B1 * m_ref[...] + (1.0 - ADAM_B1) * gv
        v2 = ADAM_B2 * v_ref[...] + (1.0 - ADAM_B2) * (gv * gv)
        m_hat = m2 / (1.0 - ADAM_B1 ** ADAM_STEP)
        v_hat = v2 / (1.0 - ADAM_B2 ** ADAM_STEP)
        d_ref[...] = -ADAM_LR * (m_hat / (jnp.sqrt(v_hat) + ADAM_EPS) + ADAM_WD * w_ref[...])
        mo_ref[...] = m2
        vo_ref[...] = v2

    blk = pl.BlockSpec((None, tr, C), lambda i: (0, i, 0))
    o = jax.ShapeDtypeStruct((1, R, C), F32)
    return _pcall(
        body, name=name, out_shape=(o, o, o), grid=(R // tr,),
        in_specs=[blk, pl.BlockSpec((tr, C), lambda i: (i, 0)), blk, blk], out_specs=(blk, blk, blk),
        compiler_params=pltpu.CompilerParams(dimension_semantics=("parallel",), vmem_limit_bytes=_limit(20 * _nbytes((tr, C), F32))),
    )(w, g, m, v)


def _place():
    x, y, c = lax.axis_index("x"), lax.axis_index("y"), lax.axis_index("c")
    return x, y, c, [(1 - x, y), (x, 1 - y), (1 - x, 1 - y)]


def _two_level_gather(x_ref, out_ref, send_sems, recv_sems, local_sem):
    x, y, c, chips = _place()
    me, sibling = (x, y, c), (x, y, 1 - c)

    def blk(px, py, pc):
        return out_ref.at[4 * px + 2 * py + pc]

    def copy(k, block, to, src=None):
        return pltpu.make_async_remote_copy(
            src_ref=blk(*block) if src is None else src, dst_ref=blk(*block),
            send_sem=send_sems.at[k], recv_sem=recv_sems.at[k], device_id=to, device_id_type=MESH)

    mine = pltpu.make_async_copy(x_ref, blk(*me), local_sem)
    mine.start()
    first = [copy(0, me, sibling, src=x_ref)]
    first += [copy(1 + j, me, (*chip, c), src=x_ref) for j, chip in enumerate(chips)]
    for cp in first:
        cp.start()
    passed = [copy(4 + j, (*chip, c), sibling) for j, chip in enumerate(chips)]
    for j, chip in enumerate(chips):
        copy(1 + j, (*chip, c), me).wait_recv()
        passed[j].start()
    copy(0, sibling, me).wait_recv()
    for j, chip in enumerate(chips):
        copy(4 + j, (*chip, 1 - c), me).wait_recv()
    for cp in first + passed:
        cp.wait_send()
    mine.wait()


_GATHER_SEMS = [pltpu.SemaphoreType.DMA((7,)), pltpu.SemaphoreType.DMA((7,)), pltpu.SemaphoreType.DMA]


class _SideJob(NamedTuple):
    inputs: tuple
    out_shapes: tuple
    aliases: dict
    scratch: tuple
    start: Callable
    finish: Callable


def _block_index(px, py, pc):
    return 4 * px + 2 * py + pc


def _gather_level1_job(halves):
    nw = len(halves)

    def copies(ins, outs, scratch, n):
        sends, recvs, _ = scratch
        x, y, c, chips = _place()
        mine = outs[n].at[_block_index(x, y, c)]
        peers = [(x, y, 1 - c)] + [(*chip, c) for chip in chips]
        out = []
        for t, peer in enumerate(peers):
            sem = dict(send_sem=sends.at[4 * n + t], recv_sem=recvs.at[4 * n + t], device_id_type=MESH)
            landing = outs[n].at[_block_index(*peer)]
            out.append((pltpu.make_async_remote_copy(src_ref=ins[n], dst_ref=mine, device_id=peer, **sem),
                        pltpu.make_async_remote_copy(src_ref=landing, dst_ref=landing, device_id=peer, **sem)))
        local = pltpu.make_async_copy(ins[n], mine, scratch[2].at[n])
        return out, local

    def start(ins, outs, scratch):
        for n in range(nw):
            pairs, local = copies(ins, outs, scratch, n)
            local.start()
            for to, _ in pairs:
                to.start()

    def finish(ins, outs, scratch):
        for n in range(nw):
            pairs, local = copies(ins, outs, scratch, n)
            for to, frm in pairs:
                frm.wait_recv()
                to.wait_send()
            local.wait()

    return _SideJob(
        inputs=tuple(halves), out_shapes=tuple(jax.ShapeDtypeStruct((8,) + h.shape, h.dtype) for h in halves), aliases={},
        scratch=(pltpu.SemaphoreType.DMA((4 * nw,)), pltpu.SemaphoreType.DMA((4 * nw,)), pltpu.SemaphoreType.DMA((nw,))),
        start=start, finish=finish)


def _gather_direct_job(halves):
    nw = len(halves)

    def copies(ins, outs, scratch, n):
        sends, recvs, _ = scratch
        x, y, c, _ = _place()
        mine = outs[n].at[_block_index(x, y, c)]
        out = []
        for f, (fx, fy, fc) in enumerate(_FLIPS):
            peer = (_flipped(x, fx), _flipped(y, fy), _flipped(c, fc))
            sem = dict(send_sem=sends.at[7 * n + f], recv_sem=recvs.at[7 * n + f], device_id=peer, device_id_type=MESH)
            landing = outs[n].at[_block_index(*peer)]
            out.append((pltpu.make_async_remote_copy(src_ref=ins[n], dst_ref=mine, **sem),
                        pltpu.make_async_remote_copy(src_ref=landing, dst_ref=landing, **sem)))
        return out, pltpu.make_async_copy(ins[n], mine, scratch[2].at[n])

    def start(ins, outs, scratch):
        for n in range(nw):
            pairs, local = copies(ins, outs, scratch, n)
            local.start()
            for to, _ in pairs:
                to.start()

    def finish(ins, outs, scratch):
        for n in range(nw):
            pairs, local = copies(ins, outs, scratch, n)
            for to, frm in pairs:
                frm.wait_recv()
                to.wait_send()
            local.wait()

    return _SideJob(
        inputs=tuple(halves), out_shapes=tuple(jax.ShapeDtypeStruct((8,) + h.shape, h.dtype) for h in halves), aliases={},
        scratch=(pltpu.SemaphoreType.DMA((7 * nw,)), pltpu.SemaphoreType.DMA((7 * nw,)), pltpu.SemaphoreType.DMA((nw,))),
        start=start, finish=finish)


def _join_jobs(a, b):
    cut = (len(a.inputs), len(a.out_shapes), len(a.scratch))

    def parts(ins, outs, scratch):
        return ((ins[:cut[0]], outs[:cut[1]], scratch[:cut[2]]), (ins[cut[0]:], outs[cut[1]:], scratch[cut[2]:]))

    def start(ins, outs, scratch):
        pa, pb = parts(ins, outs, scratch)
        a.start(*pa)
        b.start(*pb)

    def finish(ins, outs, scratch):
        pa, pb = parts(ins, outs, scratch)
        a.finish(*pa)
        b.finish(*pb)

    aliases = dict(a.aliases)
    aliases.update({cut[0] + i: cut[1] + o for i, o in b.aliases.items()})
    return _SideJob(inputs=a.inputs + b.inputs, out_shapes=a.out_shapes + b.out_shapes, aliases=aliases,
                    scratch=a.scratch + b.scratch, start=start, finish=finish)


def _gather_level2_job(gathered):
    nw = len(gathered)

    def copies(outs, scratch, n):
        sends, recvs = scratch
        x, y, c, chips = _place()
        out = []
        for j, chip in enumerate(chips):
            sem = dict(send_sem=sends.at[3 * n + j], recv_sem=recvs.at[3 * n + j], device_id=(x, y, 1 - c), device_id_type=MESH)
            going = outs[n].at[_block_index(*chip, c)]
            landing = outs[n].at[_block_index(*chip, 1 - c)]
            out.append((pltpu.make_async_remote_copy(src_ref=going, dst_ref=going, **sem),
                        pltpu.make_async_remote_copy(src_ref=landing, dst_ref=landing, **sem)))
        return out

    def start(ins, outs, scratch):
        for n in range(nw):
            for to, _ in copies(outs, scratch, n):
                to.start()

    def finish(ins, outs, scratch):
        for n in range(nw):
            for to, frm in copies(outs, scratch, n):
                frm.wait_recv()
                to.wait_send()

    return _SideJob(
        inputs=tuple(gathered), out_shapes=tuple(jax.ShapeDtypeStruct(g.shape, g.dtype) for g in gathered),
        aliases={n: n for n in range(nw)},
        scratch=(pltpu.SemaphoreType.DMA((3 * nw,)), pltpu.SemaphoreType.DMA((3 * nw,))),
        start=start, finish=finish)


def _all_gather_rows(x, *, name):
    R, C = x.shape

    def body(x_ref, out_ref, send_sems, recv_sems, local_sem):
        _two_level_gather(x_ref, out_ref, send_sems, recv_sems, local_sem)

    return pl.pallas_call(
        body, name=name, out_shape=jax.ShapeDtypeStruct((8, R, C), x.dtype),
        in_specs=[pl.BlockSpec(memory_space=pltpu.VMEM)], out_specs=pl.BlockSpec(memory_space=pltpu.VMEM),
        scratch_shapes=list(_GATHER_SEMS),
        compiler_params=pltpu.CompilerParams(vmem_limit_bytes=_limit(10 * _nbytes((R, C), x.dtype))),
    )(x)


CAST_ROWS = 16
_FLIPS = [(fx, fy, fc) for fx in (0, 1) for fy in (0, 1) for fc in (0, 1)][1:]


def _flipped(v, bit):
    return 1 - v if bit else v


def _scatter_direct_job(pieces, rows=None, partly=None):
    nw = len(pieces)

    def band(ref):
        return ref if rows is None else ref.at[pl.ds(rows[0], rows[1]), :]

    def copies(ins, outs, scratch, n):
        sends, recvs = scratch
        x, y, c, _ = _place()
        out = []
        for f, (fx, fy, fc) in enumerate(_FLIPS):
            peer = (_flipped(x, fx), _flipped(y, fy), _flipped(c, fc))
            sem = dict(send_sem=sends.at[7 * n + f], recv_sem=recvs.at[7 * n + f], device_id=peer, device_id_type=MESH)
            landing = band(outs[n].at[f])
            out.append((pltpu.make_async_remote_copy(src_ref=band(ins[n].at[_block_index(*peer)]), dst_ref=landing, **sem),
                        pltpu.make_async_remote_copy(src_ref=landing, dst_ref=landing, **sem)))
        return out

    def start(ins, outs, scratch):
        for n in range(nw):
            for to, _ in copies(ins, outs, scratch, n):
                to.start()

    def finish(ins, outs, scratch):
        for n in range(nw):
            for to, frm in copies(ins, outs, scratch, n):
                frm.wait_recv()
                to.wait_send()

    return _SideJob(
        inputs=tuple(pieces) + tuple(partly or ()),
        out_shapes=tuple(jax.ShapeDtypeStruct((7,) + p.shape[1:], p.dtype) for p in pieces),
        aliases={nw + n: n for n in range(nw)} if partly else {},
        scratch=(pltpu.SemaphoreType.DMA((7 * nw,)), pltpu.SemaphoreType.DMA((7 * nw,))),
        start=start, finish=finish)


def _scatter_finish(g4s, landed, *, name):
    nw = len(g4s)
    dims = [g.shape[1:] for g in g4s]

    def body(*refs):
        g_refs, l_refs, out_refs, own = refs[:nw], refs[nw:2 * nw], refs[2 * nw:3 * nw], refs[3 * nw:4 * nw]
        load_sems, send_sems, recv_sems = refs[4 * nw:]
        x, y, core, _ = _place()
        k = 2 * x + y
        loads = []
        for n, (r, c) in enumerate(dims):
            my0 = pl.multiple_of(core * (r // 2), CAST_ROWS)
            ld = pltpu.make_async_copy(g_refs[n].at[k, pl.ds(my0, r // 2), :], own[n], load_sems.at[n])
            ld.start()
            loads.append(ld)
        swaps = []
        for n, (r, c) in enumerate(dims):
            rh = r // 2
            my0 = pl.multiple_of(core * rh, CAST_ROWS)
            loads[n].wait()

            def fin(i, carry, n=n, my0=my0):
                r0 = pl.multiple_of(i * CAST_ROWS, CAST_ROWS)
                s = own[n][pl.ds(r0, CAST_ROWS), :]
                for f in range(7):
                    s = s + l_refs[n][f, pl.ds(r0, CAST_ROWS), :].astype(F32)
                out_refs[n][pl.ds(my0 + r0, CAST_ROWS), :] = s
                return carry

            lax.fori_loop(0, rh // CAST_ROWS, fin, 0)
            half = out_refs[n].at[pl.ds(my0, rh), :]
            sw = pltpu.make_async_remote_copy(src_ref=half, dst_ref=half, send_sem=send_sems.at[n], recv_sem=recv_sems.at[n],
                                              device_id=(x, y, 1 - core), device_id_type=MESH)
            sw.start()
            swaps.append(sw)
        for sw in swaps:
            sw.wait()

    need = sum(_nbytes((7, r // 2, c), BF16) + _nbytes((r // 2, c), F32) + _nbytes((r, c), F32) for r, c in dims)
    vmem = pl.BlockSpec(memory_space=pltpu.VMEM)
    return pl.pallas_call(
        body, name=name, out_shape=tuple(jax.ShapeDtypeStruct((r, c), F32) for r, c in dims),
        in_specs=[pl.BlockSpec(memory_space=pl.ANY)] * nw + [vmem] * nw, out_specs=(vmem,) * nw,
        scratch_shapes=[pltpu.VMEM((r // 2, c), F32) for r, c in dims]
        + [pltpu.SemaphoreType.DMA((nw,)), pltpu.SemaphoreType.DMA((nw,)), pltpu.SemaphoreType.DMA((nw,))],
        compiler_params=pltpu.CompilerParams(vmem_limit_bytes=_limit(need * 1.1)),
    )(*g4s, *landed)


def _gather_weight(w, *, name):
    r, c = w.shape
    rh = r // 2
    assert rh % CAST_ROWS == 0

    def body(w_hbm, out_ref, tmp, xb, send_sems, recv_sems, local_sem):
        core = lax.axis_index("c")
        ld = pltpu.make_async_copy(w_hbm.at[pl.ds(pl.multiple_of(core * rh, CAST_ROWS), rh), :], tmp, local_sem)
        ld.start()
        ld.wait()

        def cast(i, carry):
            r0 = pl.multiple_of(i * CAST_ROWS, CAST_ROWS)
            xb[pl.ds(r0, CAST_ROWS), :] = tmp[pl.ds(r0, CAST_ROWS), :].astype(BF16)
            return carry

        lax.fori_loop(0, rh // CAST_ROWS, cast, 0)
        _two_level_gather(xb, out_ref, send_sems, recv_sems, local_sem)

    need = _nbytes((8, rh, c), BF16) + _nbytes((rh, c), F32) + _nbytes((rh, c), BF16)
    out = pl.pallas_call(
        body, name=name, out_shape=jax.ShapeDtypeStruct((8, rh, c), BF16),
        in_specs=[pl.BlockSpec(memory_space=pl.ANY)], out_specs=pl.BlockSpec(memory_space=pltpu.VMEM),
        scratch_shapes=[pltpu.VMEM((rh, c), F32), pltpu.VMEM((rh, c), BF16)] + list(_GATHER_SEMS),
        compiler_params=pltpu.CompilerParams(vmem_limit_bytes=_limit(need * 1.3)),
    )(w)
    return out.reshape(4, r, c)


def _reduce_scatter_weight(g4, *, name):
    _, r, c = g4.shape
    rh = r // 2
    assert rh % CAST_ROWS == 0
    nsteps = rh // CAST_ROWS

    def body(g_hbm, out_ref, mine, tmp, sbuf, rbuf_a, rbuf_b, a_send, a_recv, b_send, b_recv, c_send, c_recv, lsem):
        x, y, core, chips = _place()
        sibling = (x, y, 1 - core)
        k = 2 * x + y
        my0 = pl.multiple_of(core * rh, CAST_ROWS)
        ot0 = pl.multiple_of((1 - core) * rh, CAST_ROWS)

        ld = pltpu.make_async_copy(g_hbm.at[:, pl.ds(my0, rh), :], mine, lsem)
        ld.start()
        ld.wait()
        for j in range(4):
            ldj = pltpu.make_async_copy(g_hbm.at[j, pl.ds(ot0, rh), :], tmp, lsem)
            ldj.start()
            ldj.wait()

            def cast(i, carry, j=j):
                r0 = pl.multiple_of(i * CAST_ROWS, CAST_ROWS)
                sbuf[j, pl.ds(r0, CAST_ROWS), :] = tmp[pl.ds(r0, CAST_ROWS), :].astype(BF16)
                return carry

            lax.fori_loop(0, nsteps, cast, 0)

        to_sib = pltpu.make_async_remote_copy(src_ref=sbuf, dst_ref=rbuf_a, send_sem=a_send, recv_sem=a_recv,
                                              device_id=sibling, device_id_type=MESH)
        to_sib.start()
        to_sib.wait()

        for j in range(4):
            def add(i, carry, j=j):
                r0 = pl.multiple_of(i * CAST_ROWS, CAST_ROWS)
                s = mine[j, pl.ds(r0, CAST_ROWS), :] + rbuf_a[j, pl.ds(r0, CAST_ROWS), :].astype(F32)
                mine[j, pl.ds(r0, CAST_ROWS), :] = s
                sbuf[j, pl.ds(r0, CAST_ROWS), :] = s.astype(BF16)
                return carry

            lax.fori_loop(0, nsteps, add, 0)

        sends = []
        for d, (px, py) in enumerate(chips):
            cp = pltpu.make_async_remote_copy(src_ref=sbuf.at[2 * px + py], dst_ref=rbuf_b.at[d], send_sem=b_send.at[d],
                                              recv_sem=b_recv.at[d], device_id=(px, py, core), device_id_type=MESH)
            cp.start()
            sends.append(cp)
        for cp in sends:
            cp.wait()

        def fin(i, carry):
            r0 = pl.multiple_of(i * CAST_ROWS, CAST_ROWS)
            s = mine[k, pl.ds(r0, CAST_ROWS), :]
            for d in range(3):
                s = s + rbuf_b[d, pl.ds(r0, CAST_ROWS), :].astype(F32)
            out_ref[pl.ds(my0 + r0, CAST_ROWS), :] = s
            return carry

        lax.fori_loop(0, nsteps, fin, 0)
        half = out_ref.at[pl.ds(my0, rh), :]
        swap = pltpu.make_async_remote_copy(src_ref=half, dst_ref=half, send_sem=c_send, recv_sem=c_recv,
                                            device_id=sibling, device_id_type=MESH)
        swap.start()
        swap.wait()

    need = (_nbytes((4, rh, c), F32) + _nbytes((rh, c), F32) + 2 * _nbytes((4, rh, c), BF16) + _nbytes((3, rh, c), BF16)
            + _nbytes((r, c), F32))
    return pl.pallas_call(
        body, name=name, out_shape=jax.ShapeDtypeStruct((r, c), F32),
        in_specs=[pl.BlockSpec(memory_space=pl.ANY)], out_specs=pl.BlockSpec(memory_space=pltpu.VMEM),
        scratch_shapes=[pltpu.VMEM((4, rh, c), F32), pltpu.VMEM((rh, c), F32), pltpu.VMEM((4, rh, c), BF16),
                        pltpu.VMEM((4, rh, c), BF16), pltpu.VMEM((3, rh, c), BF16),
                        pltpu.SemaphoreType.DMA, pltpu.SemaphoreType.DMA, pltpu.SemaphoreType.DMA((3,)),
                        pltpu.SemaphoreType.DMA((3,)), pltpu.SemaphoreType.DMA, pltpu.SemaphoreType.DMA,
                        pltpu.SemaphoreType.DMA],
        compiler_params=pltpu.CompilerParams(vmem_limit_bytes=_limit(need * 1.2)),
    )(g4)


def _cols_from_shards(g):
    n, K, c = g.shape
    return g.transpose(1, 0, 2).reshape(K, n * c)


def _cols_to_shards(w):
    K, N = w.shape
    return w.reshape(K, 4, N // 4).transpose(1, 0, 2)


def _pad_heads_cols(w, width, lane0=0):
    K = w.shape[0]
    w3 = w.reshape(K, H, width)
    return jnp.pad(w3, ((0, 0), (0, 0), (lane0, HP - lane0 - width))).reshape(K, H * HP)


def _unpad_heads_cols(w, width, lane0=0):
    K = w.shape[0]
    return w.reshape(K, H, HP)[:, :, lane0:lane0 + width].reshape(K, H * width)


def _pad_block(w, lane0=0):
    return jnp.pad(w, ((0, 0), (lane0, LANES - lane0 - w.shape[1])))


_IN_SPLITS = [512, 1024, 1536, 1544, 2312, 2568, 2600, 3624]


def _pad_w_in(w):
    fq, fk, fv, flog, cq, ckv, krin, gfox, gmla = jnp.split(w, _IN_SPLITS, axis=1)
    return jnp.concatenate([_pad_heads_cols(fq, FOX_HD), _pad_heads_cols(fk, FOX_HD), _pad_heads_cols(fv, FOX_HD),
                            cq, ckv, gfox, gmla, _pad_block(flog), _pad_block(krin, KRIN_LANE)], axis=1)


def _unpad_w_in(wp):
    qkv, rest = wp[:, :NQKV], wp[:, NQKV:]
    fq, fk, fv = (_unpad_heads_cols(qkv[:, i * H * HP:(i + 1) * H * HP], FOX_HD) for i in range(3))
    return jnp.concatenate([fq, fk, fv, rest[:, OFF_FLOG:OFF_FLOG + H], rest[:, OFF_CQ:OFF_CQ + Q_LORA],
                            rest[:, OFF_CKV:OFF_CKV + KV_LORA], rest[:, OFF_KRIN + KRIN_LANE:OFF_KRIN + KRIN_LANE + MLA_ROPE],
                            rest[:, OFF_GFOX:OFF_GFOX + D], rest[:, OFF_GMLA:OFF_GMLA + D]], axis=1)


def _pad_w_ukv(w):
    w3 = w.reshape(KV_LORA, H, MLA_NOPE + MLA_V)
    kp = jnp.pad(w3[:, :, :MLA_NOPE], ((0, 0), (0, 0), (0, HP - MLA_NOPE))).reshape(KV_LORA, H * HP)
    vp = jnp.pad(w3[:, :, MLA_NOPE:], ((0, 0), (0, 0), (0, HP - MLA_V))).reshape(KV_LORA, H * HP)
    return jnp.concatenate([kp, vp], axis=1)


def _unpad_w_ukv(wp):
    kp = wp[:, :H * HP].reshape(KV_LORA, H, HP)[:, :, :MLA_NOPE]
    vp = wp[:, H * HP:].reshape(KV_LORA, H, HP)[:, :, :MLA_V]
    return jnp.concatenate([kp, vp], axis=2).reshape(KV_LORA, H * (MLA_NOPE + MLA_V))


def _pad_heads_rows(w, width):
    N = w.shape[1]
    return jnp.pad(w.reshape(H, width, N), ((0, 0), (0, HP - width), (0, 0))).reshape(H * HP, N)


def _unpad_heads_rows(w, width):
    N = w.shape[1]
    return w.reshape(H, HP, N)[:, :width, :].reshape(H * width, N)


def _rope_tables(positions):
    inv_freq = 1.0 / (ROPE_THETA ** (jnp.arange(0, MLA_ROPE, 2, dtype=F32) / MLA_ROPE))
    ang = positions.reshape(S, 1).astype(F32) * inv_freq
    cos, sin = jnp.cos(ang), jnp.sin(ang)
    ones = jnp.ones((S, KRIN_LANE), F32)
    tail = jnp.zeros((S, LANES - KRIN_LANE - MLA_ROPE), F32)
    ctab = jnp.concatenate([ones, cos, cos, tail], axis=1)
    stab = jnp.concatenate([0.0 * ones, -sin, sin, tail], axis=1)
    return ctab, stab


def _local_step(x, target, mod, positions, gains, bf, W, late=None):
    W = dict(W)
    sh1, sc1, gt1, sh2, sc2, gt2 = (mod[:, i * D:(i + 1) * D] for i in range(6))
    ops1, ops2 = 1.0 + sc1, 1.0 + sc2
    ones = lambda w: jnp.ones((1, w), F32)
    zeros = lambda w: jnp.zeros((1, w), F32)
    bf_blk = _pad_block(bf)
    ctab, stab = _rope_tables(positions)
    fox_scale = 1.0 / math.sqrt(FOX_HD)
    mla_scale = 1.0 / math.sqrt(MLA_NOPE + MLA_ROPE)

    h1 = _norm_mod(x, 0, D, gains["g_pre_mix"], ops1, sh1, name="f_pre_mix")
    qkv = _matmul(h1, W["w_in_qkv"], out_dtype=_MXU, name="f_proj_qkv", tm_cap=2048)
    rest = _matmul(h1, W["w_in_rest"], name="f_proj_rest", tm_cap=2048, tn_cap=256)
    F = _fox_prep(rest, bf_blk, name="f_fox_prep")
    Ft = F[:, :H].T
    fcol, frow = Ft.reshape(H, S, 1), Ft.reshape(H, S // TKF, 1, TKF)
    fox_heads = (0, H, 2 * H)
    job = _gather_level1_job([late[n] for n in _GATHER_A]) if late else None
    oa, lse_a, *landed = _attn_fwd(qkv, qkv, qkv, frow, scale=fox_scale, name="f_attn_fox", side=job, heads0=fox_heads)

    cqn = _norm_mod(rest, OFF_CQ // Q_LORA, Q_LORA, gains["g_q_lora"], ones(Q_LORA), zeros(Q_LORA), name="f_norm_cq")
    ckvn = _norm_mod(rest, OFF_CKV // KV_LORA, KV_LORA, gains["g_kv_lora"], ones(KV_LORA), zeros(KV_LORA), name="f_norm_ckv")
    qb = _matmul(cqn, W["w_uq"], name="f_uq")
    kvb = _matmul(ckvn, W["w_ukv"], name="f_ukv")
    qm, km, vm = _mla_assemble(qb, kvb, rest, ctab, stab, name="f_mla_assemble")
    job = _join_jobs(_gather_level2_job(landed), _gather_direct_job([late[n] for n in _GATHER_B])) if late else None
    ob, lse_b, *landed = _attn_fwd(qm, km, vm, None, scale=mla_scale, name="f_attn_mla", side=job)
    if late:
        W.update(_late_weights({n: g.reshape(4, 2 * g.shape[1], g.shape[2]) for n, g in zip(_GATHER_A + _GATHER_B, landed)}))

    pa = _matmul(oa, W["w_proj_fox"], name="f_proj_fox")
    pb = _matmul(ob, W["w_proj_mla"], name="f_proj_mla")
    merged = _merge(rest, pa, pb, name="f_merge")
    y1 = _matmul(merged, W["w_out"], name="f_out")
    x2 = _post_res(x, y1, gains["g_post_mix"], gt1, name="f_post_mix")
    h2 = _norm_mod(x2, 0, D, gains["g_pre_ffn"], ops2, sh2, name="f_pre_ffn")
    gu = _matmul(h2, W["w_ffn_in"], name="f_ffn_in", b_shards=True, tn_cap=1408)
    act = _swiglu(gu, name="f_swiglu")
    y2 = _matmul(act, W["w_ffn_out"], name="f_ffn_out", tk_cap=1408, tn_cap=1024)
    dout, loss = _post_res_loss(x2, y2, gains["g_post_ffn"], gt2, target, name="f_post_ffn_loss")

    dy2, s_gt2, s_gpost2 = _post_res_bwd(dout, y2, gains["g_post_ffn"], gt2, name="b_post_ffn")
    dact = _matmul(dy2, W["w_ffn_out"], tb=True, name="b_ffn_out_dx", tn_cap=1408)
    dW_ffn_out = _matmul(act, dy2, ta=True, name="b_ffn_out_dw", tm_cap=1408, tk_cap=2048)
    dgu = _swiglu_bwd(gu, dact, name="b_swiglu")
    dh2 = _matmul(dgu, W["w_ffn_in"], tb=True, b_shards=True, name="b_ffn_in_dx", tk_cap=1408, tn_cap=1024)
    dW_ffn_in = _matmul(h2, dgu, ta=True, name="b_ffn_in_dw", out_shards=True, tn_cap=1408, tk_cap=2048)
    dx2, s_sh2, s_a2 = _norm_mod_bwd(x2, 0, D, dh2, gains["g_pre_ffn"], ops2, dout, name="b_pre_ffn")
    dy1, s_gt1, s_gpost1 = _post_res_bwd(dx2, y1, gains["g_post_mix"], gt1, name="b_post_mix")
    dmerged = _matmul(dy1, W["w_out"], tb=True, name="b_out_dx")
    dW_out = _matmul(merged, dy1, ta=True, name="b_out_dw", tk_cap=2048)
    dpa, dpb, dgfox, dgmla = _merge_bwd(rest, pa, pb, dmerged, name="b_merge")
    doa = _matmul(dpa, W["w_proj_fox"], tb=True, name="b_proj_fox_dx")
    dW_proj_fox = _matmul(oa, dpa, ta=True, name="b_proj_fox_dw")
    dob = _matmul(dpb, W["w_proj_mla"], tb=True, name="b_proj_mla_dx")
    dW_proj_mla = _matmul(ob, dpb, ta=True, name="b_proj_mla_dw")

    delta_a, doa16 = _attn_delta(doa, oa, name="b_delta_fox")
    as_rows = lambda a: a.reshape(H, S // TQB, 1, TQB)
    dW = dict(w_proj_fox=dW_proj_fox, w_proj_mla=dW_proj_mla, w_out=dW_out, w_ffn_in=dW_ffn_in, w_ffn_out=dW_ffn_out)
    job_a = job_b = None
    if late:
        late_shards = _grad_shards(dW)
        pieces = {n: s.reshape(8, s.shape[1] // 2, s.shape[2]).astype(BF16) for n, s in late_shards.items()}
        first = [pieces[n] for n in _SCATTER_A]
        job_a = _scatter_direct_job(first, rows=(0, _SCATTER_SPLIT))
    dqa, dka, dva, *landed_a = _attn_bwd(qkv, qkv, qkv, doa16, as_rows(lse_a), as_rows(delta_a), fcol, scale=fox_scale,
                                         name="b_attn_fox", side=job_a, heads0=fox_heads)
    delta_b, dob16 = _attn_delta(dob, ob, name="b_delta_mla")
    if late:
        rest_rows = (_SCATTER_SPLIT, first[0].shape[1] - _SCATTER_SPLIT)
        job_b = _join_jobs(_scatter_direct_job(first, rows=rest_rows, partly=landed_a),
                           _scatter_direct_job([pieces[n] for n in _SCATTER_B]))
    dqm, dkm, dvm, *landed_b = _attn_bwd(qm, km, vm, dob16, as_rows(lse_b), as_rows(delta_b), None, scale=mla_scale,
                                         name="b_attn_mla", side=job_b)
    reduced = {}
    if late:
        order = _SCATTER_A + _SCATTER_B
        done = _scatter_finish([late_shards[n] for n in order], landed_b, name="scatter_late")
        reduced = dict(zip(order, done))
        dW = {}

    dF = (dqa[:, ROW_SUM_LANE::HP] - dka[:, COL_SUM_LANE::HP]) * (1.0 / fox_scale)
    dflog, s_bf = _fox_bwd_prep(rest, bf_blk, _pad_block(dF), name="b_fox_prep")

    dqb, dkvb, dkrin = _mla_assemble_bwd(dqm, dkm, dvm, ctab, stab, name="b_mla_assemble")
    dcqn = _matmul(dqb, W["w_uq"], tb=True, name="b_uq_dx")
    dW_uq = _matmul(cqn, dqb, ta=True, name="b_uq_dw", tk_cap=2048)
    dckvn = _matmul(dkvb, W["w_ukv"], tb=True, name="b_ukv_dx")
    dW_ukv = _matmul(ckvn, dkvb, ta=True, name="b_ukv_dw", tk_cap=2048)
    dcq, _, s_gq = _norm_mod_bwd(rest, OFF_CQ // Q_LORA, Q_LORA, dcqn, gains["g_q_lora"], ones(Q_LORA), None, name="b_norm_cq")
    dckv, _, s_gkv = _norm_mod_bwd(rest, OFF_CKV // KV_LORA, KV_LORA, dckvn, gains["g_kv_lora"], ones(KV_LORA), None, name="b_norm_ckv")

    c16 = lambda a: a.astype(_MXU)
    dproj = jnp.concatenate([c16(dqa), c16(dka), c16(dva), c16(dcq), c16(dckv), dgfox, dgmla, c16(dflog), c16(dkrin)], axis=1)
    w_in_full = jnp.concatenate([W["w_in_qkv"], W["w_in_rest"]], axis=1)
    dh1 = _matmul(dproj, w_in_full, tb=True, name="b_in_dx", tk_cap=1280, tn_cap=1024)
    dW_in = _matmul(h1, dproj, ta=True, name="b_in_dw", tn_cap=640, tk_cap=2048)
    grad_x, s_sh1, s_a1 = _norm_mod_bwd(x, 0, D, dh1, gains["g_pre_mix"], ops1, dx2, name="b_pre_mix")

    dmod = jnp.concatenate([s_sh1, s_a1 * gains["g_pre_mix"], s_gt1, s_sh2, s_a2 * gains["g_pre_ffn"], s_gt2], axis=1)
    small = dict(dmod=dmod, g_pre_mix=s_a1 * ops1, g_post_mix=s_gpost1, g_pre_ffn=s_a2 * ops2, g_post_ffn=s_gpost2,
                 g_q_lora=s_gq, g_kv_lora=s_gkv, b_forget=s_bf)
    dW = dict(dW, w_in=dW_in, w_uq=dW_uq, w_ukv=dW_ukv)
    return loss, grad_x, dW, reduced, small


_BIG = ["w_in", "w_uq", "w_ukv", "w_proj_fox", "w_proj_mla", "w_out", "w_ffn_in", "w_ffn_out"]
_COL_SHARDED = {"w_in", "w_ukv", "w_proj_fox", "w_proj_mla", "w_ffn_in"}
_SMALL = ["b_ada", "g_pre_mix", "g_post_mix", "g_pre_ffn", "g_post_ffn", "b_forget", "g_q_lora", "g_kv_lora"]
_ORDER = ["w_ada", "b_ada", "g_pre_mix", "g_post_mix", "g_pre_ffn", "g_post_ffn", "w_in", "b_forget", "g_q_lora", "w_uq",
          "g_kv_lora", "w_ukv", "w_proj_fox", "w_proj_mla", "w_out", "w_ffn_in", "w_ffn_out"]
_ROW = {}
_off = 0
for _n, _w in [("dmod", 6 * D), ("g_pre_mix", D), ("g_post_mix", D), ("g_pre_ffn", D), ("g_post_ffn", D), ("g_q_lora", Q_LORA),
               ("g_kv_lora", KV_LORA), ("b_forget", LANES), ("loss", LANES)]:
    _ROW[_n] = (_off, _w)
    _off += _w
_ROW_LEN = _off


_EARLY = ["w_in", "w_uq", "w_ukv"]
_LATE = ["w_proj_fox", "w_proj_mla", "w_out", "w_ffn_in", "w_ffn_out"]
_GATHER_A = ["w_proj_fox", "w_proj_mla", "w_out", "w_ffn_in"]
_GATHER_B = ["w_ffn_out"]
_SCATTER_A = ["w_ffn_in"]
_SCATTER_SPLIT = 448
_SCATTER_B = ["w_ffn_out", "w_out", "w_proj_fox", "w_proj_mla"]


def _rows_from_shards(g):
    return g.reshape(-1, g.shape[2])


def _early_weights(G):
    w_in = _pad_w_in(_cols_from_shards(G["w_in"]))
    return dict(
        w_in_qkv=w_in[:, :NQKV], w_in_rest=w_in[:, NQKV:],
        w_uq=_pad_heads_cols(_rows_from_shards(G["w_uq"]), MLA_NOPE + MLA_ROPE),
        w_ukv=_pad_w_ukv(_cols_from_shards(G["w_ukv"])))


def _late_weights(G):
    return dict(
        w_proj_fox=_pad_heads_rows(_cols_from_shards(G["w_proj_fox"]), FOX_HD),
        w_proj_mla=_pad_heads_rows(_cols_from_shards(G["w_proj_mla"]), MLA_V),
        w_out=_rows_from_shards(G["w_out"]), w_ffn_in=G["w_ffn_in"], w_ffn_out=_rows_from_shards(G["w_ffn_out"]))


def _full_weights(G):
    return {**_early_weights(G), **_late_weights(G)}


_UNPAD = dict(
    w_in=_unpad_w_in, w_uq=lambda g: _unpad_heads_cols(g, MLA_NOPE + MLA_ROPE), w_ukv=_unpad_w_ukv,
    w_proj_fox=lambda g: _unpad_heads_rows(g, FOX_HD), w_proj_mla=lambda g: _unpad_heads_rows(g, MLA_V),
    w_out=lambda g: g, w_ffn_out=lambda g: g)


def _grad_shards(dW):
    out = {}
    for n, g in dW.items():
        if n == "w_ffn_in":
            out[n] = g
        else:
            nat = _UNPAD[n](g)
            out[n] = _cols_to_shards(nat) if n in _COL_SHARDED else nat.reshape(4, nat.shape[0] // 4, nat.shape[1])
    return out


def kernel(x, c, positions, w_ada, b_ada, g_pre_mix, g_post_mix, g_pre_ffn, g_post_ffn, w_in, b_forget, g_q_lora, w_uq, g_kv_lora, w_ukv, w_proj_fox, w_proj_mla, w_out, w_ffn_in, w_ffn_out, loss_target, m_w_ada, m_b_ada, m_g_pre_mix, m_g_post_mix, m_g_pre_ffn, m_g_post_ffn, m_w_in, m_b_forget, m_g_q_lora, m_w_uq, m_g_kv_lora, m_w_ukv, m_w_proj_fox, m_w_proj_mla, m_w_out, m_w_ffn_in, m_w_ffn_out, v_w_ada, v_b_ada, v_g_pre_mix, v_g_post_mix, v_g_pre_ffn, v_g_post_ffn, v_w_in, v_b_forget, v_g_q_lora, v_w_uq, v_g_kv_lora, v_w_ukv, v_w_proj_fox, v_w_proj_mla, v_w_out, v_w_ffn_in, v_w_ffn_out):
    P = dict(w_ada=w_ada, b_ada=b_ada, g_pre_mix=g_pre_mix, g_post_mix=g_post_mix, g_pre_ffn=g_pre_ffn, g_post_ffn=g_post_ffn,
             w_in=w_in, b_forget=b_forget, g_q_lora=g_q_lora, w_uq=w_uq, g_kv_lora=g_kv_lora, w_ukv=w_ukv,
             w_proj_fox=w_proj_fox, w_proj_mla=w_proj_mla, w_out=w_out, w_ffn_in=w_ffn_in, w_ffn_out=w_ffn_out)
    M = dict(w_ada=m_w_ada, b_ada=m_b_ada, g_pre_mix=m_g_pre_mix, g_post_mix=m_g_post_mix, g_pre_ffn=m_g_pre_ffn,
             g_post_ffn=m_g_post_ffn, w_in=m_w_in, b_forget=m_b_forget, g_q_lora=m_g_q_lora, w_uq=m_w_uq, g_kv_lora=m_g_kv_lora,
             w_ukv=m_w_ukv, w_proj_fox=m_w_proj_fox, w_proj_mla=m_w_proj_mla, w_out=m_w_out, w_ffn_in=m_w_ffn_in,
             w_ffn_out=m_w_ffn_out)
    V = dict(w_ada=v_w_ada, b_ada=v_b_ada, g_pre_mix=v_g_pre_mix, g_post_mix=v_g_post_mix, g_pre_ffn=v_g_pre_ffn,
             g_post_ffn=v_g_post_ffn, w_in=v_w_in, b_forget=v_b_forget, g_q_lora=v_g_q_lora, w_uq=v_w_uq, g_kv_lora=v_g_kv_lora,
             w_ukv=v_w_ukv, w_proj_fox=v_w_proj_fox, w_proj_mla=v_w_proj_mla, w_out=v_w_out, w_ffn_in=v_w_ffn_in,
             w_ffn_out=v_w_ffn_out)
    ax, ay, ac = lax.axis_index("x"), lax.axis_index("y"), lax.axis_index("c")
    chip = 2 * ax + ay
    me = 4 * ax + 2 * ay + ac
    n_ada = w_ada.shape[2]

    c_all = _all_gather_rows(jnp.pad(c, ((0, 7), (0, 0))), name="gather_c")[:, 0, :]
    c_all = jnp.pad(c_all, ((0, 8), (0, 0)))
    b_shard = lax.dynamic_slice(b_ada, (0, chip * n_ada), (1, n_ada))
    mod_blk, silu_c = _ada_mod(c_all, w_ada[0], b_shard, name="ada_mod")
    mod_all = _all_gather_rows(mod_blk, name="gather_mod")
    mod_mine = lax.dynamic_index_in_dim(mod_all, me, axis=1, keepdims=False)
    mod = lax.dynamic_index_in_dim(mod_mine.reshape(4, 2, n_ada), ac, axis=1, keepdims=False).reshape(1, 6 * D)

    W = _early_weights({n: _gather_weight(P[n][0], name="gather_" + n) for n in _EARLY})
    late = {}
    for n in _LATE:
        rh = P[n].shape[1] // 2
        late[n] = lax.dynamic_slice_in_dim(P[n][0], ac * rh, rh, axis=0).astype(BF16)

    gains = {n: P[n] for n in ["g_pre_mix", "g_post_mix", "g_pre_ffn", "g_post_ffn", "g_q_lora", "g_kv_lora"]}
    loss, grad_x, dW, grads, small = _local_step(x[0], loss_target[0], mod, positions, gains, b_forget, W, late)

    shards = _grad_shards(dW)
    grads.update({n: _reduce_scatter_weight(shards[n], name="scatter_" + n) for n in shards})

    small = dict(small, loss=_pad_block(loss))
    row = jnp.concatenate([small[n] for n in _ROW], axis=1)
    rows = _all_gather_rows(jnp.pad(row, ((0, 7), (0, 0))), name="gather_small")[:, 0, :]
    tot = _rowsum(rows, name="sum_small")
    piece = lambda n: tot[:, _ROW[n][0]:_ROW[n][0] + _ROW[n][1]]
    grads["b_ada"] = piece("dmod")
    for n in ["g_pre_mix", "g_post_mix", "g_pre_ffn", "g_post_ffn", "g_q_lora", "g_kv_lora"]:
        grads[n] = piece(n)
    grads["b_forget"] = piece("b_forget")[:, :H]
    loss_out = piece("loss")[0, 0]
    dmod_all = rows[:, _ROW["dmod"][0]:_ROW["dmod"][0] + 6 * D]
    dmod_shard = jnp.pad(lax.dynamic_slice(dmod_all, (0, chip * n_ada), (8, n_ada)), ((0, 8), (0, 0)))
    grads["w_ada"] = _matmul(silu_c, dmod_shard, ta=True, name="ada_dw")

    delta, new_m, new_v = {}, {}, {}
    for n in ["w_ada"] + _BIG:
        delta[n], new_m[n], new_v[n] = _adamw(P[n], grads[n], M[n], V[n], name="adamw_" + n)
    cat = lambda T: jnp.concatenate([T[n] for n in _SMALL], axis=1)
    d_s, m_s, v_s = (t[0] for t in _adamw(cat(P)[None], cat(grads), cat(M)[None], cat(V)[None], name="adamw_small"))
    o = 0
    for n in _SMALL:
        wdt = P[n].shape[1]
        delta[n], new_m[n], new_v[n] = d_s[:, o:o + wdt], m_s[:, o:o + wdt], v_s[:, o:o + wdt]
        o += wdt

    def shaped(T, n):
        return T[n].reshape(P[n].shape)

    return (loss_out, grad_x[None], *[shaped(grads, n) for n in _ORDER], *[shaped(delta, n) for n in _ORDER],
            *[shaped(new_m, n) for n in _ORDER], *[shaped(new_v, n) for n in _ORDER])
```

```python
import functools
import math
from typing import Callable, NamedTuple

import jax
import jax.numpy as jnp
from jax import lax
from jax.experimental import pallas as pl
from jax.experimental.pallas import tpu as pltpu

F32 = jnp.float32
BF16 = jnp.bfloat16
_MXU = jnp.bfloat16

S = 2048
D = 1024
H = 8
HP = 128
FOX_HD = 64
MLA_NOPE = 64
MLA_ROPE = 32
MLA_V = 64
Q_LORA = 768
KV_LORA = 256
D_FF = 2816
NORM_EPS = 1e-6
ROPE_THETA = 10000.0
NEG = -1e30

ADAM_LR = 0.001
ADAM_B1 = 0.9
ADAM_B2 = 0.999
ADAM_EPS = 1e-08
ADAM_WD = 0.01
ADAM_STEP = 10

LANES = 128
VMEM_CAP = 60 * 1024 * 1024
MESH = pl.DeviceIdType.MESH

NQKV = 3 * H * HP
OFF_CQ = 0
OFF_CKV = Q_LORA
OFF_GFOX = 1024
OFF_GMLA = 2048
OFF_FLOG = 3072
OFF_KRIN = 3200
NREST = 3328
KRIN_LANE = 64
ROW_SUM_LANE = 64
COL_SUM_LANE = 65


def _limit(nbytes):
    return int(min(VMEM_CAP, nbytes * 1.25 + (4 << 20)))


def _nbytes(shape, dtype):
    n = 1
    for s in shape:
        n *= s
    return n * jnp.dtype(dtype).itemsize


def _pick(n, cap):
    best = None
    for t in range(LANES, min(n, cap) + 1, LANES):
        if n % t == 0:
            best = t
    return best if best is not None else n


def _pcall(body, *, out_shape, **kw):
    outs = jax.tree.map(lambda s: pltpu.HBM(s.shape, s.dtype), out_shape)
    call = pl.pallas_call(body, out_shape=outs, **kw)
    return lambda *args: call(*[pltpu.with_memory_space_constraint(a, pltpu.HBM) for a in args])


def _matmul(a, b, *, ta=False, tb=False, out_dtype=F32, name, tm_cap=1024, tn_cap=512, tk_cap=1024,
            b_shards=False, out_shards=False, side=None):
    if ta:
        K, M = a.shape
    else:
        M, K = a.shape
    if b_shards:
        _, R, cb = b.shape
        N, K2 = (R, 4 * cb) if tb else (4 * cb, R)
    elif tb:
        N, K2 = b.shape
    else:
        K2, N = b.shape
    assert K == K2, (a.shape, b.shape, ta, tb)
    tm = _pick(M, tm_cap)
    tn = _pick(N, tn_cap)
    tk = K if K <= tk_cap else _pick(K, tk_cap)
    nk = K // tk
    dims = (((0 if ta else 1,), (1 if tb else 0,)), ((), ()))

    n_side_in = len(side.inputs) if side else 0
    n_side_out = len(side.out_shapes) if side else 0
    steps = (M // tm, N // tn, nk)

    def body(*refs):
        a_ref, b_ref = refs[:2]
        side_in = refs[2:2 + n_side_in]
        o_ref = refs[2 + n_side_in]
        side_out = refs[3 + n_side_in:3 + n_side_in + n_side_out]
        acc_ref = refs[3 + n_side_in + n_side_out]
        side_scratch = refs[4 + n_side_in + n_side_out:]
        k = pl.program_id(2)
        if side:
            @pl.when((pl.program_id(0) == 0) & (pl.program_id(1) == 0) & (k == 0))
            def _():
                side.start(side_in, side_out, side_scratch)

        @pl.when(k == 0)
        def _():
            acc_ref[...] = jnp.zeros_like(acc_ref)

        acc_ref[...] += lax.dot_general(a_ref[...].astype(_MXU), b_ref[...].astype(_MXU), dims,
                                        preferred_element_type=F32)

        @pl.when(k == nk - 1)
        def _():
            o_ref[...] = acc_ref[...].astype(out_dtype)

        if side:
            @pl.when((pl.program_id(0) == steps[0] - 1) & (pl.program_id(1) == steps[1] - 1) & (k == nk - 1))
            def _():
                side.finish(side_in, side_out, side_scratch)

    a_spec = pl.BlockSpec((tk, tm), lambda i, j, k: (k, i)) if ta else pl.BlockSpec((tm, tk), lambda i, j, k: (i, k))
    if b_shards and tb:
        assert cb % tk == 0
        per = cb // tk
        b_spec = pl.BlockSpec((None, tn, tk), lambda i, j, k: (k // per, j, k % per))
    elif b_shards:
        assert cb % tn == 0
        per = cb // tn
        b_spec = pl.BlockSpec((None, tk, tn), lambda i, j, k: (j // per, k, j % per))
    elif tb:
        b_spec = pl.BlockSpec((tn, tk), lambda i, j, k: (j, k))
    else:
        b_spec = pl.BlockSpec((tk, tn), lambda i, j, k: (k, j))
    if out_shards:
        assert (N // 4) % tn == 0
        pern = N // 4 // tn
        out_shape = jax.ShapeDtypeStruct((4, M, N // 4), out_dtype)
        out_spec = pl.BlockSpec((None, tm, tn), lambda i, j, k: (j // pern, i, j % pern))
    else:
        out_shape = jax.ShapeDtypeStruct((M, N), out_dtype)
        out_spec = pl.BlockSpec((tm, tn), lambda i, j, k: (i, j))
    need = (2 * _nbytes((tm, tk), a.dtype) + 2 * _nbytes((tk, tn), b.dtype) + 2 * _nbytes((tm, tn), out_dtype)
            + _nbytes((tm, tn), F32) * 2 + _nbytes((tm, tk), _MXU) + _nbytes((tk, tn), _MXU))
    in_specs, args, scratch, extra = [a_spec, b_spec], (a, b), [pltpu.VMEM((tm, tn), F32)], {}
    if side:
        anywhere = pl.BlockSpec(memory_space=pl.ANY)
        in_specs += [anywhere] * n_side_in
        args += tuple(side.inputs)
        out_shape = (out_shape,) + tuple(side.out_shapes)
        out_spec = (out_spec,) + (anywhere,) * n_side_out
        scratch += list(side.scratch)
        extra = dict(input_output_aliases={2 + i: 1 + o for i, o in side.aliases.items()})
    semantics = ("arbitrary",) * 3 if side else ("parallel", "parallel", "arbitrary")
    return _pcall(
        body, name=name, out_shape=out_shape, grid=steps, in_specs=in_specs, out_specs=out_spec, scratch_shapes=scratch,
        compiler_params=pltpu.CompilerParams(dimension_semantics=semantics, vmem_limit_bytes=_limit(need)), **extra,
    )(*args)


def _matmul_carrying(side, *args, **kw):
    out = _matmul(*args, side=side, **kw)
    return out if side else (out,)


TM = 512
TM_FF = 256


def _vec(w):
    return pl.BlockSpec((1, w), lambda i: (0, 0))


def _rows(w, col=0, tm=TM):
    return pl.BlockSpec((tm, w), lambda i: (i, col))


def _row_params(need, carried=False):
    return pltpu.CompilerParams(dimension_semantics=("arbitrary" if carried else "parallel",),
                                vmem_limit_bytes=_limit(need))


def _norm_mod(x, col, w, g, ops, sh, *, name):
    def body(x_ref, g_ref, ops_ref, sh_ref, o_ref):
        xv = x_ref[...]
        r = lax.rsqrt(jnp.mean(xv * xv, axis=-1, keepdims=True) + NORM_EPS)
        o_ref[...] = (((xv * r) * g_ref[...]) * ops_ref[...] + sh_ref[...]).astype(o_ref.dtype)

    return _pcall(
        body, name=name, out_shape=jax.ShapeDtypeStruct((S, w), _MXU), grid=(S // TM,),
        in_specs=[_rows(w, col), _vec(w), _vec(w), _vec(w)], out_specs=_rows(w),
        compiler_params=_row_params(8 * _nbytes((TM, w), F32)),
    )(x, g, ops, sh)


def _norm_mod_bwd(x, col, w, dh, g, ops, dres, *, name):
    has_res = dres is not None

    def body(*refs):
        if has_res:
            x_ref, dh_ref, g_ref, ops_ref, dres_ref, dx_ref, s1_ref, s2_ref = refs
        else:
            x_ref, dh_ref, g_ref, ops_ref, dx_ref, s1_ref, s2_ref = refs
        i = pl.program_id(0)

        @pl.when(i == 0)
        def _():
            s1_ref[...] = jnp.zeros_like(s1_ref)
            s2_ref[...] = jnp.zeros_like(s2_ref)

        xv = x_ref[...]
        dhv = dh_ref[...]
        r = lax.rsqrt(jnp.mean(xv * xv, axis=-1, keepdims=True) + NORM_EPS)
        xn = xv * r
        dxn = dhv * (g_ref[...] * ops_ref[...])
        dx = r * (dxn - xn * jnp.mean(dxn * xn, axis=-1, keepdims=True))
        if has_res:
            dx = dx + dres_ref[...]
        dx_ref[...] = dx
        s1_ref[...] += jnp.sum(dhv, axis=0, keepdims=True)
        s2_ref[...] += jnp.sum(dhv * xn, axis=0, keepdims=True)

    in_specs = [_rows(w, col), _rows(w), _vec(w), _vec(w)] + ([_rows(w)] if has_res else [])
    args = (x, dh, g, ops) + ((dres,) if has_res else ())
    return _pcall(
        body, name=name,
        out_shape=(jax.ShapeDtypeStruct((S, w), F32), jax.ShapeDtypeStruct((1, w), F32), jax.ShapeDtypeStruct((1, w), F32)),
        grid=(S // TM,), in_specs=in_specs, out_specs=(_rows(w), _vec(w), _vec(w)),
        compiler_params=_row_params(12 * _nbytes((TM, w), F32), carried=True),
    )(*args)


def _post_res(xres, y, g, gt, *, name):
    def body(x_ref, y_ref, g_ref, gt_ref, o_ref):
        yv = y_ref[...]
        r = lax.rsqrt(jnp.mean(yv * yv, axis=-1, keepdims=True) + NORM_EPS)
        o_ref[...] = x_ref[...] + gt_ref[...] * ((yv * r) * g_ref[...])

    return _pcall(
        body, name=name, out_shape=jax.ShapeDtypeStruct((S, D), F32), grid=(S // TM,),
        in_specs=[_rows(D), _rows(D), _vec(D), _vec(D)], out_specs=_rows(D),
        compiler_params=_row_params(8 * _nbytes((TM, D), F32)),
    )(xres, y, g, gt)


def _post_res_loss(xres, y, g, gt, target, *, name):
    def body(x_ref, y_ref, g_ref, gt_ref, t_ref, dout_ref, loss_ref):
        i = pl.program_id(0)

        @pl.when(i == 0)
        def _():
            loss_ref[...] = jnp.zeros_like(loss_ref)

        yv = y_ref[...]
        r = lax.rsqrt(jnp.mean(yv * yv, axis=-1, keepdims=True) + NORM_EPS)
        out = x_ref[...] + gt_ref[...] * ((yv * r) * g_ref[...])
        err = out - t_ref[...]
        dout_ref[...] = err * (1.0 / D)
        per_row = jnp.mean(err * err, axis=-1, keepdims=True)
        loss_ref[...] += 0.5 * jnp.sum(per_row, axis=0, keepdims=True)

    return _pcall(
        body, name=name,
        out_shape=(jax.ShapeDtypeStruct((S, D), F32), jax.ShapeDtypeStruct((1, 1), F32)), grid=(S // TM,),
        in_specs=[_rows(D), _rows(D), _vec(D), _vec(D), _rows(D)],
        out_specs=(_rows(D), pl.BlockSpec((1, 1), lambda i: (0, 0))),
        compiler_params=_row_params(10 * _nbytes((TM, D), F32), carried=True),
    )(xres, y, g, gt, target)


def _post_res_bwd(dxn, y, g, gt, *, name):
    def body(d_ref, y_ref, g_ref, gt_ref, dy_ref, sgt_ref, sg_ref):
        i = pl.program_id(0)

        @pl.when(i == 0)
        def _():
            sgt_ref[...] = jnp.zeros_like(sgt_ref)
            sg_ref[...] = jnp.zeros_like(sg_ref)

        yv = y_ref[...]
        dv = d_ref[...]
        r = lax.rsqrt(jnp.mean(yv * yv, axis=-1, keepdims=True) + NORM_EPS)
        yn = yv * r
        dn = dv * gt_ref[...]
        dyn = dn * g_ref[...]
        dy_ref[...] = (r * (dyn - yn * jnp.mean(dyn * yn, axis=-1, keepdims=True))).astype(dy_ref.dtype)
        sgt_ref[...] += jnp.sum(dv * (yn * g_ref[...]), axis=0, keepdims=True)
        sg_ref[...] += jnp.sum(dn * yn, axis=0, keepdims=True)

    return _pcall(
        body, name=name,
        out_shape=(jax.ShapeDtypeStruct((S, D), _MXU), jax.ShapeDtypeStruct((1, D), F32), jax.ShapeDtypeStruct((1, D), F32)),
        grid=(S // TM,), in_specs=[_rows(D), _rows(D), _vec(D), _vec(D)], out_specs=(_rows(D), _vec(D), _vec(D)),
        compiler_params=_row_params(10 * _nbytes((TM, D), F32), carried=True),
    )(dxn, y, g, gt)


def _swiglu(gu, *, name):
    def body(g_ref, u_ref, o_ref):
        gv = g_ref[...]
        o_ref[...] = ((gv * jax.nn.sigmoid(gv)) * u_ref[...]).astype(o_ref.dtype)

    return _pcall(
        body, name=name, out_shape=jax.ShapeDtypeStruct((S, D_FF), _MXU), grid=(S // TM_FF,),
        in_specs=[_rows(D_FF, 0, TM_FF), _rows(D_FF, 1, TM_FF)], out_specs=_rows(D_FF, 0, TM_FF),
        compiler_params=_row_params(8 * _nbytes((TM_FF, D_FF), F32)),
    )(gu, gu)


def _swiglu_bwd(gu, dact, *, name):
    def body(g_ref, u_ref, d_ref, o_ref):
        gv = g_ref[...]
        dv = d_ref[...]
        sg = jax.nn.sigmoid(gv)
        o_ref[:, :D_FF] = (dv * u_ref[...] * (sg * (1.0 + gv * (1.0 - sg)))).astype(o_ref.dtype)
        o_ref[:, D_FF:] = (dv * (gv * sg)).astype(o_ref.dtype)

    return _pcall(
        body, name=name, out_shape=jax.ShapeDtypeStruct((S, 2 * D_FF), _MXU), grid=(S // TM_FF,),
        in_specs=[_rows(D_FF, 0, TM_FF), _rows(D_FF, 1, TM_FF), _rows(D_FF, 0, TM_FF)], out_specs=_rows(2 * D_FF, 0, TM_FF),
        compiler_params=_row_params(12 * _nbytes((TM_FF, D_FF), F32)),
    )(gu, gu, dact)


def _merge(rest, pa, pb, *, name):
    def body(ga_ref, gb_ref, pa_ref, pb_ref, o_ref):
        o_ref[...] = (jax.nn.sigmoid(ga_ref[...]) * pa_ref[...] + jax.nn.sigmoid(gb_ref[...]) * pb_ref[...]).astype(o_ref.dtype)

    return _pcall(
        body, name=name, out_shape=jax.ShapeDtypeStruct((S, D), _MXU), grid=(S // TM,),
        in_specs=[_rows(D, OFF_GFOX // D), _rows(D, OFF_GMLA // D), _rows(D), _rows(D)], out_specs=_rows(D),
        compiler_params=_row_params(10 * _nbytes((TM, D), F32)),
    )(rest, rest, pa, pb)


def _merge_bwd(rest, pa, pb, dm, *, name):
    def body(ga_ref, gb_ref, pa_ref, pb_ref, d_ref, dpa_ref, dpb_ref, dga_ref, dgb_ref):
        dv = d_ref[...]
        sa = jax.nn.sigmoid(ga_ref[...])
        sb = jax.nn.sigmoid(gb_ref[...])
        dpa_ref[...] = (dv * sa).astype(dpa_ref.dtype)
        dpb_ref[...] = (dv * sb).astype(dpb_ref.dtype)
        dga_ref[...] = (dv * pa_ref[...] * (sa * (1.0 - sa))).astype(dga_ref.dtype)
        dgb_ref[...] = (dv * pb_ref[...] * (sb * (1.0 - sb))).astype(dgb_ref.dtype)

    o = jax.ShapeDtypeStruct((S, D), _MXU)
    return _pcall(
        body, name=name, out_shape=(o, o, o, o), grid=(S // TM,),
        in_specs=[_rows(D, OFF_GFOX // D), _rows(D, OFF_GMLA // D), _rows(D), _rows(D), _rows(D)],
        out_specs=(_rows(D), _rows(D), _rows(D), _rows(D)),
        compiler_params=_row_params(16 * _nbytes((TM, D), F32)),
    )(rest, rest, pa, pb, dm)


SCAN = 256


def _split_dot(tri, x):
    hi = x.astype(_MXU)
    r1 = x - hi.astype(F32)
    mid = r1.astype(_MXU)
    lo = (r1 - mid.astype(F32)).astype(_MXU)
    dot = functools.partial(jnp.dot, preferred_element_type=F32)
    return dot(tri, hi) + dot(tri, mid) + dot(tri, lo)


def _fox_prep(rest, bf, *, name):
    def body(z_ref, b_ref, f_ref):
        lane = lax.broadcasted_iota(jnp.int32, (SCAN, LANES), 1)
        tri = (lax.broadcasted_iota(jnp.int32, (SCAN, SCAN), 1) <= lax.broadcasted_iota(jnp.int32, (SCAN, SCAN), 0)).astype(_MXU)
        carry = jnp.zeros((1, LANES), F32)
        for c in range(S // SCAN):
            z = z_ref[c * SCAN:(c + 1) * SCAN, :] + b_ref[...]
            lf = jnp.minimum(z, 0.0) - jnp.log(1.0 + jnp.exp(-jnp.abs(z)))
            lf = jnp.where(lane < H, lf, 0.0)
            cum = _split_dot(tri, lf) + carry
            f_ref[c * SCAN:(c + 1) * SCAN, :] = cum
            carry = cum[SCAN - 1:SCAN, :]

    return _pcall(
        body, name=name, out_shape=jax.ShapeDtypeStruct((S, LANES), F32), grid=(1,),
        in_specs=[pl.BlockSpec((S, LANES), lambda i: (0, OFF_FLOG // LANES)), pl.BlockSpec((1, LANES), lambda i: (0, 0))],
        out_specs=pl.BlockSpec((S, LANES), lambda i: (0, 0)),
        compiler_params=pltpu.CompilerParams(vmem_limit_bytes=_limit(8 * _nbytes((S, LANES), F32))),
    )(rest, bf)


def _fox_bwd_prep(rest, bf, dF, *, name):
    def body(z_ref, b_ref, d_ref, o_ref, db_ref):
        lane = lax.broadcasted_iota(jnp.int32, (SCAN, LANES), 1)
        tri = (lax.broadcasted_iota(jnp.int32, (SCAN, SCAN), 1) >= lax.broadcasted_iota(jnp.int32, (SCAN, SCAN), 0)).astype(_MXU)
        carry = jnp.zeros((1, LANES), F32)
        db = jnp.zeros((1, LANES), F32)
        for c in range(S // SCAN - 1, -1, -1):
            rc = _split_dot(tri, d_ref[c * SCAN:(c + 1) * SCAN, :]) + carry
            z = z_ref[c * SCAN:(c + 1) * SCAN, :] + b_ref[...]
            dz = jnp.where(lane < H, rc * jax.nn.sigmoid(-z), 0.0)
            o_ref[c * SCAN:(c + 1) * SCAN, :] = dz
            db = db + jnp.sum(dz, axis=0, keepdims=True)
            carry = rc[0:1, :]
        db_ref[...] = db

    return _pcall(
        body, name=name,
        out_shape=(jax.ShapeDtypeStruct((S, LANES), F32), jax.ShapeDtypeStruct((1, LANES), F32)), grid=(1,),
        in_specs=[pl.BlockSpec((S, LANES), lambda i: (0, OFF_FLOG // LANES)), pl.BlockSpec((1, LANES), lambda i: (0, 0)),
                  pl.BlockSpec((S, LANES), lambda i: (0, 0))],
        out_specs=(pl.BlockSpec((S, LANES), lambda i: (0, 0)), pl.BlockSpec((1, LANES), lambda i: (0, 0))),
        compiler_params=pltpu.CompilerParams(vmem_limit_bytes=_limit(10 * _nbytes((S, LANES), F32))),
    )(rest, bf, dF)


def _swap16(x):
    lane = lax.broadcasted_iota(jnp.int32, x.shape, 1)
    half = MLA_ROPE // 2
    sw = jnp.where(lane < KRIN_LANE + half, pltpu.roll(x, LANES - half, 1), pltpu.roll(x, half, 1))
    return jnp.where((lane >= KRIN_LANE) & (lane < KRIN_LANE + MLA_ROPE), sw, 0.0)


def _mla_assemble(qb, kvb, rest, ctab, stab, *, name):
    def body(q_ref, kk_ref, kv_ref, kr_ref, c_ref, s_ref, qo_ref, ko_ref, vo_ref):
        cv = c_ref[...]
        sv = s_ref[...]
        kr = kr_ref[...]
        kpe = kr * cv + _swap16(kr) * sv
        for h in range(H):
            sl = slice(h * HP, (h + 1) * HP)
            qh = q_ref[:, sl]
            qo_ref[:, sl] = (qh * cv + _swap16(qh) * sv).astype(qo_ref.dtype)
            ko_ref[:, sl] = (kk_ref[:, sl] + kpe).astype(ko_ref.dtype)
        vo_ref[...] = kv_ref[...].astype(vo_ref.dtype)

    o = jax.ShapeDtypeStruct((S, H * HP), _MXU)
    return _pcall(
        body, name=name, out_shape=(o, o, o), grid=(S // TM,),
        in_specs=[_rows(H * HP), _rows(H * HP, 0), _rows(H * HP, 1), _rows(LANES, OFF_KRIN // LANES), _rows(LANES), _rows(LANES)],
        out_specs=(_rows(H * HP), _rows(H * HP), _rows(H * HP)),
        compiler_params=_row_params(14 * _nbytes((TM, H * HP), F32)),
    )(qb, kvb, kvb, rest, ctab, stab)


def _mla_assemble_bwd(dq, dk, dv, ctab, stab, *, name):
    def body(dq_ref, dk_ref, dv_ref, c_ref, s_ref, dqo_ref, dkv_ref, dkr_ref):
        cv = c_ref[...]
        sv = s_ref[...]
        lane = lax.broadcasted_iota(jnp.int32, (TM, LANES), 1)
        dsum = jnp.zeros((TM, LANES), F32)
        for h in range(H):
            sl = slice(h * HP, (h + 1) * HP)
            dqh = dq_ref[:, sl]
            dqo_ref[:, sl] = (dqh * cv + _swap16(dqh * sv)).astype(dqo_ref.dtype)
            dsum = dsum + dk_ref[:, sl]
        dkv_ref[:, :H * HP] = dk_ref[...].astype(dkv_ref.dtype)
        dkv_ref[:, H * HP:] = dv_ref[...].astype(dkv_ref.dtype)
        dkr = dsum * cv + _swap16(dsum * sv)
        dkr_ref[...] = jnp.where((lane >= KRIN_LANE) & (lane < KRIN_LANE + MLA_ROPE), dkr, 0.0)

    return _pcall(
        body, name=name,
        out_shape=(jax.ShapeDtypeStruct((S, H * HP), _MXU), jax.ShapeDtypeStruct((S, 2 * H * HP), _MXU),
                   jax.ShapeDtypeStruct((S, LANES), F32)),
        grid=(S // TM,),
        in_specs=[_rows(H * HP), _rows(H * HP), _rows(H * HP), _rows(LANES), _rows(LANES)],
        out_specs=(_rows(H * HP), _rows(2 * H * HP), _rows(LANES)),
        compiler_params=_row_params(14 * _nbytes((TM, H * HP), F32)),
    )(dq, dk, dv, ctab, stab)


TQ = 512
TKF = 512
TKB = 512
TQB = 512
_NT = (((1,), (1,)), ((), ()))
_TN = (((0,), (0,)), ((), ()))


def _is_pow2(x):
    return math.frexp(x)[0] == 0.5


def _attn_fwd(q, k, v, frow, *, scale, name, side=None, heads0=(0, 0, 0)):
    has_decay = frow is not None
    fold = _is_pow2(scale)
    n_in = 4 if has_decay else 3
    n_side_in = len(side.inputs) if side else 0
    n_side_out = len(side.out_shapes) if side else 0

    def body(*refs):
        q_ref, k_ref, v_ref = refs[:3]
        fr_ref = refs[3] if has_decay else None
        side_in = refs[n_in:n_in + n_side_in]
        o_ref, lse_ref = refs[n_in + n_side_in:n_in + n_side_in + 2]
        side_out = refs[n_in + n_side_in + 2:n_in + n_side_in + 2 + n_side_out]
        side_scratch = refs[n_in + n_side_in + 2 + n_side_out:]
        i = pl.program_id(1)
        if side:
            @pl.when((pl.program_id(0) == 0) & (i == 0))
            def _():
                side.start(side_in, side_out, side_scratch)
        qv = q_ref[...]
        if fold:
            qv = (qv * scale).astype(qv.dtype)
        last = (i * TQ) // TKF

        def tile(j, carry, masked):
            m, l, acc = carry
            k0 = pl.multiple_of(j * TKF, TKF)
            kj = k_ref[pl.ds(k0, TKF), :]
            vj = v_ref[pl.ds(k0, TKF), :]
            s = lax.dot_general(qv, kj, _NT, preferred_element_type=F32)
            if not fold:
                s = s * scale
            if has_decay:
                s = s - fr_ref[0, j]
            if masked:
                rows = i * TQ + lax.broadcasted_iota(jnp.int32, (TQ, TKF), 0)
                cols = j * TKF + lax.broadcasted_iota(jnp.int32, (TQ, TKF), 1)
                s = jnp.where(cols <= rows, s, NEG)
            m_new = jnp.maximum(m, jnp.max(s, axis=-1, keepdims=True))
            alpha = jnp.exp(m - m_new)
            p = jnp.exp(s - m_new)
            l = alpha * l + jnp.sum(p, axis=-1, keepdims=True)
            acc = alpha * acc + jnp.dot(p.astype(_MXU), vj, preferred_element_type=F32)
            return m_new, l, acc

        init = (jnp.full((TQ, 1), NEG, F32), jnp.zeros((TQ, 1), F32), jnp.zeros((TQ, HP), F32))
        carry = lax.fori_loop(0, last, lambda j, c: tile(j, c, False), init)
        m, l, acc = tile(last, carry, True)
        o_ref[...] = acc / l
        lse_ref[0] = m + jnp.log(l)
        if side:
            @pl.when((pl.program_id(0) == H - 1) & (i == S // TQ - 1))
            def _():
                side.finish(side_in, side_out, side_scratch)

    q0, k0, v0 = heads0
    in_specs = [pl.BlockSpec((TQ, HP), lambda h, i: (i, h + q0)), pl.BlockSpec((S, HP), lambda h, i: (0, h + k0)),
                pl.BlockSpec((S, HP), lambda h, i: (0, h + v0))]
    args = (q, k, v)
    if has_decay:
        in_specs += [pl.BlockSpec((1, S // TKF, 1, TKF), lambda h, i: (h, 0, 0, 0))]
        args += (frow,)
    out_shape = (jax.ShapeDtypeStruct((S, H * HP), F32), jax.ShapeDtypeStruct((H, S, 1), F32))
    out_specs = (pl.BlockSpec((TQ, HP), lambda h, i: (i, h)), pl.BlockSpec((1, TQ, 1), lambda h, i: (h, i, 0)))
    extra = {}
    if side:
        anywhere = pl.BlockSpec(memory_space=pl.ANY)
        in_specs += [anywhere] * n_side_in
        args += tuple(side.inputs)
        out_shape += tuple(side.out_shapes)
        out_specs += (anywhere,) * n_side_out
        extra = dict(scratch_shapes=list(side.scratch),
                     input_output_aliases={n_in + a: 2 + b for a, b in side.aliases.items()})
    return _pcall(
        body, name=name, out_shape=out_shape, grid=(H, S // TQ), in_specs=in_specs, out_specs=out_specs,
        compiler_params=pltpu.CompilerParams(dimension_semantics=("arbitrary", "arbitrary") if side else ("parallel", "parallel"),
                                             vmem_limit_bytes=_limit(8 * _nbytes((S, HP), F32))),
        **extra,
    )(*args)


def _attn_delta(do, o, *, name):
    def body(do_ref, o_ref, dl_ref, dob_ref):
        for h in range(H):
            sl = slice(h * HP, (h + 1) * HP)
            dl_ref[h] = jnp.sum(do_ref[:, sl] * o_ref[:, sl], axis=-1, keepdims=True)
        dob_ref[...] = do_ref[...].astype(dob_ref.dtype)

    return _pcall(
        body, name=name,
        out_shape=(jax.ShapeDtypeStruct((H, S, 1), F32), jax.ShapeDtypeStruct((S, H * HP), _MXU)), grid=(S // TM,),
        in_specs=[_rows(H * HP), _rows(H * HP)],
        out_specs=(pl.BlockSpec((H, TM, 1), lambda i: (0, i, 0)), _rows(H * HP)),
        compiler_params=_row_params(8 * _nbytes((TM, H * HP), F32)),
    )(do, o)


def _attn_bwd(q, k, v, dob, lse_row, delta_row, fcol, *, scale, name, side=None, heads0=(0, 0, 0)):
    has_decay = fcol is not None
    fold = _is_pow2(scale)
    n_in = 7 if has_decay else 6
    n_side_in = len(side.inputs) if side else 0
    n_side_out = len(side.out_shapes) if side else 0

    def body(*refs):
        q_ref, k_ref, v_ref, do_ref, lse_ref, dl_ref = refs[:6]
        fc_ref = refs[6] if has_decay else None
        side_in = refs[n_in:n_in + n_side_in]
        dq_ref, dk_ref, dv_ref = refs[n_in + n_side_in:n_in + n_side_in + 3]
        side_out = refs[n_in + n_side_in + 3:n_in + n_side_in + 3 + n_side_out]
        dq_acc = refs[n_in + n_side_in + 3 + n_side_out]
        side_scratch = refs[n_in + n_side_in + 4 + n_side_out:]
        j = pl.program_id(1)
        if side:
            @pl.when((pl.program_id(0) == 0) & (j == 0))
            def _():
                side.start(side_in, side_out, side_scratch)

        @pl.when(j == 0)
        def _():
            dq_acc[...] = jnp.zeros_like(dq_acc)

        kj = k_ref[...]
        vj = v_ref[...]
        kjs = (kj * scale).astype(kj.dtype) if fold else kj
        if has_decay:
            klane = lax.broadcasted_iota(jnp.int32, (TKB, HP), 1)
            kj = jnp.where(klane == ROW_SUM_LANE, 1.0, kj).astype(kj.dtype)
        first = (j * TKB) // TQB

        def tile(t, carry, masked):
            dk, dv = carry
            r0 = pl.multiple_of(t * TQB, TQB)
            qi = q_ref[pl.ds(r0, TQB), :]
            doi = do_ref[pl.ds(r0, TQB), :]
            st = lax.dot_general(kjs, qi, _NT, preferred_element_type=F32)
            if not fold:
                st = st * scale
            if has_decay:
                st = st - fc_ref[0]
            if masked:
                keys = j * TKB + lax.broadcasted_iota(jnp.int32, (TKB, TQB), 0)
                qpos = t * TQB + lax.broadcasted_iota(jnp.int32, (TKB, TQB), 1)
                st = jnp.where(keys <= qpos, st, NEG)
            pt = jnp.exp(st - lse_ref[0, t])
            dv = dv + jnp.dot(pt.astype(_MXU), doi, preferred_element_type=F32)
            dpt = lax.dot_general(vj, doi, _NT, preferred_element_type=F32)
            dst = (pt * (dpt - dl_ref[0, t])).astype(_MXU)
            if has_decay:
                lane = lax.broadcasted_iota(jnp.int32, (TQB, HP), 1)
                qi = jnp.where(lane == COL_SUM_LANE, 1.0, qi).astype(qi.dtype)
            dk = dk + jnp.dot(dst, qi, preferred_element_type=F32)
            dq_acc[pl.ds(r0, TQB), :] += lax.dot_general(dst, kj, _TN, preferred_element_type=F32)
            return dk, dv

        zero = jnp.zeros((TKB, HP), F32)
        carry = tile(first, (zero, zero), True)
        dk, dv = lax.fori_loop(first + 1, S // TQB, lambda t, c: tile(t, c, False), carry)
        dk_ref[...] = dk * scale
        dv_ref[...] = dv

        @pl.when(j == S // TKB - 1)
        def _():
            dq_ref[...] = dq_acc[...] * scale

        if side:
            @pl.when((pl.program_id(0) == H - 1) & (j == S // TKB - 1))
            def _():
                side.finish(side_in, side_out, side_scratch)

    q0, k0, v0 = heads0
    head = pl.BlockSpec((S, HP), lambda h, j: (0, h))
    kv = pl.BlockSpec((TKB, HP), lambda h, j: (j, h))
    stat = pl.BlockSpec((1, S // TQB, 1, TQB), lambda h, j: (h, 0, 0, 0))
    in_specs = [pl.BlockSpec((S, HP), lambda h, j: (0, h + q0)), pl.BlockSpec((TKB, HP), lambda h, j: (j, h + k0)),
                pl.BlockSpec((TKB, HP), lambda h, j: (j, h + v0)), head, stat, stat]
    args = (q, k, v, dob, lse_row, delta_row)
    if has_decay:
        in_specs += [pl.BlockSpec((1, TKB, 1), lambda h, j: (h, j, 0))]
        args += (fcol,)
    o = jax.ShapeDtypeStruct((S, H * HP), F32)
    out_shape, out_specs, scratch, aliases = (o, o, o), (head, kv, kv), [pltpu.VMEM((S, HP), F32)], {}
    if side:
        anywhere = pl.BlockSpec(memory_space=pl.ANY)
        in_specs += [anywhere] * n_side_in
        args += tuple(side.inputs)
        out_shape += tuple(side.out_shapes)
        out_specs += (anywhere,) * n_side_out
        scratch += list(side.scratch)
        aliases = {n_in + a: 3 + b for a, b in side.aliases.items()}
    return _pcall(
        body, name=name, out_shape=out_shape, grid=(H, S // TKB), in_specs=in_specs,
        out_specs=out_specs, scratch_shapes=scratch, input_output_aliases=aliases,
        compiler_params=pltpu.CompilerParams(dimension_semantics=("arbitrary" if side else "parallel", "arbitrary"),
                                             vmem_limit_bytes=_limit(12 * _nbytes((S, HP), F32))),
    )(*args)


def _ada_mod(c_all, w_shard, b_shard, *, name):
    R = c_all.shape[0]
    N = w_shard.shape[1]
    tn = 512

    def body(c_ref, w_ref, b_ref, o_ref, sc_ref):
        cv = c_ref[...]
        sc = (cv * jax.nn.sigmoid(cv)).astype(_MXU)
        sc_ref[...] = sc
        o_ref[...] = jnp.dot(sc, w_ref[...].astype(_MXU), preferred_element_type=F32) + b_ref[...]

    return _pcall(
        body, name=name,
        out_shape=(jax.ShapeDtypeStruct((R, N), F32), jax.ShapeDtypeStruct((R, D), _MXU)), grid=(N // tn,),
        in_specs=[pl.BlockSpec((R, D), lambda j: (0, 0)), pl.BlockSpec((D, tn), lambda j: (0, j)), pl.BlockSpec((1, tn), lambda j: (0, j))],
        out_specs=(pl.BlockSpec((R, tn), lambda j: (0, j)), pl.BlockSpec((R, D), lambda j: (0, 0))),
        compiler_params=pltpu.CompilerParams(dimension_semantics=("arbitrary",), vmem_limit_bytes=_limit(6 * _nbytes((D, tn), F32))),
    )(c_all, w_shard, b_shard)


def _rowsum(x, *, name):
    R, L = x.shape

    def body(x_ref, o_ref):
        acc = x_ref[0:1, :]
        for r in range(1, R):
            acc = acc + x_ref[r:r + 1, :]
        o_ref[...] = acc

    return pl.pallas_call(body, name=name, out_shape=jax.ShapeDtypeStruct((1, L), F32),
                          in_specs=[pl.BlockSpec(memory_space=pltpu.VMEM)], out_specs=pl.BlockSpec(memory_space=pltpu.VMEM))(x)


def _adamw(w, g, m, v, *, name):
    _, R, C = w.shape
    tr = R
    for t in range(8, R + 1, 8):
        if R % t == 0 and t * C * 4 <= (1 << 20):
            tr = t

    def body(w_ref, g_ref, m_ref, v_ref, d_ref, mo_ref, vo_ref):
        gv = g_ref[...]
        m2 = ADAM_B1 * m_ref[...] + (1.0 - ADAM_B1) * gv
        v2 = ADAM_B2 * v_ref[...] + (1.0 - ADAM_B2) * (gv * gv)
        m_hat = m2 / (1.0 - ADAM_B1 ** ADAM_STEP)
        v_hat = v2 / (1.0 - ADAM_B2 ** ADAM_STEP)
        d_ref[...] = -ADAM_LR * (m_hat / (jnp.sqrt(v_hat) + ADAM_EPS) + ADAM_WD * w_ref[...])
        mo_ref[...] = m2
        vo_ref[...] = v2

    blk = pl.BlockSpec((None, tr, C), lambda i: (0, i, 0))
    o = jax.ShapeDtypeStruct((1, R, C), F32)
    return _pcall(
        body, name=name, out_shape=(o, o, o), grid=(R // tr,),
        in_specs=[blk, pl.BlockSpec((tr, C), lambda i: (i, 0)), blk, blk], out_specs=(blk, blk, blk),
        compiler_params=pltpu.CompilerParams(dimension_semantics=("parallel",), vmem_limit_bytes=_limit(20 * _nbytes((tr, C), F32))),
    )(w, g, m, v)


def _place():
    x, y, c = lax.axis_index("x"), lax.axis_index("y"), lax.axis_index("c")
    return x, y, c, [(1 - x, y), (x, 1 - y), (1 - x, 1 - y)]


def _two_level_gather(x_ref, out_ref, send_sems, recv_sems, local_sem):
    x, y, c, chips = _place()
    me, sibling = (x, y, c), (x, y, 1 - c)

    def blk(px, py, pc):
        return out_ref.at[4 * px + 2 * py + pc]

    def copy(k, block, to, src=None):
        return pltpu.make_async_remote_copy(
            src_ref=blk(*block) if src is None else src, dst_ref=blk(*block),
            send_sem=send_sems.at[k], recv_sem=recv_sems.at[k], device_id=to, device_id_type=MESH)

    mine = pltpu.make_async_copy(x_ref, blk(*me), local_sem)
    mine.start()
    first = [copy(0, me, sibling, src=x_ref)]
    first += [copy(1 + j, me, (*chip, c), src=x_ref) for j, chip in enumerate(chips)]
    for cp in first:
        cp.start()
    passed = [copy(4 + j, (*chip, c), sibling) for j, chip in enumerate(chips)]
    for j, chip in enumerate(chips):
        copy(1 + j, (*chip, c), me).wait_recv()
        passed[j].start()
    copy(0, sibling, me).wait_recv()
    for j, chip in enumerate(chips):
        copy(4 + j, (*chip, 1 - c), me).wait_recv()
    for cp in first + passed:
        cp.wait_send()
    mine.wait()


_GATHER_SEMS = [pltpu.SemaphoreType.DMA((7,)), pltpu.SemaphoreType.DMA((7,)), pltpu.SemaphoreType.DMA]


class _SideJob(NamedTuple):
    inputs: tuple
    out_shapes: tuple
    aliases: dict
    scratch: tuple
    start: Callable
    finish: Callable


def _block_index(px, py, pc):
    return 4 * px + 2 * py + pc


def _gather_level1_job(halves):
    nw = len(halves)

    def copies(ins, outs, scratch, n):
        sends, recvs, _ = scratch
        x, y, c, chips = _place()
        mine = outs[n].at[_block_index(x, y, c)]
        peers = [(x, y, 1 - c)] + [(*chip, c) for chip in chips]
        out = []
        for t, peer in enumerate(peers):
            sem = dict(send_sem=sends.at[4 * n + t], recv_sem=recvs.at[4 * n + t], device_id_type=MESH)
            landing = outs[n].at[_block_index(*peer)]
            out.append((pltpu.make_async_remote_copy(src_ref=ins[n], dst_ref=mine, device_id=peer, **sem),
                        pltpu.make_async_remote_copy(src_ref=landing, dst_ref=landing, device_id=peer, **sem)))
        local = pltpu.make_async_copy(ins[n], mine, scratch[2].at[n])
        return out, local

    def start(ins, outs, scratch):
        for n in range(nw):
            pairs, local = copies(ins, outs, scratch, n)
            local.start()
            for to, _ in pairs:
                to.start()

    def finish(ins, outs, scratch):
        for n in range(nw):
            pairs, local = copies(ins, outs, scratch, n)
            for to, frm in pairs:
                frm.wait_recv()
                to.wait_send()
            local.wait()

    return _SideJob(
        inputs=tuple(halves), out_shapes=tuple(jax.ShapeDtypeStruct((8,) + h.shape, h.dtype) for h in halves), aliases={},
        scratch=(pltpu.SemaphoreType.DMA((4 * nw,)), pltpu.SemaphoreType.DMA((4 * nw,)), pltpu.SemaphoreType.DMA((nw,))),
        start=start, finish=finish)


def _gather_direct_job(halves):
    nw = len(halves)

    def copies(ins, outs, scratch, n):
        sends, recvs, _ = scratch
        x, y, c, _ = _place()
        mine = outs[n].at[_block_index(x, y, c)]
        out = []
        for f, (fx, fy, fc) in enumerate(_FLIPS):
            peer = (_flipped(x, fx), _flipped(y, fy), _flipped(c, fc))
            sem = dict(send_sem=sends.at[7 * n + f], recv_sem=recvs.at[7 * n + f], device_id=peer, device_id_type=MESH)
            landing = outs[n].at[_block_index(*peer)]
            out.append((pltpu.make_async_remote_copy(src_ref=ins[n], dst_ref=mine, **sem),
                        pltpu.make_async_remote_copy(src_ref=landing, dst_ref=landing, **sem)))
        return out, pltpu.make_async_copy(ins[n], mine, scratch[2].at[n])

    def start(ins, outs, scratch):
        for n in range(nw):
            pairs, local = copies(ins, outs, scratch, n)
            local.start()
            for to, _ in pairs:
                to.start()

    def finish(ins, outs, scratch):
        for n in range(nw):
            pairs, local = copies(ins, outs, scratch, n)
            for to, frm in pairs:
                frm.wait_recv()
                to.wait_send()
            local.wait()

    return _SideJob(
        inputs=tuple(halves), out_shapes=tuple(jax.ShapeDtypeStruct((8,) + h.shape, h.dtype) for h in halves), aliases={},
        scratch=(pltpu.SemaphoreType.DMA((7 * nw,)), pltpu.SemaphoreType.DMA((7 * nw,)), pltpu.SemaphoreType.DMA((nw,))),
        start=start, finish=finish)


def _join_jobs(a, b):
    cut = (len(a.inputs), len(a.out_shapes), len(a.scratch))

    def parts(ins, outs, scratch):
        return ((ins[:cut[0]], outs[:cut[1]], scratch[:cut[2]]), (ins[cut[0]:], outs[cut[1]:], scratch[cut[2]:]))

    def start(ins, outs, scratch):
        pa, pb = parts(ins, outs, scratch)
        a.start(*pa)
        b.start(*pb)

    def finish(ins, outs, scratch):
        pa, pb = parts(ins, outs, scratch)
        a.finish(*pa)
        b.finish(*pb)

    aliases = dict(a.aliases)
    aliases.update({cut[0] + i: cut[1] + o for i, o in b.aliases.items()})
    return _SideJob(inputs=a.inputs + b.inputs, out_shapes=a.out_shapes + b.out_shapes, aliases=aliases,
                    scratch=a.scratch + b.scratch, start=start, finish=finish)


def _gather_level2_job(gathered):
    nw = len(gathered)

    def copies(outs, scratch, n):
        sends, recvs = scratch
        x, y, c, chips = _place()
        out = []
        for j, chip in enumerate(chips):
            sem = dict(send_sem=sends.at[3 * n + j], recv_sem=recvs.at[3 * n + j], device_id=(x, y, 1 - c), device_id_type=MESH)
            going = outs[n].at[_block_index(*chip, c)]
            landing = outs[n].at[_block_index(*chip, 1 - c)]
            out.append((pltpu.make_async_remote_copy(src_ref=going, dst_ref=going, **sem),
                        pltpu.make_async_remote_copy(src_ref=landing, dst_ref=landing, **sem)))
        return out

    def start(ins, outs, scratch):
        for n in range(nw):
            for to, _ in copies(outs, scratch, n):
                to.start()

    def finish(ins, outs, scratch):
        for n in range(nw):
            for to, frm in copies(outs, scratch, n):
                frm.wait_recv()
                to.wait_send()

    return _SideJob(
        inputs=tuple(gathered), out_shapes=tuple(jax.ShapeDtypeStruct(g.shape, g.dtype) for g in gathered),
        aliases={n: n for n in range(nw)},
        scratch=(pltpu.SemaphoreType.DMA((3 * nw,)), pltpu.SemaphoreType.DMA((3 * nw,))),
        start=start, finish=finish)


def _all_gather_rows(x, *, name):
    R, C = x.shape

    def body(x_ref, out_ref, send_sems, recv_sems, local_sem):
        _two_level_gather(x_ref, out_ref, send_sems, recv_sems, local_sem)

    return pl.pallas_call(
        body, name=name, out_shape=jax.ShapeDtypeStruct((8, R, C), x.dtype),
        in_specs=[pl.BlockSpec(memory_space=pltpu.VMEM)], out_specs=pl.BlockSpec(memory_space=pltpu.VMEM),
        scratch_shapes=list(_GATHER_SEMS),
        compiler_params=pltpu.CompilerParams(vmem_limit_bytes=_limit(10 * _nbytes((R, C), x.dtype))),
    )(x)


CAST_ROWS = 16
_FLIPS = [(fx, fy, fc) for fx in (0, 1) for fy in (0, 1) for fc in (0, 1)][1:]


def _flipped(v, bit):
    return 1 - v if bit else v


def _scatter_direct_job(pieces, rows=None, partly=None):
    nw = len(pieces)

    def band(ref):
        return ref if rows is None else ref.at[pl.ds(rows[0], rows[1]), :]

    def copies(ins, outs, scratch, n):
        sends, recvs = scratch
        x, y, c, _ = _place()
        out = []
        for f, (fx, fy, fc) in enumerate(_FLIPS):
            peer = (_flipped(x, fx), _flipped(y, fy), _flipped(c, fc))
            sem = dict(send_sem=sends.at[7 * n + f], recv_sem=recvs.at[7 * n + f], device_id=peer, device_id_type=MESH)
            landing = band(outs[n].at[f])
            out.append((pltpu.make_async_remote_copy(src_ref=band(ins[n].at[_block_index(*peer)]), dst_ref=landing, **sem),
                        pltpu.make_async_remote_copy(src_ref=landing, dst_ref=landing, **sem)))
        return out

    def start(ins, outs, scratch):
        for n in range(nw):
            for to, _ in copies(ins, outs, scratch, n):
                to.start()

    def finish(ins, outs, scratch):
        for n in range(nw):
            for to, frm in copies(ins, outs, scratch, n):
                frm.wait_recv()
                to.wait_send()

    return _SideJob(
        inputs=tuple(pieces) + tuple(partly or ()),
        out_shapes=tuple(jax.ShapeDtypeStruct((7,) + p.shape[1:], p.dtype) for p in pieces),
        aliases={nw + n: n for n in range(nw)} if partly else {},
        scratch=(pltpu.SemaphoreType.DMA((7 * nw,)), pltpu.SemaphoreType.DMA((7 * nw,))),
        start=start, finish=finish)


def _scatter_finish(g4s, landed, *, name):
    nw = len(g4s)
    dims = [g.shape[1:] for g in g4s]

    def body(*refs):
        g_refs, l_refs, out_refs, own = refs[:nw], refs[nw:2 * nw], refs[2 * nw:3 * nw], refs[3 * nw:4 * nw]
        load_sems, send_sems, recv_sems = refs[4 * nw:]
        x, y, core, _ = _place()
        k = 2 * x + y
        loads = []
        for n, (r, c) in enumerate(dims):
            my0 = pl.multiple_of(core * (r // 2), CAST_ROWS)
            ld = pltpu.make_async_copy(g_refs[n].at[k, pl.ds(my0, r // 2), :], own[n], load_sems.at[n])
            ld.start()
            loads.append(ld)
        swaps = []
        for n, (r, c) in enumerate(dims):
            rh = r // 2
            my0 = pl.multiple_of(core * rh, CAST_ROWS)
            loads[n].wait()

            def fin(i, carry, n=n, my0=my0):
                r0 = pl.multiple_of(i * CAST_ROWS, CAST_ROWS)
                s = own[n][pl.ds(r0, CAST_ROWS), :]
                for f in range(7):
                    s = s + l_refs[n][f, pl.ds(r0, CAST_ROWS), :].astype(F32)
                out_refs[n][pl.ds(my0 + r0, CAST_ROWS), :] = s
                return carry

            lax.fori_loop(0, rh // CAST_ROWS, fin, 0)
            half = out_refs[n].at[pl.ds(my0, rh), :]
            sw = pltpu.make_async_remote_copy(src_ref=half, dst_ref=half, send_sem=send_sems.at[n], recv_sem=recv_sems.at[n],
                                              device_id=(x, y, 1 - core), device_id_type=MESH)
            sw.start()
            swaps.append(sw)
        for sw in swaps:
            sw.wait()

    need = sum(_nbytes((7, r // 2, c), BF16) + _nbytes((r // 2, c), F32) + _nbytes((r, c), F32) for r, c in dims)
    vmem = pl.BlockSpec(memory_space=pltpu.VMEM)
    return pl.pallas_call(
        body, name=name, out_shape=tuple(jax.ShapeDtypeStruct((r, c), F32) for r, c in dims),
        in_specs=[pl.BlockSpec(memory_space=pl.ANY)] * nw + [vmem] * nw, out_specs=(vmem,) * nw,
        scratch_shapes=[pltpu.VMEM((r // 2, c), F32) for r, c in dims]
        + [pltpu.SemaphoreType.DMA((nw,)), pltpu.SemaphoreType.DMA((nw,)), pltpu.SemaphoreType.DMA((nw,))],
        compiler_params=pltpu.CompilerParams(vmem_limit_bytes=_limit(need * 1.1)),
    )(*g4s, *landed)


def _gather_weight(w, *, name):
    r, c = w.shape
    rh = r // 2
    assert rh % CAST_ROWS == 0

    def body(w_hbm, out_ref, tmp, xb, send_sems, recv_sems, local_sem):
        core = lax.axis_index("c")
        ld = pltpu.make_async_copy(w_hbm.at[pl.ds(pl.multiple_of(core * rh, CAST_ROWS), rh), :], tmp, local_sem)
        ld.start()
        ld.wait()

        def cast(i, carry):
            r0 = pl.multiple_of(i * CAST_ROWS, CAST_ROWS)
            xb[pl.ds(r0, CAST_ROWS), :] = tmp[pl.ds(r0, CAST_ROWS), :].astype(BF16)
            return carry

        lax.fori_loop(0, rh // CAST_ROWS, cast, 0)
        _two_level_gather(xb, out_ref, send_sems, recv_sems, local_sem)

    need = _nbytes((8, rh, c), BF16) + _nbytes((rh, c), F32) + _nbytes((rh, c), BF16)
    out = pl.pallas_call(
        body, name=name, out_shape=jax.ShapeDtypeStruct((8, rh, c), BF16),
        in_specs=[pl.BlockSpec(memory_space=pl.ANY)], out_specs=pl.BlockSpec(memory_space=pltpu.VMEM),
        scratch_shapes=[pltpu.VMEM((rh, c), F32), pltpu.VMEM((rh, c), BF16)] + list(_GATHER_SEMS),
        compiler_params=pltpu.CompilerParams(vmem_limit_bytes=_limit(need * 1.3)),
    )(w)
    return out.reshape(4, r, c)


def _reduce_scatter_weight(g4, *, name):
    _, r, c = g4.shape
    rh = r // 2
    assert rh % CAST_ROWS == 0
    nsteps = rh // CAST_ROWS

    def body(g_hbm, out_ref, mine, tmp, sbuf, rbuf_a, rbuf_b, a_send, a_recv, b_send, b_recv, c_send, c_recv, lsem):
        x, y, core, chips = _place()
        sibling = (x, y, 1 - core)
        k = 2 * x + y
        my0 = pl.multiple_of(core * rh, CAST_ROWS)
        ot0 = pl.multiple_of((1 - core) * rh, CAST_ROWS)

        ld = pltpu.make_async_copy(g_hbm.at[:, pl.ds(my0, rh), :], mine, lsem)
        ld.start()
        ld.wait()
        for j in range(4):
            ldj = pltpu.make_async_copy(g_hbm.at[j, pl.ds(ot0, rh), :], tmp, lsem)
            ldj.start()
            ldj.wait()

            def cast(i, carry, j=j):
                r0 = pl.multiple_of(i * CAST_ROWS, CAST_ROWS)
                sbuf[j, pl.ds(r0, CAST_ROWS), :] = tmp[pl.ds(r0, CAST_ROWS), :].astype(BF16)
                return carry

            lax.fori_loop(0, nsteps, cast, 0)

        to_sib = pltpu.make_async_remote_copy(src_ref=sbuf, dst_ref=rbuf_a, send_sem=a_send, recv_sem=a_recv,
                                              device_id=sibling, device_id_type=MESH)
        to_sib.start()
        to_sib.wait()

        for j in range(4):
            def add(i, carry, j=j):
                r0 = pl.multiple_of(i * CAST_ROWS, CAST_ROWS)
                s = mine[j, pl.ds(r0, CAST_ROWS), :] + rbuf_a[j, pl.ds(r0, CAST_ROWS), :].astype(F32)
                mine[j, pl.ds(r0, CAST_ROWS), :] = s
                sbuf[j, pl.ds(r0, CAST_ROWS), :] = s.astype(BF16)
                return carry

            lax.fori_loop(0, nsteps, add, 0)

        sends = []
        for d, (px, py) in enumerate(chips):
            cp = pltpu.make_async_remote_copy(src_ref=sbuf.at[2 * px + py], dst_ref=rbuf_b.at[d], send_sem=b_send.at[d],
                                              recv_sem=b_recv.at[d], device_id=(px, py, core), device_id_type=MESH)
            cp.start()
            sends.append(cp)
        for cp in sends:
            cp.wait()

        def fin(i, carry):
            r0 = pl.multiple_of(i * CAST_ROWS, CAST_ROWS)
            s = mine[k, pl.ds(r0, CAST_ROWS), :]
            for d in range(3):
                s = s + rbuf_b[d, pl.ds(r0, CAST_ROWS), :].astype(F32)
            out_ref[pl.ds(my0 + r0, CAST_ROWS), :] = s
            return carry

        lax.fori_loop(0, nsteps, fin, 0)
        half = out_ref.at[pl.ds(my0, rh), :]
        swap = pltpu.make_async_remote_copy(src_ref=half, dst_ref=half, send_sem=c_send, recv_sem=c_recv,
                                            device_id=sibling, device_id_type=MESH)
        swap.start()
        swap.wait()

    need = (_nbytes((4, rh, c), F32) + _nbytes((rh, c), F32) + 2 * _nbytes((4, rh, c), BF16) + _nbytes((3, rh, c), BF16)
            + _nbytes((r, c), F32))
    return pl.pallas_call(
        body, name=name, out_shape=jax.ShapeDtypeStruct((r, c), F32),
        in_specs=[pl.BlockSpec(memory_space=pl.ANY)], out_specs=pl.BlockSpec(memory_space=pltpu.VMEM),
        scratch_shapes=[pltpu.VMEM((4, rh, c), F32), pltpu.VMEM((rh, c), F32), pltpu.VMEM((4, rh, c), BF16),
                        pltpu.VMEM((4, rh, c), BF16), pltpu.VMEM((3, rh, c), BF16),
                        pltpu.SemaphoreType.DMA, pltpu.SemaphoreType.DMA, pltpu.SemaphoreType.DMA((3,)),
                        pltpu.SemaphoreType.DMA((3,)), pltpu.SemaphoreType.DMA, pltpu.SemaphoreType.DMA,
                        pltpu.SemaphoreType.DMA],
        compiler_params=pltpu.CompilerParams(vmem_limit_bytes=_limit(need * 1.2)),
    )(g4)


def _cols_from_shards(g):
    n, K, c = g.shape
    return g.transpose(1, 0, 2).reshape(K, n * c)


def _cols_to_shards(w):
    K, N = w.shape
    return w.reshape(K, 4, N // 4).transpose(1, 0, 2)


def _pad_heads_cols(w, width, lane0=0):
    K = w.shape[0]
    w3 = w.reshape(K, H, width)
    return jnp.pad(w3, ((0, 0), (0, 0), (lane0, HP - lane0 - width))).reshape(K, H * HP)


def _unpad_heads_cols(w, width, lane0=0):
    K = w.shape[0]
    return w.reshape(K, H, HP)[:, :, lane0:lane0 + width].reshape(K, H * width)


def _pad_block(w, lane0=0):
    return jnp.pad(w, ((0, 0), (lane0, LANES - lane0 - w.shape[1])))


_IN_SPLITS = [512, 1024, 1536, 1544, 2312, 2568, 2600, 3624]


def _pad_w_in(w):
    fq, fk, fv, flog, cq, ckv, krin, gfox, gmla = jnp.split(w, _IN_SPLITS, axis=1)
    return jnp.concatenate([_pad_heads_cols(fq, FOX_HD), _pad_heads_cols(fk, FOX_HD), _pad_heads_cols(fv, FOX_HD),
                            cq, ckv, gfox, gmla, _pad_block(flog), _pad_block(krin, KRIN_LANE)], axis=1)


def _unpad_w_in(wp):
    qkv, rest = wp[:, :NQKV], wp[:, NQKV:]
    fq, fk, fv = (_unpad_heads_cols(qkv[:, i * H * HP:(i + 1) * H * HP], FOX_HD) for i in range(3))
    return jnp.concatenate([fq, fk, fv, rest[:, OFF_FLOG:OFF_FLOG + H], rest[:, OFF_CQ:OFF_CQ + Q_LORA],
                            rest[:, OFF_CKV:OFF_CKV + KV_LORA], rest[:, OFF_KRIN + KRIN_LANE:OFF_KRIN + KRIN_LANE + MLA_ROPE],
                            rest[:, OFF_GFOX:OFF_GFOX + D], rest[:, OFF_GMLA:OFF_GMLA + D]], axis=1)


def _pad_w_ukv(w):
    w3 = w.reshape(KV_LORA, H, MLA_NOPE + MLA_V)
    kp = jnp.pad(w3[:, :, :MLA_NOPE], ((0, 0), (0, 0), (0, HP - MLA_NOPE))).reshape(KV_LORA, H * HP)
    vp = jnp.pad(w3[:, :, MLA_NOPE:], ((0, 0), (0, 0), (0, HP - MLA_V))).reshape(KV_LORA, H * HP)
    return jnp.concatenate([kp, vp], axis=1)


def _unpad_w_ukv(wp):
    kp = wp[:, :H * HP].reshape(KV_LORA, H, HP)[:, :, :MLA_NOPE]
    vp = wp[:, H * HP:].reshape(KV_LORA, H, HP)[:, :, :MLA_V]
    return jnp.concatenate([kp, vp], axis=2).reshape(KV_LORA, H * (MLA_NOPE + MLA_V))


def _pad_heads_rows(w, width):
    N = w.shape[1]
    return jnp.pad(w.reshape(H, width, N), ((0, 0), (0, HP - width), (0, 0))).reshape(H * HP, N)


def _unpad_heads_rows(w, width):
    N = w.shape[1]
    return w.reshape(H, HP, N)[:, :width, :].reshape(H * width, N)


def _rope_tables(positions):
    inv_freq = 1.0 / (ROPE_THETA ** (jnp.arange(0, MLA_ROPE, 2, dtype=F32) / MLA_ROPE))
    ang = positions.reshape(S, 1).astype(F32) * inv_freq
    cos, sin = jnp.cos(ang), jnp.sin(ang)
    ones = jnp.ones((S, KRIN_LANE), F32)
    tail = jnp.zeros((S, LANES - KRIN_LANE - MLA_ROPE), F32)
    ctab = jnp.concatenate([ones, cos, cos, tail], axis=1)
    stab = jnp.concatenate([0.0 * ones, -sin, sin, tail], axis=1)
    return ctab, stab


def _local_step(x, target, mod, positions, gains, bf, W, late=None):
    W = dict(W)
    sh1, sc1, gt1, sh2, sc2, gt2 = (mod[:, i * D:(i + 1) * D] for i in range(6))
    ops1, ops2 = 1.0 + sc1, 1.0 + sc2
    ones = lambda w: jnp.ones((1, w), F32)
    zeros = lambda w: jnp.zeros((1, w), F32)
    bf_blk = _pad_block(bf)
    ctab, stab = _rope_tables(positions)
    fox_scale = 1.0 / math.sqrt(FOX_HD)
    mla_scale = 1.0 / math.sqrt(MLA_NOPE + MLA_ROPE)

    h1 = _norm_mod(x, 0, D, gains["g_pre_mix"], ops1, sh1, name="f_pre_mix")
    qkv = _matmul(h1, W["w_in_qkv"], out_dtype=_MXU, name="f_proj_qkv", tm_cap=2048)
    rest = _matmul(h1, W["w_in_rest"], name="f_proj_rest", tm_cap=2048, tn_cap=256)
    F = _fox_prep(rest, bf_blk, name="f_fox_prep")
    Ft = F[:, :H].T
    fcol, frow = Ft.reshape(H, S, 1), Ft.reshape(H, S // TKF, 1, TKF)
    fox_heads = (0, H, 2 * H)
    job = _gather_level1_job([late[n] for n in _GATHER_A]) if late else None
    oa, lse_a, *landed = _attn_fwd(qkv, qkv, qkv, frow, scale=fox_scale, name="f_attn_fox", side=job, heads0=fox_heads)

    cqn = _norm_mod(rest, OFF_CQ // Q_LORA, Q_LORA, gains["g_q_lora"], ones(Q_LORA), zeros(Q_LORA), name="f_norm_cq")
    ckvn = _norm_mod(rest, OFF_CKV // KV_LORA, KV_LORA, gains["g_kv_lora"], ones(KV_LORA), zeros(KV_LORA), name="f_norm_ckv")
    qb = _matmul(cqn, W["w_uq"], name="f_uq")
    kvb = _matmul(ckvn, W["w_ukv"], name="f_ukv")
    qm, km, vm = _mla_assemble(qb, kvb, rest, ctab, stab, name="f_mla_assemble")
    job = _join_jobs(_gather_level2_job(landed), _gather_direct_job([late[n] for n in _GATHER_B])) if late else None
    ob, lse_b, *landed = _attn_fwd(qm, km, vm, None, scale=mla_scale, name="f_attn_mla", side=job)
    if late:
        W.update(_late_weights({n: g.reshape(4, 2 * g.shape[1], g.shape[2]) for n, g in zip(_GATHER_A + _GATHER_B, landed)}))

    pa = _matmul(oa, W["w_proj_fox"], name="f_proj_fox")
    pb = _matmul(ob, W["w_proj_mla"], name="f_proj_mla")
    merged = _merge(rest, pa, pb, name="f_merge")
    y1 = _matmul(merged, W["w_out"], name="f_out")
    x2 = _post_res(x, y1, gains["g_post_mix"], gt1, name="f_post_mix")
    h2 = _norm_mod(x2, 0, D, gains["g_pre_ffn"], ops2, sh2, name="f_pre_ffn")
    gu = _matmul(h2, W["w_ffn_in"], name="f_ffn_in", b_shards=True, tn_cap=1408)
    act = _swiglu(gu, name="f_swiglu")
    y2 = _matmul(act, W["w_ffn_out"], name="f_ffn_out", tk_cap=1408, tn_cap=1024)
    dout, loss = _post_res_loss(x2, y2, gains["g_post_ffn"], gt2, target, name="f_post_ffn_loss")

    dy2, s_gt2, s_gpost2 = _post_res_bwd(dout, y2, gains["g_post_ffn"], gt2, name="b_post_ffn")
    dact = _matmul(dy2, W["w_ffn_out"], tb=True, name="b_ffn_out_dx", tn_cap=1408)
    dW_ffn_out = _matmul(act, dy2, ta=True, name="b_ffn_out_dw", tm_cap=1408, tk_cap=2048)
    dgu = _swiglu_bwd(gu, dact, name="b_swiglu")
    job_x = job_w = None
    landed_c = []
    if late:
        fo = dW_ffn_out.reshape(8, D_FF // 8, D).astype(BF16)
        half = fo.shape[1] // 2
        job_x = _scatter_direct_job([fo], rows=(0, half))
    dh2, *landed_c = _matmul_carrying(job_x, dgu, W["w_ffn_in"], tb=True, b_shards=True, name="b_ffn_in_dx", tk_cap=1408,
                                      tn_cap=1024)
    if late:
        job_w = _scatter_direct_job([fo], rows=(half, half), partly=landed_c)
    dW_ffn_in, *landed_c = _matmul_carrying(job_w, h2, dgu, ta=True, name="b_ffn_in_dw", out_shards=True, tn_cap=1408,
                                            tk_cap=2048)
    dx2, s_sh2, s_a2 = _norm_mod_bwd(x2, 0, D, dh2, gains["g_pre_ffn"], ops2, dout, name="b_pre_ffn")
    dy1, s_gt1, s_gpost1 = _post_res_bwd(dx2, y1, gains["g_post_mix"], gt1, name="b_post_mix")
    dmerged = _matmul(dy1, W["w_out"], tb=True, name="b_out_dx")
    dW_out = _matmul(merged, dy1, ta=True, name="b_out_dw", tk_cap=2048)
    dpa, dpb, dgfox, dgmla = _merge_bwd(rest, pa, pb, dmerged, name="b_merge")
    doa = _matmul(dpa, W["w_proj_fox"], tb=True, name="b_proj_fox_dx")
    dW_proj_fox = _matmul(oa, dpa, ta=True, name="b_proj_fox_dw")
    dob = _matmul(dpb, W["w_proj_mla"], tb=True, name="b_proj_mla_dx")
    dW_proj_mla = _matmul(ob, dpb, ta=True, name="b_proj_mla_dw")

    delta_a, doa16 = _attn_delta(doa, oa, name="b_delta_fox")
    as_rows = lambda a: a.reshape(H, S // TQB, 1, TQB)
    dW = dict(w_proj_fox=dW_proj_fox, w_proj_mla=dW_proj_mla, w_out=dW_out, w_ffn_in=dW_ffn_in, w_ffn_out=dW_ffn_out)
    job_a = job_b = None
    if late:
        late_shards = _grad_shards(dW)
        pieces = {n: s.reshape(8, s.shape[1] // 2, s.shape[2]).astype(BF16) for n, s in late_shards.items()}
        first = [pieces[n] for n in _SCATTER_A]
        job_a = _scatter_direct_job(first, rows=(0, _SCATTER_SPLIT))
    dqa, dka, dva, *landed_a = _attn_bwd(qkv, qkv, qkv, doa16, as_rows(lse_a), as_rows(delta_a), fcol, scale=fox_scale,
                                         name="b_attn_fox", side=job_a, heads0=fox_heads)
    delta_b, dob16 = _attn_delta(dob, ob, name="b_delta_mla")
    if late:
        rest_rows = (_SCATTER_SPLIT, first[0].shape[1] - _SCATTER_SPLIT)
        job_b = _join_jobs(_scatter_direct_job(first, rows=rest_rows, partly=landed_a),
                           _scatter_direct_job([pieces[n] for n in _SCATTER_B]))
    dqm, dkm, dvm, *landed_b = _attn_bwd(qm, km, vm, dob16, as_rows(lse_b), as_rows(delta_b), None, scale=mla_scale,
                                         name="b_attn_mla", side=job_b)
    reduced = {}
    if late:
        order = _SCATTER_A + _SCATTER_B + _SCATTER_C
        done = _scatter_finish([late_shards[n] for n in order], landed_b + landed_c, name="scatter_late")
        reduced = dict(zip(order, done))
        dW = {}

    dF = (dqa[:, ROW_SUM_LANE::HP] - dka[:, COL_SUM_LANE::HP]) * (1.0 / fox_scale)
    dflog, s_bf = _fox_bwd_prep(rest, bf_blk, _pad_block(dF), name="b_fox_prep")

    dqb, dkvb, dkrin = _mla_assemble_bwd(dqm, dkm, dvm, ctab, stab, name="b_mla_assemble")
    dcqn = _matmul(dqb, W["w_uq"], tb=True, name="b_uq_dx")
    dW_uq = _matmul(cqn, dqb, ta=True, name="b_uq_dw", tk_cap=2048)
    dckvn = _matmul(dkvb, W["w_ukv"], tb=True, name="b_ukv_dx")
    dW_ukv = _matmul(ckvn, dkvb, ta=True, name="b_ukv_dw", tk_cap=2048)
    dcq, _, s_gq = _norm_mod_bwd(rest, OFF_CQ // Q_LORA, Q_LORA, dcqn, gains["g_q_lora"], ones(Q_LORA), None, name="b_norm_cq")
    dckv, _, s_gkv = _norm_mod_bwd(rest, OFF_CKV // KV_LORA, KV_LORA, dckvn, gains["g_kv_lora"], ones(KV_LORA), None, name="b_norm_ckv")

    c16 = lambda a: a.astype(_MXU)
    dproj = jnp.concatenate([c16(dqa), c16(dka), c16(dva), c16(dcq), c16(dckv), dgfox, dgmla, c16(dflog), c16(dkrin)], axis=1)
    w_in_full = jnp.concatenate([W["w_in_qkv"], W["w_in_rest"]], axis=1)
    dh1 = _matmul(dproj, w_in_full, tb=True, name="b_in_dx", tk_cap=1280, tn_cap=1024)
    dW_in = _matmul(h1, dproj, ta=True, name="b_in_dw", tn_cap=640, tk_cap=2048)
    grad_x, s_sh1, s_a1 = _norm_mod_bwd(x, 0, D, dh1, gains["g_pre_mix"], ops1, dx2, name="b_pre_mix")

    dmod = jnp.concatenate([s_sh1, s_a1 * gains["g_pre_mix"], s_gt1, s_sh2, s_a2 * gains["g_pre_ffn"], s_gt2], axis=1)
    small = dict(dmod=dmod, g_pre_mix=s_a1 * ops1, g_post_mix=s_gpost1, g_pre_ffn=s_a2 * ops2, g_post_ffn=s_gpost2,
                 g_q_lora=s_gq, g_kv_lora=s_gkv, b_forget=s_bf)
    dW = dict(dW, w_in=dW_in, w_uq=dW_uq, w_ukv=dW_ukv)
    return loss, grad_x, dW, reduced, small


_BIG = ["w_in", "w_uq", "w_ukv", "w_proj_fox", "w_proj_mla", "w_out", "w_ffn_in", "w_ffn_out"]
_COL_SHARDED = {"w_in", "w_ukv", "w_proj_fox", "w_proj_mla", "w_ffn_in"}
_SMALL = ["b_ada", "g_pre_mix", "g_post_mix", "g_pre_ffn", "g_post_ffn", "b_forget", "g_q_lora", "g_kv_lora"]
_ORDER = ["w_ada", "b_ada", "g_pre_mix", "g_post_mix", "g_pre_ffn", "g_post_ffn", "w_in", "b_forget", "g_q_lora", "w_uq",
          "g_kv_lora", "w_ukv", "w_proj_fox", "w_proj_mla", "w_out", "w_ffn_in", "w_ffn_out"]
_ROW = {}
_off = 0
for _n, _w in [("dmod", 6 * D), ("g_pre_mix", D), ("g_post_mix", D), ("g_pre_ffn", D), ("g_post_ffn", D), ("g_q_lora", Q_LORA),
               ("g_kv_lora", KV_LORA), ("b_forget", LANES), ("loss", LANES)]:
    _ROW[_n] = (_off, _w)
    _off += _w
_ROW_LEN = _off


_EARLY = ["w_in", "w_uq", "w_ukv"]
_LATE = ["w_proj_fox", "w_proj_mla", "w_out", "w_ffn_in", "w_ffn_out"]
_GATHER_A = ["w_proj_fox", "w_proj_mla", "w_out", "w_ffn_in"]
_GATHER_B = ["w_ffn_out"]
_SCATTER_A = ["w_ffn_in"]
_SCATTER_SPLIT = 320
_SCATTER_B = ["w_out", "w_proj_fox", "w_proj_mla"]
_SCATTER_C = ["w_ffn_out"]


def _rows_from_shards(g):
    return g.reshape(-1, g.shape[2])


def _early_weights(G):
    w_in = _pad_w_in(_cols_from_shards(G["w_in"]))
    return dict(
        w_in_qkv=w_in[:, :NQKV], w_in_rest=w_in[:, NQKV:],
        w_uq=_pad_heads_cols(_rows_from_shards(G["w_uq"]), MLA_NOPE + MLA_ROPE),
        w_ukv=_pad_w_ukv(_cols_from_shards(G["w_ukv"])))


def _late_weights(G):
    return dict(
        w_proj_fox=_pad_heads_rows(_cols_from_shards(G["w_proj_fox"]), FOX_HD),
        w_proj_mla=_pad_heads_rows(_cols_from_shards(G["w_proj_mla"]), MLA_V),
        w_out=_rows_from_shards(G["w_out"]), w_ffn_in=G["w_ffn_in"], w_ffn_out=_rows_from_shards(G["w_ffn_out"]))


def _full_weights(G):
    return {**_early_weights(G), **_late_weights(G)}


_UNPAD = dict(
    w_in=_unpad_w_in, w_uq=lambda g: _unpad_heads_cols(g, MLA_NOPE + MLA_ROPE), w_ukv=_unpad_w_ukv,
    w_proj_fox=lambda g: _unpad_heads_rows(g, FOX_HD), w_proj_mla=lambda g: _unpad_heads_rows(g, MLA_V),
    w_out=lambda g: g, w_ffn_out=lambda g: g)


def _grad_shards(dW):
    out = {}
    for n, g in dW.items():
        if n == "w_ffn_in":
            out[n] = g
        else:
            nat = _UNPAD[n](g)
            out[n] = _cols_to_shards(nat) if n in _COL_SHARDED else nat.reshape(4, nat.shape[0] // 4, nat.shape[1])
    return out


def kernel(x, c, positions, w_ada, b_ada, g_pre_mix, g_post_mix, g_pre_ffn, g_post_ffn, w_in, b_forget, g_q_lora, w_uq, g_kv_lora, w_ukv, w_proj_fox, w_proj_mla, w_out, w_ffn_in, w_ffn_out, loss_target, m_w_ada, m_b_ada, m_g_pre_mix, m_g_post_mix, m_g_pre_ffn, m_g_post_ffn, m_w_in, m_b_forget, m_g_q_lora, m_w_uq, m_g_kv_lora, m_w_ukv, m_w_proj_fox, m_w_proj_mla, m_w_out, m_w_ffn_in, m_w_ffn_out, v_w_ada, v_b_ada, v_g_pre_mix, v_g_post_mix, v_g_pre_ffn, v_g_post_ffn, v_w_in, v_b_forget, v_g_q_lora, v_w_uq, v_g_kv_lora, v_w_ukv, v_w_proj_fox, v_w_proj_mla, v_w_out, v_w_ffn_in, v_w_ffn_out):
    P = dict(w_ada=w_ada, b_ada=b_ada, g_pre_mix=g_pre_mix, g_post_mix=g_post_mix, g_pre_ffn=g_pre_ffn, g_post_ffn=g_post_ffn,
             w_in=w_in, b_forget=b_forget, g_q_lora=g_q_lora, w_uq=w_uq, g_kv_lora=g_kv_lora, w_ukv=w_ukv,
             w_proj_fox=w_proj_fox, w_proj_mla=w_proj_mla, w_out=w_out, w_ffn_in=w_ffn_in, w_ffn_out=w_ffn_out)
    M = dict(w_ada=m_w_ada, b_ada=m_b_ada, g_pre_mix=m_g_pre_mix, g_post_mix=m_g_post_mix, g_pre_ffn=m_g_pre_ffn,
             g_post_ffn=m_g_post_ffn, w_in=m_w_in, b_forget=m_b_forget, g_q_lora=m_g_q_lora, w_uq=m_w_uq, g_kv_lora=m_g_kv_lora,
             w_ukv=m_w_ukv, w_proj_fox=m_w_proj_fox, w_proj_mla=m_w_proj_mla, w_out=m_w_out, w_ffn_in=m_w_ffn_in,
             w_ffn_out=m_w_ffn_out)
    V = dict(w_ada=v_w_ada, b_ada=v_b_ada, g_pre_mix=v_g_pre_mix, g_post_mix=v_g_post_mix, g_pre_ffn=v_g_pre_ffn,
             g_post_ffn=v_g_post_ffn, w_in=v_w_in, b_forget=v_b_forget, g_q_lora=v_g_q_lora, w_uq=v_w_uq, g_kv_lora=v_g_kv_lora,
             w_ukv=v_w_ukv, w_proj_fox=v_w_proj_fox, w_proj_mla=v_w_proj_mla, w_out=v_w_out, w_ffn_in=v_w_ffn_in,
             w_ffn_out=v_w_ffn_out)
    ax, ay, ac = lax.axis_index("x"), lax.axis_index("y"), lax.axis_index("c")
    chip = 2 * ax + ay
    me = 4 * ax + 2 * ay + ac
    n_ada = w_ada.shape[2]

    c_all = _all_gather_rows(jnp.pad(c, ((0, 7), (0, 0))), name="gather_c")[:, 0, :]
    c_all = jnp.pad(c_all, ((0, 8), (0, 0)))
    b_shard = lax.dynamic_slice(b_ada, (0, chip * n_ada), (1, n_ada))
    mod_blk, silu_c = _ada_mod(c_all, w_ada[0], b_shard, name="ada_mod")
    mod_all = _all_gather_rows(mod_blk, name="gather_mod")
    mod_mine = lax.dynamic_index_in_dim(mod_all, me, axis=1, keepdims=False)
    mod = lax.dynamic_index_in_dim(mod_mine.reshape(4, 2, n_ada), ac, axis=1, keepdims=False).reshape(1, 6 * D)

    W = _early_weights({n: _gather_weight(P[n][0], name="gather_" + n) for n in _EARLY})
    late = {}
    for n in _LATE:
        rh = P[n].shape[1] // 2
        late[n] = lax.dynamic_slice_in_dim(P[n][0], ac * rh, rh, axis=0).astype(BF16)

    gains = {n: P[n] for n in ["g_pre_mix", "g_post_mix", "g_pre_ffn", "g_post_ffn", "g_q_lora", "g_kv_lora"]}
    loss, grad_x, dW, grads, small = _local_step(x[0], loss_target[0], mod, positions, gains, b_forget, W, late)

    shards = _grad_shards(dW)
    grads.update({n: _reduce_scatter_weight(shards[n], name="scatter_" + n) for n in shards})

    small = dict(small, loss=_pad_block(loss))
    row = jnp.concatenate([small[n] for n in _ROW], axis=1)
    rows = _all_gather_rows(jnp.pad(row, ((0, 7), (0, 0))), name="gather_small")[:, 0, :]
    tot = _rowsum(rows, name="sum_small")
    piece = lambda n: tot[:, _ROW[n][0]:_ROW[n][0] + _ROW[n][1]]
    grads["b_ada"] = piece("dmod")
    for n in ["g_pre_mix", "g_post_mix", "g_pre_ffn", "g_post_ffn", "g_q_lora", "g_kv_lora"]:
        grads[n] = piece(n)
    grads["b_forget"] = piece("b_forget")[:, :H]
    loss_out = piece("loss")[0, 0]
    dmod_all = rows[:, _ROW["dmod"][0]:_ROW["dmod"][0] + 6 * D]
    dmod_shard = jnp.pad(lax.dynamic_slice(dmod_all, (0, chip * n_ada), (8, n_ada)), ((0, 8), (0, 0)))
    grads["w_ada"] = _matmul(silu_c, dmod_shard, ta=True, name="ada_dw")

    delta, new_m, new_v = {}, {}, {}
    for n in ["w_ada"] + _BIG:
        delta[n], new_m[n], new_v[n] = _adamw(P[n], grads[n], M[n], V[n], name="adamw_" + n)
    cat = lambda T: jnp.concatenate([T[n] for n in _SMALL], axis=1)
    d_s, m_s, v_s = (t[0] for t in _adamw(cat(P)[None], cat(grads), cat(M)[None], cat(V)[None], name="adamw_small"))
    o = 0
    for n in _SMALL:
        wdt = P[n].shape[1]
        delta[n], new_m[n], new_v[n] = d_s[:, o:o + wdt], m_s[:, o:o + wdt], v_s[:, o:o + wdt]
        o += wdt

    def shaped(T, n):
        return T[n].reshape(P[n].shape)

    return (loss_out, grad_x[None], *[shaped(grads, n) for n in _ORDER], *[shaped(delta, n) for n in _ORDER],
            *[shaped(new_m, n) for n in _ORDER], *[shaped(new_v, n) for n in _ORDER])
```

```python
import functools
import math
from typing import Callable, NamedTuple

import jax
import jax.numpy as jnp
from jax import lax
from jax.experimental import pallas as pl
from jax.experimental.pallas import tpu as pltpu

F32 = jnp.float32
BF16 = jnp.bfloat16
_MXU = jnp.bfloat16

S = 2048
D = 1024
H = 8
HP = 128
FOX_HD = 64
MLA_NOPE = 64
MLA_ROPE = 32
MLA_V = 64
Q_LORA = 768
KV_LORA = 256
D_FF = 2816
NORM_EPS = 1e-6
ROPE_THETA = 10000.0
NEG = -1e30

ADAM_LR = 0.001
ADAM_B1 = 0.9
ADAM_B2 = 0.999
ADAM_EPS = 1e-08
ADAM_WD = 0.01
ADAM_STEP = 10

LANES = 128
VMEM_CAP = 60 * 1024 * 1024
MESH = pl.DeviceIdType.MESH

NQKV = 3 * H * HP
OFF_CQ = 0
OFF_CKV = Q_LORA
OFF_GFOX = 1024
OFF_GMLA = 2048
OFF_FLOG = 3072
OFF_KRIN = 3200
NREST = 3328
KRIN_LANE = 64
ROW_SUM_LANE = 64
COL_SUM_LANE = 65


def _limit(nbytes):
    return int(min(VMEM_CAP, nbytes * 1.25 + (4 << 20)))


def _nbytes(shape, dtype):
    n = 1
    for s in shape:
        n *= s
    return n * jnp.dtype(dtype).itemsize


def _pick(n, cap):
    best = None
    for t in range(LANES, min(n, cap) + 1, LANES):
        if n % t == 0:
            best = t
    return best if best is not None else n


def _pcall(body, *, out_shape, **kw):
    outs = jax.tree.map(lambda s: pltpu.HBM(s.shape, s.dtype), out_shape)
    call = pl.pallas_call(body, out_shape=outs, **kw)
    return lambda *args: call(*[pltpu.with_memory_space_constraint(a, pltpu.HBM) for a in args])


def _matmul(a, b, *, ta=False, tb=False, out_dtype=F32, name, tm_cap=1024, tn_cap=512, tk_cap=1024,
            b_shards=False, out_shards=False, side=None):
    if ta:
        K, M = a.shape
    else:
        M, K = a.shape
    if b_shards:
        _, R, cb = b.shape
        N, K2 = (R, 4 * cb) if tb else (4 * cb, R)
    elif tb:
        N, K2 = b.shape
    else:
        K2, N = b.shape
    assert K == K2, (a.shape, b.shape, ta, tb)
    tm = _pick(M, tm_cap)
    tn = _pick(N, tn_cap)
    tk = K if K <= tk_cap else _pick(K, tk_cap)
    nk = K // tk
    dims = (((0 if ta else 1,), (1 if tb else 0,)), ((), ()))

    n_side_in = len(side.inputs) if side else 0
    n_side_out = len(side.out_shapes) if side else 0
    steps = (M // tm, N // tn, nk)

    def body(*refs):
        a_ref, b_ref = refs[:2]
        side_in = refs[2:2 + n_side_in]
        o_ref = refs[2 + n_side_in]
        side_out = refs[3 + n_side_in:3 + n_side_in + n_side_out]
        acc_ref = refs[3 + n_side_in + n_side_out]
        side_scratch = refs[4 + n_side_in + n_side_out:]
        k = pl.program_id(2)
        if side:
            @pl.when((pl.program_id(0) == 0) & (pl.program_id(1) == 0) & (k == 0))
            def _():
                side.start(side_in, side_out, side_scratch)

        @pl.when(k == 0)
        def _():
            acc_ref[...] = jnp.zeros_like(acc_ref)

        acc_ref[...] += lax.dot_general(a_ref[...].astype(_MXU), b_ref[...].astype(_MXU), dims,
                                        preferred_element_type=F32)

        @pl.when(k == nk - 1)
        def _():
            o_ref[...] = acc_ref[...].astype(out_dtype)

        if side:
            @pl.when((pl.program_id(0) == steps[0] - 1) & (pl.program_id(1) == steps[1] - 1) & (k == nk - 1))
            def _():
                side.finish(side_in, side_out, side_scratch)

    a_spec = pl.BlockSpec((tk, tm), lambda i, j, k: (k, i)) if ta else pl.BlockSpec((tm, tk), lambda i, j, k: (i, k))
    if b_shards and tb:
        assert cb % tk == 0
        per = cb // tk
        b_spec = pl.BlockSpec((None, tn, tk), lambda i, j, k: (k // per, j, k % per))
    elif b_shards:
        assert cb % tn == 0
        per = cb // tn
        b_spec = pl.BlockSpec((None, tk, tn), lambda i, j, k: (j // per, k, j % per))
    elif tb:
        b_spec = pl.BlockSpec((tn, tk), lambda i, j, k: (j, k))
    else:
        b_spec = pl.BlockSpec((tk, tn), lambda i, j, k: (k, j))
    if out_shards:
        assert (N // 4) % tn == 0
        pern = N // 4 // tn
        out_shape = jax.ShapeDtypeStruct((4, M, N // 4), out_dtype)
        out_spec = pl.BlockSpec((None, tm, tn), lambda i, j, k: (j // pern, i, j % pern))
    else:
        out_shape = jax.ShapeDtypeStruct((M, N), out_dtype)
        out_spec = pl.BlockSpec((tm, tn), lambda i, j, k: (i, j))
    need = (2 * _nbytes((tm, tk), a.dtype) + 2 * _nbytes((tk, tn), b.dtype) + 2 * _nbytes((tm, tn), out_dtype)
            + _nbytes((tm, tn), F32) * 2 + _nbytes((tm, tk), _MXU) + _nbytes((tk, tn), _MXU))
    in_specs, args, scratch, extra = [a_spec, b_spec], (a, b), [pltpu.VMEM((tm, tn), F32)], {}
    if side:
        anywhere = pl.BlockSpec(memory_space=pl.ANY)
        in_specs += [anywhere] * n_side_in
        args += tuple(side.inputs)
        out_shape = (out_shape,) + tuple(side.out_shapes)
        out_spec = (out_spec,) + (anywhere,) * n_side_out
        scratch += list(side.scratch)
        extra = dict(input_output_aliases={2 + i: 1 + o for i, o in side.aliases.items()})
    semantics = ("arbitrary",) * 3 if side else ("parallel", "parallel", "arbitrary")
    return _pcall(
        body, name=name, out_shape=out_shape, grid=steps, in_specs=in_specs, out_specs=out_spec, scratch_shapes=scratch,
        compiler_params=pltpu.CompilerParams(dimension_semantics=semantics, vmem_limit_bytes=_limit(need)), **extra,
    )(*args)


def _matmul_carrying(side, *args, **kw):
    out = _matmul(*args, side=side, **kw)
    return out if side else (out,)


TM = 512
TM_FF = 256


def _vec(w):
    return pl.BlockSpec((1, w), lambda i: (0, 0))


def _rows(w, col=0, tm=TM):
    return pl.BlockSpec((tm, w), lambda i: (i, col))


def _row_params(need, carried=False):
    return pltpu.CompilerParams(dimension_semantics=("arbitrary" if carried else "parallel",),
                                vmem_limit_bytes=_limit(need))


def _norm_mod(x, col, w, g, ops, sh, *, name):
    def body(x_ref, g_ref, ops_ref, sh_ref, o_ref):
        xv = x_ref[...]
        r = lax.rsqrt(jnp.mean(xv * xv, axis=-1, keepdims=True) + NORM_EPS)
        o_ref[...] = (((xv * r) * g_ref[...]) * ops_ref[...] + sh_ref[...]).astype(o_ref.dtype)

    return _pcall(
        body, name=name, out_shape=jax.ShapeDtypeStruct((S, w), _MXU), grid=(S // TM,),
        in_specs=[_rows(w, col), _vec(w), _vec(w), _vec(w)], out_specs=_rows(w),
        compiler_params=_row_params(8 * _nbytes((TM, w), F32)),
    )(x, g, ops, sh)


def _norm_mod_bwd(x, col, w, dh, g, ops, dres, *, name):
    has_res = dres is not None

    def body(*refs):
        if has_res:
            x_ref, dh_ref, g_ref, ops_ref, dres_ref, dx_ref, s1_ref, s2_ref = refs
        else:
            x_ref, dh_ref, g_ref, ops_ref, dx_ref, s1_ref, s2_ref = refs
        i = pl.program_id(0)

        @pl.when(i == 0)
        def _():
            s1_ref[...] = jnp.zeros_like(s1_ref)
            s2_ref[...] = jnp.zeros_like(s2_ref)

        xv = x_ref[...]
        dhv = dh_ref[...]
        r = lax.rsqrt(jnp.mean(xv * xv, axis=-1, keepdims=True) + NORM_EPS)
        xn = xv * r
        dxn = dhv * (g_ref[...] * ops_ref[...])
        dx = r * (dxn - xn * jnp.mean(dxn * xn, axis=-1, keepdims=True))
        if has_res:
            dx = dx + dres_ref[...]
        dx_ref[...] = dx
        s1_ref[...] += jnp.sum(dhv, axis=0, keepdims=True)
        s2_ref[...] += jnp.sum(dhv * xn, axis=0, keepdims=True)

    in_specs = [_rows(w, col), _rows(w), _vec(w), _vec(w)] + ([_rows(w)] if has_res else [])
    args = (x, dh, g, ops) + ((dres,) if has_res else ())
    return _pcall(
        body, name=name,
        out_shape=(jax.ShapeDtypeStruct((S, w), F32), jax.ShapeDtypeStruct((1, w), F32), jax.ShapeDtypeStruct((1, w), F32)),
        grid=(S // TM,), in_specs=in_specs, out_specs=(_rows(w), _vec(w), _vec(w)),
        compiler_params=_row_params(12 * _nbytes((TM, w), F32), carried=True),
    )(*args)


def _post_res(xres, y, g, gt, *, name):
    def body(x_ref, y_ref, g_ref, gt_ref, o_ref):
        yv = y_ref[...]
        r = lax.rsqrt(jnp.mean(yv * yv, axis=-1, keepdims=True) + NORM_EPS)
        o_ref[...] = x_ref[...] + gt_ref[...] * ((yv * r) * g_ref[...])

    return _pcall(
        body, name=name, out_shape=jax.ShapeDtypeStruct((S, D), F32), grid=(S // TM,),
        in_specs=[_rows(D), _rows(D), _vec(D), _vec(D)], out_specs=_rows(D),
        compiler_params=_row_params(8 * _nbytes((TM, D), F32)),
    )(xres, y, g, gt)


def _post_res_loss(xres, y, g, gt, target, *, name):
    def body(x_ref, y_ref, g_ref, gt_ref, t_ref, dout_ref, loss_ref):
        i = pl.program_id(0)

        @pl.when(i == 0)
        def _():
            loss_ref[...] = jnp.zeros_like(loss_ref)

        yv = y_ref[...]
        r = lax.rsqrt(jnp.mean(yv * yv, axis=-1, keepdims=True) + NORM_EPS)
        out = x_ref[...] + gt_ref[...] * ((yv * r) * g_ref[...])
        err = out - t_ref[...]
        dout_ref[...] = err * (1.0 / D)
        per_row = jnp.mean(err * err, axis=-1, keepdims=True)
        loss_ref[...] += 0.5 * jnp.sum(per_row, axis=0, keepdims=True)

    return _pcall(
        body, name=name,
        out_shape=(jax.ShapeDtypeStruct((S, D), F32), jax.ShapeDtypeStruct((1, 1), F32)), grid=(S // TM,),
        in_specs=[_rows(D), _rows(D), _vec(D), _vec(D), _rows(D)],
        out_specs=(_rows(D), pl.BlockSpec((1, 1), lambda i: (0, 0))),
        compiler_params=_row_params(10 * _nbytes((TM, D), F32), carried=True),
    )(xres, y, g, gt, target)


def _post_res_bwd(dxn, y, g, gt, *, name):
    def body(d_ref, y_ref, g_ref, gt_ref, dy_ref, sgt_ref, sg_ref):
        i = pl.program_id(0)

        @pl.when(i == 0)
        def _():
            sgt_ref[...] = jnp.zeros_like(sgt_ref)
            sg_ref[...] = jnp.zeros_like(sg_ref)

        yv = y_ref[...]
        dv = d_ref[...]
        r = lax.rsqrt(jnp.mean(yv * yv, axis=-1, keepdims=True) + NORM_EPS)
        yn = yv * r
        dn = dv * gt_ref[...]
        dyn = dn * g_ref[...]
        dy_ref[...] = (r * (dyn - yn * jnp.mean(dyn * yn, axis=-1, keepdims=True))).astype(dy_ref.dtype)
        sgt_ref[...] += jnp.sum(dv * (yn * g_ref[...]), axis=0, keepdims=True)
        sg_ref[...] += jnp.sum(dn * yn, axis=0, keepdims=True)

    return _pcall(
        body, name=name,
        out_shape=(jax.ShapeDtypeStruct((S, D), _MXU), jax.ShapeDtypeStruct((1, D), F32), jax.ShapeDtypeStruct((1, D), F32)),
        grid=(S // TM,), in_specs=[_rows(D), _rows(D), _vec(D), _vec(D)], out_specs=(_rows(D), _vec(D), _vec(D)),
        compiler_params=_row_params(10 * _nbytes((TM, D), F32), carried=True),
    )(dxn, y, g, gt)


def _swiglu(gu, *, name):
    def body(g_ref, u_ref, o_ref):
        gv = g_ref[...]
        o_ref[...] = ((gv * jax.nn.sigmoid(gv)) * u_ref[...]).astype(o_ref.dtype)

    return _pcall(
        body, name=name, out_shape=jax.ShapeDtypeStruct((S, D_FF), _MXU), grid=(S // TM_FF,),
        in_specs=[_rows(D_FF, 0, TM_FF), _rows(D_FF, 1, TM_FF)], out_specs=_rows(D_FF, 0, TM_FF),
        compiler_params=_row_params(8 * _nbytes((TM_FF, D_FF), F32)),
    )(gu, gu)


def _swiglu_bwd(gu, dact, *, name):
    def body(g_ref, u_ref, d_ref, o_ref):
        gv = g_ref[...]
        dv = d_ref[...]
        sg = jax.nn.sigmoid(gv)
        o_ref[:, :D_FF] = (dv * u_ref[...] * (sg * (1.0 + gv * (1.0 - sg)))).astype(o_ref.dtype)
        o_ref[:, D_FF:] = (dv * (gv * sg)).astype(o_ref.dtype)

    return _pcall(
        body, name=name, out_shape=jax.ShapeDtypeStruct((S, 2 * D_FF), _MXU), grid=(S // TM_FF,),
        in_specs=[_rows(D_FF, 0, TM_FF), _rows(D_FF, 1, TM_FF), _rows(D_FF, 0, TM_FF)], out_specs=_rows(2 * D_FF, 0, TM_FF),
        compiler_params=_row_params(12 * _nbytes((TM_FF, D_FF), F32)),
    )(gu, gu, dact)


def _merge(rest, pa, pb, *, name):
    def body(ga_ref, gb_ref, pa_ref, pb_ref, o_ref):
        o_ref[...] = (jax.nn.sigmoid(ga_ref[...]) * pa_ref[...] + jax.nn.sigmoid(gb_ref[...]) * pb_ref[...]).astype(o_ref.dtype)

    return _pcall(
        body, name=name, out_shape=jax.ShapeDtypeStruct((S, D), _MXU), grid=(S // TM,),
        in_specs=[_rows(D, OFF_GFOX // D), _rows(D, OFF_GMLA // D), _rows(D), _rows(D)], out_specs=_rows(D),
        compiler_params=_row_params(10 * _nbytes((TM, D), F32)),
    )(rest, rest, pa, pb)


def _merge_bwd(rest, pa, pb, dm, *, name):
    def body(ga_ref, gb_ref, pa_ref, pb_ref, d_ref, dpa_ref, dpb_ref, dga_ref, dgb_ref):
        dv = d_ref[...]
        sa = jax.nn.sigmoid(ga_ref[...])
        sb = jax.nn.sigmoid(gb_ref[...])
        dpa_ref[...] = (dv * sa).astype(dpa_ref.dtype)
        dpb_ref[...] = (dv * sb).astype(dpb_ref.dtype)
        dga_ref[...] = (dv * pa_ref[...] * (sa * (1.0 - sa))).astype(dga_ref.dtype)
        dgb_ref[...] = (dv * pb_ref[...] * (sb * (1.0 - sb))).astype(dgb_ref.dtype)

    o = jax.ShapeDtypeStruct((S, D), _MXU)
    return _pcall(
        body, name=name, out_shape=(o, o, o, o), grid=(S // TM,),
        in_specs=[_rows(D, OFF_GFOX // D), _rows(D, OFF_GMLA // D), _rows(D), _rows(D), _rows(D)],
        out_specs=(_rows(D), _rows(D), _rows(D), _rows(D)),
        compiler_params=_row_params(16 * _nbytes((TM, D), F32)),
    )(rest, rest, pa, pb, dm)


SCAN = 256


def _split_dot(tri, x):
    hi = x.astype(_MXU)
    r1 = x - hi.astype(F32)
    mid = r1.astype(_MXU)
    lo = (r1 - mid.astype(F32)).astype(_MXU)
    dot = functools.partial(jnp.dot, preferred_element_type=F32)
    return dot(tri, hi) + dot(tri, mid) + dot(tri, lo)


def _fox_prep(rest, bf, *, name):
    def body(z_ref, b_ref, f_ref):
        lane = lax.broadcasted_iota(jnp.int32, (SCAN, LANES), 1)
        tri = (lax.broadcasted_iota(jnp.int32, (SCAN, SCAN), 1) <= lax.broadcasted_iota(jnp.int32, (SCAN, SCAN), 0)).astype(_MXU)
        carry = jnp.zeros((1, LANES), F32)
        for c in range(S // SCAN):
            z = z_ref[c * SCAN:(c + 1) * SCAN, :] + b_ref[...]
            lf = jnp.minimum(z, 0.0) - jnp.log(1.0 + jnp.exp(-jnp.abs(z)))
            lf = jnp.where(lane < H, lf, 0.0)
            cum = _split_dot(tri, lf) + carry
            f_ref[c * SCAN:(c + 1) * SCAN, :] = cum
            carry = cum[SCAN - 1:SCAN, :]

    return _pcall(
        body, name=name, out_shape=jax.ShapeDtypeStruct((S, LANES), F32), grid=(1,),
        in_specs=[pl.BlockSpec((S, LANES), lambda i: (0, OFF_FLOG // LANES)), pl.BlockSpec((1, LANES), lambda i: (0, 0))],
        out_specs=pl.BlockSpec((S, LANES), lambda i: (0, 0)),
        compiler_params=pltpu.CompilerParams(vmem_limit_bytes=_limit(8 * _nbytes((S, LANES), F32))),
    )(rest, bf)


def _fox_bwd_prep(rest, bf, dF, *, name):
    def body(z_ref, b_ref, d_ref, o_ref, db_ref):
        lane = lax.broadcasted_iota(jnp.int32, (SCAN, LANES), 1)
        tri = (lax.broadcasted_iota(jnp.int32, (SCAN, SCAN), 1) >= lax.broadcasted_iota(jnp.int32, (SCAN, SCAN), 0)).astype(_MXU)
        carry = jnp.zeros((1, LANES), F32)
        db = jnp.zeros((1, LANES), F32)
        for c in range(S // SCAN - 1, -1, -1):
            rc = _split_dot(tri, d_ref[c * SCAN:(c + 1) * SCAN, :]) + carry
            z = z_ref[c * SCAN:(c + 1) * SCAN, :] + b_ref[...]
            dz = jnp.where(lane < H, rc * jax.nn.sigmoid(-z), 0.0)
            o_ref[c * SCAN:(c + 1) * SCAN, :] = dz
            db = db + jnp.sum(dz, axis=0, keepdims=True)
            carry = rc[0:1, :]
        db_ref[...] = db

    return _pcall(
        body, name=name,
        out_shape=(jax.ShapeDtypeStruct((S, LANES), F32), jax.ShapeDtypeStruct((1, LANES), F32)), grid=(1,),
        in_specs=[pl.BlockSpec((S, LANES), lambda i: (0, OFF_FLOG // LANES)), pl.BlockSpec((1, LANES), lambda i: (0, 0)),
                  pl.BlockSpec((S, LANES), lambda i: (0, 0))],
        out_specs=(pl.BlockSpec((S, LANES), lambda i: (0, 0)), pl.BlockSpec((1, LANES), lambda i: (0, 0))),
        compiler_params=pltpu.CompilerParams(vmem_limit_bytes=_limit(10 * _nbytes((S, LANES), F32))),
    )(rest, bf, dF)


def _swap16(x):
    lane = lax.broadcasted_iota(jnp.int32, x.shape, 1)
    half = MLA_ROPE // 2
    sw = jnp.where(lane < KRIN_LANE + half, pltpu.roll(x, LANES - half, 1), pltpu.roll(x, half, 1))
    return jnp.where((lane >= KRIN_LANE) & (lane < KRIN_LANE + MLA_ROPE), sw, 0.0)


def _mla_assemble(qb, kvb, rest, ctab, stab, *, name):
    def body(q_ref, kk_ref, kv_ref, kr_ref, c_ref, s_ref, qo_ref, ko_ref, vo_ref):
        cv = c_ref[...]
        sv = s_ref[...]
        kr = kr_ref[...]
        kpe = kr * cv + _swap16(kr) * sv
        for h in range(H):
            sl = slice(h * HP, (h + 1) * HP)
            qh = q_ref[:, sl]
            qo_ref[:, sl] = (qh * cv + _swap16(qh) * sv).astype(qo_ref.dtype)
            ko_ref[:, sl] = (kk_ref[:, sl] + kpe).astype(ko_ref.dtype)
        vo_ref[...] = kv_ref[...].astype(vo_ref.dtype)

    o = jax.ShapeDtypeStruct((S, H * HP), _MXU)
    return _pcall(
        body, name=name, out_shape=(o, o, o), grid=(S // TM,),
        in_specs=[_rows(H * HP), _rows(H * HP, 0), _rows(H * HP, 1), _rows(LANES, OFF_KRIN // LANES), _rows(LANES), _rows(LANES)],
        out_specs=(_rows(H * HP), _rows(H * HP), _rows(H * HP)),
        compiler_params=_row_params(14 * _nbytes((TM, H * HP), F32)),
    )(qb, kvb, kvb, rest, ctab, stab)


def _mla_assemble_bwd(dq, dk, dv, ctab, stab, *, name):
    def body(dq_ref, dk_ref, dv_ref, c_ref, s_ref, dqo_ref, dkv_ref, dkr_ref):
        cv = c_ref[...]
        sv = s_ref[...]
        lane = lax.broadcasted_iota(jnp.int32, (TM, LANES), 1)
        dsum = jnp.zeros((TM, LANES), F32)
        for h in range(H):
            sl = slice(h * HP, (h + 1) * HP)
            dqh = dq_ref[:, sl]
            dqo_ref[:, sl] = (dqh * cv + _swap16(dqh * sv)).astype(dqo_ref.dtype)
            dsum = dsum + dk_ref[:, sl]
        dkv_ref[:, :H * HP] = dk_ref[...].astype(dkv_ref.dtype)
        dkv_ref[:, H * HP:] = dv_ref[...].astype(dkv_ref.dtype)
        dkr = dsum * cv + _swap16(dsum * sv)
        dkr_ref[...] = jnp.where((lane >= KRIN_LANE) & (lane < KRIN_LANE + MLA_ROPE), dkr, 0.0)

    return _pcall(
        body, name=name,
        out_shape=(jax.ShapeDtypeStruct((S, H * HP), _MXU), jax.ShapeDtypeStruct((S, 2 * H * HP), _MXU),
                   jax.ShapeDtypeStruct((S, LANES), F32)),
        grid=(S // TM,),
        in_specs=[_rows(H * HP), _rows(H * HP), _rows(H * HP), _rows(LANES), _rows(LANES)],
        out_specs=(_rows(H * HP), _rows(2 * H * HP), _rows(LANES)),
        compiler_params=_row_params(14 * _nbytes((TM, H * HP), F32)),
    )(dq, dk, dv, ctab, stab)


TQ = 512
TKF = 512
TKB = 512
TQB = 512
_NT = (((1,), (1,)), ((), ()))
_TN = (((0,), (0,)), ((), ()))


def _is_pow2(x):
    return math.frexp(x)[0] == 0.5


def _attn_fwd(q, k, v, frow, *, scale, name, side=None, heads0=(0, 0, 0)):
    has_decay = frow is not None
    fold = _is_pow2(scale)
    n_in = 4 if has_decay else 3
    n_side_in = len(side.inputs) if side else 0
    n_side_out = len(side.out_shapes) if side else 0

    def body(*refs):
        q_ref, k_ref, v_ref = refs[:3]
        fr_ref = refs[3] if has_decay else None
        side_in = refs[n_in:n_in + n_side_in]
        o_ref, lse_ref = refs[n_in + n_side_in:n_in + n_side_in + 2]
        side_out = refs[n_in + n_side_in + 2:n_in + n_side_in + 2 + n_side_out]
        side_scratch = refs[n_in + n_side_in + 2 + n_side_out:]
        i = pl.program_id(1)
        if side:
            @pl.when((pl.program_id(0) == 0) & (i == 0))
            def _():
                side.start(side_in, side_out, side_scratch)
        qv = q_ref[...]
        if fold:
            qv = (qv * scale).astype(qv.dtype)
        last = (i * TQ) // TKF

        def tile(j, carry, masked):
            m, l, acc = carry
            k0 = pl.multiple_of(j * TKF, TKF)
            kj = k_ref[pl.ds(k0, TKF), :]
            vj = v_ref[pl.ds(k0, TKF), :]
            s = lax.dot_general(qv, kj, _NT, preferred_element_type=F32)
            if not fold:
                s = s * scale
            if has_decay:
                s = s - fr_ref[0, j]
            if masked:
                rows = i * TQ + lax.broadcasted_iota(jnp.int32, (TQ, TKF), 0)
                cols = j * TKF + lax.broadcasted_iota(jnp.int32, (TQ, TKF), 1)
                s = jnp.where(cols <= rows, s, NEG)
            m_new = jnp.maximum(m, jnp.max(s, axis=-1, keepdims=True))
            alpha = jnp.exp(m - m_new)
            p = jnp.exp(s - m_new)
            l = alpha * l + jnp.sum(p, axis=-1, keepdims=True)
            acc = alpha * acc + jnp.dot(p.astype(_MXU), vj, preferred_element_type=F32)
            return m_new, l, acc

        init = (jnp.full((TQ, 1), NEG, F32), jnp.zeros((TQ, 1), F32), jnp.zeros((TQ, HP), F32))
        carry = lax.fori_loop(0, last, lambda j, c: tile(j, c, False), init)
        m, l, acc = tile(last, carry, True)
        o_ref[...] = acc / l
        lse_ref[0] = m + jnp.log(l)
        if side:
            @pl.when((pl.program_id(0) == H - 1) & (i == S // TQ - 1))
            def _():
                side.finish(side_in, side_out, side_scratch)

    q0, k0, v0 = heads0
    in_specs = [pl.BlockSpec((TQ, HP), lambda h, i: (i, h + q0)), pl.BlockSpec((S, HP), lambda h, i: (0, h + k0)),
                pl.BlockSpec((S, HP), lambda h, i: (0, h + v0))]
    args = (q, k, v)
    if has_decay:
        in_specs += [pl.BlockSpec((1, S // TKF, 1, TKF), lambda h, i: (h, 0, 0, 0))]
        args += (frow,)
    out_shape = (jax.ShapeDtypeStruct((S, H * HP), F32), jax.ShapeDtypeStruct((H, S, 1), F32))
    out_specs = (pl.BlockSpec((TQ, HP), lambda h, i: (i, h)), pl.BlockSpec((1, TQ, 1), lambda h, i: (h, i, 0)))
    extra = {}
    if side:
        anywhere = pl.BlockSpec(memory_space=pl.ANY)
        in_specs += [anywhere] * n_side_in
        args += tuple(side.inputs)
        out_shape += tuple(side.out_shapes)
        out_specs += (anywhere,) * n_side_out
        extra = dict(scratch_shapes=list(side.scratch),
                     input_output_aliases={n_in + a: 2 + b for a, b in side.aliases.items()})
    return _pcall(
        body, name=name, out_shape=out_shape, grid=(H, S // TQ), in_specs=in_specs, out_specs=out_specs,
        compiler_params=pltpu.CompilerParams(dimension_semantics=("arbitrary", "arbitrary") if side else ("parallel", "parallel"),
                                             vmem_limit_bytes=_limit(8 * _nbytes((S, HP), F32))),
        **extra,
    )(*args)


def _attn_delta(do, o, *, name):
    def body(do_ref, o_ref, dl_ref, dob_ref):
        for h in range(H):
            sl = slice(h * HP, (h + 1) * HP)
            dl_ref[h] = jnp.sum(do_ref[:, sl] * o_ref[:, sl], axis=-1, keepdims=True)
        dob_ref[...] = do_ref[...].astype(dob_ref.dtype)

    return _pcall(
        body, name=name,
        out_shape=(jax.ShapeDtypeStruct((H, S, 1), F32), jax.ShapeDtypeStruct((S, H * HP), _MXU)), grid=(S // TM,),
        in_specs=[_rows(H * HP), _rows(H * HP)],
        out_specs=(pl.BlockSpec((H, TM, 1), lambda i: (0, i, 0)), _rows(H * HP)),
        compiler_params=_row_params(8 * _nbytes((TM, H * HP), F32)),
    )(do, o)


def _attn_bwd(q, k, v, dob, lse_row, delta_row, fcol, *, scale, name, side=None, heads0=(0, 0, 0)):
    has_decay = fcol is not None
    fold = _is_pow2(scale)
    n_in = 7 if has_decay else 6
    n_side_in = len(side.inputs) if side else 0
    n_side_out = len(side.out_shapes) if side else 0

    def body(*refs):
        q_ref, k_ref, v_ref, do_ref, lse_ref, dl_ref = refs[:6]
        fc_ref = refs[6] if has_decay else None
        side_in = refs[n_in:n_in + n_side_in]
        dq_ref, dk_ref, dv_ref = refs[n_in + n_side_in:n_in + n_side_in + 3]
        side_out = refs[n_in + n_side_in + 3:n_in + n_side_in + 3 + n_side_out]
        dq_acc = refs[n_in + n_side_in + 3 + n_side_out]
        side_scratch = refs[n_in + n_side_in + 4 + n_side_out:]
        j = pl.program_id(1)
        if side:
            @pl.when((pl.program_id(0) == 0) & (j == 0))
            def _():
                side.start(side_in, side_out, side_scratch)

        @pl.when(j == 0)
        def _():
            dq_acc[...] = jnp.zeros_like(dq_acc)

        kj = k_ref[...]
        vj = v_ref[...]
        kjs = (kj * scale).astype(kj.dtype) if fold else kj
        if has_decay:
            klane = lax.broadcasted_iota(jnp.int32, (TKB, HP), 1)
            kj = jnp.where(klane == ROW_SUM_LANE, 1.0, kj).astype(kj.dtype)
        first = (j * TKB) // TQB

        def tile(t, carry, masked):
            dk, dv = carry
            r0 = pl.multiple_of(t * TQB, TQB)
            qi = q_ref[pl.ds(r0, TQB), :]
            doi = do_ref[pl.ds(r0, TQB), :]
            st = lax.dot_general(kjs, qi, _NT, preferred_element_type=F32)
            if not fold:
                st = st * scale
            if has_decay:
                st = st - fc_ref[0]
            if masked:
                keys = j * TKB + lax.broadcasted_iota(jnp.int32, (TKB, TQB), 0)
                qpos = t * TQB + lax.broadcasted_iota(jnp.int32, (TKB, TQB), 1)
                st = jnp.where(keys <= qpos, st, NEG)
            pt = jnp.exp(st - lse_ref[0, t])
            dv = dv + jnp.dot(pt.astype(_MXU), doi, preferred_element_type=F32)
            dpt = lax.dot_general(vj, doi, _NT, preferred_element_type=F32)
            dst = (pt * (dpt - dl_ref[0, t])).astype(_MXU)
            if has_decay:
                lane = lax.broadcasted_iota(jnp.int32, (TQB, HP), 1)
                qi = jnp.where(lane == COL_SUM_LANE, 1.0, qi).astype(qi.dtype)
            dk = dk + jnp.dot(dst, qi, preferred_element_type=F32)
            dq_acc[pl.ds(r0, TQB), :] += lax.dot_general(dst, kj, _TN, preferred_element_type=F32)
            return dk, dv

        zero = jnp.zeros((TKB, HP), F32)
        carry = tile(first, (zero, zero), True)
        dk, dv = lax.fori_loop(first + 1, S // TQB, lambda t, c: tile(t, c, False), carry)
        dk_ref[...] = dk * scale
        dv_ref[...] = dv

        @pl.when(j == S // TKB - 1)
        def _():
            dq_ref[...] = dq_acc[...] * scale

        if side:
            @pl.when((pl.program_id(0) == H - 1) & (j == S // TKB - 1))
            def _():
                side.finish(side_in, side_out, side_scratch)

    q0, k0, v0 = heads0
    head = pl.BlockSpec((S, HP), lambda h, j: (0, h))
    kv = pl.BlockSpec((TKB, HP), lambda h, j: (j, h))
    stat = pl.BlockSpec((1, S // TQB, 1, TQB), lambda h, j: (h, 0, 0, 0))
    in_specs = [pl.BlockSpec((S, HP), lambda h, j: (0, h + q0)), pl.BlockSpec((TKB, HP), lambda h, j: (j, h + k0)),
                pl.BlockSpec((TKB, HP), lambda h, j: (j, h + v0)), head, stat, stat]
    args = (q, k, v, dob, lse_row, delta_row)
    if has_decay:
        in_specs += [pl.BlockSpec((1, TKB, 1), lambda h, j: (h, j, 0))]
        args += (fcol,)
    o = jax.ShapeDtypeStruct((S, H * HP), F32)
    out_shape, out_specs, scratch, aliases = (o, o, o), (head, kv, kv), [pltpu.VMEM((S, HP), F32)], {}
    if side:
        anywhere = pl.BlockSpec(memory_space=pl.ANY)
        in_specs += [anywhere] * n_side_in
        args += tuple(side.inputs)
        out_shape += tuple(side.out_shapes)
        out_specs += (anywhere,) * n_side_out
        scratch += list(side.scratch)
        aliases = {n_in + a: 3 + b for a, b in side.aliases.items()}
    return _pcall(
        body, name=name, out_shape=out_shape, grid=(H, S // TKB), in_specs=in_specs,
        out_specs=out_specs, scratch_shapes=scratch, input_output_aliases=aliases,
        compiler_params=pltpu.CompilerParams(dimension_semantics=("arbitrary" if side else "parallel", "arbitrary"),
                                             vmem_limit_bytes=_limit(12 * _nbytes((S, HP), F32))),
    )(*args)


def _ada_mod(c_all, w_shard, b_shard, *, name):
    R = c_all.shape[0]
    N = w_shard.shape[1]
    tn = 512

    def body(c_ref, w_ref, b_ref, o_ref, sc_ref):
        cv = c_ref[...]
        sc = (cv * jax.nn.sigmoid(cv)).astype(_MXU)
        sc_ref[...] = sc
        o_ref[...] = jnp.dot(sc, w_ref[...].astype(_MXU), preferred_element_type=F32) + b_ref[...]

    return _pcall(
        body, name=name,
        out_shape=(jax.ShapeDtypeStruct((R, N), F32), jax.ShapeDtypeStruct((R, D), _MXU)), grid=(N // tn,),
        in_specs=[pl.BlockSpec((R, D), lambda j: (0, 0)), pl.BlockSpec((D, tn), lambda j: (0, j)), pl.BlockSpec((1, tn), lambda j: (0, j))],
        out_specs=(pl.BlockSpec((R, tn), lambda j: (0, j)), pl.BlockSpec((R, D), lambda j: (0, 0))),
        compiler_params=pltpu.CompilerParams(dimension_semantics=("arbitrary",), vmem_limit_bytes=_limit(6 * _nbytes((D, tn), F32))),
    )(c_all, w_shard, b_shard)


def _rowsum(x, *, name):
    R, L = x.shape

    def body(x_ref, o_ref):
        acc = x_ref[0:1, :]
        for r in range(1, R):
            acc = acc + x_ref[r:r + 1, :]
        o_ref[...] = acc

    return pl.pallas_call(body, name=name, out_shape=jax.ShapeDtypeStruct((1, L), F32),
                          in_specs=[pl.BlockSpec(memory_space=pltpu.VMEM)], out_specs=pl.BlockSpec(memory_space=pltpu.VMEM))(x)


def _adamw(w, g, m, v, *, name):
    _, R, C = w.shape
    tr = R
    for t in range(8, R + 1, 8):
        if R % t == 0 and t * C * 4 <= (1 << 20):
            tr = t

    def body(w_ref, g_ref, m_ref, v_ref, d_ref, mo_ref, vo_ref):
        gv = g_ref[...]
        m2 = ADAM_B1 * m_ref[...] + (1.0 - ADAM_B1) * gv
        v2 = ADAM_B2 * v_ref[...] + (1.0 - ADAM_B2) * (gv * gv)
        m_hat = m2 / (1.0 - ADAM_B1 ** ADAM_STEP)
        v_hat = v2 / (1.0 - ADAM_B2 ** ADAM_STEP)
        d_ref[...] = -ADAM_LR * (m_hat / (jnp.sqrt(v_hat) + ADAM_EPS) + ADAM_WD * w_ref[...])
        mo_ref[...] = m2
        vo_ref[...] = v2

    blk = pl.BlockSpec((None, tr, C), lambda i: (0, i, 0))
    o = jax.ShapeDtypeStruct((1, R, C), F32)
    return _pcall(
        body, name=name, out_shape=(o, o, o), grid=(R // tr,),
        in_specs=[blk, pl.BlockSpec((tr, C), lambda i: (i, 0)), blk, blk], out_specs=(blk, blk, blk),
        compiler_params=pltpu.CompilerParams(dimension_semantics=("parallel",), vmem_limit_bytes=_limit(20 * _nbytes((tr, C), F32))),
    )(w, g, m, v)


def _place():
    x, y, c = lax.axis_index("x"), lax.axis_index("y"), lax.axis_index("c")
    return x, y, c, [(1 - x, y), (x, 1 - y), (1 - x, 1 - y)]


def _two_level_gather(x_ref, out_ref, send_sems, recv_sems, local_sem):
    x, y, c, chips = _place()
    me, sibling = (x, y, c), (x, y, 1 - c)

    def blk(px, py, pc):
        return out_ref.at[4 * px + 2 * py + pc]

    def copy(k, block, to, src=None):
        return pltpu.make_async_remote_copy(
            src_ref=blk(*block) if src is None else src, dst_ref=blk(*block),
            send_sem=send_sems.at[k], recv_sem=recv_sems.at[k], device_id=to, device_id_type=MESH)

    mine = pltpu.make_async_copy(x_ref, blk(*me), local_sem)
    mine.start()
    first = [copy(0, me, sibling, src=x_ref)]
    first += [copy(1 + j, me, (*chip, c), src=x_ref) for j, chip in enumerate(chips)]
    for cp in first:
        cp.start()
    passed = [copy(4 + j, (*chip, c), sibling) for j, chip in enumerate(chips)]
    for j, chip in enumerate(chips):
        copy(1 + j, (*chip, c), me).wait_recv()
        passed[j].start()
    copy(0, sibling, me).wait_recv()
    for j, chip in enumerate(chips):
        copy(4 + j, (*chip, 1 - c), me).wait_recv()
    for cp in first + passed:
        cp.wait_send()
    mine.wait()


_GATHER_SEMS = [pltpu.SemaphoreType.DMA((7,)), pltpu.SemaphoreType.DMA((7,)), pltpu.SemaphoreType.DMA]


class _SideJob(NamedTuple):
    inputs: tuple
    out_shapes: tuple
    aliases: dict
    scratch: tuple
    start: Callable
    finish: Callable


def _block_index(px, py, pc):
    return 4 * px + 2 * py + pc


def _gather_level1_job(halves):
    nw = len(halves)

    def copies(ins, outs, scratch, n):
        sends, recvs, _ = scratch
        x, y, c, chips = _place()
        mine = outs[n].at[_block_index(x, y, c)]
        peers = [(x, y, 1 - c)] + [(*chip, c) for chip in chips]
        out = []
        for t, peer in enumerate(peers):
            sem = dict(send_sem=sends.at[4 * n + t], recv_sem=recvs.at[4 * n + t], device_id_type=MESH)
            landing = outs[n].at[_block_index(*peer)]
            out.append((pltpu.make_async_remote_copy(src_ref=ins[n], dst_ref=mine, device_id=peer, **sem),
                        pltpu.make_async_remote_copy(src_ref=landing, dst_ref=landing, device_id=peer, **sem)))
        local = pltpu.make_async_copy(ins[n], mine, scratch[2].at[n])
        return out, local

    def start(ins, outs, scratch):
        for n in range(nw):
            pairs, local = copies(ins, outs, scratch, n)
            local.start()
            for to, _ in pairs:
                to.start()

    def finish(ins, outs, scratch):
        for n in range(nw):
            pairs, local = copies(ins, outs, scratch, n)
            for to, frm in pairs:
                frm.wait_recv()
                to.wait_send()
            local.wait()

    return _SideJob(
        inputs=tuple(halves), out_shapes=tuple(jax.ShapeDtypeStruct((8,) + h.shape, h.dtype) for h in halves), aliases={},
        scratch=(pltpu.SemaphoreType.DMA((4 * nw,)), pltpu.SemaphoreType.DMA((4 * nw,)), pltpu.SemaphoreType.DMA((nw,))),
        start=start, finish=finish)


def _gather_direct_job(halves):
    nw = len(halves)

    def copies(ins, outs, scratch, n):
        sends, recvs, _ = scratch
        x, y, c, _ = _place()
        mine = outs[n].at[_block_index(x, y, c)]
        out = []
        for f, (fx, fy, fc) in enumerate(_FLIPS):
            peer = (_flipped(x, fx), _flipped(y, fy), _flipped(c, fc))
            sem = dict(send_sem=sends.at[7 * n + f], recv_sem=recvs.at[7 * n + f], device_id=peer, device_id_type=MESH)
            landing = outs[n].at[_block_index(*peer)]
            out.append((pltpu.make_async_remote_copy(src_ref=ins[n], dst_ref=mine, **sem),
                        pltpu.make_async_remote_copy(src_ref=landing, dst_ref=landing, **sem)))
        return out, pltpu.make_async_copy(ins[n], mine, scratch[2].at[n])

    def start(ins, outs, scratch):
        for n in range(nw):
            pairs, local = copies(ins, outs, scratch, n)
            local.start()
            for to, _ in pairs:
                to.start()

    def finish(ins, outs, scratch):
        for n in range(nw):
            pairs, local = copies(ins, outs, scratch, n)
            for to, frm in pairs:
                frm.wait_recv()
                to.wait_send()
            local.wait()

    return _SideJob(
        inputs=tuple(halves), out_shapes=tuple(jax.ShapeDtypeStruct((8,) + h.shape, h.dtype) for h in halves), aliases={},
        scratch=(pltpu.SemaphoreType.DMA((7 * nw,)), pltpu.SemaphoreType.DMA((7 * nw,)), pltpu.SemaphoreType.DMA((nw,))),
        start=start, finish=finish)


def _join_jobs(a, b):
    cut = (len(a.inputs), len(a.out_shapes), len(a.scratch))

    def parts(ins, outs, scratch):
        return ((ins[:cut[0]], outs[:cut[1]], scratch[:cut[2]]), (ins[cut[0]:], outs[cut[1]:], scratch[cut[2]:]))

    def start(ins, outs, scratch):
        pa, pb = parts(ins, outs, scratch)
        a.start(*pa)
        b.start(*pb)

    def finish(ins, outs, scratch):
        pa, pb = parts(ins, outs, scratch)
        a.finish(*pa)
        b.finish(*pb)

    aliases = dict(a.aliases)
    aliases.update({cut[0] + i: cut[1] + o for i, o in b.aliases.items()})
    return _SideJob(inputs=a.inputs + b.inputs, out_shapes=a.out_shapes + b.out_shapes, aliases=aliases,
                    scratch=a.scratch + b.scratch, start=start, finish=finish)


def _gather_level2_job(gathered):
    nw = len(gathered)

    def copies(outs, scratch, n):
        sends, recvs = scratch
        x, y, c, chips = _place()
        out = []
        for j, chip in enumerate(chips):
            sem = dict(send_sem=sends.at[3 * n + j], recv_sem=recvs.at[3 * n + j], device_id=(x, y, 1 - c), device_id_type=MESH)
            going = outs[n].at[_block_index(*chip, c)]
            landing = outs[n].at[_block_index(*chip, 1 - c)]
            out.append((pltpu.make_async_remote_copy(src_ref=going, dst_ref=going, **sem),
                        pltpu.make_async_remote_copy(src_ref=landing, dst_ref=landing, **sem)))
        return out

    def start(ins, outs, scratch):
        for n in range(nw):
            for to, _ in copies(outs, scratch, n):
                to.start()

    def finish(ins, outs, scratch):
        for n in range(nw):
            for to, frm in copies(outs, scratch, n):
                frm.wait_recv()
                to.wait_send()

    return _SideJob(
        inputs=tuple(gathered), out_shapes=tuple(jax.ShapeDtypeStruct(g.shape, g.dtype) for g in gathered),
        aliases={n: n for n in range(nw)},
        scratch=(pltpu.SemaphoreType.DMA((3 * nw,)), pltpu.SemaphoreType.DMA((3 * nw,))),
        start=start, finish=finish)


def _all_gather_rows(x, *, name):
    R, C = x.shape

    def body(x_ref, out_ref, send_sems, recv_sems, local_sem):
        _two_level_gather(x_ref, out_ref, send_sems, recv_sems, local_sem)

    return pl.pallas_call(
        body, name=name, out_shape=jax.ShapeDtypeStruct((8, R, C), x.dtype),
        in_specs=[pl.BlockSpec(memory_space=pltpu.VMEM)], out_specs=pl.BlockSpec(memory_space=pltpu.VMEM),
        scratch_shapes=list(_GATHER_SEMS),
        compiler_params=pltpu.CompilerParams(vmem_limit_bytes=_limit(10 * _nbytes((R, C), x.dtype))),
    )(x)


CAST_ROWS = 16
_FLIPS = [(fx, fy, fc) for fx in (0, 1) for fy in (0, 1) for fc in (0, 1)][1:]


def _flipped(v, bit):
    return 1 - v if bit else v


def _scatter_direct_job(pieces, rows=None, partly=None):
    nw = len(pieces)

    def band(ref):
        return ref if rows is None else ref.at[pl.ds(rows[0], rows[1]), :]

    def copies(ins, outs, scratch, n):
        sends, recvs = scratch
        x, y, c, _ = _place()
        out = []
        for f, (fx, fy, fc) in enumerate(_FLIPS):
            peer = (_flipped(x, fx), _flipped(y, fy), _flipped(c, fc))
            sem = dict(send_sem=sends.at[7 * n + f], recv_sem=recvs.at[7 * n + f], device_id=peer, device_id_type=MESH)
            landing = band(outs[n].at[f])
            out.append((pltpu.make_async_remote_copy(src_ref=band(ins[n].at[_block_index(*peer)]), dst_ref=landing, **sem),
                        pltpu.make_async_remote_copy(src_ref=landing, dst_ref=landing, **sem)))
        return out

    def start(ins, outs, scratch):
        for n in range(nw):
            for to, _ in copies(ins, outs, scratch, n):
                to.start()

    def finish(ins, outs, scratch):
        for n in range(nw):
            for to, frm in copies(ins, outs, scratch, n):
                frm.wait_recv()
                to.wait_send()

    return _SideJob(
        inputs=tuple(pieces) + tuple(partly or ()),
        out_shapes=tuple(jax.ShapeDtypeStruct((7,) + p.shape[1:], p.dtype) for p in pieces),
        aliases={nw + n: n for n in range(nw)} if partly else {},
        scratch=(pltpu.SemaphoreType.DMA((7 * nw,)), pltpu.SemaphoreType.DMA((7 * nw,))),
        start=start, finish=finish)


def _scatter_finish(g4s, landed, *, name):
    nw = len(g4s)
    dims = [g.shape[1:] for g in g4s]

    def body(*refs):
        g_refs, l_refs, out_refs, own = refs[:nw], refs[nw:2 * nw], refs[2 * nw:3 * nw], refs[3 * nw:4 * nw]
        load_sems, send_sems, recv_sems = refs[4 * nw:]
        x, y, core, _ = _place()
        k = 2 * x + y
        loads = []
        for n, (r, c) in enumerate(dims):
            my0 = pl.multiple_of(core * (r // 2), CAST_ROWS)
            ld = pltpu.make_async_copy(g_refs[n].at[k, pl.ds(my0, r // 2), :], own[n], load_sems.at[n])
            ld.start()
            loads.append(ld)
        swaps = []
        for n, (r, c) in enumerate(dims):
            rh = r // 2
            my0 = pl.multiple_of(core * rh, CAST_ROWS)
            loads[n].wait()

            def fin(i, carry, n=n, my0=my0):
                r0 = pl.multiple_of(i * CAST_ROWS, CAST_ROWS)
                s = own[n][pl.ds(r0, CAST_ROWS), :]
                for f in range(7):
                    s = s + l_refs[n][f, pl.ds(r0, CAST_ROWS), :].astype(F32)
                out_refs[n][pl.ds(my0 + r0, CAST_ROWS), :] = s
                return carry

            lax.fori_loop(0, rh // CAST_ROWS, fin, 0)
            half = out_refs[n].at[pl.ds(my0, rh), :]
            sw = pltpu.make_async_remote_copy(src_ref=half, dst_ref=half, send_sem=send_sems.at[n], recv_sem=recv_sems.at[n],
                                              device_id=(x, y, 1 - core), device_id_type=MESH)
            sw.start()
            swaps.append(sw)
        for sw in swaps:
            sw.wait()

    need = sum(_nbytes((7, r // 2, c), BF16) + _nbytes((r // 2, c), F32) + _nbytes((r, c), F32) for r, c in dims)
    vmem = pl.BlockSpec(memory_space=pltpu.VMEM)
    return pl.pallas_call(
        body, name=name, out_shape=tuple(jax.ShapeDtypeStruct((r, c), F32) for r, c in dims),
        in_specs=[pl.BlockSpec(memory_space=pl.ANY)] * nw + [vmem] * nw, out_specs=(vmem,) * nw,
        scratch_shapes=[pltpu.VMEM((r // 2, c), F32) for r, c in dims]
        + [pltpu.SemaphoreType.DMA((nw,)), pltpu.SemaphoreType.DMA((nw,)), pltpu.SemaphoreType.DMA((nw,))],
        compiler_params=pltpu.CompilerParams(vmem_limit_bytes=_limit(need * 1.1)),
    )(*g4s, *landed)


def _gather_weight(w, *, name):
    r, c = w.shape
    rh = r // 2
    assert rh % CAST_ROWS == 0

    def body(w_hbm, out_ref, tmp, xb, send_sems, recv_sems, local_sem):
        core = lax.axis_index("c")
        ld = pltpu.make_async_copy(w_hbm.at[pl.ds(pl.multiple_of(core * rh, CAST_ROWS), rh), :], tmp, local_sem)
        ld.start()
        ld.wait()

        def cast(i, carry):
            r0 = pl.multiple_of(i * CAST_ROWS, CAST_ROWS)
            xb[pl.ds(r0, CAST_ROWS), :] = tmp[pl.ds(r0, CAST_ROWS), :].astype(BF16)
            return carry

        lax.fori_loop(0, rh // CAST_ROWS, cast, 0)
        _two_level_gather(xb, out_ref, send_sems, recv_sems, local_sem)

    need = _nbytes((8, rh, c), BF16) + _nbytes((rh, c), F32) + _nbytes((rh, c), BF16)
    out = pl.pallas_call(
        body, name=name, out_shape=jax.ShapeDtypeStruct((8, rh, c), BF16),
        in_specs=[pl.BlockSpec(memory_space=pl.ANY)], out_specs=pl.BlockSpec(memory_space=pltpu.VMEM),
        scratch_shapes=[pltpu.VMEM((rh, c), F32), pltpu.VMEM((rh, c), BF16)] + list(_GATHER_SEMS),
        compiler_params=pltpu.CompilerParams(vmem_limit_bytes=_limit(need * 1.3)),
    )(w)
    return out.reshape(4, r, c)


def _reduce_scatter_weight(g4, *, name):
    _, r, c = g4.shape
    rh = r // 2
    assert rh % CAST_ROWS == 0
    nsteps = rh // CAST_ROWS

    def body(g_hbm, out_ref, mine, tmp, sbuf, rbuf_a, rbuf_b, a_send, a_recv, b_send, b_recv, c_send, c_recv, lsem):
        x, y, core, chips = _place()
        sibling = (x, y, 1 - core)
        k = 2 * x + y
        my0 = pl.multiple_of(core * rh, CAST_ROWS)
        ot0 = pl.multiple_of((1 - core) * rh, CAST_ROWS)

        ld = pltpu.make_async_copy(g_hbm.at[:, pl.ds(my0, rh), :], mine, lsem)
        ld.start()
        ld.wait()
        for j in range(4):
            ldj = pltpu.make_async_copy(g_hbm.at[j, pl.ds(ot0, rh), :], tmp, lsem)
            ldj.start()
            ldj.wait()

            def cast(i, carry, j=j):
                r0 = pl.multiple_of(i * CAST_ROWS, CAST_ROWS)
                sbuf[j, pl.ds(r0, CAST_ROWS), :] = tmp[pl.ds(r0, CAST_ROWS), :].astype(BF16)
                return carry

            lax.fori_loop(0, nsteps, cast, 0)

        to_sib = pltpu.make_async_remote_copy(src_ref=sbuf, dst_ref=rbuf_a, send_sem=a_send, recv_sem=a_recv,
                                              device_id=sibling, device_id_type=MESH)
        to_sib.start()
        to_sib.wait()

        for j in range(4):
            def add(i, carry, j=j):
                r0 = pl.multiple_of(i * CAST_ROWS, CAST_ROWS)
                s = mine[j, pl.ds(r0, CAST_ROWS), :] + rbuf_a[j, pl.ds(r0, CAST_ROWS), :].astype(F32)
                mine[j, pl.ds(r0, CAST_ROWS), :] = s
                sbuf[j, pl.ds(r0, CAST_ROWS), :] = s.astype(BF16)
                return carry

            lax.fori_loop(0, nsteps, add, 0)

        sends = []
        for d, (px, py) in enumerate(chips):
            cp = pltpu.make_async_remote_copy(src_ref=sbuf.at[2 * px + py], dst_ref=rbuf_b.at[d], send_sem=b_send.at[d],
                                              recv_sem=b_recv.at[d], device_id=(px, py, core), device_id_type=MESH)
            cp.start()
            sends.append(cp)
        for cp in sends:
            cp.wait()

        def fin(i, carry):
            r0 = pl.multiple_of(i * CAST_ROWS, CAST_ROWS)
            s = mine[k, pl.ds(r0, CAST_ROWS), :]
            for d in range(3):
                s = s + rbuf_b[d, pl.ds(r0, CAST_ROWS), :].astype(F32)
            out_ref[pl.ds(my0 + r0, CAST_ROWS), :] = s
            return carry

        lax.fori_loop(0, nsteps, fin, 0)
        half = out_ref.at[pl.ds(my0, rh), :]
        swap = pltpu.make_async_remote_copy(src_ref=half, dst_ref=half, send_sem=c_send, recv_sem=c_recv,
                                            device_id=sibling, device_id_type=MESH)
        swap.start()
        swap.wait()

    need = (_nbytes((4, rh, c), F32) + _nbytes((rh, c), F32) + 2 * _nbytes((4, rh, c), BF16) + _nbytes((3, rh, c), BF16)
            + _nbytes((r, c), F32))
    return pl.pallas_call(
        body, name=name, out_shape=jax.ShapeDtypeStruct((r, c), F32),
        in_specs=[pl.BlockSpec(memory_space=pl.ANY)], out_specs=pl.BlockSpec(memory_space=pltpu.VMEM),
        scratch_shapes=[pltpu.VMEM((4, rh, c), F32), pltpu.VMEM((rh, c), F32), pltpu.VMEM((4, rh, c), BF16),
                        pltpu.VMEM((4, rh, c), BF16), pltpu.VMEM((3, rh, c), BF16),
                        pltpu.SemaphoreType.DMA, pltpu.SemaphoreType.DMA, pltpu.SemaphoreType.DMA((3,)),
                        pltpu.SemaphoreType.DMA((3,)), pltpu.SemaphoreType.DMA, pltpu.SemaphoreType.DMA,
                        pltpu.SemaphoreType.DMA],
        compiler_params=pltpu.CompilerParams(vmem_limit_bytes=_limit(need * 1.2)),
    )(g4)


def _cols_from_shards(g):
    n, K, c = g.shape
    return g.transpose(1, 0, 2).reshape(K, n * c)


def _cols_to_shards(w):
    K, N = w.shape
    return w.reshape(K, 4, N // 4).transpose(1, 0, 2)


def _pad_heads_cols(w, width, lane0=0):
    K = w.shape[0]
    w3 = w.reshape(K, H, width)
    return jnp.pad(w3, ((0, 0), (0, 0), (lane0, HP - lane0 - width))).reshape(K, H * HP)


def _unpad_heads_cols(w, width, lane0=0):
    K = w.shape[0]
    return w.reshape(K, H, HP)[:, :, lane0:lane0 + width].reshape(K, H * width)


def _pad_block(w, lane0=0):
    return jnp.pad(w, ((0, 0), (lane0, LANES - lane0 - w.shape[1])))


_IN_SPLITS = [512, 1024, 1536, 1544, 2312, 2568, 2600, 3624]


def _pad_w_in(w):
    fq, fk, fv, flog, cq, ckv, krin, gfox, gmla = jnp.split(w, _IN_SPLITS, axis=1)
    return jnp.concatenate([_pad_heads_cols(fq, FOX_HD), _pad_heads_cols(fk, FOX_HD), _pad_heads_cols(fv, FOX_HD),
                            cq, ckv, gfox, gmla, _pad_block(flog), _pad_block(krin, KRIN_LANE)], axis=1)


def _unpad_w_in(wp):
    qkv, rest = wp[:, :NQKV], wp[:, NQKV:]
    fq, fk, fv = (_unpad_heads_cols(qkv[:, i * H * HP:(i + 1) * H * HP], FOX_HD) for i in range(3))
    return jnp.concatenate([fq, fk, fv, rest[:, OFF_FLOG:OFF_FLOG + H], rest[:, OFF_CQ:OFF_CQ + Q_LORA],
                            rest[:, OFF_CKV:OFF_CKV + KV_LORA], rest[:, OFF_KRIN + KRIN_LANE:OFF_KRIN + KRIN_LANE + MLA_ROPE],
                            rest[:, OFF_GFOX:OFF_GFOX + D], rest[:, OFF_GMLA:OFF_GMLA + D]], axis=1)


def _pad_w_ukv(w):
    w3 = w.reshape(KV_LORA, H, MLA_NOPE + MLA_V)
    kp = jnp.pad(w3[:, :, :MLA_NOPE], ((0, 0), (0, 0), (0, HP - MLA_NOPE))).reshape(KV_LORA, H * HP)
    vp = jnp.pad(w3[:, :, MLA_NOPE:], ((0, 0), (0, 0), (0, HP - MLA_V))).reshape(KV_LORA, H * HP)
    return jnp.concatenate([kp, vp], axis=1)


def _unpad_w_ukv(wp):
    kp = wp[:, :H * HP].reshape(KV_LORA, H, HP)[:, :, :MLA_NOPE]
    vp = wp[:, H * HP:].reshape(KV_LORA, H, HP)[:, :, :MLA_V]
    return jnp.concatenate([kp, vp], axis=2).reshape(KV_LORA, H * (MLA_NOPE + MLA_V))


def _pad_heads_rows(w, width):
    N = w.shape[1]
    return jnp.pad(w.reshape(H, width, N), ((0, 0), (0, HP - width), (0, 0))).reshape(H * HP, N)


def _unpad_heads_rows(w, width):
    N = w.shape[1]
    return w.reshape(H, HP, N)[:, :width, :].reshape(H * width, N)


def _rope_tables(positions):
    inv_freq = 1.0 / (ROPE_THETA ** (jnp.arange(0, MLA_ROPE, 2, dtype=F32) / MLA_ROPE))
    ang = positions.reshape(S, 1).astype(F32) * inv_freq
    cos, sin = jnp.cos(ang), jnp.sin(ang)
    ones = jnp.ones((S, KRIN_LANE), F32)
    tail = jnp.zeros((S, LANES - KRIN_LANE - MLA_ROPE), F32)
    ctab = jnp.concatenate([ones, cos, cos, tail], axis=1)
    stab = jnp.concatenate([0.0 * ones, -sin, sin, tail], axis=1)
    return ctab, stab


def _local_step(x, target, mod, positions, gains, bf, W, late=None):
    W = dict(W)
    sh1, sc1, gt1, sh2, sc2, gt2 = (mod[:, i * D:(i + 1) * D] for i in range(6))
    ops1, ops2 = 1.0 + sc1, 1.0 + sc2
    ones = lambda w: jnp.ones((1, w), F32)
    zeros = lambda w: jnp.zeros((1, w), F32)
    bf_blk = _pad_block(bf)
    ctab, stab = _rope_tables(positions)
    fox_scale = 1.0 / math.sqrt(FOX_HD)
    mla_scale = 1.0 / math.sqrt(MLA_NOPE + MLA_ROPE)

    h1 = _norm_mod(x, 0, D, gains["g_pre_mix"], ops1, sh1, name="f_pre_mix")
    qkv = _matmul(h1, W["w_in_qkv"], out_dtype=_MXU, name="f_proj_qkv", tm_cap=2048)
    job = _gather_level1_job([late[n] for n in _GATHER_A0]) if late else None
    rest, *landed0 = _matmul_carrying(job, h1, W["w_in_rest"], name="f_proj_rest", tm_cap=2048, tn_cap=256)
    F = _fox_prep(rest, bf_blk, name="f_fox_prep")
    Ft = F[:, :H].T
    fcol, frow = Ft.reshape(H, S, 1), Ft.reshape(H, S // TKF, 1, TKF)
    fox_heads = (0, H, 2 * H)
    job = _gather_level1_job([late[n] for n in _GATHER_A1]) if late else None
    oa, lse_a, *landed = _attn_fwd(qkv, qkv, qkv, frow, scale=fox_scale, name="f_attn_fox", side=job, heads0=fox_heads)
    landed = landed0 + landed

    cqn = _norm_mod(rest, OFF_CQ // Q_LORA, Q_LORA, gains["g_q_lora"], ones(Q_LORA), zeros(Q_LORA), name="f_norm_cq")
    ckvn = _norm_mod(rest, OFF_CKV // KV_LORA, KV_LORA, gains["g_kv_lora"], ones(KV_LORA), zeros(KV_LORA), name="f_norm_ckv")
    qb = _matmul(cqn, W["w_uq"], name="f_uq")
    kvb = _matmul(ckvn, W["w_ukv"], name="f_ukv")
    qm, km, vm = _mla_assemble(qb, kvb, rest, ctab, stab, name="f_mla_assemble")
    job = _join_jobs(_gather_level2_job(landed), _gather_direct_job([late[n] for n in _GATHER_B])) if late else None
    ob, lse_b, *landed = _attn_fwd(qm, km, vm, None, scale=mla_scale, name="f_attn_mla", side=job)
    if late:
        W.update(_late_weights({n: g.reshape(4, 2 * g.shape[1], g.shape[2]) for n, g in zip(_GATHER_A + _GATHER_B, landed)}))

    pa = _matmul(oa, W["w_proj_fox"], name="f_proj_fox")
    pb = _matmul(ob, W["w_proj_mla"], name="f_proj_mla")
    merged = _merge(rest, pa, pb, name="f_merge")
    y1 = _matmul(merged, W["w_out"], name="f_out")
    x2 = _post_res(x, y1, gains["g_post_mix"], gt1, name="f_post_mix")
    h2 = _norm_mod(x2, 0, D, gains["g_pre_ffn"], ops2, sh2, name="f_pre_ffn")
    gu = _matmul(h2, W["w_ffn_in"], name="f_ffn_in", b_shards=True, tn_cap=1408)
    act = _swiglu(gu, name="f_swiglu")
    y2 = _matmul(act, W["w_ffn_out"], name="f_ffn_out", tk_cap=1408, tn_cap=1024)
    dout, loss = _post_res_loss(x2, y2, gains["g_post_ffn"], gt2, target, name="f_post_ffn_loss")

    dy2, s_gt2, s_gpost2 = _post_res_bwd(dout, y2, gains["g_post_ffn"], gt2, name="b_post_ffn")
    dact = _matmul(dy2, W["w_ffn_out"], tb=True, name="b_ffn_out_dx", tn_cap=1408)
    dW_ffn_out = _matmul(act, dy2, ta=True, name="b_ffn_out_dw", tm_cap=1408, tk_cap=2048)
    dgu = _swiglu_bwd(gu, dact, name="b_swiglu")
    job_x = job_w = None
    landed_c = []
    if late:
        fo = dW_ffn_out.reshape(8, D_FF // 8, D).astype(BF16)
        half = fo.shape[1] // 2
        job_x = _scatter_direct_job([fo], rows=(0, half))
    dh2, *landed_c = _matmul_carrying(job_x, dgu, W["w_ffn_in"], tb=True, b_shards=True, name="b_ffn_in_dx", tk_cap=1408,
                                      tn_cap=1024)
    if late:
        job_w = _scatter_direct_job([fo], rows=(half, half), partly=landed_c)
    dW_ffn_in, *landed_c = _matmul_carrying(job_w, h2, dgu, ta=True, name="b_ffn_in_dw", out_shards=True, tn_cap=1408,
                                            tk_cap=2048)
    dx2, s_sh2, s_a2 = _norm_mod_bwd(x2, 0, D, dh2, gains["g_pre_ffn"], ops2, dout, name="b_pre_ffn")
    dy1, s_gt1, s_gpost1 = _post_res_bwd(dx2, y1, gains["g_post_mix"], gt1, name="b_post_mix")
    dmerged = _matmul(dy1, W["w_out"], tb=True, name="b_out_dx")
    dW_out = _matmul(merged, dy1, ta=True, name="b_out_dw", tk_cap=2048)
    dpa, dpb, dgfox, dgmla = _merge_bwd(rest, pa, pb, dmerged, name="b_merge")
    doa = _matmul(dpa, W["w_proj_fox"], tb=True, name="b_proj_fox_dx")
    dW_proj_fox = _matmul(oa, dpa, ta=True, name="b_proj_fox_dw")
    dob = _matmul(dpb, W["w_proj_mla"], tb=True, name="b_proj_mla_dx")
    dW_proj_mla = _matmul(ob, dpb, ta=True, name="b_proj_mla_dw")

    delta_a, doa16 = _attn_delta(doa, oa, name="b_delta_fox")
    as_rows = lambda a: a.reshape(H, S // TQB, 1, TQB)
    dW = dict(w_proj_fox=dW_proj_fox, w_proj_mla=dW_proj_mla, w_out=dW_out, w_ffn_in=dW_ffn_in, w_ffn_out=dW_ffn_out)
    job_a = job_b = None
    if late:
        late_shards = _grad_shards(dW)
        pieces = {n: s.reshape(8, s.shape[1] // 2, s.shape[2]).astype(BF16) for n, s in late_shards.items()}
        first = [pieces[n] for n in _SCATTER_A]
        job_a = _scatter_direct_job(first, rows=(0, _SCATTER_SPLIT))
    dqa, dka, dva, *landed_a = _attn_bwd(qkv, qkv, qkv, doa16, as_rows(lse_a), as_rows(delta_a), fcol, scale=fox_scale,
                                         name="b_attn_fox", side=job_a, heads0=fox_heads)
    delta_b, dob16 = _attn_delta(dob, ob, name="b_delta_mla")
    if late:
        rest_rows = (_SCATTER_SPLIT, first[0].shape[1] - _SCATTER_SPLIT)
        job_b = _join_jobs(_scatter_direct_job(first, rows=rest_rows, partly=landed_a),
                           _scatter_direct_job([pieces[n] for n in _SCATTER_B]))
    dqm, dkm, dvm, *landed_b = _attn_bwd(qm, km, vm, dob16, as_rows(lse_b), as_rows(delta_b), None, scale=mla_scale,
                                         name="b_attn_mla", side=job_b)
    reduced = {}
    if late:
        order = _SCATTER_A + _SCATTER_B + _SCATTER_C
        done = _scatter_finish([late_shards[n] for n in order], landed_b + landed_c, name="scatter_late")
        reduced = dict(zip(order, done))
        dW = {}

    dF = (dqa[:, ROW_SUM_LANE::HP] - dka[:, COL_SUM_LANE::HP]) * (1.0 / fox_scale)
    dflog, s_bf = _fox_bwd_prep(rest, bf_blk, _pad_block(dF), name="b_fox_prep")

    dqb, dkvb, dkrin = _mla_assemble_bwd(dqm, dkm, dvm, ctab, stab, name="b_mla_assemble")
    dcqn = _matmul(dqb, W["w_uq"], tb=True, name="b_uq_dx")
    dW_uq = _matmul(cqn, dqb, ta=True, name="b_uq_dw", tk_cap=2048)
    dckvn = _matmul(dkvb, W["w_ukv"], tb=True, name="b_ukv_dx")
    dW_ukv = _matmul(ckvn, dkvb, ta=True, name="b_ukv_dw", tk_cap=2048)
    dcq, _, s_gq = _norm_mod_bwd(rest, OFF_CQ // Q_LORA, Q_LORA, dcqn, gains["g_q_lora"], ones(Q_LORA), None, name="b_norm_cq")
    dckv, _, s_gkv = _norm_mod_bwd(rest, OFF_CKV // KV_LORA, KV_LORA, dckvn, gains["g_kv_lora"], ones(KV_LORA), None, name="b_norm_ckv")

    c16 = lambda a: a.astype(_MXU)
    dproj = jnp.concatenate([c16(dqa), c16(dka), c16(dva), c16(dcq), c16(dckv), dgfox, dgmla, c16(dflog), c16(dkrin)], axis=1)
    w_in_full = jnp.concatenate([W["w_in_qkv"], W["w_in_rest"]], axis=1)
    dh1 = _matmul(dproj, w_in_full, tb=True, name="b_in_dx", tk_cap=1280, tn_cap=1024)
    dW_in = _matmul(h1, dproj, ta=True, name="b_in_dw", tn_cap=640, tk_cap=2048)
    grad_x, s_sh1, s_a1 = _norm_mod_bwd(x, 0, D, dh1, gains["g_pre_mix"], ops1, dx2, name="b_pre_mix")

    dmod = jnp.concatenate([s_sh1, s_a1 * gains["g_pre_mix"], s_gt1, s_sh2, s_a2 * gains["g_pre_ffn"], s_gt2], axis=1)
    small = dict(dmod=dmod, g_pre_mix=s_a1 * ops1, g_post_mix=s_gpost1, g_pre_ffn=s_a2 * ops2, g_post_ffn=s_gpost2,
                 g_q_lora=s_gq, g_kv_lora=s_gkv, b_forget=s_bf)
    dW = dict(dW, w_in=dW_in, w_uq=dW_uq, w_ukv=dW_ukv)
    return loss, grad_x, dW, reduced, small


_BIG = ["w_in", "w_uq", "w_ukv", "w_proj_fox", "w_proj_mla", "w_out", "w_ffn_in", "w_ffn_out"]
_COL_SHARDED = {"w_in", "w_ukv", "w_proj_fox", "w_proj_mla", "w_ffn_in"}
_SMALL = ["b_ada", "g_pre_mix", "g_post_mix", "g_pre_ffn", "g_post_ffn", "b_forget", "g_q_lora", "g_kv_lora"]
_ORDER = ["w_ada", "b_ada", "g_pre_mix", "g_post_mix", "g_pre_ffn", "g_post_ffn", "w_in", "b_forget", "g_q_lora", "w_uq",
          "g_kv_lora", "w_ukv", "w_proj_fox", "w_proj_mla", "w_out", "w_ffn_in", "w_ffn_out"]
_ROW = {}
_off = 0
for _n, _w in [("dmod", 6 * D), ("g_pre_mix", D), ("g_post_mix", D), ("g_pre_ffn", D), ("g_post_ffn", D), ("g_q_lora", Q_LORA),
               ("g_kv_lora", KV_LORA), ("b_forget", LANES), ("loss", LANES)]:
    _ROW[_n] = (_off, _w)
    _off += _w
_ROW_LEN = _off


_EARLY = ["w_in", "w_uq", "w_ukv"]
_LATE = ["w_proj_fox", "w_proj_mla", "w_out", "w_ffn_in", "w_ffn_out"]
_GATHER_A0 = ["w_proj_fox", "w_proj_mla", "w_out"]
_GATHER_A1 = ["w_ffn_in"]
_GATHER_A = _GATHER_A0 + _GATHER_A1
_GATHER_B = ["w_ffn_out"]
_SCATTER_A = ["w_ffn_in"]
_SCATTER_SPLIT = 320
_SCATTER_B = ["w_out", "w_proj_fox", "w_proj_mla"]
_SCATTER_C = ["w_ffn_out"]


def _rows_from_shards(g):
    return g.reshape(-1, g.shape[2])


def _early_weights(G):
    w_in = _pad_w_in(_cols_from_shards(G["w_in"]))
    return dict(
        w_in_qkv=w_in[:, :NQKV], w_in_rest=w_in[:, NQKV:],
        w_uq=_pad_heads_cols(_rows_from_shards(G["w_uq"]), MLA_NOPE + MLA_ROPE),
        w_ukv=_pad_w_ukv(_cols_from_shards(G["w_ukv"])))


def _late_weights(G):
    return dict(
        w_proj_fox=_pad_heads_rows(_cols_from_shards(G["w_proj_fox"]), FOX_HD),
        w_proj_mla=_pad_heads_rows(_cols_from_shards(G["w_proj_mla"]), MLA_V),
        w_out=_rows_from_shards(G["w_out"]), w_ffn_in=G["w_ffn_in"], w_ffn_out=_rows_from_shards(G["w_ffn_out"]))


def _full_weights(G):
    return {**_early_weights(G), **_late_weights(G)}


_UNPAD = dict(
    w_in=_unpad_w_in, w_uq=lambda g: _unpad_heads_cols(g, MLA_NOPE + MLA_ROPE), w_ukv=_unpad_w_ukv,
    w_proj_fox=lambda g: _unpad_heads_rows(g, FOX_HD), w_proj_mla=lambda g: _unpad_heads_rows(g, MLA_V),
    w_out=lambda g: g, w_ffn_out=lambda g: g)


def _grad_shards(dW):
    out = {}
    for n, g in dW.items():
        if n == "w_ffn_in":
            out[n] = g
        else:
            nat = _UNPAD[n](g)
            out[n] = _cols_to_shards(nat) if n in _COL_SHARDED else nat.reshape(4, nat.shape[0] // 4, nat.shape[1])
    return out


def kernel(x, c, positions, w_ada, b_ada, g_pre_mix, g_post_mix, g_pre_ffn, g_post_ffn, w_in, b_forget, g_q_lora, w_uq, g_kv_lora, w_ukv, w_proj_fox, w_proj_mla, w_out, w_ffn_in, w_ffn_out, loss_target, m_w_ada, m_b_ada, m_g_pre_mix, m_g_post_mix, m_g_pre_ffn, m_g_post_ffn, m_w_in, m_b_forget, m_g_q_lora, m_w_uq, m_g_kv_lora, m_w_ukv, m_w_proj_fox, m_w_proj_mla, m_w_out, m_w_ffn_in, m_w_ffn_out, v_w_ada, v_b_ada, v_g_pre_mix, v_g_post_mix, v_g_pre_ffn, v_g_post_ffn, v_w_in, v_b_forget, v_g_q_lora, v_w_uq, v_g_kv_lora, v_w_ukv, v_w_proj_fox, v_w_proj_mla, v_w_out, v_w_ffn_in, v_w_ffn_out):
    P = dict(w_ada=w_ada, b_ada=b_ada, g_pre_mix=g_pre_mix, g_post_mix=g_post_mix, g_pre_ffn=g_pre_ffn, g_post_ffn=g_post_ffn,
             w_in=w_in, b_forget=b_forget, g_q_lora=g_q_lora, w_uq=w_uq, g_kv_lora=g_kv_lora, w_ukv=w_ukv,
             w_proj_fox=w_proj_fox, w_proj_mla=w_proj_mla, w_out=w_out, w_ffn_in=w_ffn_in, w_ffn_out=w_ffn_out)
    M = dict(w_ada=m_w_ada, b_ada=m_b_ada, g_pre_mix=m_g_pre_mix, g_post_mix=m_g_post_mix, g_pre_ffn=m_g_pre_ffn,
             g_post_ffn=m_g_post_ffn, w_in=m_w_in, b_forget=m_b_forget, g_q_lora=m_g_q_lora, w_uq=m_w_uq, g_kv_lora=m_g_kv_lora,
             w_ukv=m_w_ukv, w_proj_fox=m_w_proj_fox, w_proj_mla=m_w_proj_mla, w_out=m_w_out, w_ffn_in=m_w_ffn_in,
             w_ffn_out=m_w_ffn_out)
    V = dict(w_ada=v_w_ada, b_ada=v_b_ada, g_pre_mix=v_g_pre_mix, g_post_mix=v_g_post_mix, g_pre_ffn=v_g_pre_ffn,
             g_post_ffn=v_g_post_ffn, w_in=v_w_in, b_forget=v_b_forget, g_q_lora=v_g_q_lora, w_uq=v_w_uq, g_kv_lora=v_g_kv_lora,
             w_ukv=v_w_ukv, w_proj_fox=v_w_proj_fox, w_proj_mla=v_w_proj_mla, w_out=v_w_out, w_ffn_in=v_w_ffn_in,
             w_ffn_out=v_w_ffn_out)
    ax, ay, ac = lax.axis_index("x"), lax.axis_index("y"), lax.axis_index("c")
    chip = 2 * ax + ay
    me = 4 * ax + 2 * ay + ac
    n_ada = w_ada.shape[2]

    c_all = _all_gather_rows(jnp.pad(c, ((0, 7), (0, 0))), name="gather_c")[:, 0, :]
    c_all = jnp.pad(c_all, ((0, 8), (0, 0)))
    b_shard = lax.dynamic_slice(b_ada, (0, chip * n_ada), (1, n_ada))
    mod_blk, silu_c = _ada_mod(c_all, w_ada[0], b_shard, name="ada_mod")
    mod_all = _all_gather_rows(mod_blk, name="gather_mod")
    mod_mine = lax.dynamic_index_in_dim(mod_all, me, axis=1, keepdims=False)
    mod = lax.dynamic_index_in_dim(mod_mine.reshape(4, 2, n_ada), ac, axis=1, keepdims=False).reshape(1, 6 * D)

    W = _early_weights({n: _gather_weight(P[n][0], name="gather_" + n) for n in _EARLY})
    late = {}
    for n in _LATE:
        rh = P[n].shape[1] // 2
        late[n] = lax.dynamic_slice_in_dim(P[n][0], ac * rh, rh, axis=0).astype(BF16)

    gains = {n: P[n] for n in ["g_pre_mix", "g_post_mix", "g_pre_ffn", "g_post_ffn", "g_q_lora", "g_kv_lora"]}
    loss, grad_x, dW, grads, small = _local_step(x[0], loss_target[0], mod, positions, gains, b_forget, W, late)

    shards = _grad_shards(dW)
    grads.update({n: _reduce_scatter_weight(shards[n], name="scatter_" + n) for n in shards})

    small = dict(small, loss=_pad_block(loss))
    row = jnp.concatenate([small[n] for n in _ROW], axis=1)
    rows = _all_gather_rows(jnp.pad(row, ((0, 7), (0, 0))), name="gather_small")[:, 0, :]
    tot = _rowsum(rows, name="sum_small")
    piece = lambda n: tot[:, _ROW[n][0]:_ROW[n][0] + _ROW[n][1]]
    grads["b_ada"] = piece("dmod")
    for n in ["g_pre_mix", "g_post_mix", "g_pre_ffn", "g_post_ffn", "g_q_lora", "g_kv_lora"]:
        grads[n] = piece(n)
    grads["b_forget"] = piece("b_forget")[:, :H]
    loss_out = piece("loss")[0, 0]
    dmod_all = rows[:, _ROW["dmod"][0]:_ROW["dmod"][0] + 6 * D]
    dmod_shard = jnp.pad(lax.dynamic_slice(dmod_all, (0, chip * n_ada), (8, n_ada)), ((0, 8), (0, 0)))
    grads["w_ada"] = _matmul(silu_c, dmod_shard, ta=True, name="ada_dw")

    delta, new_m, new_v = {}, {}, {}
    for n in ["w_ada"] + _BIG:
        delta[n], new_m[n], new_v[n] = _adamw(P[n], grads[n], M[n], V[n], name="adamw_" + n)
    cat = lambda T: jnp.concatenate([T[n] for n in _SMALL], axis=1)
    d_s, m_s, v_s = (t[0] for t in _adamw(cat(P)[None], cat(grads), cat(M)[None], cat(V)[None], name="adamw_small"))
    o = 0
    for n in _SMALL:
        wdt = P[n].shape[1]
        delta[n], new_m[n], new_v[n] = d_s[:, o:o + wdt], m_s[:, o:o + wdt], v_s[:, o:o + wdt]
        o += wdt

    def shaped(T, n):
        return T[n].reshape(P[n].shape)

    return (loss_out, grad_x[None], *[shaped(grads, n) for n in _ORDER], *[shaped(delta, n) for n in _ORDER],
            *[shaped(new_m, n) for n in _ORDER], *[shaped(new_v, n) for n in _ORDER])
```

```python
import functools
import math
from typing import Callable, NamedTuple

import jax
import jax.numpy as jnp
from jax import lax
from jax.experimental import pallas as pl
from jax.experimental.pallas import tpu as pltpu

F32 = jnp.float32
BF16 = jnp.bfloat16
_MXU = jnp.bfloat16

S = 2048
D = 1024
H = 8
HP = 128
FOX_HD = 64
MLA_NOPE = 64
MLA_ROPE = 32
MLA_V = 64
Q_LORA = 768
KV_LORA = 256
D_FF = 2816
NORM_EPS = 1e-6
ROPE_THETA = 10000.0
NEG = -1e30

ADAM_LR = 0.001
ADAM_B1 = 0.9
ADAM_B2 = 0.999
ADAM_EPS = 1e-08
ADAM_WD = 0.01
ADAM_STEP = 10

LANES = 128
VMEM_CAP = 60 * 1024 * 1024
MESH = pl.DeviceIdType.MESH

NQKV = 3 * H * HP
OFF_CQ = 0
OFF_CKV = Q_LORA
OFF_GFOX = 1024
OFF_GMLA = 2048
OFF_FLOG = 3072
OFF_KRIN = 3200
NREST = 3328
KRIN_LANE = 64
ROW_SUM_LANE = 64
COL_SUM_LANE = 65


def _limit(nbytes):
    return int(min(VMEM_CAP, nbytes * 1.25 + (4 << 20)))


def _nbytes(shape, dtype):
    n = 1
    for s in shape:
        n *= s
    return n * jnp.dtype(dtype).itemsize


def _pick(n, cap):
    best = None
    for t in range(LANES, min(n, cap) + 1, LANES):
        if n % t == 0:
            best = t
    return best if best is not None else n


def _pcall(body, *, out_shape, **kw):
    outs = jax.tree.map(lambda s: pltpu.HBM(s.shape, s.dtype), out_shape)
    call = pl.pallas_call(body, out_shape=outs, **kw)
    return lambda *args: call(*[pltpu.with_memory_space_constraint(a, pltpu.HBM) for a in args])


def _matmul(a, b, *, ta=False, tb=False, out_dtype=F32, name, tm_cap=1024, tn_cap=512, tk_cap=1024,
            b_shards=False, out_shards=False, side=None):
    if ta:
        K, M = a.shape
    else:
        M, K = a.shape
    if b_shards:
        _, R, cb = b.shape
        N, K2 = (R, 4 * cb) if tb else (4 * cb, R)
    elif tb:
        N, K2 = b.shape
    else:
        K2, N = b.shape
    assert K == K2, (a.shape, b.shape, ta, tb)
    tm = _pick(M, tm_cap)
    tn = _pick(N, tn_cap)
    tk = K if K <= tk_cap else _pick(K, tk_cap)
    nk = K // tk
    dims = (((0 if ta else 1,), (1 if tb else 0,)), ((), ()))

    n_side_in = len(side.inputs) if side else 0
    n_side_out = len(side.out_shapes) if side else 0
    steps = (M // tm, N // tn, nk)

    def body(*refs):
        a_ref, b_ref = refs[:2]
        side_in = refs[2:2 + n_side_in]
        o_ref = refs[2 + n_side_in]
        side_out = refs[3 + n_side_in:3 + n_side_in + n_side_out]
        acc_ref = refs[3 + n_side_in + n_side_out]
        side_scratch = refs[4 + n_side_in + n_side_out:]
        k = pl.program_id(2)
        if side:
            @pl.when((pl.program_id(0) == 0) & (pl.program_id(1) == 0) & (k == 0))
            def _():
                side.start(side_in, side_out, side_scratch)

        @pl.when(k == 0)
        def _():
            acc_ref[...] = jnp.zeros_like(acc_ref)

        acc_ref[...] += lax.dot_general(a_ref[...].astype(_MXU), b_ref[...].astype(_MXU), dims,
                                        preferred_element_type=F32)

        @pl.when(k == nk - 1)
        def _():
            o_ref[...] = acc_ref[...].astype(out_dtype)

        if side:
            @pl.when((pl.program_id(0) == steps[0] - 1) & (pl.program_id(1) == steps[1] - 1) & (k == nk - 1))
            def _():
                side.finish(side_in, side_out, side_scratch)

    a_spec = pl.BlockSpec((tk, tm), lambda i, j, k: (k, i)) if ta else pl.BlockSpec((tm, tk), lambda i, j, k: (i, k))
    if b_shards and tb:
        assert cb % tk == 0
        per = cb // tk
        b_spec = pl.BlockSpec((None, tn, tk), lambda i, j, k: (k // per, j, k % per))
    elif b_shards:
        assert cb % tn == 0
        per = cb // tn
        b_spec = pl.BlockSpec((None, tk, tn), lambda i, j, k: (j // per, k, j % per))
    elif tb:
        b_spec = pl.BlockSpec((tn, tk), lambda i, j, k: (j, k))
    else:
        b_spec = pl.BlockSpec((tk, tn), lambda i, j, k: (k, j))
    if out_shards:
        assert (N // 4) % tn == 0
        pern = N // 4 // tn
        out_shape = jax.ShapeDtypeStruct((4, M, N // 4), out_dtype)
        out_spec = pl.BlockSpec((None, tm, tn), lambda i, j, k: (j // pern, i, j % pern))
    else:
        out_shape = jax.ShapeDtypeStruct((M, N), out_dtype)
        out_spec = pl.BlockSpec((tm, tn), lambda i, j, k: (i, j))
    need = (2 * _nbytes((tm, tk), a.dtype) + 2 * _nbytes((tk, tn), b.dtype) + 2 * _nbytes((tm, tn), out_dtype)
            + _nbytes((tm, tn), F32) * 2 + _nbytes((tm, tk), _MXU) + _nbytes((tk, tn), _MXU))
    in_specs, args, scratch, extra = [a_spec, b_spec], (a, b), [pltpu.VMEM((tm, tn), F32)], {}
    if side:
        anywhere = pl.BlockSpec(memory_space=pl.ANY)
        in_specs += [anywhere] * n_side_in
        args += tuple(side.inputs)
        out_shape = (out_shape,) + tuple(side.out_shapes)
        out_spec = (out_spec,) + (anywhere,) * n_side_out
        scratch += list(side.scratch)
        extra = dict(input_output_aliases={2 + i: 1 + o for i, o in side.aliases.items()})
    semantics = ("arbitrary",) * 3 if side else ("parallel", "parallel", "arbitrary")
    return _pcall(
        body, name=name, out_shape=out_shape, grid=steps, in_specs=in_specs, out_specs=out_spec, scratch_shapes=scratch,
        compiler_params=pltpu.CompilerParams(dimension_semantics=semantics, vmem_limit_bytes=_limit(need)), **extra,
    )(*args)


def _matmul_carrying(side, *args, **kw):
    out = _matmul(*args, side=side, **kw)
    return out if side else (out,)


TM = 512
TM_FF = 256


def _vec(w):
    return pl.BlockSpec((1, w), lambda i: (0, 0))


def _rows(w, col=0, tm=TM):
    return pl.BlockSpec((tm, w), lambda i: (i, col))


def _row_params(need, carried=False):
    return pltpu.CompilerParams(dimension_semantics=("arbitrary" if carried else "parallel",),
                                vmem_limit_bytes=_limit(need))


def _norm_mod(x, col, w, g, ops, sh, *, name):
    def body(x_ref, g_ref, ops_ref, sh_ref, o_ref):
        xv = x_ref[...]
        r = lax.rsqrt(jnp.mean(xv * xv, axis=-1, keepdims=True) + NORM_EPS)
        o_ref[...] = (((xv * r) * g_ref[...]) * ops_ref[...] + sh_ref[...]).astype(o_ref.dtype)

    return _pcall(
        body, name=name, out_shape=jax.ShapeDtypeStruct((S, w), _MXU), grid=(S // TM,),
        in_specs=[_rows(w, col), _vec(w), _vec(w), _vec(w)], out_specs=_rows(w),
        compiler_params=_row_params(8 * _nbytes((TM, w), F32)),
    )(x, g, ops, sh)


def _norm_mod_bwd(x, col, w, dh, g, ops, dres, *, name):
    has_res = dres is not None

    def body(*refs):
        if has_res:
            x_ref, dh_ref, g_ref, ops_ref, dres_ref, dx_ref, s1_ref, s2_ref = refs
        else:
            x_ref, dh_ref, g_ref, ops_ref, dx_ref, s1_ref, s2_ref = refs
        i = pl.program_id(0)

        @pl.when(i == 0)
        def _():
            s1_ref[...] = jnp.zeros_like(s1_ref)
            s2_ref[...] = jnp.zeros_like(s2_ref)

        xv = x_ref[...]
        dhv = dh_ref[...]
        r = lax.rsqrt(jnp.mean(xv * xv, axis=-1, keepdims=True) + NORM_EPS)
        xn = xv * r
        dxn = dhv * (g_ref[...] * ops_ref[...])
        dx = r * (dxn - xn * jnp.mean(dxn * xn, axis=-1, keepdims=True))
        if has_res:
            dx = dx + dres_ref[...]
        dx_ref[...] = dx
        s1_ref[...] += jnp.sum(dhv, axis=0, keepdims=True)
        s2_ref[...] += jnp.sum(dhv * xn, axis=0, keepdims=True)

    in_specs = [_rows(w, col), _rows(w), _vec(w), _vec(w)] + ([_rows(w)] if has_res else [])
    args = (x, dh, g, ops) + ((dres,) if has_res else ())
    return _pcall(
        body, name=name,
        out_shape=(jax.ShapeDtypeStruct((S, w), F32), jax.ShapeDtypeStruct((1, w), F32), jax.ShapeDtypeStruct((1, w), F32)),
        grid=(S // TM,), in_specs=in_specs, out_specs=(_rows(w), _vec(w), _vec(w)),
        compiler_params=_row_params(12 * _nbytes((TM, w), F32), carried=True),
    )(*args)


def _post_res(xres, y, g, gt, *, name):
    def body(x_ref, y_ref, g_ref, gt_ref, o_ref):
        yv = y_ref[...]
        r = lax.rsqrt(jnp.mean(yv * yv, axis=-1, keepdims=True) + NORM_EPS)
        o_ref[...] = x_ref[...] + gt_ref[...] * ((yv * r) * g_ref[...])

    return _pcall(
        body, name=name, out_shape=jax.ShapeDtypeStruct((S, D), F32), grid=(S // TM,),
        in_specs=[_rows(D), _rows(D), _vec(D), _vec(D)], out_specs=_rows(D),
        compiler_params=_row_params(8 * _nbytes((TM, D), F32)),
    )(xres, y, g, gt)


def _post_res_loss(xres, y, g, gt, target, *, name):
    def body(x_ref, y_ref, g_ref, gt_ref, t_ref, dout_ref, loss_ref):
        i = pl.program_id(0)

        @pl.when(i == 0)
        def _():
            loss_ref[...] = jnp.zeros_like(loss_ref)

        yv = y_ref[...]
        r = lax.rsqrt(jnp.mean(yv * yv, axis=-1, keepdims=True) + NORM_EPS)
        out = x_ref[...] + gt_ref[...] * ((yv * r) * g_ref[...])
        err = out - t_ref[...]
        dout_ref[...] = err * (1.0 / D)
        per_row = jnp.mean(err * err, axis=-1, keepdims=True)
        loss_ref[...] += 0.5 * jnp.sum(per_row, axis=0, keepdims=True)

    return _pcall(
        body, name=name,
        out_shape=(jax.ShapeDtypeStruct((S, D), F32), jax.ShapeDtypeStruct((1, 1), F32)), grid=(S // TM,),
        in_specs=[_rows(D), _rows(D), _vec(D), _vec(D), _rows(D)],
        out_specs=(_rows(D), pl.BlockSpec((1, 1), lambda i: (0, 0))),
        compiler_params=_row_params(10 * _nbytes((TM, D), F32), carried=True),
    )(xres, y, g, gt, target)


def _post_res_bwd(dxn, y, g, gt, *, name):
    def body(d_ref, y_ref, g_ref, gt_ref, dy_ref, sgt_ref, sg_ref):
        i = pl.program_id(0)

        @pl.when(i == 0)
        def _():
            sgt_ref[...] = jnp.zeros_like(sgt_ref)
            sg_ref[...] = jnp.zeros_like(sg_ref)

        yv = y_ref[...]
        dv = d_ref[...]
        r = lax.rsqrt(jnp.mean(yv * yv, axis=-1, keepdims=True) + NORM_EPS)
        yn = yv * r
        dn = dv * gt_ref[...]
        dyn = dn * g_ref[...]
        dy_ref[...] = (r * (dyn - yn * jnp.mean(dyn * yn, axis=-1, keepdims=True))).astype(dy_ref.dtype)
        sgt_ref[...] += jnp.sum(dv * (yn * g_ref[...]), axis=0, keepdims=True)
        sg_ref[...] += jnp.sum(dn * yn, axis=0, keepdims=True)

    return _pcall(
        body, name=name,
        out_shape=(jax.ShapeDtypeStruct((S, D), _MXU), jax.ShapeDtypeStruct((1, D), F32), jax.ShapeDtypeStruct((1, D), F32)),
        grid=(S // TM,), in_specs=[_rows(D), _rows(D), _vec(D), _vec(D)], out_specs=(_rows(D), _vec(D), _vec(D)),
        compiler_params=_row_params(10 * _nbytes((TM, D), F32), carried=True),
    )(dxn, y, g, gt)


def _swiglu(gu, *, name):
    def body(g_ref, u_ref, o_ref):
        gv = g_ref[...]
        o_ref[...] = ((gv * jax.nn.sigmoid(gv)) * u_ref[...]).astype(o_ref.dtype)

    return _pcall(
        body, name=name, out_shape=jax.ShapeDtypeStruct((S, D_FF), _MXU), grid=(S // TM_FF,),
        in_specs=[_rows(D_FF, 0, TM_FF), _rows(D_FF, 1, TM_FF)], out_specs=_rows(D_FF, 0, TM_FF),
        compiler_params=_row_params(8 * _nbytes((TM_FF, D_FF), F32)),
    )(gu, gu)


def _swiglu_bwd(gu, dact, *, name):
    def body(g_ref, u_ref, d_ref, o_ref):
        gv = g_ref[...]
        dv = d_ref[...]
        sg = jax.nn.sigmoid(gv)
        o_ref[:, :D_FF] = (dv * u_ref[...] * (sg * (1.0 + gv * (1.0 - sg)))).astype(o_ref.dtype)
        o_ref[:, D_FF:] = (dv * (gv * sg)).astype(o_ref.dtype)

    return _pcall(
        body, name=name, out_shape=jax.ShapeDtypeStruct((S, 2 * D_FF), _MXU), grid=(S // TM_FF,),
        in_specs=[_rows(D_FF, 0, TM_FF), _rows(D_FF, 1, TM_FF), _rows(D_FF, 0, TM_FF)], out_specs=_rows(2 * D_FF, 0, TM_FF),
        compiler_params=_row_params(12 * _nbytes((TM_FF, D_FF), F32)),
    )(gu, gu, dact)


def _merge(rest, pa, pb, *, name):
    def body(ga_ref, gb_ref, pa_ref, pb_ref, o_ref):
        o_ref[...] = (jax.nn.sigmoid(ga_ref[...]) * pa_ref[...] + jax.nn.sigmoid(gb_ref[...]) * pb_ref[...]).astype(o_ref.dtype)

    return _pcall(
        body, name=name, out_shape=jax.ShapeDtypeStruct((S, D), _MXU), grid=(S // TM,),
        in_specs=[_rows(D, OFF_GFOX // D), _rows(D, OFF_GMLA // D), _rows(D), _rows(D)], out_specs=_rows(D),
        compiler_params=_row_params(10 * _nbytes((TM, D), F32)),
    )(rest, rest, pa, pb)


def _merge_bwd(rest, pa, pb, dm, *, name):
    def body(ga_ref, gb_ref, pa_ref, pb_ref, d_ref, dpa_ref, dpb_ref, dga_ref, dgb_ref):
        dv = d_ref[...]
        sa = jax.nn.sigmoid(ga_ref[...])
        sb = jax.nn.sigmoid(gb_ref[...])
        dpa_ref[...] = (dv * sa).astype(dpa_ref.dtype)
        dpb_ref[...] = (dv * sb).astype(dpb_ref.dtype)
        dga_ref[...] = (dv * pa_ref[...] * (sa * (1.0 - sa))).astype(dga_ref.dtype)
        dgb_ref[...] = (dv * pb_ref[...] * (sb * (1.0 - sb))).astype(dgb_ref.dtype)

    o = jax.ShapeDtypeStruct((S, D), _MXU)
    return _pcall(
        body, name=name, out_shape=(o, o, o, o), grid=(S // TM,),
        in_specs=[_rows(D, OFF_GFOX // D), _rows(D, OFF_GMLA // D), _rows(D), _rows(D), _rows(D)],
        out_specs=(_rows(D), _rows(D), _rows(D), _rows(D)),
        compiler_params=_row_params(16 * _nbytes((TM, D), F32)),
    )(rest, rest, pa, pb, dm)


SCAN = 256


def _split_dot(tri, x):
    hi = x.astype(_MXU)
    r1 = x - hi.astype(F32)
    mid = r1.astype(_MXU)
    lo = (r1 - mid.astype(F32)).astype(_MXU)
    dot = functools.partial(jnp.dot, preferred_element_type=F32)
    return dot(tri, hi) + dot(tri, mid) + dot(tri, lo)


def _fox_prep(rest, bf, *, name):
    def body(z_ref, b_ref, f_ref):
        lane = lax.broadcasted_iota(jnp.int32, (SCAN, LANES), 1)
        tri = (lax.broadcasted_iota(jnp.int32, (SCAN, SCAN), 1) <= lax.broadcasted_iota(jnp.int32, (SCAN, SCAN), 0)).astype(_MXU)
        carry = jnp.zeros((1, LANES), F32)
        for c in range(S // SCAN):
            z = z_ref[c * SCAN:(c + 1) * SCAN, :] + b_ref[...]
            lf = jnp.minimum(z, 0.0) - jnp.log(1.0 + jnp.exp(-jnp.abs(z)))
            lf = jnp.where(lane < H, lf, 0.0)
            cum = _split_dot(tri, lf) + carry
            f_ref[c * SCAN:(c + 1) * SCAN, :] = cum
            carry = cum[SCAN - 1:SCAN, :]

    return _pcall(
        body, name=name, out_shape=jax.ShapeDtypeStruct((S, LANES), F32), grid=(1,),
        in_specs=[pl.BlockSpec((S, LANES), lambda i: (0, OFF_FLOG // LANES)), pl.BlockSpec((1, LANES), lambda i: (0, 0))],
        out_specs=pl.BlockSpec((S, LANES), lambda i: (0, 0)),
        compiler_params=pltpu.CompilerParams(vmem_limit_bytes=_limit(8 * _nbytes((S, LANES), F32))),
    )(rest, bf)


def _fox_bwd_prep(rest, bf, dF, *, name):
    def body(z_ref, b_ref, d_ref, o_ref, db_ref):
        lane = lax.broadcasted_iota(jnp.int32, (SCAN, LANES), 1)
        tri = (lax.broadcasted_iota(jnp.int32, (SCAN, SCAN), 1) >= lax.broadcasted_iota(jnp.int32, (SCAN, SCAN), 0)).astype(_MXU)
        carry = jnp.zeros((1, LANES), F32)
        db = jnp.zeros((1, LANES), F32)
        for c in range(S // SCAN - 1, -1, -1):
            rc = _split_dot(tri, d_ref[c * SCAN:(c + 1) * SCAN, :]) + carry
            z = z_ref[c * SCAN:(c + 1) * SCAN, :] + b_ref[...]
            dz = jnp.where(lane < H, rc * jax.nn.sigmoid(-z), 0.0)
            o_ref[c * SCAN:(c + 1) * SCAN, :] = dz
            db = db + jnp.sum(dz, axis=0, keepdims=True)
            carry = rc[0:1, :]
        db_ref[...] = db

    return _pcall(
        body, name=name,
        out_shape=(jax.ShapeDtypeStruct((S, LANES), F32), jax.ShapeDtypeStruct((1, LANES), F32)), grid=(1,),
        in_specs=[pl.BlockSpec((S, LANES), lambda i: (0, OFF_FLOG // LANES)), pl.BlockSpec((1, LANES), lambda i: (0, 0)),
                  pl.BlockSpec((S, LANES), lambda i: (0, 0))],
        out_specs=(pl.BlockSpec((S, LANES), lambda i: (0, 0)), pl.BlockSpec((1, LANES), lambda i: (0, 0))),
        compiler_params=pltpu.CompilerParams(vmem_limit_bytes=_limit(10 * _nbytes((S, LANES), F32))),
    )(rest, bf, dF)


def _swap16(x):
    lane = lax.broadcasted_iota(jnp.int32, x.shape, 1)
    half = MLA_ROPE // 2
    sw = jnp.where(lane < KRIN_LANE + half, pltpu.roll(x, LANES - half, 1), pltpu.roll(x, half, 1))
    return jnp.where((lane >= KRIN_LANE) & (lane < KRIN_LANE + MLA_ROPE), sw, 0.0)


def _mla_assemble(qb, kvb, rest, ctab, stab, *, name):
    def body(q_ref, kk_ref, kv_ref, kr_ref, c_ref, s_ref, qo_ref, ko_ref, vo_ref):
        cv = c_ref[...]
        sv = s_ref[...]
        kr = kr_ref[...]
        kpe = kr * cv + _swap16(kr) * sv
        for h in range(H):
            sl = slice(h * HP, (h + 1) * HP)
            qh = q_ref[:, sl]
            qo_ref[:, sl] = (qh * cv + _swap16(qh) * sv).astype(qo_ref.dtype)
            ko_ref[:, sl] = (kk_ref[:, sl] + kpe).astype(ko_ref.dtype)
        vo_ref[...] = kv_ref[...].astype(vo_ref.dtype)

    o = jax.ShapeDtypeStruct((S, H * HP), _MXU)
    return _pcall(
        body, name=name, out_shape=(o, o, o), grid=(S // TM,),
        in_specs=[_rows(H * HP), _rows(H * HP, 0), _rows(H * HP, 1), _rows(LANES, OFF_KRIN // LANES), _rows(LANES), _rows(LANES)],
        out_specs=(_rows(H * HP), _rows(H * HP), _rows(H * HP)),
        compiler_params=_row_params(14 * _nbytes((TM, H * HP), F32)),
    )(qb, kvb, kvb, rest, ctab, stab)


def _mla_assemble_bwd(dq, dk, dv, ctab, stab, *, name):
    def body(dq_ref, dk_ref, dv_ref, c_ref, s_ref, dqo_ref, dkv_ref, dkr_ref):
        cv = c_ref[...]
        sv = s_ref[...]
        lane = lax.broadcasted_iota(jnp.int32, (TM, LANES), 1)
        dsum = jnp.zeros((TM, LANES), F32)
        for h in range(H):
            sl = slice(h * HP, (h + 1) * HP)
            dqh = dq_ref[:, sl]
            dqo_ref[:, sl] = (dqh * cv + _swap16(dqh * sv)).astype(dqo_ref.dtype)
            dsum = dsum + dk_ref[:, sl]
        dkv_ref[:, :H * HP] = dk_ref[...].astype(dkv_ref.dtype)
        dkv_ref[:, H * HP:] = dv_ref[...].astype(dkv_ref.dtype)
        dkr = dsum * cv + _swap16(dsum * sv)
        dkr_ref[...] = jnp.where((lane >= KRIN_LANE) & (lane < KRIN_LANE + MLA_ROPE), dkr, 0.0)

    return _pcall(
        body, name=name,
        out_shape=(jax.ShapeDtypeStruct((S, H * HP), _MXU), jax.ShapeDtypeStruct((S, 2 * H * HP), _MXU),
                   jax.ShapeDtypeStruct((S, LANES), F32)),
        grid=(S // TM,),
        in_specs=[_rows(H * HP), _rows(H * HP), _rows(H * HP), _rows(LANES), _rows(LANES)],
        out_specs=(_rows(H * HP), _rows(2 * H * HP), _rows(LANES)),
        compiler_params=_row_params(14 * _nbytes((TM, H * HP), F32)),
    )(dq, dk, dv, ctab, stab)


TQ = 512
TKF = 512
TKB = 512
TQB = 512
_NT = (((1,), (1,)), ((), ()))
_TN = (((0,), (0,)), ((), ()))


def _is_pow2(x):
    return math.frexp(x)[0] == 0.5


def _attn_fwd(q, k, v, frow, *, scale, name, side=None, heads0=(0, 0, 0)):
    has_decay = frow is not None
    fold = _is_pow2(scale)
    n_in = 4 if has_decay else 3
    n_side_in = len(side.inputs) if side else 0
    n_side_out = len(side.out_shapes) if side else 0

    def body(*refs):
        q_ref, k_ref, v_ref = refs[:3]
        fr_ref = refs[3] if has_decay else None
        side_in = refs[n_in:n_in + n_side_in]
        o_ref, lse_ref = refs[n_in + n_side_in:n_in + n_side_in + 2]
        side_out = refs[n_in + n_side_in + 2:n_in + n_side_in + 2 + n_side_out]
        side_scratch = refs[n_in + n_side_in + 2 + n_side_out:]
        i = pl.program_id(1)
        if side:
            @pl.when((pl.program_id(0) == 0) & (i == 0))
            def _():
                side.start(side_in, side_out, side_scratch)
        qv = q_ref[...]
        if fold:
            qv = (qv * scale).astype(qv.dtype)
        last = (i * TQ) // TKF

        def tile(j, carry, masked):
            m, l, acc = carry
            k0 = pl.multiple_of(j * TKF, TKF)
            kj = k_ref[pl.ds(k0, TKF), :]
            vj = v_ref[pl.ds(k0, TKF), :]
            s = lax.dot_general(qv, kj, _NT, preferred_element_type=F32)
            if not fold:
                s = s * scale
            if has_decay:
                s = s - fr_ref[0, j]
            if masked:
                rows = i * TQ + lax.broadcasted_iota(jnp.int32, (TQ, TKF), 0)
                cols = j * TKF + lax.broadcasted_iota(jnp.int32, (TQ, TKF), 1)
                s = jnp.where(cols <= rows, s, NEG)
            m_new = jnp.maximum(m, jnp.max(s, axis=-1, keepdims=True))
            alpha = jnp.exp(m - m_new)
            p = jnp.exp(s - m_new)
            l = alpha * l + jnp.sum(p, axis=-1, keepdims=True)
            acc = alpha * acc + jnp.dot(p.astype(_MXU), vj, preferred_element_type=F32)
            return m_new, l, acc

        init = (jnp.full((TQ, 1), NEG, F32), jnp.zeros((TQ, 1), F32), jnp.zeros((TQ, HP), F32))
        carry = lax.fori_loop(0, last, lambda j, c: tile(j, c, False), init)
        m, l, acc = tile(last, carry, True)
        o_ref[...] = acc / l
        lse_ref[0] = m + jnp.log(l)
        if side:
            @pl.when((pl.program_id(0) == H - 1) & (i == S // TQ - 1))
            def _():
                side.finish(side_in, side_out, side_scratch)

    q0, k0, v0 = heads0
    in_specs = [pl.BlockSpec((TQ, HP), lambda h, i: (i, h + q0)), pl.BlockSpec((S, HP), lambda h, i: (0, h + k0)),
                pl.BlockSpec((S, HP), lambda h, i: (0, h + v0))]
    args = (q, k, v)
    if has_decay:
        in_specs += [pl.BlockSpec((1, S // TKF, 1, TKF), lambda h, i: (h, 0, 0, 0))]
        args += (frow,)
    out_shape = (jax.ShapeDtypeStruct((S, H * HP), F32), jax.ShapeDtypeStruct((H, S, 1), F32))
    out_specs = (pl.BlockSpec((TQ, HP), lambda h, i: (i, h)), pl.BlockSpec((1, TQ, 1), lambda h, i: (h, i, 0)))
    extra = {}
    if side:
        anywhere = pl.BlockSpec(memory_space=pl.ANY)
        in_specs += [anywhere] * n_side_in
        args += tuple(side.inputs)
        out_shape += tuple(side.out_shapes)
        out_specs += (anywhere,) * n_side_out
        extra = dict(scratch_shapes=list(side.scratch),
                     input_output_aliases={n_in + a: 2 + b for a, b in side.aliases.items()})
    return _pcall(
        body, name=name, out_shape=out_shape, grid=(H, S // TQ), in_specs=in_specs, out_specs=out_specs,
        compiler_params=pltpu.CompilerParams(dimension_semantics=("arbitrary", "arbitrary") if side else ("parallel", "parallel"),
                                             vmem_limit_bytes=_limit(8 * _nbytes((S, HP), F32))),
        **extra,
    )(*args)


def _attn_delta(do, o, *, name):
    def body(do_ref, o_ref, dl_ref, dob_ref):
        for h in range(H):
            sl = slice(h * HP, (h + 1) * HP)
            dl_ref[h] = jnp.sum(do_ref[:, sl] * o_ref[:, sl], axis=-1, keepdims=True)
        dob_ref[...] = do_ref[...].astype(dob_ref.dtype)

    return _pcall(
        body, name=name,
        out_shape=(jax.ShapeDtypeStruct((H, S, 1), F32), jax.ShapeDtypeStruct((S, H * HP), _MXU)), grid=(S // TM,),
        in_specs=[_rows(H * HP), _rows(H * HP)],
        out_specs=(pl.BlockSpec((H, TM, 1), lambda i: (0, i, 0)), _rows(H * HP)),
        compiler_params=_row_params(8 * _nbytes((TM, H * HP), F32)),
    )(do, o)


def _attn_bwd(q, k, v, dob, lse_row, delta_row, fcol, *, scale, name, side=None, heads0=(0, 0, 0)):
    has_decay = fcol is not None
    fold = _is_pow2(scale)
    n_in = 7 if has_decay else 6
    n_side_in = len(side.inputs) if side else 0
    n_side_out = len(side.out_shapes) if side else 0

    def body(*refs):
        q_ref, k_ref, v_ref, do_ref, lse_ref, dl_ref = refs[:6]
        fc_ref = refs[6] if has_decay else None
        side_in = refs[n_in:n_in + n_side_in]
        dq_ref, dk_ref, dv_ref = refs[n_in + n_side_in:n_in + n_side_in + 3]
        side_out = refs[n_in + n_side_in + 3:n_in + n_side_in + 3 + n_side_out]
        dq_acc = refs[n_in + n_side_in + 3 + n_side_out]
        side_scratch = refs[n_in + n_side_in + 4 + n_side_out:]
        j = pl.program_id(1)
        if side:
            @pl.when((pl.program_id(0) == 0) & (j == 0))
            def _():
                side.start(side_in, side_out, side_scratch)

        @pl.when(j == 0)
        def _():
            dq_acc[...] = jnp.zeros_like(dq_acc)

        kj = k_ref[...]
        vj = v_ref[...]
        kjs = (kj * scale).astype(kj.dtype) if fold else kj
        if has_decay:
            klane = lax.broadcasted_iota(jnp.int32, (TKB, HP), 1)
            kj = jnp.where(klane == ROW_SUM_LANE, 1.0, kj).astype(kj.dtype)
        first = (j * TKB) // TQB

        def tile(t, carry, masked):
            dk, dv = carry
            r0 = pl.multiple_of(t * TQB, TQB)
            qi = q_ref[pl.ds(r0, TQB), :]
            doi = do_ref[pl.ds(r0, TQB), :]
            st = lax.dot_general(kjs, qi, _NT, preferred_element_type=F32)
            if not fold:
                st = st * scale
            if has_decay:
                st = st - fc_ref[0]
            if masked:
                keys = j * TKB + lax.broadcasted_iota(jnp.int32, (TKB, TQB), 0)
                qpos = t * TQB + lax.broadcasted_iota(jnp.int32, (TKB, TQB), 1)
                st = jnp.where(keys <= qpos, st, NEG)
            pt = jnp.exp(st - lse_ref[0, t])
            dv = dv + jnp.dot(pt.astype(_MXU), doi, preferred_element_type=F32)
            dpt = lax.dot_general(vj, doi, _NT, preferred_element_type=F32)
            dst = (pt * (dpt - dl_ref[0, t])).astype(_MXU)
            if has_decay:
                lane = lax.broadcasted_iota(jnp.int32, (TQB, HP), 1)
                qi = jnp.where(lane == COL_SUM_LANE, 1.0, qi).astype(qi.dtype)
            dk = dk + jnp.dot(dst, qi, preferred_element_type=F32)
            dq_acc[pl.ds(r0, TQB), :] += lax.dot_general(dst, kj, _TN, preferred_element_type=F32)
            return dk, dv

        zero = jnp.zeros((TKB, HP), F32)
        carry = tile(first, (zero, zero), True)
        dk, dv = lax.fori_loop(first + 1, S // TQB, lambda t, c: tile(t, c, False), carry)
        dk_ref[...] = dk * scale
        dv_ref[...] = dv

        @pl.when(j == S // TKB - 1)
        def _():
            dq_ref[...] = dq_acc[...] * scale

        if side:
            @pl.when((pl.program_id(0) == H - 1) & (j == S // TKB - 1))
            def _():
                side.finish(side_in, side_out, side_scratch)

    q0, k0, v0 = heads0
    head = pl.BlockSpec((S, HP), lambda h, j: (0, h))
    kv = pl.BlockSpec((TKB, HP), lambda h, j: (j, h))
    stat = pl.BlockSpec((1, S // TQB, 1, TQB), lambda h, j: (h, 0, 0, 0))
    in_specs = [pl.BlockSpec((S, HP), lambda h, j: (0, h + q0)), pl.BlockSpec((TKB, HP), lambda h, j: (j, h + k0)),
                pl.BlockSpec((TKB, HP), lambda h, j: (j, h + v0)), head, stat, stat]
    args = (q, k, v, dob, lse_row, delta_row)
    if has_decay:
        in_specs += [pl.BlockSpec((1, TKB, 1), lambda h, j: (h, j, 0))]
        args += (fcol,)
    o = jax.ShapeDtypeStruct((S, H * HP), F32)
    out_shape, out_specs, scratch, aliases = (o, o, o), (head, kv, kv), [pltpu.VMEM((S, HP), F32)], {}
    if side:
        anywhere = pl.BlockSpec(memory_space=pl.ANY)
        in_specs += [anywhere] * n_side_in
        args += tuple(side.inputs)
        out_shape += tuple(side.out_shapes)
        out_specs += (anywhere,) * n_side_out
        scratch += list(side.scratch)
        aliases = {n_in + a: 3 + b for a, b in side.aliases.items()}
    return _pcall(
        body, name=name, out_shape=out_shape, grid=(H, S // TKB), in_specs=in_specs,
        out_specs=out_specs, scratch_shapes=scratch, input_output_aliases=aliases,
        compiler_params=pltpu.CompilerParams(dimension_semantics=("arbitrary" if side else "parallel", "arbitrary"),
                                             vmem_limit_bytes=_limit(12 * _nbytes((S, HP), F32))),
    )(*args)


def _ada_mod(c_all, w_shard, b_shard, *, name):
    R = c_all.shape[0]
    N = w_shard.shape[1]
    tn = 512

    def body(c_ref, w_ref, b_ref, o_ref, sc_ref):
        cv = c_ref[...]
        sc = (cv * jax.nn.sigmoid(cv)).astype(_MXU)
        sc_ref[...] = sc
        o_ref[...] = jnp.dot(sc, w_ref[...].astype(_MXU), preferred_element_type=F32) + b_ref[...]

    return _pcall(
        body, name=name,
        out_shape=(jax.ShapeDtypeStruct((R, N), F32), jax.ShapeDtypeStruct((R, D), _MXU)), grid=(N // tn,),
        in_specs=[pl.BlockSpec((R, D), lambda j: (0, 0)), pl.BlockSpec((D, tn), lambda j: (0, j)), pl.BlockSpec((1, tn), lambda j: (0, j))],
        out_specs=(pl.BlockSpec((R, tn), lambda j: (0, j)), pl.BlockSpec((R, D), lambda j: (0, 0))),
        compiler_params=pltpu.CompilerParams(dimension_semantics=("arbitrary",), vmem_limit_bytes=_limit(6 * _nbytes((D, tn), F32))),
    )(c_all, w_shard, b_shard)


def _rowsum(x, *, name):
    R, L = x.shape

    def body(x_ref, o_ref):
        acc = x_ref[0:1, :]
        for r in range(1, R):
            acc = acc + x_ref[r:r + 1, :]
        o_ref[...] = acc

    return pl.pallas_call(body, name=name, out_shape=jax.ShapeDtypeStruct((1, L), F32),
                          in_specs=[pl.BlockSpec(memory_space=pltpu.VMEM)], out_specs=pl.BlockSpec(memory_space=pltpu.VMEM))(x)


def _adamw(w, g, m, v, *, name):
    _, R, C = w.shape
    tr = R
    for t in range(8, R + 1, 8):
        if R % t == 0 and t * C * 4 <= (1 << 20):
            tr = t

    def body(w_ref, g_ref, m_ref, v_ref, d_ref, mo_ref, vo_ref):
        gv = g_ref[...]
        m2 = ADAM_B1 * m_ref[...] + (1.0 - ADAM_B1) * gv
        v2 = ADAM_B2 * v_ref[...] + (1.0 - ADAM_B2) * (gv * gv)
        m_hat = m2 / (1.0 - ADAM_B1 ** ADAM_STEP)
        v_hat = v2 / (1.0 - ADAM_B2 ** ADAM_STEP)
        d_ref[...] = -ADAM_LR * (m_hat / (jnp.sqrt(v_hat) + ADAM_EPS) + ADAM_WD * w_ref[...])
        mo_ref[...] = m2
        vo_ref[...] = v2

    blk = pl.BlockSpec((None, tr, C), lambda i: (0, i, 0))
    o = jax.ShapeDtypeStruct((1, R, C), F32)
    return _pcall(
        body, name=name, out_shape=(o, o, o), grid=(R // tr,),
        in_specs=[blk, pl.BlockSpec((tr, C), lambda i: (i, 0)), blk, blk], out_specs=(blk, blk, blk),
        compiler_params=pltpu.CompilerParams(dimension_semantics=("parallel",), vmem_limit_bytes=_limit(20 * _nbytes((tr, C), F32))),
    )(w, g, m, v)


def _place():
    x, y, c = lax.axis_index("x"), lax.axis_index("y"), lax.axis_index("c")
    return x, y, c, [(1 - x, y), (x, 1 - y), (1 - x, 1 - y)]


def _two_level_gather(x_ref, out_ref, send_sems, recv_sems, local_sem):
    x, y, c, chips = _place()
    me, sibling = (x, y, c), (x, y, 1 - c)

    def blk(px, py, pc):
        return out_ref.at[4 * px + 2 * py + pc]

    def copy(k, block, to, src=None):
        return pltpu.make_async_remote_copy(
            src_ref=blk(*block) if src is None else src, dst_ref=blk(*block),
            send_sem=send_sems.at[k], recv_sem=recv_sems.at[k], device_id=to, device_id_type=MESH)

    mine = pltpu.make_async_copy(x_ref, blk(*me), local_sem)
    mine.start()
    first = [copy(0, me, sibling, src=x_ref)]
    first += [copy(1 + j, me, (*chip, c), src=x_ref) for j, chip in enumerate(chips)]
    for cp in first:
        cp.start()
    passed = [copy(4 + j, (*chip, c), sibling) for j, chip in enumerate(chips)]
    for j, chip in enumerate(chips):
        copy(1 + j, (*chip, c), me).wait_recv()
        passed[j].start()
    copy(0, sibling, me).wait_recv()
    for j, chip in enumerate(chips):
        copy(4 + j, (*chip, 1 - c), me).wait_recv()
    for cp in first + passed:
        cp.wait_send()
    mine.wait()


_GATHER_SEMS = [pltpu.SemaphoreType.DMA((7,)), pltpu.SemaphoreType.DMA((7,)), pltpu.SemaphoreType.DMA]


class _SideJob(NamedTuple):
    inputs: tuple
    out_shapes: tuple
    aliases: dict
    scratch: tuple
    start: Callable
    finish: Callable


def _block_index(px, py, pc):
    return 4 * px + 2 * py + pc


def _gather_level1_job(halves):
    nw = len(halves)

    def copies(ins, outs, scratch, n):
        sends, recvs, _ = scratch
        x, y, c, chips = _place()
        mine = outs[n].at[_block_index(x, y, c)]
        peers = [(x, y, 1 - c)] + [(*chip, c) for chip in chips]
        out = []
        for t, peer in enumerate(peers):
            sem = dict(send_sem=sends.at[4 * n + t], recv_sem=recvs.at[4 * n + t], device_id_type=MESH)
            landing = outs[n].at[_block_index(*peer)]
            out.append((pltpu.make_async_remote_copy(src_ref=ins[n], dst_ref=mine, device_id=peer, **sem),
                        pltpu.make_async_remote_copy(src_ref=landing, dst_ref=landing, device_id=peer, **sem)))
        local = pltpu.make_async_copy(ins[n], mine, scratch[2].at[n])
        return out, local

    def start(ins, outs, scratch):
        for n in range(nw):
            pairs, local = copies(ins, outs, scratch, n)
            local.start()
            for to, _ in pairs:
                to.start()

    def finish(ins, outs, scratch):
        for n in range(nw):
            pairs, local = copies(ins, outs, scratch, n)
            for to, frm in pairs:
                frm.wait_recv()
                to.wait_send()
            local.wait()

    return _SideJob(
        inputs=tuple(halves), out_shapes=tuple(jax.ShapeDtypeStruct((8,) + h.shape, h.dtype) for h in halves), aliases={},
        scratch=(pltpu.SemaphoreType.DMA((4 * nw,)), pltpu.SemaphoreType.DMA((4 * nw,)), pltpu.SemaphoreType.DMA((nw,))),
        start=start, finish=finish)


def _gather_direct_job(halves):
    nw = len(halves)

    def copies(ins, outs, scratch, n):
        sends, recvs, _ = scratch
        x, y, c, _ = _place()
        mine = outs[n].at[_block_index(x, y, c)]
        out = []
        for f, (fx, fy, fc) in enumerate(_FLIPS):
            peer = (_flipped(x, fx), _flipped(y, fy), _flipped(c, fc))
            sem = dict(send_sem=sends.at[7 * n + f], recv_sem=recvs.at[7 * n + f], device_id=peer, device_id_type=MESH)
            landing = outs[n].at[_block_index(*peer)]
            out.append((pltpu.make_async_remote_copy(src_ref=ins[n], dst_ref=mine, **sem),
                        pltpu.make_async_remote_copy(src_ref=landing, dst_ref=landing, **sem)))
        return out, pltpu.make_async_copy(ins[n], mine, scratch[2].at[n])

    def start(ins, outs, scratch):
        for n in range(nw):
            pairs, local = copies(ins, outs, scratch, n)
            local.start()
            for to, _ in pairs:
                to.start()

    def finish(ins, outs, scratch):
        for n in range(nw):
            pairs, local = copies(ins, outs, scratch, n)
            for to, frm in pairs:
                frm.wait_recv()
                to.wait_send()
            local.wait()

    return _SideJob(
        inputs=tuple(halves), out_shapes=tuple(jax.ShapeDtypeStruct((8,) + h.shape, h.dtype) for h in halves), aliases={},
        scratch=(pltpu.SemaphoreType.DMA((7 * nw,)), pltpu.SemaphoreType.DMA((7 * nw,)), pltpu.SemaphoreType.DMA((nw,))),
        start=start, finish=finish)


def _join_jobs(a, b):
    cut = (len(a.inputs), len(a.out_shapes), len(a.scratch))

    def parts(ins, outs, scratch):
        return ((ins[:cut[0]], outs[:cut[1]], scratch[:cut[2]]), (ins[cut[0]:], outs[cut[1]:], scratch[cut[2]:]))

    def start(ins, outs, scratch):
        pa, pb = parts(ins, outs, scratch)
        a.start(*pa)
        b.start(*pb)

    def finish(ins, outs, scratch):
        pa, pb = parts(ins, outs, scratch)
        a.finish(*pa)
        b.finish(*pb)

    aliases = dict(a.aliases)
    aliases.update({cut[0] + i: cut[1] + o for i, o in b.aliases.items()})
    return _SideJob(inputs=a.inputs + b.inputs, out_shapes=a.out_shapes + b.out_shapes, aliases=aliases,
                    scratch=a.scratch + b.scratch, start=start, finish=finish)


def _gather_level2_job(gathered):
    nw = len(gathered)

    def copies(outs, scratch, n):
        sends, recvs = scratch
        x, y, c, chips = _place()
        out = []
        for j, chip in enumerate(chips):
            sem = dict(send_sem=sends.at[3 * n + j], recv_sem=recvs.at[3 * n + j], device_id=(x, y, 1 - c), device_id_type=MESH)
            going = outs[n].at[_block_index(*chip, c)]
            landing = outs[n].at[_block_index(*chip, 1 - c)]
            out.append((pltpu.make_async_remote_copy(src_ref=going, dst_ref=going, **sem),
                        pltpu.make_async_remote_copy(src_ref=landing, dst_ref=landing, **sem)))
        return out

    def start(ins, outs, scratch):
        for n in range(nw):
            for to, _ in copies(outs, scratch, n):
                to.start()

    def finish(ins, outs, scratch):
        for n in range(nw):
            for to, frm in copies(outs, scratch, n):
                frm.wait_recv()
                to.wait_send()

    return _SideJob(
        inputs=tuple(gathered), out_shapes=tuple(jax.ShapeDtypeStruct(g.shape, g.dtype) for g in gathered),
        aliases={n: n for n in range(nw)},
        scratch=(pltpu.SemaphoreType.DMA((3 * nw,)), pltpu.SemaphoreType.DMA((3 * nw,))),
        start=start, finish=finish)


def _all_gather_rows(x, *, name):
    R, C = x.shape

    def body(x_ref, out_ref, send_sems, recv_sems, local_sem):
        _two_level_gather(x_ref, out_ref, send_sems, recv_sems, local_sem)

    return pl.pallas_call(
        body, name=name, out_shape=jax.ShapeDtypeStruct((8, R, C), x.dtype),
        in_specs=[pl.BlockSpec(memory_space=pltpu.VMEM)], out_specs=pl.BlockSpec(memory_space=pltpu.VMEM),
        scratch_shapes=list(_GATHER_SEMS),
        compiler_params=pltpu.CompilerParams(vmem_limit_bytes=_limit(10 * _nbytes((R, C), x.dtype))),
    )(x)


CAST_ROWS = 16
_FLIPS = [(fx, fy, fc) for fx in (0, 1) for fy in (0, 1) for fc in (0, 1)][1:]


def _flipped(v, bit):
    return 1 - v if bit else v


def _scatter_direct_job(pieces, rows=None, partly=None):
    nw = len(pieces)

    def band(ref):
        return ref if rows is None else ref.at[pl.ds(rows[0], rows[1]), :]

    def copies(ins, outs, scratch, n):
        sends, recvs = scratch
        x, y, c, _ = _place()
        out = []
        for f, (fx, fy, fc) in enumerate(_FLIPS):
            peer = (_flipped(x, fx), _flipped(y, fy), _flipped(c, fc))
            sem = dict(send_sem=sends.at[7 * n + f], recv_sem=recvs.at[7 * n + f], device_id=peer, device_id_type=MESH)
            landing = band(outs[n].at[f])
            out.append((pltpu.make_async_remote_copy(src_ref=band(ins[n].at[_block_index(*peer)]), dst_ref=landing, **sem),
                        pltpu.make_async_remote_copy(src_ref=landing, dst_ref=landing, **sem)))
        return out

    def start(ins, outs, scratch):
        for n in range(nw):
            for to, _ in copies(ins, outs, scratch, n):
                to.start()

    def finish(ins, outs, scratch):
        for n in range(nw):
            for to, frm in copies(ins, outs, scratch, n):
                frm.wait_recv()
                to.wait_send()

    return _SideJob(
        inputs=tuple(pieces) + tuple(partly or ()),
        out_shapes=tuple(jax.ShapeDtypeStruct((7,) + p.shape[1:], p.dtype) for p in pieces),
        aliases={nw + n: n for n in range(nw)} if partly else {},
        scratch=(pltpu.SemaphoreType.DMA((7 * nw,)), pltpu.SemaphoreType.DMA((7 * nw,))),
        start=start, finish=finish)


def _scatter_finish(g4s, landed, *, name):
    nw = len(g4s)
    dims = [g.shape[1:] for g in g4s]

    def body(*refs):
        g_refs, l_refs, out_refs, own = refs[:nw], refs[nw:2 * nw], refs[2 * nw:3 * nw], refs[3 * nw:4 * nw]
        load_sems, send_sems, recv_sems = refs[4 * nw:]
        x, y, core, _ = _place()
        k = 2 * x + y
        loads = []
        for n, (r, c) in enumerate(dims):
            my0 = pl.multiple_of(core * (r // 2), CAST_ROWS)
            ld = pltpu.make_async_copy(g_refs[n].at[k, pl.ds(my0, r // 2), :], own[n], load_sems.at[n])
            ld.start()
            loads.append(ld)
        swaps = []
        for n, (r, c) in enumerate(dims):
            rh = r // 2
            my0 = pl.multiple_of(core * rh, CAST_ROWS)
            loads[n].wait()

            def fin(i, carry, n=n, my0=my0):
                r0 = pl.multiple_of(i * CAST_ROWS, CAST_ROWS)
                s = own[n][pl.ds(r0, CAST_ROWS), :]
                for f in range(7):
                    s = s + l_refs[n][f, pl.ds(r0, CAST_ROWS), :].astype(F32)
                out_refs[n][pl.ds(my0 + r0, CAST_ROWS), :] = s
                return carry

            lax.fori_loop(0, rh // CAST_ROWS, fin, 0)
            half = out_refs[n].at[pl.ds(my0, rh), :]
            sw = pltpu.make_async_remote_copy(src_ref=half, dst_ref=half, send_sem=send_sems.at[n], recv_sem=recv_sems.at[n],
                                              device_id=(x, y, 1 - core), device_id_type=MESH)
            sw.start()
            swaps.append(sw)
        for sw in swaps:
            sw.wait()

    need = sum(_nbytes((7, r // 2, c), BF16) + _nbytes((r // 2, c), F32) + _nbytes((r, c), F32) for r, c in dims)
    vmem = pl.BlockSpec(memory_space=pltpu.VMEM)
    return pl.pallas_call(
        body, name=name, out_shape=tuple(jax.ShapeDtypeStruct((r, c), F32) for r, c in dims),
        in_specs=[pl.BlockSpec(memory_space=pl.ANY)] * nw + [vmem] * nw, out_specs=(vmem,) * nw,
        scratch_shapes=[pltpu.VMEM((r // 2, c), F32) for r, c in dims]
        + [pltpu.SemaphoreType.DMA((nw,)), pltpu.SemaphoreType.DMA((nw,)), pltpu.SemaphoreType.DMA((nw,))],
        compiler_params=pltpu.CompilerParams(vmem_limit_bytes=_limit(need * 1.1)),
    )(*g4s, *landed)


def _gather_weight(w, *, name):
    r, c = w.shape
    rh = r // 2
    assert rh % CAST_ROWS == 0

    def body(w_hbm, out_ref, tmp, xb, send_sems, recv_sems, local_sem):
        core = lax.axis_index("c")
        ld = pltpu.make_async_copy(w_hbm.at[pl.ds(pl.multiple_of(core * rh, CAST_ROWS), rh), :], tmp, local_sem)
        ld.start()
        ld.wait()

        def cast(i, carry):
            r0 = pl.multiple_of(i * CAST_ROWS, CAST_ROWS)
            xb[pl.ds(r0, CAST_ROWS), :] = tmp[pl.ds(r0, CAST_ROWS), :].astype(BF16)
            return carry

        lax.fori_loop(0, rh // CAST_ROWS, cast, 0)
        _two_level_gather(xb, out_ref, send_sems, recv_sems, local_sem)

    need = _nbytes((8, rh, c), BF16) + _nbytes((rh, c), F32) + _nbytes((rh, c), BF16)
    out = pl.pallas_call(
        body, name=name, out_shape=jax.ShapeDtypeStruct((8, rh, c), BF16),
        in_specs=[pl.BlockSpec(memory_space=pl.ANY)], out_specs=pl.BlockSpec(memory_space=pltpu.VMEM),
        scratch_shapes=[pltpu.VMEM((rh, c), F32), pltpu.VMEM((rh, c), BF16)] + list(_GATHER_SEMS),
        compiler_params=pltpu.CompilerParams(vmem_limit_bytes=_limit(need * 1.3)),
    )(w)
    return out.reshape(4, r, c)


def _reduce_scatter_weight(g4, *, name):
    _, r, c = g4.shape
    rh = r // 2
    assert rh % CAST_ROWS == 0
    nsteps = rh // CAST_ROWS

    def body(g_hbm, out_ref, mine, tmp, sbuf, rbuf_a, rbuf_b, a_send, a_recv, b_send, b_recv, c_send, c_recv, lsem):
        x, y, core, chips = _place()
        sibling = (x, y, 1 - core)
        k = 2 * x + y
        my0 = pl.multiple_of(core * rh, CAST_ROWS)
        ot0 = pl.multiple_of((1 - core) * rh, CAST_ROWS)

        ld = pltpu.make_async_copy(g_hbm.at[:, pl.ds(my0, rh), :], mine, lsem)
        ld.start()
        ld.wait()
        for j in range(4):
            ldj = pltpu.make_async_copy(g_hbm.at[j, pl.ds(ot0, rh), :], tmp, lsem)
            ldj.start()
            ldj.wait()

            def cast(i, carry, j=j):
                r0 = pl.multiple_of(i * CAST_ROWS, CAST_ROWS)
                sbuf[j, pl.ds(r0, CAST_ROWS), :] = tmp[pl.ds(r0, CAST_ROWS), :].astype(BF16)
                return carry

            lax.fori_loop(0, nsteps, cast, 0)

        to_sib = pltpu.make_async_remote_copy(src_ref=sbuf, dst_ref=rbuf_a, send_sem=a_send, recv_sem=a_recv,
                                              device_id=sibling, device_id_type=MESH)
        to_sib.start()
        to_sib.wait()

        for j in range(4):
            def add(i, carry, j=j):
                r0 = pl.multiple_of(i * CAST_ROWS, CAST_ROWS)
                s = mine[j, pl.ds(r0, CAST_ROWS), :] + rbuf_a[j, pl.ds(r0, CAST_ROWS), :].astype(F32)
                mine[j, pl.ds(r0, CAST_ROWS), :] = s
                sbuf[j, pl.ds(r0, CAST_ROWS), :] = s.astype(BF16)
                return carry

            lax.fori_loop(0, nsteps, add, 0)

        sends = []
        for d, (px, py) in enumerate(chips):
            cp = pltpu.make_async_remote_copy(src_ref=sbuf.at[2 * px + py], dst_ref=rbuf_b.at[d], send_sem=b_send.at[d],
                                              recv_sem=b_recv.at[d], device_id=(px, py, core), device_id_type=MESH)
            cp.start()
            sends.append(cp)
        for cp in sends:
            cp.wait()

        def fin(i, carry):
            r0 = pl.multiple_of(i * CAST_ROWS, CAST_ROWS)
            s = mine[k, pl.ds(r0, CAST_ROWS), :]
            for d in range(3):
                s = s + rbuf_b[d, pl.ds(r0, CAST_ROWS), :].astype(F32)
            out_ref[pl.ds(my0 + r0, CAST_ROWS), :] = s
            return carry

        lax.fori_loop(0, nsteps, fin, 0)
        half = out_ref.at[pl.ds(my0, rh), :]
        swap = pltpu.make_async_remote_copy(src_ref=half, dst_ref=half, send_sem=c_send, recv_sem=c_recv,
                                            device_id=sibling, device_id_type=MESH)
        swap.start()
        swap.wait()

    need = (_nbytes((4, rh, c), F32) + _nbytes((rh, c), F32) + 2 * _nbytes((4, rh, c), BF16) + _nbytes((3, rh, c), BF16)
            + _nbytes((r, c), F32))
    return pl.pallas_call(
        body, name=name, out_shape=jax.ShapeDtypeStruct((r, c), F32),
        in_specs=[pl.BlockSpec(memory_space=pl.ANY)], out_specs=pl.BlockSpec(memory_space=pltpu.VMEM),
        scratch_shapes=[pltpu.VMEM((4, rh, c), F32), pltpu.VMEM((rh, c), F32), pltpu.VMEM((4, rh, c), BF16),
                        pltpu.VMEM((4, rh, c), BF16), pltpu.VMEM((3, rh, c), BF16),
                        pltpu.SemaphoreType.DMA, pltpu.SemaphoreType.DMA, pltpu.SemaphoreType.DMA((3,)),
                        pltpu.SemaphoreType.DMA((3,)), pltpu.SemaphoreType.DMA, pltpu.SemaphoreType.DMA,
                        pltpu.SemaphoreType.DMA],
        compiler_params=pltpu.CompilerParams(vmem_limit_bytes=_limit(need * 1.2)),
    )(g4)


def _cols_from_shards(g):
    n, K, c = g.shape
    return g.transpose(1, 0, 2).reshape(K, n * c)


def _cols_to_shards(w):
    K, N = w.shape
    return w.reshape(K, 4, N // 4).transpose(1, 0, 2)


def _pad_heads_cols(w, width, lane0=0):
    K = w.shape[0]
    w3 = w.reshape(K, H, width)
    return jnp.pad(w3, ((0, 0), (0, 0), (lane0, HP - lane0 - width))).reshape(K, H * HP)


def _unpad_heads_cols(w, width, lane0=0):
    K = w.shape[0]
    return w.reshape(K, H, HP)[:, :, lane0:lane0 + width].reshape(K, H * width)


def _pad_block(w, lane0=0):
    return jnp.pad(w, ((0, 0), (lane0, LANES - lane0 - w.shape[1])))


_IN_SPLITS = [512, 1024, 1536, 1544, 2312, 2568, 2600, 3624]


def _pad_w_in(w):
    fq, fk, fv, flog, cq, ckv, krin, gfox, gmla = jnp.split(w, _IN_SPLITS, axis=1)
    return jnp.concatenate([_pad_heads_cols(fq, FOX_HD), _pad_heads_cols(fk, FOX_HD), _pad_heads_cols(fv, FOX_HD),
                            cq, ckv, gfox, gmla, _pad_block(flog), _pad_block(krin, KRIN_LANE)], axis=1)


def _unpad_w_in(wp):
    qkv, rest = wp[:, :NQKV], wp[:, NQKV:]
    fq, fk, fv = (_unpad_heads_cols(qkv[:, i * H * HP:(i + 1) * H * HP], FOX_HD) for i in range(3))
    return jnp.concatenate([fq, fk, fv, rest[:, OFF_FLOG:OFF_FLOG + H], rest[:, OFF_CQ:OFF_CQ + Q_LORA],
                            rest[:, OFF_CKV:OFF_CKV + KV_LORA], rest[:, OFF_KRIN + KRIN_LANE:OFF_KRIN + KRIN_LANE + MLA_ROPE],
                            rest[:, OFF_GFOX:OFF_GFOX + D], rest[:, OFF_GMLA:OFF_GMLA + D]], axis=1)


def _pad_w_ukv(w):
    w3 = w.reshape(KV_LORA, H, MLA_NOPE + MLA_V)
    kp = jnp.pad(w3[:, :, :MLA_NOPE], ((0, 0), (0, 0), (0, HP - MLA_NOPE))).reshape(KV_LORA, H * HP)
    vp = jnp.pad(w3[:, :, MLA_NOPE:], ((0, 0), (0, 0), (0, HP - MLA_V))).reshape(KV_LORA, H * HP)
    return jnp.concatenate([kp, vp], axis=1)


def _unpad_w_ukv(wp):
    kp = wp[:, :H * HP].reshape(KV_LORA, H, HP)[:, :, :MLA_NOPE]
    vp = wp[:, H * HP:].reshape(KV_LORA, H, HP)[:, :, :MLA_V]
    return jnp.concatenate([kp, vp], axis=2).reshape(KV_LORA, H * (MLA_NOPE + MLA_V))


def _pad_heads_rows(w, width):
    N = w.shape[1]
    return jnp.pad(w.reshape(H, width, N), ((0, 0), (0, HP - width), (0, 0))).reshape(H * HP, N)


def _unpad_heads_rows(w, width):
    N = w.shape[1]
    return w.reshape(H, HP, N)[:, :width, :].reshape(H * width, N)


def _rope_tables(positions):
    inv_freq = 1.0 / (ROPE_THETA ** (jnp.arange(0, MLA_ROPE, 2, dtype=F32) / MLA_ROPE))
    ang = positions.reshape(S, 1).astype(F32) * inv_freq
    cos, sin = jnp.cos(ang), jnp.sin(ang)
    ones = jnp.ones((S, KRIN_LANE), F32)
    tail = jnp.zeros((S, LANES - KRIN_LANE - MLA_ROPE), F32)
    ctab = jnp.concatenate([ones, cos, cos, tail], axis=1)
    stab = jnp.concatenate([0.0 * ones, -sin, sin, tail], axis=1)
    return ctab, stab


def _local_step(x, target, mod, positions, gains, bf, W, late=None):
    W = dict(W)
    sh1, sc1, gt1, sh2, sc2, gt2 = (mod[:, i * D:(i + 1) * D] for i in range(6))
    ops1, ops2 = 1.0 + sc1, 1.0 + sc2
    ones = lambda w: jnp.ones((1, w), F32)
    zeros = lambda w: jnp.zeros((1, w), F32)
    bf_blk = _pad_block(bf)
    ctab, stab = _rope_tables(positions)
    fox_scale = 1.0 / math.sqrt(FOX_HD)
    mla_scale = 1.0 / math.sqrt(MLA_NOPE + MLA_ROPE)

    h1 = _norm_mod(x, 0, D, gains["g_pre_mix"], ops1, sh1, name="f_pre_mix")
    qkv = _matmul(h1, W["w_in_qkv"], out_dtype=_MXU, name="f_proj_qkv", tm_cap=2048)
    job = _gather_level1_job([late[n] for n in _GATHER_A0]) if late else None
    rest, *landed0 = _matmul_carrying(job, h1, W["w_in_rest"], name="f_proj_rest", tm_cap=2048, tn_cap=256)
    F = _fox_prep(rest, bf_blk, name="f_fox_prep")
    Ft = F[:, :H].T
    fcol, frow = Ft.reshape(H, S, 1), Ft.reshape(H, S // TKF, 1, TKF)
    fox_heads = (0, H, 2 * H)
    job = None
    if late:
        job = _join_jobs(_gather_level1_job([late[n] for n in _GATHER_A1]), _gather_direct_job([late[n] for n in _GATHER_E]))
    oa, lse_a, *landed = _attn_fwd(qkv, qkv, qkv, frow, scale=fox_scale, name="f_attn_fox", side=job, heads0=fox_heads)
    if late:
        W.update(_mla_in_weights({n: g.reshape(4, 2 * g.shape[1], g.shape[2])
                                  for n, g in zip(_GATHER_E, landed[len(_GATHER_A1):])}))
    landed = landed0 + landed[:len(_GATHER_A1)]

    cqn = _norm_mod(rest, OFF_CQ // Q_LORA, Q_LORA, gains["g_q_lora"], ones(Q_LORA), zeros(Q_LORA), name="f_norm_cq")
    ckvn = _norm_mod(rest, OFF_CKV // KV_LORA, KV_LORA, gains["g_kv_lora"], ones(KV_LORA), zeros(KV_LORA), name="f_norm_ckv")
    qb = _matmul(cqn, W["w_uq"], name="f_uq")
    kvb = _matmul(ckvn, W["w_ukv"], name="f_ukv")
    qm, km, vm = _mla_assemble(qb, kvb, rest, ctab, stab, name="f_mla_assemble")
    job = _join_jobs(_gather_level2_job(landed), _gather_direct_job([late[n] for n in _GATHER_B])) if late else None
    ob, lse_b, *landed = _attn_fwd(qm, km, vm, None, scale=mla_scale, name="f_attn_mla", side=job)
    if late:
        W.update(_late_weights({n: g.reshape(4, 2 * g.shape[1], g.shape[2]) for n, g in zip(_GATHER_A + _GATHER_B, landed)}))

    pa = _matmul(oa, W["w_proj_fox"], name="f_proj_fox")
    pb = _matmul(ob, W["w_proj_mla"], name="f_proj_mla")
    merged = _merge(rest, pa, pb, name="f_merge")
    y1 = _matmul(merged, W["w_out"], name="f_out")
    x2 = _post_res(x, y1, gains["g_post_mix"], gt1, name="f_post_mix")
    h2 = _norm_mod(x2, 0, D, gains["g_pre_ffn"], ops2, sh2, name="f_pre_ffn")
    gu = _matmul(h2, W["w_ffn_in"], name="f_ffn_in", b_shards=True, tn_cap=1408)
    act = _swiglu(gu, name="f_swiglu")
    y2 = _matmul(act, W["w_ffn_out"], name="f_ffn_out", tk_cap=1408, tn_cap=1024)
    dout, loss = _post_res_loss(x2, y2, gains["g_post_ffn"], gt2, target, name="f_post_ffn_loss")

    dy2, s_gt2, s_gpost2 = _post_res_bwd(dout, y2, gains["g_post_ffn"], gt2, name="b_post_ffn")
    dact = _matmul(dy2, W["w_ffn_out"], tb=True, name="b_ffn_out_dx", tn_cap=1408)
    dW_ffn_out = _matmul(act, dy2, ta=True, name="b_ffn_out_dw", tm_cap=1408, tk_cap=2048)
    dgu = _swiglu_bwd(gu, dact, name="b_swiglu")
    job_x = job_w = None
    landed_c = []
    if late:
        fo = dW_ffn_out.reshape(8, D_FF // 8, D).astype(BF16)
        half = fo.shape[1] // 2
        job_x = _scatter_direct_job([fo], rows=(0, half))
    dh2, *landed_c = _matmul_carrying(job_x, dgu, W["w_ffn_in"], tb=True, b_shards=True, name="b_ffn_in_dx", tk_cap=1408,
                                      tn_cap=1024)
    if late:
        job_w = _scatter_direct_job([fo], rows=(half, half), partly=landed_c)
    dW_ffn_in, *landed_c = _matmul_carrying(job_w, h2, dgu, ta=True, name="b_ffn_in_dw", out_shards=True, tn_cap=1408,
                                            tk_cap=2048)
    dx2, s_sh2, s_a2 = _norm_mod_bwd(x2, 0, D, dh2, gains["g_pre_ffn"], ops2, dout, name="b_pre_ffn")
    dy1, s_gt1, s_gpost1 = _post_res_bwd(dx2, y1, gains["g_post_mix"], gt1, name="b_post_mix")
    dmerged = _matmul(dy1, W["w_out"], tb=True, name="b_out_dx")
    dW_out = _matmul(merged, dy1, ta=True, name="b_out_dw", tk_cap=2048)
    dpa, dpb, dgfox, dgmla = _merge_bwd(rest, pa, pb, dmerged, name="b_merge")
    doa = _matmul(dpa, W["w_proj_fox"], tb=True, name="b_proj_fox_dx")
    dW_proj_fox = _matmul(oa, dpa, ta=True, name="b_proj_fox_dw")
    dob = _matmul(dpb, W["w_proj_mla"], tb=True, name="b_proj_mla_dx")
    dW_proj_mla = _matmul(ob, dpb, ta=True, name="b_proj_mla_dw")

    delta_a, doa16 = _attn_delta(doa, oa, name="b_delta_fox")
    as_rows = lambda a: a.reshape(H, S // TQB, 1, TQB)
    dW = dict(w_proj_fox=dW_proj_fox, w_proj_mla=dW_proj_mla, w_out=dW_out, w_ffn_in=dW_ffn_in, w_ffn_out=dW_ffn_out)
    job_a = job_b = None
    if late:
        late_shards = _grad_shards(dW)
        pieces = {n: s.reshape(8, s.shape[1] // 2, s.shape[2]).astype(BF16) for n, s in late_shards.items()}
        first = [pieces[n] for n in _SCATTER_A]
        job_a = _scatter_direct_job(first, rows=(0, _SCATTER_SPLIT))
    dqa, dka, dva, *landed_a = _attn_bwd(qkv, qkv, qkv, doa16, as_rows(lse_a), as_rows(delta_a), fcol, scale=fox_scale,
                                         name="b_attn_fox", side=job_a, heads0=fox_heads)
    delta_b, dob16 = _attn_delta(dob, ob, name="b_delta_mla")
    if late:
        rest_rows = (_SCATTER_SPLIT, first[0].shape[1] - _SCATTER_SPLIT)
        job_b = _join_jobs(_scatter_direct_job(first, rows=rest_rows, partly=landed_a),
                           _scatter_direct_job([pieces[n] for n in _SCATTER_B]))
    dqm, dkm, dvm, *landed_b = _attn_bwd(qm, km, vm, dob16, as_rows(lse_b), as_rows(delta_b), None, scale=mla_scale,
                                         name="b_attn_mla", side=job_b)
    reduced = {}
    if late:
        order = _SCATTER_A + _SCATTER_B + _SCATTER_C
        done = _scatter_finish([late_shards[n] for n in order], landed_b + landed_c, name="scatter_late")
        reduced = dict(zip(order, done))
        dW = {}

    dF = (dqa[:, ROW_SUM_LANE::HP] - dka[:, COL_SUM_LANE::HP]) * (1.0 / fox_scale)
    dflog, s_bf = _fox_bwd_prep(rest, bf_blk, _pad_block(dF), name="b_fox_prep")

    dqb, dkvb, dkrin = _mla_assemble_bwd(dqm, dkm, dvm, ctab, stab, name="b_mla_assemble")
    dcqn = _matmul(dqb, W["w_uq"], tb=True, name="b_uq_dx")
    dW_uq = _matmul(cqn, dqb, ta=True, name="b_uq_dw", tk_cap=2048)
    dckvn = _matmul(dkvb, W["w_ukv"], tb=True, name="b_ukv_dx")
    dW_ukv = _matmul(ckvn, dkvb, ta=True, name="b_ukv_dw", tk_cap=2048)
    dcq, _, s_gq = _norm_mod_bwd(rest, OFF_CQ // Q_LORA, Q_LORA, dcqn, gains["g_q_lora"], ones(Q_LORA), None, name="b_norm_cq")
    dckv, _, s_gkv = _norm_mod_bwd(rest, OFF_CKV // KV_LORA, KV_LORA, dckvn, gains["g_kv_lora"], ones(KV_LORA), None, name="b_norm_ckv")

    c16 = lambda a: a.astype(_MXU)
    dproj = jnp.concatenate([c16(dqa), c16(dka), c16(dva), c16(dcq), c16(dckv), dgfox, dgmla, c16(dflog), c16(dkrin)], axis=1)
    w_in_full = jnp.concatenate([W["w_in_qkv"], W["w_in_rest"]], axis=1)
    dh1 = _matmul(dproj, w_in_full, tb=True, name="b_in_dx", tk_cap=1280, tn_cap=1024)
    dW_in = _matmul(h1, dproj, ta=True, name="b_in_dw", tn_cap=640, tk_cap=2048)
    grad_x, s_sh1, s_a1 = _norm_mod_bwd(x, 0, D, dh1, gains["g_pre_mix"], ops1, dx2, name="b_pre_mix")

    dmod = jnp.concatenate([s_sh1, s_a1 * gains["g_pre_mix"], s_gt1, s_sh2, s_a2 * gains["g_pre_ffn"], s_gt2], axis=1)
    small = dict(dmod=dmod, g_pre_mix=s_a1 * ops1, g_post_mix=s_gpost1, g_pre_ffn=s_a2 * ops2, g_post_ffn=s_gpost2,
                 g_q_lora=s_gq, g_kv_lora=s_gkv, b_forget=s_bf)
    dW = dict(dW, w_in=dW_in, w_uq=dW_uq, w_ukv=dW_ukv)
    return loss, grad_x, dW, reduced, small


_BIG = ["w_in", "w_uq", "w_ukv", "w_proj_fox", "w_proj_mla", "w_out", "w_ffn_in", "w_ffn_out"]
_COL_SHARDED = {"w_in", "w_ukv", "w_proj_fox", "w_proj_mla", "w_ffn_in"}
_SMALL = ["b_ada", "g_pre_mix", "g_post_mix", "g_pre_ffn", "g_post_ffn", "b_forget", "g_q_lora", "g_kv_lora"]
_ORDER = ["w_ada", "b_ada", "g_pre_mix", "g_post_mix", "g_pre_ffn", "g_post_ffn", "w_in", "b_forget", "g_q_lora", "w_uq",
          "g_kv_lora", "w_ukv", "w_proj_fox", "w_proj_mla", "w_out", "w_ffn_in", "w_ffn_out"]
_ROW = {}
_off = 0
for _n, _w in [("dmod", 6 * D), ("g_pre_mix", D), ("g_post_mix", D), ("g_pre_ffn", D), ("g_post_ffn", D), ("g_q_lora", Q_LORA),
               ("g_kv_lora", KV_LORA), ("b_forget", LANES), ("loss", LANES)]:
    _ROW[_n] = (_off, _w)
    _off += _w
_ROW_LEN = _off


_EARLY = ["w_in", "w_uq", "w_ukv"]
_LATE = ["w_proj_fox", "w_proj_mla", "w_out", "w_ffn_in", "w_ffn_out"]
_GATHER_A0 = ["w_proj_fox", "w_proj_mla", "w_out"]
_GATHER_A1 = ["w_ffn_in"]
_GATHER_A = _GATHER_A0 + _GATHER_A1
_GATHER_E = ["w_uq", "w_ukv"]
_GATHER_B = ["w_ffn_out"]
_SCATTER_A = ["w_ffn_in"]
_SCATTER_SPLIT = 320
_SCATTER_B = ["w_out", "w_proj_fox", "w_proj_mla"]
_SCATTER_C = ["w_ffn_out"]


def _rows_from_shards(g):
    return g.reshape(-1, g.shape[2])


def _w_in_weights(g):
    w_in = _pad_w_in(_cols_from_shards(g))
    return dict(w_in_qkv=w_in[:, :NQKV], w_in_rest=w_in[:, NQKV:])


def _mla_in_weights(G):
    return dict(w_uq=_pad_heads_cols(_rows_from_shards(G["w_uq"]), MLA_NOPE + MLA_ROPE),
                w_ukv=_pad_w_ukv(_cols_from_shards(G["w_ukv"])))


def _early_weights(G):
    return {**_w_in_weights(G["w_in"]), **_mla_in_weights(G)}


def _late_weights(G):
    return dict(
        w_proj_fox=_pad_heads_rows(_cols_from_shards(G["w_proj_fox"]), FOX_HD),
        w_proj_mla=_pad_heads_rows(_cols_from_shards(G["w_proj_mla"]), MLA_V),
        w_out=_rows_from_shards(G["w_out"]), w_ffn_in=G["w_ffn_in"], w_ffn_out=_rows_from_shards(G["w_ffn_out"]))


def _full_weights(G):
    return {**_early_weights(G), **_late_weights(G)}


_UNPAD = dict(
    w_in=_unpad_w_in, w_uq=lambda g: _unpad_heads_cols(g, MLA_NOPE + MLA_ROPE), w_ukv=_unpad_w_ukv,
    w_proj_fox=lambda g: _unpad_heads_rows(g, FOX_HD), w_proj_mla=lambda g: _unpad_heads_rows(g, MLA_V),
    w_out=lambda g: g, w_ffn_out=lambda g: g)


def _grad_shards(dW):
    out = {}
    for n, g in dW.items():
        if n == "w_ffn_in":
            out[n] = g
        else:
            nat = _UNPAD[n](g)
            out[n] = _cols_to_shards(nat) if n in _COL_SHARDED else nat.reshape(4, nat.shape[0] // 4, nat.shape[1])
    return out


def kernel(x, c, positions, w_ada, b_ada, g_pre_mix, g_post_mix, g_pre_ffn, g_post_ffn, w_in, b_forget, g_q_lora, w_uq, g_kv_lora, w_ukv, w_proj_fox, w_proj_mla, w_out, w_ffn_in, w_ffn_out, loss_target, m_w_ada, m_b_ada, m_g_pre_mix, m_g_post_mix, m_g_pre_ffn, m_g_post_ffn, m_w_in, m_b_forget, m_g_q_lora, m_w_uq, m_g_kv_lora, m_w_ukv, m_w_proj_fox, m_w_proj_mla, m_w_out, m_w_ffn_in, m_w_ffn_out, v_w_ada, v_b_ada, v_g_pre_mix, v_g_post_mix, v_g_pre_ffn, v_g_post_ffn, v_w_in, v_b_forget, v_g_q_lora, v_w_uq, v_g_kv_lora, v_w_ukv, v_w_proj_fox, v_w_proj_mla, v_w_out, v_w_ffn_in, v_w_ffn_out):
    P = dict(w_ada=w_ada, b_ada=b_ada, g_pre_mix=g_pre_mix, g_post_mix=g_post_mix, g_pre_ffn=g_pre_ffn, g_post_ffn=g_post_ffn,
             w_in=w_in, b_forget=b_forget, g_q_lora=g_q_lora, w_uq=w_uq, g_kv_lora=g_kv_lora, w_ukv=w_ukv,
             w_proj_fox=w_proj_fox, w_proj_mla=w_proj_mla, w_out=w_out, w_ffn_in=w_ffn_in, w_ffn_out=w_ffn_out)
    M = dict(w_ada=m_w_ada, b_ada=m_b_ada, g_pre_mix=m_g_pre_mix, g_post_mix=m_g_post_mix, g_pre_ffn=m_g_pre_ffn,
             g_post_ffn=m_g_post_ffn, w_in=m_w_in, b_forget=m_b_forget, g_q_lora=m_g_q_lora, w_uq=m_w_uq, g_kv_lora=m_g_kv_lora,
             w_ukv=m_w_ukv, w_proj_fox=m_w_proj_fox, w_proj_mla=m_w_proj_mla, w_out=m_w_out, w_ffn_in=m_w_ffn_in,
             w_ffn_out=m_w_ffn_out)
    V = dict(w_ada=v_w_ada, b_ada=v_b_ada, g_pre_mix=v_g_pre_mix, g_post_mix=v_g_post_mix, g_pre_ffn=v_g_pre_ffn,
             g_post_ffn=v_g_post_ffn, w_in=v_w_in, b_forget=v_b_forget, g_q_lora=v_g_q_lora, w_uq=v_w_uq, g_kv_lora=v_g_kv_lora,
             w_ukv=v_w_ukv, w_proj_fox=v_w_proj_fox, w_proj_mla=v_w_proj_mla, w_out=v_w_out, w_ffn_in=v_w_ffn_in,
             w_ffn_out=v_w_ffn_out)
    ax, ay, ac = lax.axis_index("x"), lax.axis_index("y"), lax.axis_index("c")
    chip = 2 * ax + ay
    me = 4 * ax + 2 * ay + ac
    n_ada = w_ada.shape[2]

    c_all = _all_gather_rows(jnp.pad(c, ((0, 7), (0, 0))), name="gather_c")[:, 0, :]
    c_all = jnp.pad(c_all, ((0, 8), (0, 0)))
    b_shard = lax.dynamic_slice(b_ada, (0, chip * n_ada), (1, n_ada))
    mod_blk, silu_c = _ada_mod(c_all, w_ada[0], b_shard, name="ada_mod")
    mod_all = _all_gather_rows(mod_blk, name="gather_mod")
    mod_mine = lax.dynamic_index_in_dim(mod_all, me, axis=1, keepdims=False)
    mod = lax.dynamic_index_in_dim(mod_mine.reshape(4, 2, n_ada), ac, axis=1, keepdims=False).reshape(1, 6 * D)

    W = _w_in_weights(_gather_weight(P["w_in"][0], name="gather_w_in"))
    late = {}
    for n in _LATE + _GATHER_E:
        rh = P[n].shape[1] // 2
        late[n] = lax.dynamic_slice_in_dim(P[n][0], ac * rh, rh, axis=0).astype(BF16)

    gains = {n: P[n] for n in ["g_pre_mix", "g_post_mix", "g_pre_ffn", "g_post_ffn", "g_q_lora", "g_kv_lora"]}
    loss, grad_x, dW, grads, small = _local_step(x[0], loss_target[0], mod, positions, gains, b_forget, W, late)

    shards = _grad_shards(dW)
    grads.update({n: _reduce_scatter_weight(shards[n], name="scatter_" + n) for n in shards})

    small = dict(small, loss=_pad_block(loss))
    row = jnp.concatenate([small[n] for n in _ROW], axis=1)
    rows = _all_gather_rows(jnp.pad(row, ((0, 7), (0, 0))), name="gather_small")[:, 0, :]
    tot = _rowsum(rows, name="sum_small")
    piece = lambda n: tot[:, _ROW[n][0]:_ROW[n][0] + _ROW[n][1]]
    grads["b_ada"] = piece("dmod")
    for n in ["g_pre_mix", "g_post_mix", "g_pre_ffn", "g_post_ffn", "g_q_lora", "g_kv_lora"]:
        grads[n] = piece(n)
    grads["b_forget"] = piece("b_forget")[:, :H]
    loss_out = piece("loss")[0, 0]
    dmod_all = rows[:, _ROW["dmod"][0]:_ROW["dmod"][0] + 6 * D]
    dmod_shard = jnp.pad(lax.dynamic_slice(dmod_all, (0, chip * n_ada), (8, n_ada)), ((0, 8), (0, 0)))
    grads["w_ada"] = _matmul(silu_c, dmod_shard, ta=True, name="ada_dw")

    delta, new_m, new_v = {}, {}, {}
    for n in ["w_ada"] + _BIG:
        delta[n], new_m[n], new_v[n] = _adamw(P[n], grads[n], M[n], V[n], name="adamw_" + n)
    cat = lambda T: jnp.concatenate([T[n] for n in _SMALL], axis=1)
    d_s, m_s, v_s = (t[0] for t in _adamw(cat(P)[None], cat(grads), cat(M)[None], cat(V)[None], name="adamw_small"))
    o = 0
    for n in _SMALL:
        wdt = P[n].shape[1]
        delta[n], new_m[n], new_v[n] = d_s[:, o:o + wdt], m_s[:, o:o + wdt], v_s[:, o:o + wdt]
        o += wdt

    def shaped(T, n):
        return T[n].reshape(P[n].shape)

    return (loss_out, grad_x[None], *[shaped(grads, n) for n in _ORDER], *[shaped(delta, n) for n in _ORDER],
            *[shaped(new_m, n) for n in _ORDER], *[shaped(new_v, n) for n in _ORDER])
```
